```python
import math
import jax, jax.numpy as jnp
from jax import lax
import numpy as np

D_MODEL = 1024
BATCH = 4
SEQ = 8192
DEPTH = 2

CHUNK = 64
HEAD_DIM = 64
N_MIXERS = 4
GROUP_WIDTH = D_MODEL // N_MIXERS
D_MIX = N_MIXERS * GROUP_WIDTH

SWA_HEADS = GROUP_WIDTH // HEAD_DIM
SWA_KV_HEADS = SWA_HEADS // 2
SWA_WINDOW = 128
SWA_WIN_CHUNKS = -(-SWA_WINDOW // CHUNK)

DSA_HEADS = GROUP_WIDTH // HEAD_DIM
DSA_Q_RANK = D_MODEL // 8
IDX_HEADS = 4
IDX_DIM = 32
DSA_TOPK = 256
DSA_QBLOCK = 128

MLSTM_HEADS = GROUP_WIDTH // HEAD_DIM
MLSTM_CONV = 4

S5_GROUP_CH = 16
S5_GROUPS = GROUP_WIDTH // S5_GROUP_CH
S5_STATE = 64

D_FF = 2816
N_SUB = 3
ALPHA = (2 * DEPTH) ** 0.25
BETA = (8 * DEPTH) ** -0.25
LN_EPS = 1e-5
NEG_INF = -1e30
N_SOFTMAX_HEADS = SWA_HEADS + DSA_HEADS

IN_SIZES = (
    SWA_HEADS * HEAD_DIM,
    SWA_KV_HEADS * HEAD_DIM,
    SWA_KV_HEADS * HEAD_DIM,
    DSA_Q_RANK,
    HEAD_DIM,
    HEAD_DIM,
    IDX_DIM,
    IDX_HEADS,
    2 * MLSTM_HEADS * HEAD_DIM,
    MLSTM_HEADS * HEAD_DIM,
    MLSTM_HEADS,
    MLSTM_HEADS,
    MLSTM_HEADS * HEAD_DIM,
    GROUP_WIDTH,
)
D_IN = sum(IN_SIZES)

kernel_name = "hybrid_chunk_causal_parallel_heads"


def _layer_norm(x, g, b):
    xf = x.astype(jnp.float32)
    mu = xf.mean(-1, keepdims=True)
    var = jnp.square(xf - mu).mean(-1, keepdims=True)
    y = (xf - mu) * lax.rsqrt(var + LN_EPS) * g.astype(jnp.float32) + b.astype(jnp.float32)
    return y.astype(x.dtype)


def _alibi_slopes(n):
    return jnp.exp2(-8.0 * jnp.arange(1, n + 1, dtype=jnp.float32) / n)


def _swiglu(u, w13, w2):
    a, g = jnp.split(u @ w13, 2, axis=-1)
    return (jax.nn.silu(a) * g) @ w2


def _split_cols(z, sizes):
    outs, off = [], 0
    for s in sizes:
        outs.append(z[..., off:off + s])
        off += s
    return outs


def _causal_conv(x, w, b):
    k = w.shape[0]
    y = lax.conv_general_dilated(x, w.astype(x.dtype)[:, None, :], window_strides=(1,),
                                 padding=[(k - 1, 0)], dimension_numbers=('NWC', 'WIO', 'NWC'),
                                 feature_group_count=x.shape[-1])
    return y + b.astype(x.dtype)


def _sliding_window_sink_attention(q, k, v, sinks, slopes):
    f32 = jnp.float32
    bn, t, hq, dh = q.shape
    hkv = k.shape[2]
    rep = hq // hkv
    nc = t // CHUNK
    nb = SWA_WIN_CHUNKS + 1
    qc = q.astype(f32).reshape(bn, nc, CHUNK, hkv, rep, dh)

    def band(a):
        a = a.astype(f32).reshape(bn, nc, CHUNK, hkv, dh)
        a = jnp.pad(a, ((0, 0), (SWA_WIN_CHUNKS, 0), (0, 0), (0, 0), (0, 0)))
        return jnp.concatenate([a[:, i:i + nc] for i in range(nb)], axis=2)

    kb, vb = band(k), band(v)
    s = jnp.einsum('bcqgrd,bckgd->bgrcqk', qc, kb) * dh ** -0.5
    qi = jnp.arange(CHUNK)
    kj = jnp.arange(nb * CHUNK)
    dist = jnp.abs(qi[:, None] + SWA_WIN_CHUNKS * CHUNK - kj[None, :]).astype(f32)
    key_chunk = jnp.arange(nc)[:, None] - SWA_WIN_CHUNKS + kj[None, :] // CHUNK
    s = s - slopes.reshape(hkv, rep)[:, :, None, None, None] * dist
    s = jnp.where((key_chunk >= 0)[:, None, :], s, NEG_INF)
    sink = jnp.broadcast_to(sinks.astype(f32).reshape(hkv, rep, 1, 1, 1), s.shape[:-1] + (1,))
    p = jax.nn.softmax(jnp.concatenate([s, sink], axis=-1), axis=-1)[..., :-1]
    o = jnp.einsum('bgrcqk,bckgd->bcqgrd', p, vb)
    return o.reshape(bn, t, hq * dh)


def _dsa_attention(q, k, v, q_idx, k_idx, w_idx, slopes):
    f32 = jnp.float32
    bn, t, h, dh = q.shape
    topk = min(DSA_TOPK, t // 4)
    nqb = t // DSA_QBLOCK
    key_chunk = jnp.arange(t) // CHUNK
    kf, vf, kif = k.astype(f32), v.astype(f32), k_idx.astype(f32)
    idx_scale = (IDX_DIM * IDX_HEADS) ** -0.5

    def blocks(a):
        return jnp.swapaxes(a.astype(f32).reshape((bn, nqb, DSA_QBLOCK) + a.shape[2:]), 0, 1)

    qpos = jnp.arange(t).reshape(nqb, DSA_QBLOCK)

    def one_block(args):
        qb, qib, wb, pos = args
        rel = jax.nn.relu(jnp.einsum('bqhd,bsd->bqhs', qib, kif))
        score = jnp.einsum('bqhs,bqh->bqs', rel, wb) * idx_scale
        qchunk = pos // CHUNK
        score = jnp.where(key_chunk[None, :] <= qchunk[:, None], score, NEG_INF)
        _, sel = lax.top_k(score, topk)
        kg = jax.vmap(lambda a, i: a[i])(kf, sel)
        vg = jax.vmap(lambda a, i: a[i])(vf, sel)
        valid = key_chunk[sel] <= qchunk[None, :, None]
        dist = jnp.abs(pos[None, :, None] - sel).astype(f32)
        s = jnp.einsum('bqhd,bqkd->bhqk', qb, kg) * dh ** -0.5 - slopes[None, :, None, None] * dist[:, None]
        s = jnp.where(valid[:, None], s, NEG_INF)
        p = jax.nn.softmax(s, axis=-1)
        return jnp.einsum('bhqk,bqkd->bqhd', p, vg)

    o = lax.map(one_block, (blocks(q), blocks(q_idx), blocks(w_idx), qpos))
    return jnp.swapaxes(o, 0, 1).reshape(bn, t, h * dh)


def _mlstm(q, k, v, ig, fg, norm_g):
    f32 = jnp.float32
    bn, t, h, dh = q.shape
    nc = t // CHUNK
    k = k * dh ** -0.5
    lf = jax.nn.log_sigmoid(fg)

    def chunks4(a):
        return a.reshape(bn, nc, CHUNK, h, dh).transpose(1, 0, 3, 2, 4)

    def chunks3(a):
        return a.reshape(bn, nc, CHUNK, h).transpose(1, 0, 3, 2)

    causal = jnp.tril(jnp.ones((CHUNK, CHUNK), dtype=bool))

    def step(carry, xs):
        cmat, nvec, m = carry
        qj, kj, vj, igj, lfj = xs
        b = jnp.cumsum(lfj, axis=-1)
        dlog = b[..., :, None] - b[..., None, :] + igj[..., None, :]
        dlog = jnp.where(causal, dlog, -jnp.inf)
        inter = b + m[..., None]
        mj = jnp.maximum(inter, dlog.max(-1))
        dw = jnp.exp(dlog - mj[..., None])
        iw = jnp.exp(inter - mj)
        sc = jnp.einsum('bhjd,bhsd->bhjs', qj, kj) * dw
        num = iw[..., None] * jnp.einsum('bhjd,bhde->bhje', qj, cmat) + jnp.einsum('bhjs,bhse->bhje', sc, vj)
        den = iw * jnp.einsum('bhjd,bhd->bhj', qj, nvec) + sc.sum(-1)
        hj = num / jnp.maximum(jnp.abs(den), jnp.exp(-mj))[..., None]
        bl = b[..., -1]
        dec = bl[..., None] - b + igj
        m_new = jnp.maximum(bl + m, dec.max(-1))
        wc = jnp.exp(bl + m - m_new)
        ws = jnp.exp(dec - m_new[..., None])
        cmat = wc[..., None, None] * cmat + jnp.einsum('bhs,bhsd,bhse->bhde', ws, kj, vj)
        nvec = wc[..., None] * nvec + jnp.einsum('bhs,bhsd->bhd', ws, kj)
        return (cmat, nvec, m_new), hj

    init = (jnp.zeros((bn, h, dh, dh), f32), jnp.zeros((bn, h, dh), f32), jnp.zeros((bn, h), f32))
    _, hs = lax.scan(step, init, (chunks4(q), chunks4(k), chunks4(v), chunks3(ig), chunks3(lf)))
    hs = hs.transpose(1, 0, 3, 2, 4).reshape(bn, t, h, dh)
    mu = hs.mean(-1, keepdims=True)
    var = jnp.square(hs - mu).mean(-1, keepdims=True)
    hs = (hs - mu) * lax.rsqrt(var + LN_EPS) * norm_g.astype(f32).reshape(h, dh)
    return hs.reshape(bn, t, h * dh)


def _complex_affine_combine(e1, e2):
    a1r, a1i, b1r, b1i = e1
    a2r, a2i, b2r, b2i = e2
    ar = a2r * a1r - a2i * a1i
    ai = a2r * a1i + a2i * a1r
    br = a2r * b1r - a2i * b1i + b2r
    bi = a2r * b1i + a2i * b1r + b2i
    return (ar, ai, br, bi)


def _s5_glu(u, lam_re, lam_im, log_step, b_re, b_im, c_re, c_im, d_skip, glu_w, glu_b):
    f32 = jnp.float32
    bn, t, w = u.shape
    uf = u.astype(f32).reshape(bn, t, S5_GROUPS, S5_GROUP_CH)
    lr, li = lam_re.astype(f32), lam_im.astype(f32)
    dt = jnp.exp(log_step.astype(f32))[:, None]
    mag = jnp.exp(lr * dt)
    a_re, a_im = mag * jnp.cos(li * dt), mag * jnp.sin(li * dt)
    den = lr * lr + li * li
    kap_re = ((a_re - 1.0) * lr + a_im * li) / den
    kap_im = (a_im * lr - (a_re - 1.0) * li) / den
    br, bi = b_re.astype(f32), b_im.astype(f32)
    bb_re = kap_re[..., None] * br - kap_im[..., None] * bi
    bb_im = kap_re[..., None] * bi + kap_im[..., None] * br
    bu_re = jnp.einsum('btgh,gph->btgp', uf, bb_re)
    bu_im = jnp.einsum('btgh,gph->btgp', uf, bb_im)
    a_re_t = jnp.broadcast_to(a_re, (1, t) + a_re.shape)
    a_im_t = jnp.broadcast_to(a_im, (1, t) + a_im.shape)
    _, _, s_re, s_im = lax.associative_scan(_complex_affine_combine, (a_re_t, a_im_t, bu_re, bu_im), axis=1)
    y = (jnp.einsum('gop,btgp->btgo', c_re.astype(f32), s_re)
         - jnp.einsum('gop,btgp->btgo', c_im.astype(f32), s_im)
         + d_skip.astype(f32).reshape(S5_GROUPS, S5_GROUP_CH) * uf)
    y = jax.nn.gelu(y.reshape(bn, t, w))
    return y * jax.nn.sigmoid(y @ glu_w.astype(f32) + glu_b.astype(f32))


def _hybrid_mixer(u, w_in, w_out, sinks, w_uq, w_iq, conv_w, conv_b, ig_b, fg_b, mh_norm_g,
                  lam_re, lam_im, log_step, b_re, b_im, c_re, c_im, d_skip, glu_w, glu_b,
                  slopes_a, slopes_b):
    bn, t, _ = u.shape
    f32 = jnp.float32
    z = u @ w_in
    (qa, ka, va, cq, kb, vb, ki, wi, qkc, vc, igc, fgc, oc, ud) = _split_cols(z, IN_SIZES)
    o_a = _sliding_window_sink_attention(qa.reshape(bn, t, SWA_HEADS, HEAD_DIM),
                                         ka.reshape(bn, t, SWA_KV_HEADS, HEAD_DIM),
                                         va.reshape(bn, t, SWA_KV_HEADS, HEAD_DIM), sinks, slopes_a)
    qb = (cq @ w_uq).reshape(bn, t, DSA_HEADS, HEAD_DIM)
    qi = (cq @ w_iq).reshape(bn, t, IDX_HEADS, IDX_DIM)
    o_b = _dsa_attention(qb, kb, vb, qi, ki, wi, slopes_b)
    qk = jax.nn.silu(_causal_conv(qkc, conv_w, conv_b))
    qm, km = jnp.split(qk.astype(f32), 2, axis=-1)
    h_c = _mlstm(qm.reshape(bn, t, MLSTM_HEADS, HEAD_DIM), km.reshape(bn, t, MLSTM_HEADS, HEAD_DIM),
                 vc.astype(f32).reshape(bn, t, MLSTM_HEADS, HEAD_DIM),
                 (igc + ig_b).astype(f32), (fgc + fg_b).astype(f32), mh_norm_g)
    o_c = jax.nn.sigmoid(oc.astype(f32)) * h_c
    o_d = _s5_glu(ud, lam_re, lam_im, log_step, b_re, b_im, c_re, c_im, d_skip, glu_w, glu_b)
    o = jnp.concatenate([o_a, o_b, o_c, o_d], axis=-1).astype(u.dtype)
    return o @ w_out


def setup_inputs(seed: int = 0) -> dict:
    key = jax.random.key(seed)
    ks = jax.random.split(key, 28)
    f32 = jnp.float32

    def nrm(k, shape, s):
        return jax.random.normal(k, shape, f32) * s

    L = DEPTH
    return {
        "x": nrm(ks[0], (BATCH, SEQ, D_MODEL), 1.0),
        "c": nrm(ks[1], (BATCH, D_MODEL), 1.0),
        "ada_w": nrm(ks[2], (L, D_MODEL, N_SUB * 3 * D_MODEL), D_MODEL ** -0.5),
        "ada_b": nrm(ks[3], (L, N_SUB * 3 * D_MODEL), 0.02),
        "ln_g": 1.0 + nrm(ks[4], (L, N_SUB, D_MODEL), 0.02),
        "ln_b": nrm(ks[5], (L, N_SUB, D_MODEL), 0.02),
        "ffn_w13": nrm(ks[6], (L, 2, D_MODEL, 2 * D_FF), D_MODEL ** -0.5),
        "ffn_w2": nrm(ks[7], (L, 2, D_FF, D_MODEL), BETA * D_FF ** -0.5),
        "w_in": nrm(ks[8], (L, D_MODEL, D_IN), D_MODEL ** -0.5),
        "w_out": nrm(ks[9], (L, D_MIX, D_MODEL), BETA * D_MIX ** -0.5),
        "sinks": nrm(ks[10], (L, SWA_HEADS), 0.5),
        "w_uq": nrm(ks[11], (L, DSA_Q_RANK, DSA_HEADS * HEAD_DIM), DSA_Q_RANK ** -0.5),
        "w_iq": nrm(ks[12], (L, DSA_Q_RANK, IDX_HEADS * IDX_DIM), DSA_Q_RANK ** -0.5),
        "conv_w": nrm(ks[13], (L, MLSTM_CONV, 2 * MLSTM_HEADS * HEAD_DIM), MLSTM_CONV ** -0.5),
        "conv_b": nrm(ks[14], (L, 2 * MLSTM_HEADS * HEAD_DIM), 0.02),
        "ig_b": nrm(ks[15], (L, MLSTM_HEADS), 0.1),
        "fg_b": jnp.linspace(3.0, 6.0, MLSTM_HEADS, dtype=f32)[None, :] + nrm(ks[16], (L, MLSTM_HEADS), 0.1),
        "mh_norm_g": 1.0 + nrm(ks[17], (L, MLSTM_HEADS * HEAD_DIM), 0.02),
        "lam_re": -0.5 + nrm(ks[18], (L, S5_GROUPS, S5_STATE), 0.01),
        "lam_im": jnp.pi * jnp.arange(S5_STATE, dtype=f32) + nrm(ks[19], (L, S5_GROUPS, S5_STATE), 0.01),
        "log_step": jax.random.uniform(ks[20], (L, S5_GROUPS), f32, math.log(1e-3), math.log(1e-1)),
        "b_re": nrm(ks[21], (L, S5_GROUPS, S5_STATE, S5_GROUP_CH), (2 * S5_GROUP_CH) ** -0.5),
        "b_im": nrm(ks[22], (L, S5_GROUPS, S5_STATE, S5_GROUP_CH), (2 * S5_GROUP_CH) ** -0.5),
        "c_re": nrm(ks[23], (L, S5_GROUPS, S5_GROUP_CH, S5_STATE), S5_STATE ** -0.5),
        "c_im": nrm(ks[24], (L, S5_GROUPS, S5_GROUP_CH, S5_STATE), S5_STATE ** -0.5),
        "d_skip": nrm(ks[25], (L, GROUP_WIDTH), 1.0),
        "glu_w": nrm(ks[26], (L, GROUP_WIDTH, GROUP_WIDTH), GROUP_WIDTH ** -0.5),
        "glu_b": nrm(ks[27], (L, GROUP_WIDTH), 0.02),
    }


def reference(x, c, ada_w, ada_b, ln_g, ln_b, ffn_w13, ffn_w2, w_in, w_out, sinks, w_uq, w_iq,
              conv_w, conv_b, ig_b, fg_b, mh_norm_g, lam_re, lam_im, log_step, b_re, b_im,
              c_re, c_im, d_skip, glu_w, glu_b):
    bn = x.shape[0]
    slopes = _alibi_slopes(N_SOFTMAX_HEADS)
    slopes_a, slopes_b = slopes[0::2], slopes[1::2]
    cs = jax.nn.silu(c)
    for l in range(DEPTH):
        mod = (cs @ ada_w[l] + ada_b[l]).reshape(bn, N_SUB, 3, D_MODEL)
        shift, scale, gate = mod[:, :, 0, None], mod[:, :, 1, None], mod[:, :, 2, None]
        u = x * (1.0 + scale[:, 0]) + shift[:, 0]
        y = _swiglu(u, ffn_w13[l, 0], ffn_w2[l, 0])
        x = _layer_norm(ALPHA * x + 0.5 * gate[:, 0] * y, ln_g[l, 0], ln_b[l, 0])
        u = x * (1.0 + scale[:, 1]) + shift[:, 1]
        y = _hybrid_mixer(u, w_in[l], w_out[l], sinks[l], w_uq[l], w_iq[l], conv_w[l], conv_b[l],
                          ig_b[l], fg_b[l], mh_norm_g[l], lam_re[l], lam_im[l], log_step[l],
                          b_re[l], b_im[l], c_re[l], c_im[l], d_skip[l], glu_w[l], glu_b[l],
                          slopes_a, slopes_b)
        x = _layer_norm(ALPHA * x + gate[:, 1] * y, ln_g[l, 1], ln_b[l, 1])
        u = x * (1.0 + scale[:, 2]) + shift[:, 2]
        y = _swiglu(u, ffn_w13[l, 1], ffn_w2[l, 1])
        x = _layer_norm(ALPHA * x + 0.5 * gate[:, 2] * y, ln_g[l, 2], ln_b[l, 2])
    return x
```

```python
import functools
import math

import jax
import jax.numpy as jnp
from jax import lax
from jax.experimental import pallas as pl
from jax.experimental.pallas import tpu as pltpu

F32 = jnp.float32
BF16 = jnp.bfloat16
I32 = jnp.int32

D_MODEL = 1024
DEPTH = 2
CHUNK = 64
HEAD_DIM = 64
GROUP_WIDTH = 256
SWA_HEADS = 4
SWA_KV_HEADS = 2
SWA_WIN_CHUNKS = 2
DSA_HEADS = 4
DSA_Q_RANK = 128
IDX_HEADS = 4
IDX_DIM = 32
DSA_TOPK = 256
MLSTM_HEADS = 4
MLSTM_CONV = 4
S5_GROUP_CH = 16
S5_GROUPS = 16
S5_STATE = 64
S5_LANES = S5_GROUPS * S5_STATE
D_FF = 2816
N_SUB = 3
ALPHA = (2 * DEPTH) ** 0.25
LN_EPS = 1e-5
NEG_INF = -1e30
INT_MIN = -(2 ** 31)

SLOPES_A = tuple(2.0 ** -(i + 1) for i in range(0, 8, 2))
SLOPES_B = tuple(2.0 ** -(i + 1) for i in range(1, 8, 2))

VMEM_LIMIT_BYTES = 56 * 1024 * 1024

FFN_TF = 256
FFN_NCHUNK = D_FF // FFN_TF
ROW_TILE = 512
SWA_TQ = 128
DSA_TQ = 128
DSA_TK = 512
S5_TT = 512

ZC_QA, ZC_KVA, ZC_CQ, ZC_KB, ZC_VB, ZC_KI = 0, 256, 512, 640, 896, 1152
ZC_QKC, ZC_VC, ZC_OC, ZC_UD, ZC_SMALL, Z_WIDTH = 1280, 1792, 2048, 2304, 2560, 2688
SM_WI, SM_IG, SM_FG = 0, 4, 8


def _cparams(sem):
    return pltpu.CompilerParams(dimension_semantics=sem, vmem_limit_bytes=VMEM_LIMIT_BYTES)


def _dot(a, b):
    return jnp.dot(a, b, preferred_element_type=F32)


def _dot_nt(a, b):
    return lax.dot_general(a, b, (((1,), (1,)), ((), ())), preferred_element_type=F32)


def _sigmoid(x):
    return 1.0 / (1.0 + jnp.exp(-x))


def _residual_layer_norm(x, y, g, b):
    v = ALPHA * x + y
    mu = jnp.mean(v, axis=-1, keepdims=True)
    d = v - mu
    var = jnp.mean(d * d, axis=-1, keepdims=True)
    return d * lax.rsqrt(var + LN_EPS) * g + b


def _ada_kernel(c_ref, w_ref, b_ref, o_ref):
    c = c_ref[...]
    cs = c * _sigmoid(c)
    o_ref[...] = jnp.dot(cs, w_ref[...], preferred_element_type=F32,
                         precision=lax.Precision.HIGHEST) + b_ref[...]


def _ada_mod(c_pad, ada_w, ada_b):
    nl = ada_w.shape[0]
    ncol = ada_w.shape[2] // D_MODEL
    return pl.pallas_call(
        _ada_kernel,
        grid=(nl, ncol),
        in_specs=[
            pl.BlockSpec((8, D_MODEL), lambda l, j: (0, 0)),
            pl.BlockSpec((None, D_MODEL, D_MODEL), lambda l, j: (l, 0, j)),
            pl.BlockSpec((None, 1, D_MODEL), lambda l, j: (l, 0, j)),
        ],
        out_specs=pl.BlockSpec((None, 8, D_MODEL), lambda l, j: (l, 0, j)),
        out_shape=jax.ShapeDtypeStruct((nl, 8, ada_w.shape[2]), F32),
        compiler_params=_cparams(("arbitrary", "arbitrary")),
        name="ada_mod",
    )(c_pad, ada_w, ada_b.reshape(nl, 1, -1))


def _ffn_kernel(x_ref, mod_ref, w13_ref, w2_ref, lng_ref, lnb_ref, o_ref, acc_ref):
    x = x_ref[...]
    shift, scale, gate = mod_ref[0:1, :], mod_ref[1:2, :], mod_ref[2:3, :]
    u = (x * (1.0 + scale) + shift).astype(BF16)
    acc_ref[...] = jnp.zeros_like(acc_ref)

    def chunk(j, carry):
        a = _dot(u, w13_ref[0, j])
        g = _dot(u, w13_ref[1, j])
        h = (a * _sigmoid(a) * g).astype(BF16)
        acc_ref[...] += _dot(h, w2_ref[j])
        return carry

    lax.fori_loop(0, FFN_NCHUNK, chunk, 0)
    o_ref[...] = _residual_layer_norm(x, 0.5 * gate * acc_ref[...], lng_ref[...], lnb_ref[...])


def _ffn_sublayer(x2, mod, w13c, w2c, ln_g, ln_b, rows_per_batch):
    n = x2.shape[0]
    tm = ROW_TILE
    tiles_per_batch = rows_per_batch // tm
    return pl.pallas_call(
        _ffn_kernel,
        grid=(n // tm,),
        in_specs=[
            pl.BlockSpec((tm, D_MODEL), lambda i: (i, 0)),
            pl.BlockSpec((None, 8, D_MODEL), lambda i: (i // tiles_per_batch, 0, 0)),
            pl.BlockSpec(w13c.shape, lambda i: (0, 0, 0, 0), pipeline_mode=pl.Buffered(1)),
            pl.BlockSpec(w2c.shape, lambda i: (0, 0, 0), pipeline_mode=pl.Buffered(1)),
            pl.BlockSpec((1, D_MODEL), lambda i: (0, 0)),
            pl.BlockSpec((1, D_MODEL), lambda i: (0, 0)),
        ],
        out_specs=pl.BlockSpec((tm, D_MODEL), lambda i: (i, 0)),
        out_shape=jax.ShapeDtypeStruct((n, D_MODEL), F32),
        scratch_shapes=[pltpu.VMEM((tm, D_MODEL), F32)],
        compiler_params=_cparams(("arbitrary",)),
        name="ffn_sublayer",
    )(x2, mod, w13c, w2c, ln_g.reshape(1, -1), ln_b.reshape(1, -1))


def _inproj_kernel(x_ref, mod_ref, w_ref, wuq_ref, wiq_ref,
                   aq_ref, akv_ref, bq_ref, bqi_ref, bk_ref, bv_ref, bki_ref,
                   cqk_ref, cv_ref, co_ref, du_ref, sm_ref):
    x = x_ref[...]
    shift, scale = mod_ref[0:1, :], mod_ref[1:2, :]
    u = (x * (1.0 + scale) + shift).astype(BF16)
    z = _dot(u, w_ref[...])
    aq_ref[...] = z[:, ZC_QA:ZC_KVA].astype(BF16)
    akv_ref[...] = z[:, ZC_KVA:ZC_CQ].astype(BF16)
    cq = z[:, ZC_CQ:ZC_KB].astype(BF16)
    bq_ref[...] = (_dot(cq, wuq_ref[...]) * HEAD_DIM ** -0.5).astype(BF16)
    bqi_ref[...] = _dot(cq, wiq_ref[...]).astype(BF16)
    bk_ref[...] = z[:, ZC_KB:ZC_VB].astype(BF16)
    bv_ref[...] = z[:, ZC_VB:ZC_KI].astype(BF16)
    bki_ref[...] = z[:, ZC_KI:ZC_QKC].astype(BF16)
    cqk_ref[...] = z[:, ZC_QKC:ZC_VC]
    cv_ref[...] = z[:, ZC_VC:ZC_OC].astype(BF16)
    co_ref[...] = z[:, ZC_OC:ZC_UD]
    du_ref[...] = z[:, ZC_UD:ZC_SMALL]
    sm_ref[...] = z[:, ZC_SMALL:Z_WIDTH]


def _inproj(x2, mod, w_perm, w_uq, w_iq, rows_per_batch):
    n = x2.shape[0]
    tm = ROW_TILE
    tiles_per_batch = rows_per_batch // tm
    widths = [(256, BF16), (256, BF16), (256, BF16), (128, BF16), (256, BF16), (256, BF16), (128, BF16),
              (512, F32), (256, BF16), (256, F32), (256, F32), (128, F32)]
    return pl.pallas_call(
        _inproj_kernel,
        grid=(n // tm,),
        in_specs=[
            pl.BlockSpec((tm, D_MODEL), lambda i: (i, 0)),
            pl.BlockSpec((None, 8, D_MODEL), lambda i: (i // tiles_per_batch, 0, 0)),
            pl.BlockSpec(w_perm.shape, lambda i: (0, 0), pipeline_mode=pl.Buffered(1)),
            pl.BlockSpec(w_uq.shape, lambda i: (0, 0), pipeline_mode=pl.Buffered(1)),
            pl.BlockSpec(w_iq.shape, lambda i: (0, 0), pipeline_mode=pl.Buffered(1)),
        ],
        out_specs=[pl.BlockSpec((tm, w), lambda i: (i, 0)) for w, _ in widths],
        out_shape=[jax.ShapeDtypeStruct((n, w), dt) for w, dt in widths],
        compiler_params=_cparams(("arbitrary",)),
        name="mixer_inproj",
    )(x2, mod, w_perm, w_uq, w_iq)


def _swa_kernel(sink_ref, q_ref, kvc_ref, kvp_ref, o_ref):
    i = pl.program_id(1)
    tq = SWA_TQ
    q = q_ref[...]
    kv = jnp.concatenate([kvp_ref[...], kvc_ref[...]], axis=0)
    qpos = i * tq + lax.broadcasted_iota(I32, (tq, 2 * tq), 0)
    kpos = (i - 1) * tq + lax.broadcasted_iota(I32, (tq, 2 * tq), 1)
    qchunk = qpos // CHUNK
    kchunk = (kpos + tq) // CHUNK - tq // CHUNK
    valid = (kpos >= 0) & (kchunk <= qchunk) & (kchunk >= qchunk - SWA_WIN_CHUNKS)
    dist = jnp.abs(qpos - kpos).astype(F32)
    rep = SWA_HEADS // SWA_KV_HEADS
    for h in range(SWA_HEADS):
        g = h // rep
        qh = q[:, h * HEAD_DIM:(h + 1) * HEAD_DIM]
        kg = kv[:, g * HEAD_DIM:(g + 1) * HEAD_DIM]
        vg = kv[:, (SWA_KV_HEADS + g) * HEAD_DIM:(SWA_KV_HEADS + g + 1) * HEAD_DIM]
        s = _dot_nt(qh, kg) * HEAD_DIM ** -0.5 - SLOPES_A[h] * dist
        s = jnp.where(valid, s, NEG_INF)
        sink = sink_ref[h]
        m = jnp.maximum(jnp.max(s, axis=-1, keepdims=True), sink)
        p = jnp.exp(s - m)
        denom = jnp.sum(p, axis=-1, keepdims=True) + jnp.exp(sink - m)
        o = _dot(p.astype(BF16), vg) / denom
        o_ref[:, h * HEAD_DIM:(h + 1) * HEAD_DIM] = o.astype(BF16)


def _swa(sinks, a_q, a_kv, bn, t):
    nt = t // SWA_TQ
    return pl.pallas_call(
        _swa_kernel,
        grid=(bn, nt),
        in_specs=[
            pl.BlockSpec(memory_space=pltpu.SMEM),
            pl.BlockSpec((SWA_TQ, 256), lambda b, i: (b * nt + i, 0)),
            pl.BlockSpec((SWA_TQ, 256), lambda b, i: (b * nt + i, 0)),
            pl.BlockSpec((SWA_TQ, 256), lambda b, i: (b * nt + jnp.maximum(i - 1, 0), 0)),
        ],
        out_specs=pl.BlockSpec((SWA_TQ, 256), lambda b, i: (b * nt + i, 0)),
        out_shape=jax.ShapeDtypeStruct((bn * t, 256), BF16),
        compiler_params=_cparams(("arbitrary", "arbitrary")),
        name="swa_attention",
    )(sinks, a_q, a_kv, a_kv)


def _sortable_key(x):
    bits = lax.bitcast_convert_type(x, I32)
    return bits ^ ((bits >> 31) & 0x7FFFFFFF)


def _dsa_kernel(q_ref, qi_ref, sm_ref, k_ref, v_ref, ki_ref, o_ref, key_ref, jsel_ref, *, seq):
    i = pl.program_id(1)
    tq, tk = DSA_TQ, DSA_TK
    nblk = (i * tq + tq + tk - 1) // tk
    qpos = i * tq + lax.broadcasted_iota(I32, (tq, 1), 0)
    qchunk = qpos // CHUNK
    lane_k = lax.broadcasted_iota(I32, (1, tk), 1)

    qi = qi_ref[...]
    lane_i = lax.broadcasted_iota(I32, (1, IDX_HEADS * IDX_DIM), 1) // IDX_DIM
    qi_stack = jnp.concatenate([jnp.where(lane_i == h, qi, jnp.zeros_like(qi)) for h in range(IDX_HEADS)], axis=0)
    sm = sm_ref[...]
    w_idx = [sm[:, SM_WI + h:SM_WI + h + 1] for h in range(IDX_HEADS)]
    idx_scale = (IDX_DIM * IDX_HEADS) ** -0.5

    def score_block(j, carry):
        kib = ki_ref[pl.ds(pl.multiple_of(j * tk, tk), tk), :]
        d = _dot_nt(qi_stack, kib)
        acc = w_idx[0] * jnp.maximum(d[0:tq], 0.0)
        for h in range(1, IDX_HEADS):
            acc = acc + w_idx[h] * jnp.maximum(d[h * tq:(h + 1) * tq], 0.0)
        sc = acc * idx_scale + 0.0
        kpos = j * tk + lane_k
        sc = jnp.where(kpos // CHUNK <= qchunk, sc, NEG_INF)
        key_ref[j] = _sortable_key(sc)
        return carry

    lax.fori_loop(0, nblk, score_block, 0)

    k_eff = jnp.minimum(DSA_TOPK, (qchunk + 1) * CHUNK)

    def count(pred):
        def blk(j, c):
            m = pred(key_ref[j], j * tk + lane_k).astype(I32)
            for g in range(tk // 128):
                c = c + m[:, g * 128:(g + 1) * 128]
            return c
        c = lax.fori_loop(0, nblk, blk, jnp.zeros((tq, 128), I32))
        return jnp.sum(c, axis=-1, keepdims=True)

    def bit_step(bi, prefix):
        cand_u = prefix | (jnp.int32(1) << (31 - bi))
        cand_s = cand_u ^ INT_MIN
        cnt = count(lambda key, kpos: key >= cand_s)
        return jnp.where(cnt >= k_eff, cand_u, prefix)

    prefix = lax.fori_loop(0, 32, bit_step, jnp.zeros((tq, 1), I32))
    thr = prefix ^ INT_MIN

    c_gt = count(lambda key, kpos: key > thr)
    c_eq = count(lambda key, kpos: key == thr)
    need = k_eff - c_gt
    excess = c_eq > need
    jsel_ref[...] = jnp.full((tq, 128), seq, I32)

    @pl.when(jnp.max(excess.astype(I32)) > 0)
    def _():
        nbits = max(1, (seq - 1).bit_length())

        def idx_step(bi, x):
            cand = x | (jnp.int32(1) << (nbits - 1 - bi))
            cnt = count(lambda key, kpos: (key == thr) & (kpos < cand))
            return jnp.where(cnt <= need - 1, cand, x)

        x = lax.fori_loop(0, nbits, idx_step, jnp.zeros((tq, 1), I32))
        jsel_ref[...] = jnp.broadcast_to(jnp.where(excess, x, seq), (tq, 128))

    jsel = jsel_ref[:, 0:1]

    q = q_ref[...]
    lane_h = lax.broadcasted_iota(I32, (1, DSA_HEADS * HEAD_DIM), 1) // HEAD_DIM
    q_stack = jnp.concatenate([jnp.where(lane_h == h, q, jnp.zeros_like(q)) for h in range(DSA_HEADS)], axis=0)

    def att_block(j, carry):
        m_run, l_run, acc = carry
        start = pl.multiple_of(j * tk, tk)
        kb = k_ref[pl.ds(start, tk), :]
        vb = v_ref[pl.ds(start, tk), :]
        s = _dot_nt(q_stack, kb)
        key = key_ref[j]
        kpos = j * tk + lane_k
        sel = (key > thr) | ((key == thr) & (kpos <= jsel))
        dist = jnp.abs(qpos - kpos).astype(F32)
        ms, ls, ps, alphas = [], [], [], []
        for h in range(DSA_HEADS):
            sh = jnp.where(sel, s[h * tq:(h + 1) * tq] - SLOPES_B[h] * dist, NEG_INF)
            m_old = m_run[h * tq:(h + 1) * tq]
            m_new = jnp.maximum(m_old, jnp.max(sh, axis=-1, keepdims=True))
            alpha = jnp.exp(m_old - m_new)
            p = jnp.where(sel, jnp.exp(sh - m_new), 0.0)
            ms.append(m_new)
            ls.append(alpha * l_run[h * tq:(h + 1) * tq] + jnp.sum(p, axis=-1, keepdims=True))
            ps.append(p.astype(BF16))
            alphas.append(alpha)
        pv = _dot(jnp.concatenate(ps, axis=0), vb)
        acc = jnp.concatenate(alphas, axis=0) * acc + pv
        return jnp.concatenate(ms, axis=0), jnp.concatenate(ls, axis=0), acc

    init = (jnp.full((DSA_HEADS * tq, 1), NEG_INF, F32), jnp.zeros((DSA_HEADS * tq, 1), F32),
            jnp.zeros((DSA_HEADS * tq, DSA_HEADS * HEAD_DIM), F32))
    _, l_run, acc = lax.fori_loop(0, nblk, att_block, init)
    out = acc / l_run
    o = jnp.where(lane_h == 0, out[0:tq], 0.0)
    for h in range(1, DSA_HEADS):
        o = o + jnp.where(lane_h == h, out[h * tq:(h + 1) * tq], 0.0)
    o_ref[...] = o.astype(BF16)


def _dsa(b_q, b_qi, small, b_k, b_v, b_ki, bn, t):
    nt = t // DSA_TQ
    return pl.pallas_call(
        functools.partial(_dsa_kernel, seq=t),
        grid=(bn, nt),
        in_specs=[
            pl.BlockSpec((DSA_TQ, 256), lambda b, i: (b * nt + i, 0)),
            pl.BlockSpec((DSA_TQ, 128), lambda b, i: (b * nt + i, 0)),
            pl.BlockSpec((DSA_TQ, 128), lambda b, i: (b * nt + i, 0)),
            pl.BlockSpec((t, 256), lambda b, i: (b, 0)),
            pl.BlockSpec((t, 256), lambda b, i: (b, 0)),
            pl.BlockSpec((t, 128), lambda b, i: (b, 0)),
        ],
        out_specs=pl.BlockSpec((DSA_TQ, 256), lambda b, i: (b * nt + i, 0)),
        out_shape=jax.ShapeDtypeStruct((bn * t, 256), BF16),
        scratch_shapes=[pltpu.VMEM((t // DSA_TK, DSA_TQ, DSA_TK), I32), pltpu.VMEM((DSA_TQ, 128), I32)],
        compiler_params=_cparams(("arbitrary", "arbitrary")),
        name="dsa_attention",
    )(b_q, b_qi, small, b_k, b_v, b_ki)


def _mlstm_kernel(qk_ref, v_ref, og_ref, sm_ref, convw_ref, convb_ref, gbias_ref, normg_ref,
                  o_ref, tail_ref, cmat_ref, nvec_ref, mst_ref, *, bn):
    c = pl.program_id(0)
    L = CHUNK

    @pl.when(c == 0)
    def _():
        tail_ref[...] = jnp.zeros_like(tail_ref)
        cmat_ref[...] = jnp.zeros_like(cmat_ref)
        nvec_ref[...] = jnp.zeros_like(nvec_ref)
        mst_ref[...] = jnp.zeros_like(mst_ref)

    row = lax.broadcasted_iota(I32, (L, L), 0)
    col = lax.broadcasted_iota(I32, (L, L), 1)
    causal = col <= row
    tril = causal.astype(F32)
    convw = convw_ref[...]
    for b in range(bn):
        cur = qk_ref[b]
        ext = jnp.concatenate([tail_ref[b], cur], axis=0)
        tail_ref[b] = cur[L - 8:L, :]
        y = convb_ref[...] + convw[MLSTM_CONV - 1:MLSTM_CONV, :] * cur
        for k in range(MLSTM_CONV - 1):
            off = 8 - (MLSTM_CONV - 1) + k
            y = y + convw[k:k + 1, :] * ext[off:off + L, :]
        qk = y * _sigmoid(y)
        gates = sm_ref[b] + gbias_ref[...]
        lf = jnp.minimum(gates, 0.0) - jnp.log(1.0 + jnp.exp(-jnp.abs(gates)))
        bcum = jnp.dot(tril, lf, preferred_element_type=F32, precision=lax.Precision.HIGHEST)
        gates_t = gates.T
        bcum_t = bcum.T
        vb = v_ref[b]
        og = og_ref[b]
        for h in range(MLSTM_HEADS):
            s_idx = b * MLSTM_HEADS + h
            q = qk[:, h * HEAD_DIM:(h + 1) * HEAD_DIM]
            k = qk[:, (MLSTM_HEADS + h) * HEAD_DIM:(MLSTM_HEADS + h + 1) * HEAD_DIM] * HEAD_DIM ** -0.5
            v = vb[:, h * HEAD_DIM:(h + 1) * HEAD_DIM]
            qb, kb = q.astype(BF16), k.astype(BF16)
            b_col = bcum[:, SM_FG + h:SM_FG + h + 1]
            b_row = bcum_t[SM_FG + h:SM_FG + h + 1, :]
            ig_col = gates[:, SM_IG + h:SM_IG + h + 1]
            ig_row = gates_t[SM_IG + h:SM_IG + h + 1, :]
            m_prev = mst_ref[s_idx]
            cmat = cmat_ref[s_idx]
            nvec = nvec_ref[s_idx]
            dlog = jnp.where(causal, b_col - b_row + ig_row, NEG_INF)
            inter = b_col + m_prev
            mj = jnp.maximum(inter, jnp.max(dlog, axis=-1, keepdims=True))
            dw = jnp.exp(dlog - mj)
            iw = jnp.exp(inter - mj)
            sc = _dot_nt(qb, kb) * dw
            num = iw * _dot(qb, cmat.astype(BF16)) + _dot(sc.astype(BF16), v)
            den = iw * jnp.sum(q * nvec, axis=-1, keepdims=True) + jnp.sum(sc, axis=-1, keepdims=True)
            hj = num / jnp.maximum(jnp.abs(den), jnp.exp(-mj))
            bl = b_col[L - 1:L, :]
            dec_row = bl - b_row + ig_row
            dec_col = bl - b_col + ig_col
            m_new = jnp.maximum(bl + m_prev, jnp.max(dec_row, axis=-1, keepdims=True))
            wc = jnp.exp(bl + m_prev - m_new)
            ws = jnp.exp(dec_col - m_new)
            kw = k * ws
            cmat_ref[s_idx] = wc * cmat + _dot(kw.T.astype(BF16), v)
            nvec_ref[s_idx] = wc * nvec + jnp.sum(kw, axis=0, keepdims=True)
            mst_ref[s_idx] = m_new
            mu = jnp.mean(hj, axis=-1, keepdims=True)
            dh = hj - mu
            var = jnp.mean(dh * dh, axis=-1, keepdims=True)
            hn = dh * lax.rsqrt(var + LN_EPS) * normg_ref[:, h * HEAD_DIM:(h + 1) * HEAD_DIM]
            o = _sigmoid(og[:, h * HEAD_DIM:(h + 1) * HEAD_DIM]) * hn
            o_ref[b, :, h * HEAD_DIM:(h + 1) * HEAD_DIM] = o.astype(BF16)


def _mlstm(c_qk, c_v, c_o, small, conv_w, conv_b, gate_bias, norm_g, bn, t):
    nc = t // CHUNK
    nstate = bn * MLSTM_HEADS
    blk = lambda w: pl.BlockSpec((bn, CHUNK, w), lambda c: (0, c, 0))
    full = lambda a: pl.BlockSpec(a.shape, lambda c: (0,) * a.ndim)
    return pl.pallas_call(
        functools.partial(_mlstm_kernel, bn=bn),
        grid=(nc,),
        in_specs=[blk(512), blk(256), blk(256), blk(128), full(conv_w), full(conv_b), full(gate_bias), full(norm_g)],
        out_specs=blk(256),
        out_shape=jax.ShapeDtypeStruct((bn, t, 256), BF16),
        scratch_shapes=[pltpu.VMEM((bn, 8, 512), F32), pltpu.VMEM((nstate, HEAD_DIM, HEAD_DIM), F32),
                        pltpu.VMEM((nstate, 1, HEAD_DIM), F32), pltpu.VMEM((nstate, 1, 1), F32)],
        compiler_params=_cparams(("arbitrary",)),
        name="mlstm",
    )(c_qk.reshape(bn, t, 512), c_v.reshape(bn, t, 256), c_o.reshape(bn, t, 256), small.reshape(bn, t, 128),
      conv_w, conv_b, gate_bias, norm_g)


def _s5_kernel(u_ref, bbre_ref, bbim_ref, are_ref, aim_ref, cre_ref, cim_ref, dskip_ref, gluw_ref, glub_ref,
               o_ref, sre_ref, sim_ref, stre_ref, stim_ref):
    @pl.when(pl.program_id(1) == 0)
    def _():
        stre_ref[...] = jnp.zeros_like(stre_ref)
        stim_ref[...] = jnp.zeros_like(stim_ref)

    u = u_ref[...]
    ub = u.astype(BF16)
    sre_ref[...] = _dot(ub, bbre_ref[...])
    sim_ref[...] = _dot(ub, bbim_ref[...])
    a_re = jnp.broadcast_to(are_ref[...], (8, S5_LANES))
    a_im = jnp.broadcast_to(aim_ref[...], (8, S5_LANES))
    row = lax.broadcasted_iota(I32, (8, S5_LANES), 0)

    def group(i, carry):
        s_re, s_im = carry
        base = pl.multiple_of(i * 8, 8)
        bu_re = sre_ref[pl.ds(base, 8), :]
        bu_im = sim_ref[pl.ds(base, 8), :]
        out_re, out_im = bu_re, bu_im
        for r in range(8):
            n_re = a_re * s_re - a_im * s_im + bu_re
            n_im = a_re * s_im + a_im * s_re + bu_im
            out_re = jnp.where(row == r, n_re, out_re)
            out_im = jnp.where(row == r, n_im, out_im)
            s_re = jnp.broadcast_to(n_re[r:r + 1, :], (8, S5_LANES))
            s_im = jnp.broadcast_to(n_im[r:r + 1, :], (8, S5_LANES))
        sre_ref[pl.ds(base, 8), :] = out_re
        sim_ref[pl.ds(base, 8), :] = out_im
        return s_re, s_im

    s_re, s_im = lax.fori_loop(0, S5_TT // 8, group, (stre_ref[...], stim_ref[...]))
    stre_ref[...] = s_re
    stim_ref[...] = s_im
    y = (_dot(sre_ref[...].astype(BF16), cre_ref[...]) - _dot(sim_ref[...].astype(BF16), cim_ref[...])
         + dskip_ref[...] * u)
    y = 0.5 * y * (1.0 + jnp.tanh(math.sqrt(2.0 / math.pi) * (y + 0.044715 * (y * y * y))))
    z = _dot(y.astype(BF16), gluw_ref[...]) + glub_ref[...]
    o_ref[...] = (y * _sigmoid(z)).astype(BF16)


def _s5(d_u, bb_re, bb_im, a_re, a_im, c_re_t, c_im_t, d_skip, glu_w, glu_b, bn, t):
    nt = t // S5_TT
    full = lambda a: pl.BlockSpec(a.shape, lambda b, i: (0,) * a.ndim)
    args = (bb_re, bb_im, a_re, a_im, c_re_t, c_im_t, d_skip, glu_w, glu_b)
    return pl.pallas_call(
        _s5_kernel,
        grid=(bn, nt),
        in_specs=[pl.BlockSpec((S5_TT, 256), lambda b, i: (b * nt + i, 0))] + [full(a) for a in args],
        out_specs=pl.BlockSpec((S5_TT, 256), lambda b, i: (b * nt + i, 0)),
        out_shape=jax.ShapeDtypeStruct((bn * t, 256), BF16),
        scratch_shapes=[pltpu.VMEM((S5_TT, S5_LANES), F32), pltpu.VMEM((S5_TT, S5_LANES), F32),
                        pltpu.VMEM((8, S5_LANES), F32), pltpu.VMEM((8, S5_LANES), F32)],
        compiler_params=_cparams(("arbitrary", "arbitrary")),
        name="s5_glu",
    )(d_u, *args)


def _outproj_kernel(x_ref, mod_ref, oa_ref, ob_ref, oc_ref, od_ref, w_ref, lng_ref, lnb_ref, o_ref):
    x = x_ref[...]
    gate = mod_ref[2:3, :]
    y = _dot(oa_ref[...], w_ref[0])
    y = y + _dot(ob_ref[...], w_ref[1])
    y = y + _dot(oc_ref[...], w_ref[2])
    y = y + _dot(od_ref[...], w_ref[3])
    o_ref[...] = _residual_layer_norm(x, gate * y, lng_ref[...], lnb_ref[...])


def _outproj(x2, mod, o_a, o_b, o_c, o_d, w_out4, ln_g, ln_b, rows_per_batch):
    n = x2.shape[0]
    tm = ROW_TILE
    tiles_per_batch = rows_per_batch // tm
    mix = pl.BlockSpec((tm, GROUP_WIDTH), lambda i: (i, 0))
    return pl.pallas_call(
        _outproj_kernel,
        grid=(n // tm,),
        in_specs=[
            pl.BlockSpec((tm, D_MODEL), lambda i: (i, 0)),
            pl.BlockSpec((None, 8, D_MODEL), lambda i: (i // tiles_per_batch, 0, 0)),
            mix, mix, mix, mix,
            pl.BlockSpec(w_out4.shape, lambda i: (0, 0, 0), pipeline_mode=pl.Buffered(1)),
            pl.BlockSpec((1, D_MODEL), lambda i: (0, 0)),
            pl.BlockSpec((1, D_MODEL), lambda i: (0, 0)),
        ],
        out_specs=pl.BlockSpec((tm, D_MODEL), lambda i: (i, 0)),
        out_shape=jax.ShapeDtypeStruct((n, D_MODEL), F32),
        compiler_params=_cparams(("arbitrary",)),
        name="mixer_outproj",
    )(x2, mod, o_a, o_b, o_c, o_d, w_out4, ln_g.reshape(1, -1), ln_b.reshape(1, -1))


def _permute_w_in(w_in):
    off = {}
    o = 0
    for name, s in (("qa", 256), ("ka", 128), ("va", 128), ("cq", 128), ("kb", 64), ("vb", 64), ("ki", 32),
                    ("wi", 4), ("qkc", 512), ("vc", 256), ("ig", 4), ("fg", 4), ("oc", 256), ("ud", 256)):
        off[name] = (o, o + s)
        o += s
    col = lambda n: w_in[:, off[n][0]:off[n][1]]
    small = jnp.concatenate([col("wi"), col("ig"), col("fg"),
                             jnp.zeros((w_in.shape[0], 128 - 12), w_in.dtype)], axis=1)
    parts = [col("qa"), col("ka"), col("va"), col("cq")] + [col("kb")] * 4 + [col("vb")] * 4 + [col("ki")] * 4 + [
        col("qkc"), col("vc"), col("oc"), col("ud"), small]
    return jnp.concatenate(parts, axis=1).astype(BF16)


def _s5_params(lam_re, lam_im, log_step, b_re, b_im, c_re, c_im):
    dt = jnp.exp(log_step)[:, None]
    mag = jnp.exp(lam_re * dt)
    a_re, a_im = mag * jnp.cos(lam_im * dt), mag * jnp.sin(lam_im * dt)
    den = lam_re * lam_re + lam_im * lam_im
    kap_re = ((a_re - 1.0) * lam_re + a_im * lam_im) / den
    kap_im = (a_im * lam_re - (a_re - 1.0) * lam_im) / den
    bb_re = kap_re[..., None] * b_re - kap_im[..., None] * b_im
    bb_im = kap_re[..., None] * b_im + kap_im[..., None] * b_re
    eye = jnp.eye(S5_GROUPS, dtype=F32)

    def in_mat(bb):
        return jnp.einsum("gph,gk->ghkp", bb, eye).reshape(S5_GROUPS * S5_GROUP_CH, S5_LANES).astype(BF16)

    def out_mat(cc):
        return jnp.einsum("gop,gk->gpko", cc, eye).reshape(S5_LANES, S5_GROUPS * S5_GROUP_CH).astype(BF16)

    return (in_mat(bb_re), in_mat(bb_im), a_re.reshape(1, S5_LANES), a_im.reshape(1, S5_LANES),
            out_mat(c_re), out_mat(c_im))


def kernel(x, c, ada_w, ada_b, ln_g, ln_b, ffn_w13, ffn_w2, w_in, w_out, sinks, w_uq, w_iq, conv_w, conv_b, ig_b,
           fg_b, mh_norm_g, lam_re, lam_im, log_step, b_re, b_im, c_re, c_im, d_skip, glu_w, glu_b):
    bn, t, d = x.shape
    assert d == D_MODEL and t % max(ROW_TILE, DSA_TK, S5_TT) == 0 and bn <= 8
    n = bn * t
    nl = ada_w.shape[0]
    c_pad = jnp.zeros((8, d), F32).at[:bn].set(c)
    mod_all = _ada_mod(c_pad, ada_w, ada_b)
    mod_all = mod_all[:, :bn].reshape(nl, bn, N_SUB, 3, d).transpose(0, 2, 1, 3, 4)
    mod_all = jnp.pad(mod_all, ((0, 0), (0, 0), (0, 0), (0, 5), (0, 0)))

    x2 = x.reshape(n, d)
    for l in range(nl):
        w13c = ffn_w13[l].astype(BF16).reshape(2, d, 2, FFN_NCHUNK, FFN_TF).transpose(0, 2, 3, 1, 4)
        w2c = ffn_w2[l].astype(BF16).reshape(2, FFN_NCHUNK, FFN_TF, d)
        x2 = _ffn_sublayer(x2, mod_all[l, 0], w13c[0], w2c[0], ln_g[l, 0], ln_b[l, 0], t)
        (a_q, a_kv, b_q, b_qi, b_k, b_v, b_ki, c_qk, c_v, c_o, d_u, small) = _inproj(
            x2, mod_all[l, 1], _permute_w_in(w_in[l]), w_uq[l].astype(BF16), w_iq[l].astype(BF16), t)
        o_a = _swa(sinks[l], a_q, a_kv, bn, t)
        o_b = _dsa(b_q, b_qi, small, b_k, b_v, b_ki, bn, t)
        gate_bias = jnp.zeros((1, 128), F32).at[0, SM_IG:SM_IG + 4].set(ig_b[l]).at[0, SM_FG:SM_FG + 4].set(fg_b[l])
        o_c = _mlstm(c_qk, c_v, c_o, small, conv_w[l], conv_b[l].reshape(1, -1), gate_bias,
                     mh_norm_g[l].reshape(1, -1), bn, t).reshape(n, GROUP_WIDTH)
        s5p = _s5_params(lam_re[l], lam_im[l], log_step[l], b_re[l], b_im[l], c_re[l], c_im[l])
        o_d = _s5(d_u, *s5p, d_skip[l].reshape(1, -1), glu_w[l].astype(BF16), glu_b[l].reshape(1, -1), bn, t)
        x2 = _outproj(x2, mod_all[l, 1], o_a, o_b, o_c, o_d,
                      w_out[l].astype(BF16).reshape(4, GROUP_WIDTH, d), ln_g[l, 1], ln_b[l, 1], t)
        x2 = _ffn_sublayer(x2, mod_all[l, 2], w13c[1], w2c[1], ln_g[l, 2], ln_b[l, 2], t)
    return x2.reshape(bn, t, d)
```

```python
import functools
import math

import jax
import jax.numpy as jnp
from jax import lax
from jax.experimental import pallas as pl
from jax.experimental.pallas import tpu as pltpu

F32 = jnp.float32
BF16 = jnp.bfloat16
I32 = jnp.int32

D_MODEL = 1024
DEPTH = 2
CHUNK = 64
HEAD_DIM = 64
GROUP_WIDTH = 256
SWA_HEADS = 4
SWA_KV_HEADS = 2
SWA_WIN_CHUNKS = 2
DSA_HEADS = 4
DSA_Q_RANK = 128
IDX_HEADS = 4
IDX_DIM = 32
DSA_TOPK = 256
MLSTM_HEADS = 4
MLSTM_CONV = 4
S5_GROUP_CH = 16
S5_GROUPS = 16
S5_STATE = 64
S5_LANES = S5_GROUPS * S5_STATE
D_FF = 2816
N_SUB = 3
ALPHA = (2 * DEPTH) ** 0.25
LN_EPS = 1e-5
NEG_INF = -1e30
INT_MIN = -(2 ** 31)

SLOPES_A = tuple(2.0 ** -(i + 1) for i in range(0, 8, 2))
SLOPES_B = tuple(2.0 ** -(i + 1) for i in range(1, 8, 2))

VMEM_LIMIT_BYTES = 56 * 1024 * 1024

FFN_TF = 256
FFN_NCHUNK = D_FF // FFN_TF
ROW_TILE = 512
SWA_TQ = 128
DSA_TQ = 128
DSA_TK = 512
S5_TT = 512

ZC_QA, ZC_KVA, ZC_CQ, ZC_KVB, ZC_KI = 0, 256, 512, 640, 768
ZC_QKC, ZC_VC, ZC_OC, ZC_UD, ZC_SMALL, Z_WIDTH = 896, 1408, 1664, 1920, 2176, 2304
SM_WI, SM_IG, SM_FG = 0, 4, 8


def _cparams(sem):
    return pltpu.CompilerParams(dimension_semantics=sem, vmem_limit_bytes=VMEM_LIMIT_BYTES)


def _dot(a, b):
    return jnp.dot(a, b, preferred_element_type=F32)


def _dot_nt(a, b):
    return lax.dot_general(a, b, (((1,), (1,)), ((), ())), preferred_element_type=F32)


def _sigmoid(x):
    return 1.0 / (1.0 + jnp.exp(-x))


def _residual_layer_norm(x, y, g, b):
    v = ALPHA * x + y
    mu = jnp.mean(v, axis=-1, keepdims=True)
    d = v - mu
    var = jnp.mean(d * d, axis=-1, keepdims=True)
    return d * lax.rsqrt(var + LN_EPS) * g + b


def _ada_kernel(c_ref, w_ref, b_ref, o_ref):
    c = c_ref[...]
    cs = c * _sigmoid(c)
    o_ref[...] = jnp.dot(cs, w_ref[...], preferred_element_type=F32,
                         precision=lax.Precision.HIGHEST) + b_ref[...]


def _ada_mod(c_pad, ada_w, ada_b):
    nl = ada_w.shape[0]
    ncol = ada_w.shape[2] // D_MODEL
    return pl.pallas_call(
        _ada_kernel,
        grid=(nl, ncol),
        in_specs=[
            pl.BlockSpec((8, D_MODEL), lambda l, j: (0, 0)),
            pl.BlockSpec((None, D_MODEL, D_MODEL), lambda l, j: (l, 0, j)),
            pl.BlockSpec((None, 1, D_MODEL), lambda l, j: (l, 0, j)),
        ],
        out_specs=pl.BlockSpec((None, 8, D_MODEL), lambda l, j: (l, 0, j)),
        out_shape=jax.ShapeDtypeStruct((nl, 8, ada_w.shape[2]), F32),
        compiler_params=_cparams(("arbitrary", "arbitrary")),
        name="ada_mod",
    )(c_pad, ada_w, ada_b.reshape(nl, 1, -1))


def _ffn_kernel(x_ref, mod_ref, w13_ref, w2_ref, lng_ref, lnb_ref, o_ref, acc_ref):
    x = x_ref[...]
    shift, scale, gate = mod_ref[0:1, :], mod_ref[1:2, :], mod_ref[2:3, :]
    u = (x * (1.0 + scale) + shift).astype(BF16)
    acc_ref[...] = jnp.zeros_like(acc_ref)

    def chunk(j, carry):
        a = _dot(u, w13_ref[0, j])
        g = _dot(u, w13_ref[1, j])
        h = (a * _sigmoid(a) * g).astype(BF16)
        acc_ref[...] += _dot(h, w2_ref[j])
        return carry

    lax.fori_loop(0, FFN_NCHUNK, chunk, 0)
    o_ref[...] = _residual_layer_norm(x, 0.5 * gate * acc_ref[...], lng_ref[...], lnb_ref[...])


def _ffn_sublayer(x2, mod, w13c, w2c, ln_g, ln_b, rows_per_batch):
    n = x2.shape[0]
    tm = ROW_TILE
    tiles_per_batch = rows_per_batch // tm
    return pl.pallas_call(
        _ffn_kernel,
        grid=(n // tm,),
        in_specs=[
            pl.BlockSpec((tm, D_MODEL), lambda i: (i, 0)),
            pl.BlockSpec((None, 8, D_MODEL), lambda i: (i // tiles_per_batch, 0, 0)),
            pl.BlockSpec(w13c.shape, lambda i: (0, 0, 0, 0), pipeline_mode=pl.Buffered(1)),
            pl.BlockSpec(w2c.shape, lambda i: (0, 0, 0), pipeline_mode=pl.Buffered(1)),
            pl.BlockSpec((1, D_MODEL), lambda i: (0, 0)),
            pl.BlockSpec((1, D_MODEL), lambda i: (0, 0)),
        ],
        out_specs=pl.BlockSpec((tm, D_MODEL), lambda i: (i, 0)),
        out_shape=jax.ShapeDtypeStruct((n, D_MODEL), F32),
        scratch_shapes=[pltpu.VMEM((tm, D_MODEL), F32)],
        compiler_params=_cparams(("arbitrary",)),
        name="ffn_sublayer",
    )(x2, mod, w13c, w2c, ln_g.reshape(1, -1), ln_b.reshape(1, -1))


def _inproj_kernel(x_ref, mod_ref, w_ref, wuq_ref, wiq_ref,
                   aq_ref, akv_ref, bq_ref, bqi_ref, bkv_ref, bki_ref,
                   cqk_ref, cv_ref, co_ref, du_ref, sm_ref):
    x = x_ref[...]
    shift, scale = mod_ref[0:1, :], mod_ref[1:2, :]
    u = (x * (1.0 + scale) + shift).astype(BF16)
    z = _dot(u, w_ref[...])
    aq_ref[...] = z[:, ZC_QA:ZC_KVA].astype(BF16)
    akv_ref[...] = z[:, ZC_KVA:ZC_CQ].astype(BF16)
    cq = z[:, ZC_CQ:ZC_KVB].astype(BF16)
    bq_ref[...] = (_dot(cq, wuq_ref[...]) * HEAD_DIM ** -0.5).astype(BF16)
    bqi_ref[...] = _dot(cq, wiq_ref[...]).astype(BF16)
    bkv_ref[...] = z[:, ZC_KVB:ZC_KI].astype(BF16)
    bki_ref[...] = z[:, ZC_KI:ZC_QKC].astype(BF16)
    cqk_ref[...] = z[:, ZC_QKC:ZC_VC]
    cv_ref[...] = z[:, ZC_VC:ZC_OC].astype(BF16)
    co_ref[...] = z[:, ZC_OC:ZC_UD]
    du_ref[...] = z[:, ZC_UD:ZC_SMALL]
    sm_ref[...] = z[:, ZC_SMALL:Z_WIDTH]


def _inproj(x2, mod, w_perm, w_uq, w_iq, rows_per_batch):
    n = x2.shape[0]
    tm = ROW_TILE
    tiles_per_batch = rows_per_batch // tm
    widths = [(256, BF16), (256, BF16), (256, BF16), (128, BF16), (128, BF16), (128, BF16),
              (512, F32), (256, BF16), (256, F32), (256, F32), (128, F32)]
    return pl.pallas_call(
        _inproj_kernel,
        grid=(n // tm,),
        in_specs=[
            pl.BlockSpec((tm, D_MODEL), lambda i: (i, 0)),
            pl.BlockSpec((None, 8, D_MODEL), lambda i: (i // tiles_per_batch, 0, 0)),
            pl.BlockSpec(w_perm.shape, lambda i: (0, 0), pipeline_mode=pl.Buffered(1)),
            pl.BlockSpec(w_uq.shape, lambda i: (0, 0), pipeline_mode=pl.Buffered(1)),
            pl.BlockSpec(w_iq.shape, lambda i: (0, 0), pipeline_mode=pl.Buffered(1)),
        ],
        out_specs=[pl.BlockSpec((tm, w), lambda i: (i, 0)) for w, _ in widths],
        out_shape=[jax.ShapeDtypeStruct((n, w), dt) for w, dt in widths],
        compiler_params=_cparams(("arbitrary",)),
        name="mixer_inproj",
    )(x2, mod, w_perm, w_uq, w_iq)


def _swa_kernel(sink_ref, q_ref, kvc_ref, kvp_ref, o_ref):
    i = pl.program_id(1)
    tq = SWA_TQ
    q = q_ref[...]
    kv = jnp.concatenate([kvp_ref[...], kvc_ref[...]], axis=0)
    qpos = i * tq + lax.broadcasted_iota(I32, (tq, 2 * tq), 0)
    kpos = (i - 1) * tq + lax.broadcasted_iota(I32, (tq, 2 * tq), 1)
    qchunk = qpos // CHUNK
    kchunk = (kpos + tq) // CHUNK - tq // CHUNK
    valid = (kpos >= 0) & (kchunk <= qchunk) & (kchunk >= qchunk - SWA_WIN_CHUNKS)
    dist = jnp.abs(qpos - kpos).astype(F32)
    rep = SWA_HEADS // SWA_KV_HEADS
    for h in range(SWA_HEADS):
        g = h // rep
        qh = q[:, h * HEAD_DIM:(h + 1) * HEAD_DIM]
        kg = kv[:, g * HEAD_DIM:(g + 1) * HEAD_DIM]
        vg = kv[:, (SWA_KV_HEADS + g) * HEAD_DIM:(SWA_KV_HEADS + g + 1) * HEAD_DIM]
        s = _dot_nt(qh, kg) * HEAD_DIM ** -0.5 - SLOPES_A[h] * dist
        s = jnp.where(valid, s, NEG_INF)
        sink = sink_ref[h]
        m = jnp.maximum(jnp.max(s, axis=-1, keepdims=True), sink)
        p = jnp.exp(s - m)
        denom = jnp.sum(p, axis=-1, keepdims=True) + jnp.exp(sink - m)
        o = _dot(p.astype(BF16), vg) / denom
        o_ref[:, h * HEAD_DIM:(h + 1) * HEAD_DIM] = o.astype(BF16)


def _swa(sinks, a_q, a_kv, bn, t):
    nt = t // SWA_TQ
    return pl.pallas_call(
        _swa_kernel,
        grid=(bn, nt),
        in_specs=[
            pl.BlockSpec(memory_space=pltpu.SMEM),
            pl.BlockSpec((SWA_TQ, 256), lambda b, i: (b * nt + i, 0)),
            pl.BlockSpec((SWA_TQ, 256), lambda b, i: (b * nt + i, 0)),
            pl.BlockSpec((SWA_TQ, 256), lambda b, i: (b * nt + jnp.maximum(i - 1, 0), 0)),
        ],
        out_specs=pl.BlockSpec((SWA_TQ, 256), lambda b, i: (b * nt + i, 0)),
        out_shape=jax.ShapeDtypeStruct((bn * t, 256), BF16),
        compiler_params=_cparams(("arbitrary", "arbitrary")),
        name="swa_attention",
    )(sinks, a_q, a_kv, a_kv)


def _sortable_key(x):
    bits = lax.bitcast_convert_type(x, I32)
    return bits ^ ((bits >> 31) & 0x7FFFFFFF)


def _dsa_kernel(q_ref, qi_ref, sm_ref, kv_ref, vt_ref, ki_ref, o_ref, key_ref, jsel_ref, *, seq):
    i = pl.program_id(1)
    tq, tk = DSA_TQ, DSA_TK
    nblk = (i * tq + tq + tk - 1) // tk
    qpos = i * tq + lax.broadcasted_iota(I32, (1, tq), 1)
    qchunk = qpos // CHUNK
    row_k = lax.broadcasted_iota(I32, (tk, tq), 0)

    qi = qi_ref[...]
    lane_i = lax.broadcasted_iota(I32, (1, IDX_HEADS * IDX_DIM), 1) // IDX_DIM
    qi_stack = jnp.concatenate([jnp.where(lane_i == h, qi, jnp.zeros_like(qi)) for h in range(IDX_HEADS)], axis=0)
    sm_t = sm_ref[...].T
    w_idx = [sm_t[SM_WI + h:SM_WI + h + 1, :] for h in range(IDX_HEADS)]
    idx_scale = (IDX_DIM * IDX_HEADS) ** -0.5

    def score_block(j, carry):
        kib = ki_ref[pl.ds(pl.multiple_of(j * tk, tk), tk), :]
        d = _dot_nt(kib, qi_stack)
        acc = w_idx[0] * jnp.maximum(d[:, 0:tq], 0.0)
        for h in range(1, IDX_HEADS):
            acc = acc + w_idx[h] * jnp.maximum(d[:, h * tq:(h + 1) * tq], 0.0)
        sc = acc * idx_scale + 0.0
        kpos = j * tk + row_k
        sc = jnp.where(kpos // CHUNK <= qchunk, sc, NEG_INF)
        key_ref[j] = _sortable_key(sc)
        return carry

    lax.fori_loop(0, nblk, score_block, 0)

    k_eff = jnp.minimum(DSA_TOPK, (qchunk + 1) * CHUNK)

    def count(pred):
        def blk(j, c):
            m = pred(key_ref[j], j * tk + row_k).astype(I32)
            return c + jnp.sum(m.reshape(tk // 8, 8, tq), axis=0)
        c = lax.fori_loop(0, nblk, blk, jnp.zeros((8, tq), I32))
        return jnp.sum(c, axis=0, keepdims=True)

    def bit_step(bi, prefix):
        cand_u = prefix | (jnp.int32(1) << (31 - bi))
        cand_s = cand_u ^ INT_MIN
        cnt = count(lambda key, kpos: key >= cand_s)
        return jnp.where(cnt >= k_eff, cand_u, prefix)

    prefix = lax.fori_loop(0, 32, bit_step, jnp.zeros((1, tq), I32))
    thr = prefix ^ INT_MIN

    c_gt = count(lambda key, kpos: key > thr)
    c_eq = count(lambda key, kpos: key == thr)
    need = k_eff - c_gt
    excess = c_eq > need
    jsel_ref[...] = jnp.full((8, tq), seq, I32)

    @pl.when(jnp.max(excess.astype(I32)) > 0)
    def _():
        nbits = max(1, (seq - 1).bit_length())

        def idx_step(bi, x):
            cand = x | (jnp.int32(1) << (nbits - 1 - bi))
            cnt = count(lambda key, kpos: (key == thr) & (kpos < cand))
            return jnp.where(cnt <= need - 1, cand, x)

        x = lax.fori_loop(0, nbits, idx_step, jnp.zeros((1, tq), I32))
        jsel_ref[...] = jnp.broadcast_to(jnp.where(excess, x, seq), (8, tq))

    jsel = jsel_ref[0:1, :]

    q = q_ref[...]
    q_stack = jnp.concatenate([q[:, h * HEAD_DIM:(h + 1) * HEAD_DIM] for h in range(DSA_HEADS)], axis=0)
    qpos_f = qpos.astype(F32)

    def att_block(j, carry):
        m_run, l_run, acc = carry
        kb = kv_ref[pl.ds(pl.multiple_of(j * tk, tk), tk), 0:HEAD_DIM]
        s = _dot_nt(kb, q_stack)
        kpos = j * tk + row_k
        key = key_ref[j]
        sel = (key > thr) | ((key == thr) & (kpos <= jsel))
        dist = jnp.abs(kpos.astype(F32) - qpos_f)
        ms, ls, ps, alphas = [], [], [], []
        for h in range(DSA_HEADS):
            sh = jnp.where(sel, s[:, h * tq:(h + 1) * tq] - SLOPES_B[h] * dist, -jnp.inf)
            m_old = m_run[:, h * tq:(h + 1) * tq]
            m_new = jnp.maximum(m_old, jnp.max(sh, axis=0, keepdims=True))
            alpha = jnp.exp(m_old - m_new)
            p = jnp.exp(sh - m_new)
            ms.append(m_new)
            ls.append(alpha * l_run[:, h * tq:(h + 1) * tq] + jnp.sum(p, axis=0, keepdims=True))
            ps.append(p.astype(BF16))
            alphas.append(alpha)
        pv = _dot(vt_ref[j], jnp.concatenate(ps, axis=1))
        acc = jnp.concatenate(alphas, axis=1) * acc + pv
        return jnp.concatenate(ms, axis=1), jnp.concatenate(ls, axis=1), acc

    init = (jnp.full((1, DSA_HEADS * tq), NEG_INF, F32), jnp.zeros((1, DSA_HEADS * tq), F32),
            jnp.zeros((HEAD_DIM, DSA_HEADS * tq), F32))
    _, l_run, acc = lax.fori_loop(0, nblk, att_block, init)
    out = acc / l_run
    for h in range(DSA_HEADS):
        o_ref[:, h * HEAD_DIM:(h + 1) * HEAD_DIM] = out[:, h * tq:(h + 1) * tq].T.astype(BF16)


def _dsa(b_q, b_qi, small, b_kv, b_ki, bn, t):
    nt = t // DSA_TQ
    nkb = t // DSA_TK
    v_t = b_kv[:, HEAD_DIM:].reshape(bn * nkb, DSA_TK, HEAD_DIM).transpose(0, 2, 1)
    return pl.pallas_call(
        functools.partial(_dsa_kernel, seq=t),
        grid=(bn, nt),
        in_specs=[
            pl.BlockSpec((DSA_TQ, 256), lambda b, i: (b * nt + i, 0)),
            pl.BlockSpec((DSA_TQ, 128), lambda b, i: (b * nt + i, 0)),
            pl.BlockSpec((DSA_TQ, 128), lambda b, i: (b * nt + i, 0)),
            pl.BlockSpec((t, 128), lambda b, i: (b, 0)),
            pl.BlockSpec((nkb, HEAD_DIM, DSA_TK), lambda b, i: (b, 0, 0)),
            pl.BlockSpec((t, 128), lambda b, i: (b, 0)),
        ],
        out_specs=pl.BlockSpec((DSA_TQ, 256), lambda b, i: (b * nt + i, 0)),
        out_shape=jax.ShapeDtypeStruct((bn * t, 256), BF16),
        scratch_shapes=[pltpu.VMEM((nkb, DSA_TK, DSA_TQ), I32), pltpu.VMEM((8, DSA_TQ), I32)],
        compiler_params=_cparams(("arbitrary", "arbitrary")),
        name="dsa_attention",
    )(b_q, b_qi, small, b_kv, v_t, b_ki)


def _mlstm_kernel(qk_ref, v_ref, og_ref, sm_ref, convw_ref, convb_ref, gbias_ref, normg_ref,
                  o_ref, tail_ref, cmat_ref, nvec_ref, mst_ref, *, bn):
    c = pl.program_id(0)
    L = CHUNK

    @pl.when(c == 0)
    def _():
        tail_ref[...] = jnp.zeros_like(tail_ref)
        cmat_ref[...] = jnp.zeros_like(cmat_ref)
        nvec_ref[...] = jnp.zeros_like(nvec_ref)
        mst_ref[...] = jnp.zeros_like(mst_ref)

    row = lax.broadcasted_iota(I32, (L, L), 0)
    col = lax.broadcasted_iota(I32, (L, L), 1)
    causal = col <= row
    tril = causal.astype(F32)
    convw = convw_ref[...]
    for b in range(bn):
        cur = qk_ref[b]
        ext = jnp.concatenate([tail_ref[b], cur], axis=0)
        tail_ref[b] = cur[L - 8:L, :]
        y = convb_ref[...] + convw[MLSTM_CONV - 1:MLSTM_CONV, :] * cur
        for k in range(MLSTM_CONV - 1):
            off = 8 - (MLSTM_CONV - 1) + k
            y = y + convw[k:k + 1, :] * ext[off:off + L, :]
        qk = y * _sigmoid(y)
        gates = sm_ref[b] + gbias_ref[...]
        lf = jnp.minimum(gates, 0.0) - jnp.log(1.0 + jnp.exp(-jnp.abs(gates)))
        bcum = jnp.dot(tril, lf, preferred_element_type=F32, precision=lax.Precision.HIGHEST)
        gates_t = gates.T
        bcum_t = bcum.T
        vb = v_ref[b]
        og = og_ref[b]
        for h in range(MLSTM_HEADS):
            s_idx = b * MLSTM_HEADS + h
            q = qk[:, h * HEAD_DIM:(h + 1) * HEAD_DIM]
            k = qk[:, (MLSTM_HEADS + h) * HEAD_DIM:(MLSTM_HEADS + h + 1) * HEAD_DIM] * HEAD_DIM ** -0.5
            v = vb[:, h * HEAD_DIM:(h + 1) * HEAD_DIM]
            qb, kb = q.astype(BF16), k.astype(BF16)
            b_col = bcum[:, SM_FG + h:SM_FG + h + 1]
            b_row = bcum_t[SM_FG + h:SM_FG + h + 1, :]
            ig_col = gates[:, SM_IG + h:SM_IG + h + 1]
            ig_row = gates_t[SM_IG + h:SM_IG + h + 1, :]
            m_prev = mst_ref[s_idx]
            cmat = cmat_ref[s_idx]
            nvec = nvec_ref[s_idx]
            dlog = jnp.where(causal, b_col - b_row + ig_row, NEG_INF)
            inter = b_col + m_prev
            mj = jnp.maximum(inter, jnp.max(dlog, axis=-1, keepdims=True))
            dw = jnp.exp(dlog - mj)
            iw = jnp.exp(inter - mj)
            sc = _dot_nt(qb, kb) * dw
            num = iw * _dot(qb, cmat.astype(BF16)) + _dot(sc.astype(BF16), v)
            den = iw * jnp.sum(q * nvec, axis=-1, keepdims=True) + jnp.sum(sc, axis=-1, keepdims=True)
            hj = num / jnp.maximum(jnp.abs(den), jnp.exp(-mj))
            bl = b_col[L - 1:L, :]
            dec_row = bl - b_row + ig_row
            dec_col = bl - b_col + ig_col
            m_new = jnp.maximum(bl + m_prev, jnp.max(dec_row, axis=-1, keepdims=True))
            wc = jnp.exp(bl + m_prev - m_new)
            ws = jnp.exp(dec_col - m_new)
            kw = k * ws
            cmat_ref[s_idx] = wc * cmat + _dot(kw.T.astype(BF16), v)
            nvec_ref[s_idx] = wc * nvec + jnp.sum(kw, axis=0, keepdims=True)
            mst_ref[s_idx] = m_new
            mu = jnp.mean(hj, axis=-1, keepdims=True)
            dh = hj - mu
            var = jnp.mean(dh * dh, axis=-1, keepdims=True)
            hn = dh * lax.rsqrt(var + LN_EPS) * normg_ref[:, h * HEAD_DIM:(h + 1) * HEAD_DIM]
            o = _sigmoid(og[:, h * HEAD_DIM:(h + 1) * HEAD_DIM]) * hn
            o_ref[b, :, h * HEAD_DIM:(h + 1) * HEAD_DIM] = o.astype(BF16)


def _mlstm(c_qk, c_v, c_o, small, conv_w, conv_b, gate_bias, norm_g, bn, t):
    nc = t // CHUNK
    nstate = bn * MLSTM_HEADS
    blk = lambda w: pl.BlockSpec((bn, CHUNK, w), lambda c: (0, c, 0))
    full = lambda a: pl.BlockSpec(a.shape, lambda c: (0,) * a.ndim)
    return pl.pallas_call(
        functools.partial(_mlstm_kernel, bn=bn),
        grid=(nc,),
        in_specs=[blk(512), blk(256), blk(256), blk(128), full(conv_w), full(conv_b), full(gate_bias), full(norm_g)],
        out_specs=blk(256),
        out_shape=jax.ShapeDtypeStruct((bn, t, 256), BF16),
        scratch_shapes=[pltpu.VMEM((bn, 8, 512), F32), pltpu.VMEM((nstate, HEAD_DIM, HEAD_DIM), F32),
                        pltpu.VMEM((nstate, 1, HEAD_DIM), F32), pltpu.VMEM((nstate, 1, 1), F32)],
        compiler_params=_cparams(("arbitrary",)),
        name="mlstm",
    )(c_qk.reshape(bn, t, 512), c_v.reshape(bn, t, 256), c_o.reshape(bn, t, 256), small.reshape(bn, t, 128),
      conv_w, conv_b, gate_bias, norm_g)


def _s5_kernel(u_ref, bbre_ref, bbim_ref, are_ref, aim_ref, cre_ref, cim_ref, dskip_ref, gluw_ref, glub_ref,
               o_ref, sre_ref, sim_ref, stre_ref, stim_ref):
    @pl.when(pl.program_id(1) == 0)
    def _():
        stre_ref[...] = jnp.zeros_like(stre_ref)
        stim_ref[...] = jnp.zeros_like(stim_ref)

    u = u_ref[...]
    ub = u.astype(BF16)
    sre_ref[...] = _dot(ub, bbre_ref[...])
    sim_ref[...] = _dot(ub, bbim_ref[...])
    a_re = jnp.broadcast_to(are_ref[...], (8, S5_LANES))
    a_im = jnp.broadcast_to(aim_ref[...], (8, S5_LANES))
    row = lax.broadcasted_iota(I32, (8, S5_LANES), 0)

    def group(i, carry):
        s_re, s_im = carry
        base = pl.multiple_of(i * 8, 8)
        bu_re = sre_ref[pl.ds(base, 8), :]
        bu_im = sim_ref[pl.ds(base, 8), :]
        out_re, out_im = bu_re, bu_im
        for r in range(8):
            n_re = a_re * s_re - a_im * s_im + bu_re
            n_im = a_re * s_im + a_im * s_re + bu_im
            out_re = jnp.where(row == r, n_re, out_re)
            out_im = jnp.where(row == r, n_im, out_im)
            s_re = jnp.broadcast_to(n_re[r:r + 1, :], (8, S5_LANES))
            s_im = jnp.broadcast_to(n_im[r:r + 1, :], (8, S5_LANES))
        sre_ref[pl.ds(base, 8), :] = out_re
        sim_ref[pl.ds(base, 8), :] = out_im
        return s_re, s_im

    s_re, s_im = lax.fori_loop(0, S5_TT // 8, group, (stre_ref[...], stim_ref[...]))
    stre_ref[...] = s_re
    stim_ref[...] = s_im
    y = (_dot(sre_ref[...].astype(BF16), cre_ref[...]) - _dot(sim_ref[...].astype(BF16), cim_ref[...])
         + dskip_ref[...] * u)
    y = 0.5 * y * (1.0 + jnp.tanh(math.sqrt(2.0 / math.pi) * (y + 0.044715 * (y * y * y))))
    z = _dot(y.astype(BF16), gluw_ref[...]) + glub_ref[...]
    o_ref[...] = (y * _sigmoid(z)).astype(BF16)


def _s5(d_u, bb_re, bb_im, a_re, a_im, c_re_t, c_im_t, d_skip, glu_w, glu_b, bn, t):
    nt = t // S5_TT
    full = lambda a: pl.BlockSpec(a.shape, lambda b, i: (0,) * a.ndim)
    args = (bb_re, bb_im, a_re, a_im, c_re_t, c_im_t, d_skip, glu_w, glu_b)
    return pl.pallas_call(
        _s5_kernel,
        grid=(bn, nt),
        in_specs=[pl.BlockSpec((S5_TT, 256), lambda b, i: (b * nt + i, 0))] + [full(a) for a in args],
        out_specs=pl.BlockSpec((S5_TT, 256), lambda b, i: (b * nt + i, 0)),
        out_shape=jax.ShapeDtypeStruct((bn * t, 256), BF16),
        scratch_shapes=[pltpu.VMEM((S5_TT, S5_LANES), F32), pltpu.VMEM((S5_TT, S5_LANES), F32),
                        pltpu.VMEM((8, S5_LANES), F32), pltpu.VMEM((8, S5_LANES), F32)],
        compiler_params=_cparams(("arbitrary", "arbitrary")),
        name="s5_glu",
    )(d_u, *args)


def _outproj_kernel(x_ref, mod_ref, oa_ref, ob_ref, oc_ref, od_ref, w_ref, lng_ref, lnb_ref, o_ref):
    x = x_ref[...]
    gate = mod_ref[2:3, :]
    y = _dot(oa_ref[...], w_ref[0])
    y = y + _dot(ob_ref[...], w_ref[1])
    y = y + _dot(oc_ref[...], w_ref[2])
    y = y + _dot(od_ref[...], w_ref[3])
    o_ref[...] = _residual_layer_norm(x, gate * y, lng_ref[...], lnb_ref[...])


def _outproj(x2, mod, o_a, o_b, o_c, o_d, w_out4, ln_g, ln_b, rows_per_batch):
    n = x2.shape[0]
    tm = ROW_TILE
    tiles_per_batch = rows_per_batch // tm
    mix = pl.BlockSpec((tm, GROUP_WIDTH), lambda i: (i, 0))
    return pl.pallas_call(
        _outproj_kernel,
        grid=(n // tm,),
        in_specs=[
            pl.BlockSpec((tm, D_MODEL), lambda i: (i, 0)),
            pl.BlockSpec((None, 8, D_MODEL), lambda i: (i // tiles_per_batch, 0, 0)),
            mix, mix, mix, mix,
            pl.BlockSpec(w_out4.shape, lambda i: (0, 0, 0), pipeline_mode=pl.Buffered(1)),
            pl.BlockSpec((1, D_MODEL), lambda i: (0, 0)),
            pl.BlockSpec((1, D_MODEL), lambda i: (0, 0)),
        ],
        out_specs=pl.BlockSpec((tm, D_MODEL), lambda i: (i, 0)),
        out_shape=jax.ShapeDtypeStruct((n, D_MODEL), F32),
        compiler_params=_cparams(("arbitrary",)),
        name="mixer_outproj",
    )(x2, mod, o_a, o_b, o_c, o_d, w_out4, ln_g.reshape(1, -1), ln_b.reshape(1, -1))


def _permute_w_in(w_in):
    off = {}
    o = 0
    for name, s in (("qa", 256), ("ka", 128), ("va", 128), ("cq", 128), ("kb", 64), ("vb", 64), ("ki", 32),
                    ("wi", 4), ("qkc", 512), ("vc", 256), ("ig", 4), ("fg", 4), ("oc", 256), ("ud", 256)):
        off[name] = (o, o + s)
        o += s
    col = lambda n: w_in[:, off[n][0]:off[n][1]]
    small = jnp.concatenate([col("wi"), col("ig"), col("fg"),
                             jnp.zeros((w_in.shape[0], 128 - 12), w_in.dtype)], axis=1)
    parts = [col("qa"), col("ka"), col("va"), col("cq"), col("kb"), col("vb")] + [col("ki")] * 4 + [
        col("qkc"), col("vc"), col("oc"), col("ud"), small]
    return jnp.concatenate(parts, axis=1).astype(BF16)


def _s5_params(lam_re, lam_im, log_step, b_re, b_im, c_re, c_im):
    dt = jnp.exp(log_step)[:, None]
    mag = jnp.exp(lam_re * dt)
    a_re, a_im = mag * jnp.cos(lam_im * dt), mag * jnp.sin(lam_im * dt)
    den = lam_re * lam_re + lam_im * lam_im
    kap_re = ((a_re - 1.0) * lam_re + a_im * lam_im) / den
    kap_im = (a_im * lam_re - (a_re - 1.0) * lam_im) / den
    bb_re = kap_re[..., None] * b_re - kap_im[..., None] * b_im
    bb_im = kap_re[..., None] * b_im + kap_im[..., None] * b_re
    eye = jnp.eye(S5_GROUPS, dtype=F32)

    def in_mat(bb):
        return jnp.einsum("gph,gk->ghkp", bb, eye).reshape(S5_GROUPS * S5_GROUP_CH, S5_LANES).astype(BF16)

    def out_mat(cc):
        return jnp.einsum("gop,gk->gpko", cc, eye).reshape(S5_LANES, S5_GROUPS * S5_GROUP_CH).astype(BF16)

    return (in_mat(bb_re), in_mat(bb_im), a_re.reshape(1, S5_LANES), a_im.reshape(1, S5_LANES),
            out_mat(c_re), out_mat(c_im))


def kernel(x, c, ada_w, ada_b, ln_g, ln_b, ffn_w13, ffn_w2, w_in, w_out, sinks, w_uq, w_iq, conv_w, conv_b, ig_b,
           fg_b, mh_norm_g, lam_re, lam_im, log_step, b_re, b_im, c_re, c_im, d_skip, glu_w, glu_b):
    bn, t, d = x.shape
    assert d == D_MODEL and t % max(ROW_TILE, DSA_TK, S5_TT) == 0 and bn <= 8
    n = bn * t
    nl = ada_w.shape[0]
    c_pad = jnp.zeros((8, d), F32).at[:bn].set(c)
    mod_all = _ada_mod(c_pad, ada_w, ada_b)
    mod_all = mod_all[:, :bn].reshape(nl, bn, N_SUB, 3, d).transpose(0, 2, 1, 3, 4)
    mod_all = jnp.pad(mod_all, ((0, 0), (0, 0), (0, 0), (0, 5), (0, 0)))

    x2 = x.reshape(n, d)
    for l in range(nl):
        w13c = ffn_w13[l].astype(BF16).reshape(2, d, 2, FFN_NCHUNK, FFN_TF).transpose(0, 2, 3, 1, 4)
        w2c = ffn_w2[l].astype(BF16).reshape(2, FFN_NCHUNK, FFN_TF, d)
        x2 = _ffn_sublayer(x2, mod_all[l, 0], w13c[0], w2c[0], ln_g[l, 0], ln_b[l, 0], t)
        (a_q, a_kv, b_q, b_qi, b_kv, b_ki, c_qk, c_v, c_o, d_u, small) = _inproj(
            x2, mod_all[l, 1], _permute_w_in(w_in[l]), w_uq[l].astype(BF16), w_iq[l].astype(BF16), t)
        o_a = _swa(sinks[l], a_q, a_kv, bn, t)
        o_b = _dsa(b_q, b_qi, small, b_kv, b_ki, bn, t)
        gate_bias = jnp.zeros((1, 128), F32).at[0, SM_IG:SM_IG + 4].set(ig_b[l]).at[0, SM_FG:SM_FG + 4].set(fg_b[l])
        o_c = _mlstm(c_qk, c_v, c_o, small, conv_w[l], conv_b[l].reshape(1, -1), gate_bias,
                     mh_norm_g[l].reshape(1, -1), bn, t).reshape(n, GROUP_WIDTH)
        s5p = _s5_params(lam_re[l], lam_im[l], log_step[l], b_re[l], b_im[l], c_re[l], c_im[l])
        o_d = _s5(d_u, *s5p, d_skip[l].reshape(1, -1), glu_w[l].astype(BF16), glu_b[l].reshape(1, -1), bn, t)
        x2 = _outproj(x2, mod_all[l, 1], o_a, o_b, o_c, o_d,
                      w_out[l].astype(BF16).reshape(4, GROUP_WIDTH, d), ln_g[l, 1], ln_b[l, 1], t)
        x2 = _ffn_sublayer(x2, mod_all[l, 2], w13c[1], w2c[1], ln_g[l, 2], ln_b[l, 2], t)
    return x2.reshape(bn, t, d)
```

```python
import functools
import math

import jax
import jax.numpy as jnp
from jax import lax
from jax.experimental import pallas as pl
from jax.experimental.pallas import tpu as pltpu

F32 = jnp.float32
BF16 = jnp.bfloat16
I32 = jnp.int32

D_MODEL = 1024
DEPTH = 2
CHUNK = 64
HEAD_DIM = 64
GROUP_WIDTH = 256
SWA_HEADS = 4
SWA_KV_HEADS = 2
SWA_WIN_CHUNKS = 2
DSA_HEADS = 4
DSA_Q_RANK = 128
IDX_HEADS = 4
IDX_DIM = 32
DSA_TOPK = 256
MLSTM_HEADS = 4
MLSTM_CONV = 4
S5_GROUP_CH = 16
S5_GROUPS = 16
S5_STATE = 64
S5_LANES = S5_GROUPS * S5_STATE
D_FF = 2816
N_SUB = 3
ALPHA = (2 * DEPTH) ** 0.25
LN_EPS = 1e-5
NEG_INF = -1e30
INT_MIN = -(2 ** 31)

SLOPES_A = tuple(2.0 ** -(i + 1) for i in range(0, 8, 2))
SLOPES_B = tuple(2.0 ** -(i + 1) for i in range(1, 8, 2))

VMEM_LIMIT_BYTES = 56 * 1024 * 1024

FFN_TF = 256
FFN_NCHUNK = D_FF // FFN_TF
ROW_TILE = 512
SWA_TQ = 128
DSA_TQ = 128
DSA_TK = 512
DSA_TKA = 512
S5_TT = 512

ZC_QA, ZC_KVA, ZC_CQ, ZC_KVB, ZC_KI = 0, 256, 512, 640, 768
ZC_QKC, ZC_VC, ZC_OC, ZC_UD, ZC_SMALL, Z_WIDTH = 896, 1408, 1664, 1920, 2176, 2304
SM_WI, SM_IG, SM_FG = 0, 4, 8


def _cparams(sem):
    return pltpu.CompilerParams(dimension_semantics=sem, vmem_limit_bytes=VMEM_LIMIT_BYTES)


def _dot(a, b):
    return jnp.dot(a, b, preferred_element_type=F32)


def _dot_nt(a, b):
    return lax.dot_general(a, b, (((1,), (1,)), ((), ())), preferred_element_type=F32)


def _sigmoid(x):
    return 1.0 / (1.0 + jnp.exp(-x))


def _residual_layer_norm(x, y, g, b):
    v = ALPHA * x + y
    mu = jnp.mean(v, axis=-1, keepdims=True)
    d = v - mu
    var = jnp.mean(d * d, axis=-1, keepdims=True)
    return d * lax.rsqrt(var + LN_EPS) * g + b


def _ada_kernel(c_ref, w_ref, b_ref, o_ref):
    c = c_ref[...]
    cs = c * _sigmoid(c)
    o_ref[...] = jnp.dot(cs, w_ref[...], preferred_element_type=F32,
                         precision=lax.Precision.HIGHEST) + b_ref[...]


def _ada_mod(c_pad, ada_w, ada_b):
    nl = ada_w.shape[0]
    ncol = ada_w.shape[2] // D_MODEL
    return pl.pallas_call(
        _ada_kernel,
        grid=(nl, ncol),
        in_specs=[
            pl.BlockSpec((8, D_MODEL), lambda l, j: (0, 0)),
            pl.BlockSpec((None, D_MODEL, D_MODEL), lambda l, j: (l, 0, j)),
            pl.BlockSpec((None, 1, D_MODEL), lambda l, j: (l, 0, j)),
        ],
        out_specs=pl.BlockSpec((None, 8, D_MODEL), lambda l, j: (l, 0, j)),
        out_shape=jax.ShapeDtypeStruct((nl, 8, ada_w.shape[2]), F32),
        compiler_params=_cparams(("arbitrary", "arbitrary")),
        name="ada_mod",
    )(c_pad, ada_w, ada_b.reshape(nl, 1, -1))


def _ffn_kernel(x_ref, mod_ref, w13_ref, w2_ref, lng_ref, lnb_ref, o_ref, acc_ref):
    x = x_ref[...]
    shift, scale, gate = mod_ref[0:1, :], mod_ref[1:2, :], mod_ref[2:3, :]
    u = (x * (1.0 + scale) + shift).astype(BF16)
    acc_ref[...] = jnp.zeros_like(acc_ref)

    def chunk(j, carry):
        a = _dot(u, w13_ref[0, j])
        g = _dot(u, w13_ref[1, j])
        h = (a * _sigmoid(a) * g).astype(BF16)
        acc_ref[...] += _dot(h, w2_ref[j])
        return carry

    lax.fori_loop(0, FFN_NCHUNK, chunk, 0)
    o_ref[...] = _residual_layer_norm(x, 0.5 * gate * acc_ref[...], lng_ref[...], lnb_ref[...])


def _ffn_sublayer(x2, mod, w13c, w2c, ln_g, ln_b, rows_per_batch):
    n = x2.shape[0]
    tm = ROW_TILE
    tiles_per_batch = rows_per_batch // tm
    return pl.pallas_call(
        _ffn_kernel,
        grid=(n // tm,),
        in_specs=[
            pl.BlockSpec((tm, D_MODEL), lambda i: (i, 0)),
            pl.BlockSpec((None, 8, D_MODEL), lambda i: (i // tiles_per_batch, 0, 0)),
            pl.BlockSpec(w13c.shape, lambda i: (0, 0, 0, 0), pipeline_mode=pl.Buffered(1)),
            pl.BlockSpec(w2c.shape, lambda i: (0, 0, 0), pipeline_mode=pl.Buffered(1)),
            pl.BlockSpec((1, D_MODEL), lambda i: (0, 0)),
            pl.BlockSpec((1, D_MODEL), lambda i: (0, 0)),
        ],
        out_specs=pl.BlockSpec((tm, D_MODEL), lambda i: (i, 0)),
        out_shape=jax.ShapeDtypeStruct((n, D_MODEL), F32),
        scratch_shapes=[pltpu.VMEM((tm, D_MODEL), F32)],
        compiler_params=_cparams(("arbitrary",)),
        name="ffn_sublayer",
    )(x2, mod, w13c, w2c, ln_g.reshape(1, -1), ln_b.reshape(1, -1))


def _inproj_kernel(x_ref, mod_ref, w_ref, wuq_ref, wiq_ref,
                   aq_ref, akv_ref, bq_ref, bqi_ref, bkv_ref, bki_ref,
                   cqk_ref, cv_ref, co_ref, du_ref, sm_ref):
    x = x_ref[...]
    shift, scale = mod_ref[0:1, :], mod_ref[1:2, :]
    u = (x * (1.0 + scale) + shift).astype(BF16)
    z = _dot(u, w_ref[...])
    aq_ref[...] = z[:, ZC_QA:ZC_KVA].astype(BF16)
    akv_ref[...] = z[:, ZC_KVA:ZC_CQ].astype(BF16)
    cq = z[:, ZC_CQ:ZC_KVB].astype(BF16)
    bq_ref[...] = (_dot(cq, wuq_ref[...]) * HEAD_DIM ** -0.5).astype(BF16)
    bqi_ref[...] = _dot(cq, wiq_ref[...]).astype(BF16)
    bkv_ref[...] = z[:, ZC_KVB:ZC_KI].astype(BF16)
    bki_ref[...] = z[:, ZC_KI:ZC_QKC].astype(BF16)
    cqk_ref[...] = z[:, ZC_QKC:ZC_VC]
    cv_ref[...] = z[:, ZC_VC:ZC_OC].astype(BF16)
    co_ref[...] = z[:, ZC_OC:ZC_UD]
    du_ref[...] = z[:, ZC_UD:ZC_SMALL]
    sm_ref[...] = z[:, ZC_SMALL:Z_WIDTH]


def _inproj(x2, mod, w_perm, w_uq, w_iq, rows_per_batch):
    n = x2.shape[0]
    tm = ROW_TILE
    tiles_per_batch = rows_per_batch // tm
    widths = [(256, BF16), (256, BF16), (256, BF16), (128, BF16), (128, BF16), (128, BF16),
              (512, F32), (256, BF16), (256, F32), (256, F32), (128, F32)]
    return pl.pallas_call(
        _inproj_kernel,
        grid=(n // tm,),
        in_specs=[
            pl.BlockSpec((tm, D_MODEL), lambda i: (i, 0)),
            pl.BlockSpec((None, 8, D_MODEL), lambda i: (i // tiles_per_batch, 0, 0)),
            pl.BlockSpec(w_perm.shape, lambda i: (0, 0), pipeline_mode=pl.Buffered(1)),
            pl.BlockSpec(w_uq.shape, lambda i: (0, 0), pipeline_mode=pl.Buffered(1)),
            pl.BlockSpec(w_iq.shape, lambda i: (0, 0), pipeline_mode=pl.Buffered(1)),
        ],
        out_specs=[pl.BlockSpec((tm, w), lambda i: (i, 0)) for w, _ in widths],
        out_shape=[jax.ShapeDtypeStruct((n, w), dt) for w, dt in widths],
        compiler_params=_cparams(("arbitrary",)),
        name="mixer_inproj",
    )(x2, mod, w_perm, w_uq, w_iq)


def _swa_kernel(sink_ref, q_ref, kvc_ref, kvp_ref, o_ref):
    i = pl.program_id(1)
    tq = SWA_TQ
    q = q_ref[...]
    kv = jnp.concatenate([kvp_ref[...], kvc_ref[...]], axis=0)
    qpos = i * tq + lax.broadcasted_iota(I32, (tq, 2 * tq), 0)
    kpos = (i - 1) * tq + lax.broadcasted_iota(I32, (tq, 2 * tq), 1)
    qchunk = qpos // CHUNK
    kchunk = (kpos + tq) // CHUNK - tq // CHUNK
    valid = (kpos >= 0) & (kchunk <= qchunk) & (kchunk >= qchunk - SWA_WIN_CHUNKS)
    dist = jnp.abs(qpos - kpos).astype(F32)
    rep = SWA_HEADS // SWA_KV_HEADS
    for h in range(SWA_HEADS):
        g = h // rep
        qh = q[:, h * HEAD_DIM:(h + 1) * HEAD_DIM]
        kg = kv[:, g * HEAD_DIM:(g + 1) * HEAD_DIM]
        vg = kv[:, (SWA_KV_HEADS + g) * HEAD_DIM:(SWA_KV_HEADS + g + 1) * HEAD_DIM]
        s = _dot_nt(qh, kg) * HEAD_DIM ** -0.5 - SLOPES_A[h] * dist
        s = jnp.where(valid, s, NEG_INF)
        sink = sink_ref[h]
        m = jnp.maximum(jnp.max(s, axis=-1, keepdims=True), sink)
        p = jnp.exp(s - m)
        denom = jnp.sum(p, axis=-1, keepdims=True) + jnp.exp(sink - m)
        o = _dot(p.astype(BF16), vg) / denom
        o_ref[:, h * HEAD_DIM:(h + 1) * HEAD_DIM] = o.astype(BF16)


def _swa(sinks, a_q, a_kv, bn, t):
    nt = t // SWA_TQ
    return pl.pallas_call(
        _swa_kernel,
        grid=(bn, nt),
        in_specs=[
            pl.BlockSpec(memory_space=pltpu.SMEM),
            pl.BlockSpec((SWA_TQ, 256), lambda b, i: (b * nt + i, 0)),
            pl.BlockSpec((SWA_TQ, 256), lambda b, i: (b * nt + i, 0)),
            pl.BlockSpec((SWA_TQ, 256), lambda b, i: (b * nt + jnp.maximum(i - 1, 0), 0)),
        ],
        out_specs=pl.BlockSpec((SWA_TQ, 256), lambda b, i: (b * nt + i, 0)),
        out_shape=jax.ShapeDtypeStruct((bn * t, 256), BF16),
        compiler_params=_cparams(("arbitrary", "arbitrary")),
        name="swa_attention",
    )(sinks, a_q, a_kv, a_kv)


def _sortable_key(x):
    bits = lax.bitcast_convert_type(x, I32)
    return bits ^ ((bits >> 31) & 0x7FFFFFFF)


def _dsa_kernel(q_ref, qi_ref, sm_ref, kv_ref, vt_ref, ki_ref, o_ref, key_ref):
    i = pl.program_id(1)
    tq, tk = DSA_TQ, DSA_TK
    nblk = (i * tq + tq + tk - 1) // tk
    qpos = i * tq + lax.broadcasted_iota(I32, (1, tq), 1)
    qchunk = qpos // CHUNK
    row_k = lax.broadcasted_iota(I32, (tk, tq), 0)

    qi = qi_ref[...]
    lane_i = lax.broadcasted_iota(I32, (1, IDX_HEADS * IDX_DIM), 1) // IDX_DIM
    qi_stack = jnp.concatenate([jnp.where(lane_i == h, qi, jnp.zeros_like(qi)) for h in range(IDX_HEADS)], axis=0)
    sm_t = sm_ref[...].T
    w_idx = [sm_t[SM_WI + h:SM_WI + h + 1, :] for h in range(IDX_HEADS)]
    idx_scale = (IDX_DIM * IDX_HEADS) ** -0.5

    def score_block(j, carry):
        kib = ki_ref[pl.ds(pl.multiple_of(j * tk, tk), tk), :]
        d = _dot_nt(kib, qi_stack)
        acc = w_idx[0] * jnp.maximum(d[:, 0:tq], 0.0)
        for h in range(1, IDX_HEADS):
            acc = acc + w_idx[h] * jnp.maximum(d[:, h * tq:(h + 1) * tq], 0.0)
        sc = acc * idx_scale + 0.0
        kpos = j * tk + row_k
        sc = jnp.where(kpos // CHUNK <= qchunk, sc, NEG_INF)
        key_ref[j] = _sortable_key(sc)
        return carry

    lax.fori_loop(0, nblk, score_block, 0)

    k_eff = jnp.minimum(DSA_TOPK, (qchunk + 1) * CHUNK)

    def count(pred):
        def blk(j, c):
            m = pred(key_ref[j], j * tk + row_k).astype(I32)
            return c + jnp.sum(m.reshape(tk // 8, 8, tq), axis=0)
        c = lax.fori_loop(0, nblk, blk, jnp.zeros((8, tq), I32))
        return jnp.sum(c, axis=0, keepdims=True)

    def bit_step(bi, prefix):
        cand_u = prefix | (jnp.int32(1) << (31 - bi))
        cand_s = cand_u ^ INT_MIN
        cnt = count(lambda key, kpos: key >= cand_s)
        return jnp.where(cnt >= k_eff, cand_u, prefix)

    prefix = lax.fori_loop(0, 32, bit_step, jnp.zeros((1, tq), I32))
    thr = prefix ^ INT_MIN

    c_gt = count(lambda key, kpos: key > thr)
    need = (k_eff - c_gt).astype(F32)
    tril = (lax.broadcasted_iota(I32, (DSA_TKA, DSA_TKA), 0) >= lax.broadcasted_iota(I32, (DSA_TKA, DSA_TKA), 1)).astype(BF16)

    q = q_ref[...]
    q_stack = jnp.concatenate([q[:, h * HEAD_DIM:(h + 1) * HEAD_DIM] for h in range(DSA_HEADS)], axis=0)
    qpos_f = qpos.astype(F32)

    tka = DSA_TKA
    row_a = lax.broadcasted_iota(I32, (tka, tq), 0)

    def att_block(j, carry):
        m_run, l_run, acc, ties_seen = carry
        kb = kv_ref[pl.ds(pl.multiple_of(j * tka, tka), tka), 0:HEAD_DIM]
        s = _dot_nt(kb, q_stack)
        kpos = j * tka + row_a
        key = key_ref[j // (tk // tka), pl.ds(pl.multiple_of((j % (tk // tka)) * tka, tka), tka), :]
        tie = key == thr
        tie_rank = _dot(tril, jnp.where(tie, 1.0, 0.0).astype(BF16)) + ties_seen
        sel = (key > thr) | (tie & (tie_rank <= need))
        ties_seen = tie_rank[tka - 1:tka, :]
        dist = jnp.abs(kpos.astype(F32) - qpos_f)
        ms, ls, ps, alphas = [], [], [], []
        for h in range(DSA_HEADS):
            sh = jnp.where(sel, s[:, h * tq:(h + 1) * tq] - SLOPES_B[h] * dist, -jnp.inf)
            m_old = m_run[:, h * tq:(h + 1) * tq]
            m_new = jnp.maximum(m_old, jnp.max(sh, axis=0, keepdims=True))
            alpha = jnp.exp(m_old - m_new)
            p = jnp.exp(sh - m_new)
            ms.append(m_new)
            ls.append(alpha * l_run[:, h * tq:(h + 1) * tq] + jnp.sum(p, axis=0, keepdims=True))
            ps.append(p.astype(BF16))
            alphas.append(alpha)
        pv = _dot(vt_ref[j], jnp.concatenate(ps, axis=1))
        acc = jnp.concatenate(alphas, axis=1) * acc + pv
        return jnp.concatenate(ms, axis=1), jnp.concatenate(ls, axis=1), acc, ties_seen

    init = (jnp.full((1, DSA_HEADS * tq), NEG_INF, F32), jnp.zeros((1, DSA_HEADS * tq), F32),
            jnp.zeros((HEAD_DIM, DSA_HEADS * tq), F32), jnp.zeros((1, tq), F32))
    _, l_run, acc, _ = lax.fori_loop(0, (i * tq + tq + tka - 1) // tka, att_block, init)
    out = acc / l_run
    for h in range(DSA_HEADS):
        o_ref[:, h * HEAD_DIM:(h + 1) * HEAD_DIM] = out[:, h * tq:(h + 1) * tq].T.astype(BF16)


def _dsa(b_q, b_qi, small, b_kv, b_ki, bn, t):
    nt = t // DSA_TQ
    nkb = t // DSA_TK
    nka = t // DSA_TKA
    v_t = b_kv[:, HEAD_DIM:].reshape(bn * nka, DSA_TKA, HEAD_DIM).transpose(0, 2, 1)
    return pl.pallas_call(
        _dsa_kernel,
        grid=(bn, nt),
        in_specs=[
            pl.BlockSpec((DSA_TQ, 256), lambda b, i: (b * nt + i, 0)),
            pl.BlockSpec((DSA_TQ, 128), lambda b, i: (b * nt + i, 0)),
            pl.BlockSpec((DSA_TQ, 128), lambda b, i: (b * nt + i, 0)),
            pl.BlockSpec((t, 128), lambda b, i: (b, 0)),
            pl.BlockSpec((nka, HEAD_DIM, DSA_TKA), lambda b, i: (b, 0, 0)),
            pl.BlockSpec((t, 128), lambda b, i: (b, 0)),
        ],
        out_specs=pl.BlockSpec((DSA_TQ, 256), lambda b, i: (b * nt + i, 0)),
        out_shape=jax.ShapeDtypeStruct((bn * t, 256), BF16),
        scratch_shapes=[pltpu.VMEM((nkb, DSA_TK, DSA_TQ), I32)],
        compiler_params=_cparams(("arbitrary", "arbitrary")),
        name="dsa_attention",
    )(b_q, b_qi, small, b_kv, v_t, b_ki)


def _mlstm_kernel(qk_ref, v_ref, og_ref, sm_ref, convw_ref, convb_ref, gbias_ref, normg_ref,
                  o_ref, tail_ref, cmat_ref, nvec_ref, mst_ref, *, bn):
    c = pl.program_id(0)
    L = CHUNK

    @pl.when(c == 0)
    def _():
        tail_ref[...] = jnp.zeros_like(tail_ref)
        cmat_ref[...] = jnp.zeros_like(cmat_ref)
        nvec_ref[...] = jnp.zeros_like(nvec_ref)
        mst_ref[...] = jnp.zeros_like(mst_ref)

    row = lax.broadcasted_iota(I32, (L, L), 0)
    col = lax.broadcasted_iota(I32, (L, L), 1)
    causal = col <= row
    tril = causal.astype(F32)
    convw = convw_ref[...]
    for b in range(bn):
        cur = qk_ref[b]
        ext = jnp.concatenate([tail_ref[b], cur], axis=0)
        tail_ref[b] = cur[L - 8:L, :]
        y = convb_ref[...] + convw[MLSTM_CONV - 1:MLSTM_CONV, :] * cur
        for k in range(MLSTM_CONV - 1):
            off = 8 - (MLSTM_CONV - 1) + k
            y = y + convw[k:k + 1, :] * ext[off:off + L, :]
        qk = y * _sigmoid(y)
        gates = sm_ref[b] + gbias_ref[...]
        lf = jnp.minimum(gates, 0.0) - jnp.log(1.0 + jnp.exp(-jnp.abs(gates)))
        bcum = jnp.dot(tril, lf, preferred_element_type=F32, precision=lax.Precision.HIGHEST)
        gates_t = gates.T
        bcum_t = bcum.T
        vb = v_ref[b]
        og = og_ref[b]
        for h in range(MLSTM_HEADS):
            s_idx = b * MLSTM_HEADS + h
            q = qk[:, h * HEAD_DIM:(h + 1) * HEAD_DIM]
            k = qk[:, (MLSTM_HEADS + h) * HEAD_DIM:(MLSTM_HEADS + h + 1) * HEAD_DIM] * HEAD_DIM ** -0.5
            v = vb[:, h * HEAD_DIM:(h + 1) * HEAD_DIM]
            qb, kb = q.astype(BF16), k.astype(BF16)
            b_col = bcum[:, SM_FG + h:SM_FG + h + 1]
            b_row = bcum_t[SM_FG + h:SM_FG + h + 1, :]
            ig_col = gates[:, SM_IG + h:SM_IG + h + 1]
            ig_row = gates_t[SM_IG + h:SM_IG + h + 1, :]
            m_prev = mst_ref[s_idx]
            cmat = cmat_ref[s_idx]
            nvec = nvec_ref[s_idx]
            dlog = jnp.where(causal, b_col - b_row + ig_row, NEG_INF)
            inter = b_col + m_prev
            mj = jnp.maximum(inter, jnp.max(dlog, axis=-1, keepdims=True))
            dw = jnp.exp(dlog - mj)
            iw = jnp.exp(inter - mj)
            sc = _dot_nt(qb, kb) * dw
            num = iw * _dot(qb, cmat.astype(BF16)) + _dot(sc.astype(BF16), v)
            den = iw * jnp.sum(q * nvec, axis=-1, keepdims=True) + jnp.sum(sc, axis=-1, keepdims=True)
            hj = num / jnp.maximum(jnp.abs(den), jnp.exp(-mj))
            bl = b_col[L - 1:L, :]
            dec_row = bl - b_row + ig_row
            dec_col = bl - b_col + ig_col
            m_new = jnp.maximum(bl + m_prev, jnp.max(dec_row, axis=-1, keepdims=True))
            wc = jnp.exp(bl + m_prev - m_new)
            ws = jnp.exp(dec_col - m_new)
            kw = k * ws
            cmat_ref[s_idx] = wc * cmat + _dot(kw.T.astype(BF16), v)
            nvec_ref[s_idx] = wc * nvec + jnp.sum(kw, axis=0, keepdims=True)
            mst_ref[s_idx] = m_new
            mu = jnp.mean(hj, axis=-1, keepdims=True)
            dh = hj - mu
            var = jnp.mean(dh * dh, axis=-1, keepdims=True)
            hn = dh * lax.rsqrt(var + LN_EPS) * normg_ref[:, h * HEAD_DIM:(h + 1) * HEAD_DIM]
            o = _sigmoid(og[:, h * HEAD_DIM:(h + 1) * HEAD_DIM]) * hn
            o_ref[b, :, h * HEAD_DIM:(h + 1) * HEAD_DIM] = o.astype(BF16)


def _mlstm(c_qk, c_v, c_o, small, conv_w, conv_b, gate_bias, norm_g, bn, t):
    nc = t // CHUNK
    nstate = bn * MLSTM_HEADS
    blk = lambda w: pl.BlockSpec((bn, CHUNK, w), lambda c: (0, c, 0))
    full = lambda a: pl.BlockSpec(a.shape, lambda c: (0,) * a.ndim)
    return pl.pallas_call(
        functools.partial(_mlstm_kernel, bn=bn),
        grid=(nc,),
        in_specs=[blk(512), blk(256), blk(256), blk(128), full(conv_w), full(conv_b), full(gate_bias), full(norm_g)],
        out_specs=blk(256),
        out_shape=jax.ShapeDtypeStruct((bn, t, 256), BF16),
        scratch_shapes=[pltpu.VMEM((bn, 8, 512), F32), pltpu.VMEM((nstate, HEAD_DIM, HEAD_DIM), F32),
                        pltpu.VMEM((nstate, 1, HEAD_DIM), F32), pltpu.VMEM((nstate, 1, 1), F32)],
        compiler_params=_cparams(("arbitrary",)),
        name="mlstm",
    )(c_qk.reshape(bn, t, 512), c_v.reshape(bn, t, 256), c_o.reshape(bn, t, 256), small.reshape(bn, t, 128),
      conv_w, conv_b, gate_bias, norm_g)


def _s5_kernel(u_ref, bbre_ref, bbim_ref, are_ref, aim_ref, cre_ref, cim_ref, dskip_ref, gluw_ref, glub_ref,
               o_ref, sre_ref, sim_ref, stre_ref, stim_ref):
    @pl.when(pl.program_id(1) == 0)
    def _():
        stre_ref[...] = jnp.zeros_like(stre_ref)
        stim_ref[...] = jnp.zeros_like(stim_ref)

    u = u_ref[...]
    ub = u.astype(BF16)
    sre_ref[...] = _dot(ub, bbre_ref[...])
    sim_ref[...] = _dot(ub, bbim_ref[...])
    a_re = jnp.broadcast_to(are_ref[...], (8, S5_LANES))
    a_im = jnp.broadcast_to(aim_ref[...], (8, S5_LANES))
    row = lax.broadcasted_iota(I32, (8, S5_LANES), 0)

    def group(i, carry):
        s_re, s_im = carry
        base = pl.multiple_of(i * 8, 8)
        bu_re = sre_ref[pl.ds(base, 8), :]
        bu_im = sim_ref[pl.ds(base, 8), :]
        out_re, out_im = bu_re, bu_im
        for r in range(8):
            n_re = a_re * s_re - a_im * s_im + bu_re
            n_im = a_re * s_im + a_im * s_re + bu_im
            out_re = jnp.where(row == r, n_re, out_re)
            out_im = jnp.where(row == r, n_im, out_im)
            s_re = jnp.broadcast_to(n_re[r:r + 1, :], (8, S5_LANES))
            s_im = jnp.broadcast_to(n_im[r:r + 1, :], (8, S5_LANES))
        sre_ref[pl.ds(base, 8), :] = out_re
        sim_ref[pl.ds(base, 8), :] = out_im
        return s_re, s_im

    s_re, s_im = lax.fori_loop(0, S5_TT // 8, group, (stre_ref[...], stim_ref[...]))
    stre_ref[...] = s_re
    stim_ref[...] = s_im
    y = (_dot(sre_ref[...].astype(BF16), cre_ref[...]) - _dot(sim_ref[...].astype(BF16), cim_ref[...])
         + dskip_ref[...] * u)
    y = 0.5 * y * (1.0 + jnp.tanh(math.sqrt(2.0 / math.pi) * (y + 0.044715 * (y * y * y))))
    z = _dot(y.astype(BF16), gluw_ref[...]) + glub_ref[...]
    o_ref[...] = (y * _sigmoid(z)).astype(BF16)


def _s5(d_u, bb_re, bb_im, a_re, a_im, c_re_t, c_im_t, d_skip, glu_w, glu_b, bn, t):
    nt = t // S5_TT
    full = lambda a: pl.BlockSpec(a.shape, lambda b, i: (0,) * a.ndim)
    args = (bb_re, bb_im, a_re, a_im, c_re_t, c_im_t, d_skip, glu_w, glu_b)
    return pl.pallas_call(
        _s5_kernel,
        grid=(bn, nt),
        in_specs=[pl.BlockSpec((S5_TT, 256), lambda b, i: (b * nt + i, 0))] + [full(a) for a in args],
        out_specs=pl.BlockSpec((S5_TT, 256), lambda b, i: (b * nt + i, 0)),
        out_shape=jax.ShapeDtypeStruct((bn * t, 256), BF16),
        scratch_shapes=[pltpu.VMEM((S5_TT, S5_LANES), F32), pltpu.VMEM((S5_TT, S5_LANES), F32),
                        pltpu.VMEM((8, S5_LANES), F32), pltpu.VMEM((8, S5_LANES), F32)],
        compiler_params=_cparams(("arbitrary", "arbitrary")),
        name="s5_glu",
    )(d_u, *args)


def _outproj_kernel(x_ref, mod_ref, oa_ref, ob_ref, oc_ref, od_ref, w_ref, lng_ref, lnb_ref, o_ref):
    x = x_ref[...]
    gate = mod_ref[2:3, :]
    y = _dot(oa_ref[...], w_ref[0])
    y = y + _dot(ob_ref[...], w_ref[1])
    y = y + _dot(oc_ref[...], w_ref[2])
    y = y + _dot(od_ref[...], w_ref[3])
    o_ref[...] = _residual_layer_norm(x, gate * y, lng_ref[...], lnb_ref[...])


def _outproj(x2, mod, o_a, o_b, o_c, o_d, w_out4, ln_g, ln_b, rows_per_batch):
    n = x2.shape[0]
    tm = ROW_TILE
    tiles_per_batch = rows_per_batch // tm
    mix = pl.BlockSpec((tm, GROUP_WIDTH), lambda i: (i, 0))
    return pl.pallas_call(
        _outproj_kernel,
        grid=(n // tm,),
        in_specs=[
            pl.BlockSpec((tm, D_MODEL), lambda i: (i, 0)),
            pl.BlockSpec((None, 8, D_MODEL), lambda i: (i // tiles_per_batch, 0, 0)),
            mix, mix, mix, mix,
            pl.BlockSpec(w_out4.shape, lambda i: (0, 0, 0), pipeline_mode=pl.Buffered(1)),
            pl.BlockSpec((1, D_MODEL), lambda i: (0, 0)),
            pl.BlockSpec((1, D_MODEL), lambda i: (0, 0)),
        ],
        out_specs=pl.BlockSpec((tm, D_MODEL), lambda i: (i, 0)),
        out_shape=jax.ShapeDtypeStruct((n, D_MODEL), F32),
        compiler_params=_cparams(("arbitrary",)),
        name="mixer_outproj",
    )(x2, mod, o_a, o_b, o_c, o_d, w_out4, ln_g.reshape(1, -1), ln_b.reshape(1, -1))


def _permute_w_in(w_in):
    off = {}
    o = 0
    for name, s in (("qa", 256), ("ka", 128), ("va", 128), ("cq", 128), ("kb", 64), ("vb", 64), ("ki", 32),
                    ("wi", 4), ("qkc", 512), ("vc", 256), ("ig", 4), ("fg", 4), ("oc", 256), ("ud", 256)):
        off[name] = (o, o + s)
        o += s
    col = lambda n: w_in[:, off[n][0]:off[n][1]]
    small = jnp.concatenate([col("wi"), col("ig"), col("fg"),
                             jnp.zeros((w_in.shape[0], 128 - 12), w_in.dtype)], axis=1)
    parts = [col("qa"), col("ka"), col("va"), col("cq"), col("kb"), col("vb")] + [col("ki")] * 4 + [
        col("qkc"), col("vc"), col("oc"), col("ud"), small]
    return jnp.concatenate(parts, axis=1).astype(BF16)


def _s5_params(lam_re, lam_im, log_step, b_re, b_im, c_re, c_im):
    dt = jnp.exp(log_step)[:, None]
    mag = jnp.exp(lam_re * dt)
    a_re, a_im = mag * jnp.cos(lam_im * dt), mag * jnp.sin(lam_im * dt)
    den = lam_re * lam_re + lam_im * lam_im
    kap_re = ((a_re - 1.0) * lam_re + a_im * lam_im) / den
    kap_im = (a_im * lam_re - (a_re - 1.0) * lam_im) / den
    bb_re = kap_re[..., None] * b_re - kap_im[..., None] * b_im
    bb_im = kap_re[..., None] * b_im + kap_im[..., None] * b_re
    eye = jnp.eye(S5_GROUPS, dtype=F32)

    def in_mat(bb):
        return jnp.einsum("gph,gk->ghkp", bb, eye).reshape(S5_GROUPS * S5_GROUP_CH, S5_LANES).astype(BF16)

    def out_mat(cc):
        return jnp.einsum("gop,gk->gpko", cc, eye).reshape(S5_LANES, S5_GROUPS * S5_GROUP_CH).astype(BF16)

    return (in_mat(bb_re), in_mat(bb_im), a_re.reshape(1, S5_LANES), a_im.reshape(1, S5_LANES),
            out_mat(c_re), out_mat(c_im))


def kernel(x, c, ada_w, ada_b, ln_g, ln_b, ffn_w13, ffn_w2, w_in, w_out, sinks, w_uq, w_iq, conv_w, conv_b, ig_b,
           fg_b, mh_norm_g, lam_re, lam_im, log_step, b_re, b_im, c_re, c_im, d_skip, glu_w, glu_b):
    bn, t, d = x.shape
    assert d == D_MODEL and t % max(ROW_TILE, DSA_TK, S5_TT) == 0 and bn <= 8
    n = bn * t
    nl = ada_w.shape[0]
    c_pad = jnp.zeros((8, d), F32).at[:bn].set(c)
    mod_all = _ada_mod(c_pad, ada_w, ada_b)
    mod_all = mod_all[:, :bn].reshape(nl, bn, N_SUB, 3, d).transpose(0, 2, 1, 3, 4)
    mod_all = jnp.pad(mod_all, ((0, 0), (0, 0), (0, 0), (0, 5), (0, 0)))

    x2 = x.reshape(n, d)
    for l in range(nl):
        w13c = ffn_w13[l].astype(BF16).reshape(2, d, 2, FFN_NCHUNK, FFN_TF).transpose(0, 2, 3, 1, 4)
        w2c = ffn_w2[l].astype(BF16).reshape(2, FFN_NCHUNK, FFN_TF, d)
        x2 = _ffn_sublayer(x2, mod_all[l, 0], w13c[0], w2c[0], ln_g[l, 0], ln_b[l, 0], t)
        (a_q, a_kv, b_q, b_qi, b_kv, b_ki, c_qk, c_v, c_o, d_u, small) = _inproj(
            x2, mod_all[l, 1], _permute_w_in(w_in[l]), w_uq[l].astype(BF16), w_iq[l].astype(BF16), t)
        o_a = _swa(sinks[l], a_q, a_kv, bn, t)
        o_b = _dsa(b_q, b_qi, small, b_kv, b_ki, bn, t)
        gate_bias = jnp.zeros((1, 128), F32).at[0, SM_IG:SM_IG + 4].set(ig_b[l]).at[0, SM_FG:SM_FG + 4].set(fg_b[l])
        o_c = _mlstm(c_qk, c_v, c_o, small, conv_w[l], conv_b[l].reshape(1, -1), gate_bias,
                     mh_norm_g[l].reshape(1, -1), bn, t).reshape(n, GROUP_WIDTH)
        s5p = _s5_params(lam_re[l], lam_im[l], log_step[l], b_re[l], b_im[l], c_re[l], c_im[l])
        o_d = _s5(d_u, *s5p, d_skip[l].reshape(1, -1), glu_w[l].astype(BF16), glu_b[l].reshape(1, -1), bn, t)
        x2 = _outproj(x2, mod_all[l, 1], o_a, o_b, o_c, o_d,
                      w_out[l].astype(BF16).reshape(4, GROUP_WIDTH, d), ln_g[l, 1], ln_b[l, 1], t)
        x2 = _ffn_sublayer(x2, mod_all[l, 2], w13c[1], w2c[1], ln_g[l, 2], ln_b[l, 2], t)
    return x2.reshape(bn, t, d)
```

```python
import functools
import math

import jax
import jax.numpy as jnp
from jax import lax
from jax.experimental import pallas as pl
from jax.experimental.pallas import tpu as pltpu

F32 = jnp.float32
BF16 = jnp.bfloat16
I32 = jnp.int32

D_MODEL = 1024
DEPTH = 2
CHUNK = 64
HEAD_DIM = 64
GROUP_WIDTH = 256
SWA_HEADS = 4
SWA_KV_HEADS = 2
SWA_WIN_CHUNKS = 2
DSA_HEADS = 4
DSA_Q_RANK = 128
IDX_HEADS = 4
IDX_DIM = 32
DSA_TOPK = 256
MLSTM_HEADS = 4
MLSTM_CONV = 4
S5_GROUP_CH = 16
S5_GROUPS = 16
S5_STATE = 64
S5_LANES = S5_GROUPS * S5_STATE
D_FF = 2816
N_SUB = 3
ALPHA = (2 * DEPTH) ** 0.25
LN_EPS = 1e-5
NEG_INF = -1e30
INT_MIN = -(2 ** 31)
MIN_NORMAL_F32 = 2.0 ** -126

SLOPES_A = tuple(2.0 ** -(i + 1) for i in range(0, 8, 2))
SLOPES_B = tuple(2.0 ** -(i + 1) for i in range(1, 8, 2))

VMEM_LIMIT_BYTES = 56 * 1024 * 1024

FFN_TF = 256
FFN_NCHUNK = D_FF // FFN_TF
ROW_TILE = 512
SWA_TQ = 128
DSA_TQ = 128
DSA_TK = 512
DSA_TKA = 512
S5_TT = 512
S5_SEGS = 8

ZC_QA, ZC_KVA, ZC_CQ, ZC_KVB, ZC_KI = 0, 256, 512, 640, 768
ZC_QKC, ZC_VC, ZC_OC, ZC_UD, ZC_SMALL, Z_WIDTH = 896, 1408, 1664, 1920, 2176, 2304
SM_WI, SM_IG, SM_FG = 0, 4, 8


def _cparams(sem):
    return pltpu.CompilerParams(dimension_semantics=sem, vmem_limit_bytes=VMEM_LIMIT_BYTES)


def _dot(a, b):
    return jnp.dot(a, b, preferred_element_type=F32)


def _dot_nt(a, b):
    return lax.dot_general(a, b, (((1,), (1,)), ((), ())), preferred_element_type=F32)


def _sigmoid(x):
    return 1.0 / (1.0 + jnp.exp(-x))


def _residual_layer_norm(x, y, g, b):
    v = ALPHA * x + y
    mu = jnp.mean(v, axis=-1, keepdims=True)
    d = v - mu
    var = jnp.mean(d * d, axis=-1, keepdims=True)
    return d * lax.rsqrt(var + LN_EPS) * g + b


def _ada_kernel(c_ref, w_ref, b_ref, o_ref):
    c = c_ref[...]
    cs = c * _sigmoid(c)
    o_ref[...] = jnp.dot(cs, w_ref[...], preferred_element_type=F32,
                         precision=lax.Precision.HIGHEST) + b_ref[...]


def _ada_mod(c_pad, ada_w, ada_b):
    nl = ada_w.shape[0]
    ncol = ada_w.shape[2] // D_MODEL
    return pl.pallas_call(
        _ada_kernel,
        grid=(nl, ncol),
        in_specs=[
            pl.BlockSpec((8, D_MODEL), lambda l, j: (0, 0)),
            pl.BlockSpec((None, D_MODEL, D_MODEL), lambda l, j: (l, 0, j)),
            pl.BlockSpec((None, 1, D_MODEL), lambda l, j: (l, 0, j)),
        ],
        out_specs=pl.BlockSpec((None, 8, D_MODEL), lambda l, j: (l, 0, j)),
        out_shape=jax.ShapeDtypeStruct((nl, 8, ada_w.shape[2]), F32),
        compiler_params=_cparams(("arbitrary", "arbitrary")),
        name="ada_mod",
    )(c_pad, ada_w, ada_b.reshape(nl, 1, -1))


def _ffn_kernel(x_ref, mod_ref, w13_ref, w2_ref, lng_ref, lnb_ref, o_ref, acc_ref):
    x = x_ref[...]
    shift, scale, gate = mod_ref[0:1, :], mod_ref[1:2, :], mod_ref[2:3, :]
    u = (x * (1.0 + scale) + shift).astype(BF16)
    acc_ref[...] = jnp.zeros_like(acc_ref)

    def chunk(j, carry):
        a = _dot(u, w13_ref[0, j])
        g = _dot(u, w13_ref[1, j])
        h = (a * _sigmoid(a) * g).astype(BF16)
        acc_ref[...] += _dot(h, w2_ref[j])
        return carry

    lax.fori_loop(0, FFN_NCHUNK, chunk, 0)
    o_ref[...] = _residual_layer_norm(x, 0.5 * gate * acc_ref[...], lng_ref[...], lnb_ref[...])


def _ffn_sublayer(x2, mod, w13c, w2c, ln_g, ln_b, rows_per_batch):
    n = x2.shape[0]
    tm = ROW_TILE
    tiles_per_batch = rows_per_batch // tm
    return pl.pallas_call(
        _ffn_kernel,
        grid=(n // tm,),
        in_specs=[
            pl.BlockSpec((tm, D_MODEL), lambda i: (i, 0)),
            pl.BlockSpec((None, 8, D_MODEL), lambda i: (i // tiles_per_batch, 0, 0)),
            pl.BlockSpec(w13c.shape, lambda i: (0, 0, 0, 0), pipeline_mode=pl.Buffered(1)),
            pl.BlockSpec(w2c.shape, lambda i: (0, 0, 0), pipeline_mode=pl.Buffered(1)),
            pl.BlockSpec((1, D_MODEL), lambda i: (0, 0)),
            pl.BlockSpec((1, D_MODEL), lambda i: (0, 0)),
        ],
        out_specs=pl.BlockSpec((tm, D_MODEL), lambda i: (i, 0)),
        out_shape=jax.ShapeDtypeStruct((n, D_MODEL), F32),
        scratch_shapes=[pltpu.VMEM((tm, D_MODEL), F32)],
        compiler_params=_cparams(("arbitrary",)),
        name="ffn_sublayer",
    )(x2, mod, w13c, w2c, ln_g.reshape(1, -1), ln_b.reshape(1, -1))


def _inproj_kernel(x_ref, mod_ref, w_ref, wuq_ref, wiq_ref,
                   aq_ref, akv_ref, bq_ref, bqi_ref, bkv_ref, bki_ref,
                   cqk_ref, cv_ref, co_ref, du_ref, sm_ref):
    x = x_ref[...]
    shift, scale = mod_ref[0:1, :], mod_ref[1:2, :]
    u = (x * (1.0 + scale) + shift).astype(BF16)
    z = _dot(u, w_ref[...])
    aq_ref[...] = z[:, ZC_QA:ZC_KVA].astype(BF16)
    akv_ref[...] = z[:, ZC_KVA:ZC_CQ].astype(BF16)
    cq = z[:, ZC_CQ:ZC_KVB].astype(BF16)
    bq_ref[...] = (_dot(cq, wuq_ref[...]) * HEAD_DIM ** -0.5).astype(BF16)
    bqi_ref[...] = _dot(cq, wiq_ref[...]).astype(BF16)
    bkv_ref[...] = z[:, ZC_KVB:ZC_KI].astype(BF16)
    bki_ref[...] = z[:, ZC_KI:ZC_QKC].astype(BF16)
    cqk_ref[...] = z[:, ZC_QKC:ZC_VC]
    cv_ref[...] = z[:, ZC_VC:ZC_OC].astype(BF16)
    co_ref[...] = z[:, ZC_OC:ZC_UD]
    du_ref[...] = z[:, ZC_UD:ZC_SMALL]
    sm_ref[...] = z[:, ZC_SMALL:Z_WIDTH]


def _inproj(x2, mod, w_perm, w_uq, w_iq, rows_per_batch):
    n = x2.shape[0]
    tm = ROW_TILE
    tiles_per_batch = rows_per_batch // tm
    widths = [(256, BF16), (256, BF16), (256, BF16), (128, BF16), (128, BF16), (128, BF16),
              (512, F32), (256, BF16), (256, F32), (256, F32), (128, F32)]
    return pl.pallas_call(
        _inproj_kernel,
        grid=(n // tm,),
        in_specs=[
            pl.BlockSpec((tm, D_MODEL), lambda i: (i, 0)),
            pl.BlockSpec((None, 8, D_MODEL), lambda i: (i // tiles_per_batch, 0, 0)),
            pl.BlockSpec(w_perm.shape, lambda i: (0, 0), pipeline_mode=pl.Buffered(1)),
            pl.BlockSpec(w_uq.shape, lambda i: (0, 0), pipeline_mode=pl.Buffered(1)),
            pl.BlockSpec(w_iq.shape, lambda i: (0, 0), pipeline_mode=pl.Buffered(1)),
        ],
        out_specs=[pl.BlockSpec((tm, w), lambda i: (i, 0)) for w, _ in widths],
        out_shape=[jax.ShapeDtypeStruct((n, w), dt) for w, dt in widths],
        compiler_params=_cparams(("arbitrary",)),
        name="mixer_inproj",
    )(x2, mod, w_perm, w_uq, w_iq)


def _swa_kernel(sink_ref, q_ref, kvc_ref, kvp_ref, o_ref):
    i = pl.program_id(1)
    tq = SWA_TQ
    q = q_ref[...]
    kv = jnp.concatenate([kvp_ref[...], kvc_ref[...]], axis=0)
    qpos = i * tq + lax.broadcasted_iota(I32, (tq, 2 * tq), 0)
    kpos = (i - 1) * tq + lax.broadcasted_iota(I32, (tq, 2 * tq), 1)
    qchunk = qpos // CHUNK
    kchunk = (kpos + tq) // CHUNK - tq // CHUNK
    valid = (kpos >= 0) & (kchunk <= qchunk) & (kchunk >= qchunk - SWA_WIN_CHUNKS)
    dist = jnp.abs(qpos - kpos).astype(F32)
    rep = SWA_HEADS // SWA_KV_HEADS
    for h in range(SWA_HEADS):
        g = h // rep
        qh = q[:, h * HEAD_DIM:(h + 1) * HEAD_DIM]
        kg = kv[:, g * HEAD_DIM:(g + 1) * HEAD_DIM]
        vg = kv[:, (SWA_KV_HEADS + g) * HEAD_DIM:(SWA_KV_HEADS + g + 1) * HEAD_DIM]
        s = _dot_nt(qh, kg) * HEAD_DIM ** -0.5 - SLOPES_A[h] * dist
        s = jnp.where(valid, s, NEG_INF)
        sink = sink_ref[h]
        m = jnp.maximum(jnp.max(s, axis=-1, keepdims=True), sink)
        p = jnp.exp(s - m)
        denom = jnp.sum(p, axis=-1, keepdims=True) + jnp.exp(sink - m)
        o = _dot(p.astype(BF16), vg) / denom
        o_ref[:, h * HEAD_DIM:(h + 1) * HEAD_DIM] = o.astype(BF16)


def _swa(sinks, a_q, a_kv, bn, t):
    nt = t // SWA_TQ
    return pl.pallas_call(
        _swa_kernel,
        grid=(bn, nt),
        in_specs=[
            pl.BlockSpec(memory_space=pltpu.SMEM),
            pl.BlockSpec((SWA_TQ, 256), lambda b, i: (b * nt + i, 0)),
            pl.BlockSpec((SWA_TQ, 256), lambda b, i: (b * nt + i, 0)),
            pl.BlockSpec((SWA_TQ, 256), lambda b, i: (b * nt + jnp.maximum(i - 1, 0), 0)),
        ],
        out_specs=pl.BlockSpec((SWA_TQ, 256), lambda b, i: (b * nt + i, 0)),
        out_shape=jax.ShapeDtypeStruct((bn * t, 256), BF16),
        compiler_params=_cparams(("arbitrary", "arbitrary")),
        name="swa_attention",
    )(sinks, a_q, a_kv, a_kv)


def _sortable_key(x):
    bits = lax.bitcast_convert_type(x, I32)
    return bits ^ ((bits >> 31) & 0x7FFFFFFF)


def _dsa_kernel(q_ref, qi_ref, sm_ref, kv_ref, vt_ref, ki_ref, o_ref, key_ref, top_ref):
    i = pl.program_id(1)
    tq, tk, tka = DSA_TQ, DSA_TK, DSA_TKA
    nblk = (i * tq + tq + tk - 1) // tk
    nblk_a = (i * tq + tq + tka - 1) // tka
    qpos = i * tq + lax.broadcasted_iota(I32, (1, tq), 1)
    qchunk = qpos // CHUNK
    row_k = lax.broadcasted_iota(I32, (tk, tq), 0)

    qi = qi_ref[...]
    lane_i = lax.broadcasted_iota(I32, (1, IDX_HEADS * IDX_DIM), 1) // IDX_DIM
    qi_stack = jnp.concatenate([jnp.where(lane_i == h, qi, jnp.zeros_like(qi)) for h in range(IDX_HEADS)], axis=0)
    sm_t = sm_ref[...].T
    w_idx = [sm_t[SM_WI + h:SM_WI + h + 1, :] for h in range(IDX_HEADS)]
    idx_scale = (IDX_DIM * IDX_HEADS) ** -0.5

    def score_block(j, carry):
        kib = ki_ref[pl.ds(pl.multiple_of(j * tk, tk), tk), :]
        d = _dot_nt(kib, qi_stack)
        acc = w_idx[0] * jnp.maximum(d[:, 0:tq], 0.0)
        for h in range(1, IDX_HEADS):
            acc = acc + w_idx[h] * jnp.maximum(d[:, h * tq:(h + 1) * tq], 0.0)
        sc = acc * idx_scale
        sc = jnp.where(sc == 0.0, 0.0, sc)
        kpos = j * tk + row_k
        sc = jnp.where(kpos // CHUNK <= qchunk, sc, NEG_INF)
        key_ref[j] = _sortable_key(sc)
        top = lax.bitcast_convert_type(lax.bitcast_convert_type(sc, I32) & jnp.int32(-65536), F32)
        top_ref[j] = top.astype(BF16)
        return carry

    lax.fori_loop(0, nblk_a * (tka // tk), score_block, 0)

    k_eff = jnp.minimum(DSA_TOPK, (qchunk + 1) * CHUNK)

    def count(pred):
        def blk(j, c):
            m = pred(key_ref[j], j * tk + row_k).astype(I32)
            return c + jnp.sum(m.reshape(tk // 8, 8, tq), axis=0)
        c = lax.fori_loop(0, nblk, blk, jnp.zeros((8, tq), I32))
        return jnp.sum(c, axis=0, keepdims=True)

    one_b, zero_b = jnp.ones((tk, tq), BF16), jnp.zeros((tk, tq), BF16)

    def count_top(cand_b):
        def blk(j, c):
            m = jnp.where(top_ref[j] >= cand_b, one_b, zero_b)
            parts = [m[r * 16:(r + 1) * 16] for r in range(tk // 16)]
            while len(parts) > 1:
                parts = [parts[a] + parts[a + 1] for a in range(0, len(parts), 2)]
            return c + parts[0].astype(F32)
        c = lax.fori_loop(0, nblk, blk, jnp.zeros((16, tq), F32))
        return jnp.sum(c, axis=0, keepdims=True).astype(I32)

    def top_step(bi, prefix):
        cand_u = prefix | (jnp.int32(1) << (15 - bi))
        cand_s = cand_u - 32768
        cand_f = lax.bitcast_convert_type((cand_s ^ ((cand_s >> 15) & 0x7FFF)) << 16, F32)
        cand_f = jnp.where((cand_f != cand_f) & (cand_s < 0), -jnp.inf, cand_f)
        cand_f = jnp.where((cand_s >= 1) & (cand_s < 128), MIN_NORMAL_F32, cand_f)
        cnt = count_top(cand_f.astype(BF16))
        return jnp.where(cnt >= k_eff, cand_u, prefix)

    prefix16 = lax.fori_loop(0, 16, top_step, jnp.zeros((1, tq), I32))

    def bit_step(bi, prefix):
        cand_u = prefix | (jnp.int32(1) << (15 - bi))
        cand_s = cand_u ^ INT_MIN
        cnt = count(lambda key, kpos: key >= cand_s)
        return jnp.where(cnt >= k_eff, cand_u, prefix)

    prefix = lax.fori_loop(0, 16, bit_step, prefix16 << 16)
    thr = prefix ^ INT_MIN

    c_gt = count(lambda key, kpos: key > thr)
    need = (k_eff - c_gt).astype(F32)
    tril = (lax.broadcasted_iota(I32, (tk, tk), 0) >= lax.broadcasted_iota(I32, (tk, tk), 1)).astype(BF16)

    q = q_ref[...]
    q_stack = jnp.concatenate([q[:, h * HEAD_DIM:(h + 1) * HEAD_DIM] for h in range(DSA_HEADS)], axis=0)
    qpos_f = qpos.astype(F32)

    row_a = lax.broadcasted_iota(I32, (tka, tq), 0)

    def att_block(j, carry):
        m_run, l_run, acc, ties_seen = carry
        kb = kv_ref[pl.ds(pl.multiple_of(j * tka, tka), tka), 0:HEAD_DIM]
        s = _dot_nt(kb, q_stack)
        kpos = j * tka + row_a
        keys, ranks = [], []
        for r in range(tka // tk):
            key_r = key_ref[j * (tka // tk) + r]
            rank_r = _dot(tril, jnp.where(key_r == thr, 1.0, 0.0).astype(BF16)) + ties_seen
            ties_seen = rank_r[tk - 1:tk, :]
            keys.append(key_r)
            ranks.append(rank_r)
        key = jnp.concatenate(keys, axis=0)
        sel = (key > thr) | ((key == thr) & (jnp.concatenate(ranks, axis=0) <= need))
        dist = jnp.abs(kpos.astype(F32) - qpos_f)
        ms, ls, ps, alphas = [], [], [], []
        for h in range(DSA_HEADS):
            sh = jnp.where(sel, s[:, h * tq:(h + 1) * tq] - SLOPES_B[h] * dist, -jnp.inf)
            m_old = m_run[:, h * tq:(h + 1) * tq]
            m_new = jnp.maximum(m_old, jnp.max(sh, axis=0, keepdims=True))
            alpha = jnp.exp(m_old - m_new)
            p = jnp.exp(sh - m_new)
            ms.append(m_new)
            ls.append(alpha * l_run[:, h * tq:(h + 1) * tq] + jnp.sum(p, axis=0, keepdims=True))
            ps.append(p.astype(BF16))
            alphas.append(alpha)
        pv = _dot(vt_ref[j], jnp.concatenate(ps, axis=1))
        acc = jnp.concatenate(alphas, axis=1) * acc + pv
        return jnp.concatenate(ms, axis=1), jnp.concatenate(ls, axis=1), acc, ties_seen

    init = (jnp.full((1, DSA_HEADS * tq), NEG_INF, F32), jnp.zeros((1, DSA_HEADS * tq), F32),
            jnp.zeros((HEAD_DIM, DSA_HEADS * tq), F32), jnp.zeros((1, tq), F32))
    _, l_run, acc, _ = lax.fori_loop(0, nblk_a, att_block, init)
    out = acc / l_run
    for h in range(DSA_HEADS):
        o_ref[:, h * HEAD_DIM:(h + 1) * HEAD_DIM] = out[:, h * tq:(h + 1) * tq].T.astype(BF16)


def _dsa(b_q, b_qi, small, b_kv, b_ki, bn, t):
    nt = t // DSA_TQ
    nkb = t // DSA_TK
    nka = t // DSA_TKA
    v_t = b_kv[:, HEAD_DIM:].reshape(bn * nka, DSA_TKA, HEAD_DIM).transpose(0, 2, 1)
    return pl.pallas_call(
        _dsa_kernel,
        grid=(bn, nt),
        in_specs=[
            pl.BlockSpec((DSA_TQ, 256), lambda b, i: (b * nt + i, 0)),
            pl.BlockSpec((DSA_TQ, 128), lambda b, i: (b * nt + i, 0)),
            pl.BlockSpec((DSA_TQ, 128), lambda b, i: (b * nt + i, 0)),
            pl.BlockSpec((t, 128), lambda b, i: (b, 0)),
            pl.BlockSpec((nka, HEAD_DIM, DSA_TKA), lambda b, i: (b, 0, 0)),
            pl.BlockSpec((t, 128), lambda b, i: (b, 0)),
        ],
        out_specs=pl.BlockSpec((DSA_TQ, 256), lambda b, i: (b * nt + i, 0)),
        out_shape=jax.ShapeDtypeStruct((bn * t, 256), BF16),
        scratch_shapes=[pltpu.VMEM((nkb, DSA_TK, DSA_TQ), I32), pltpu.VMEM((nkb, DSA_TK, DSA_TQ), BF16)],
        compiler_params=_cparams(("arbitrary", "arbitrary")),
        name="dsa_attention",
    )(b_q, b_qi, small, b_kv, v_t, b_ki)


def _mlstm_kernel(qk_ref, v_ref, og_ref, sm_ref, convw_ref, convb_ref, gbias_ref, normg_ref,
                  o_ref, tail_ref, cmat_ref, nvec_ref, mst_ref, *, bn):
    c = pl.program_id(0)
    L = CHUNK

    @pl.when(c == 0)
    def _():
        tail_ref[...] = jnp.zeros_like(tail_ref)
        cmat_ref[...] = jnp.zeros_like(cmat_ref)
        nvec_ref[...] = jnp.zeros_like(nvec_ref)
        mst_ref[...] = jnp.zeros_like(mst_ref)

    row = lax.broadcasted_iota(I32, (L, L), 0)
    col = lax.broadcasted_iota(I32, (L, L), 1)
    causal = col <= row
    tril = causal.astype(F32)
    convw = convw_ref[...]
    for b in range(bn):
        cur = qk_ref[b]
        ext = jnp.concatenate([tail_ref[b], cur], axis=0)
        tail_ref[b] = cur[L - 8:L, :]
        y = convb_ref[...] + convw[MLSTM_CONV - 1:MLSTM_CONV, :] * cur
        for k in range(MLSTM_CONV - 1):
            off = 8 - (MLSTM_CONV - 1) + k
            y = y + convw[k:k + 1, :] * ext[off:off + L, :]
        qk = y * _sigmoid(y)
        gates = sm_ref[b] + gbias_ref[...]
        lf = jnp.minimum(gates, 0.0) - jnp.log(1.0 + jnp.exp(-jnp.abs(gates)))
        bcum = jnp.dot(tril, lf, preferred_element_type=F32, precision=lax.Precision.HIGHEST)
        gates_t = gates.T
        bcum_t = bcum.T
        vb = v_ref[b]
        og = og_ref[b]
        for h in range(MLSTM_HEADS):
            s_idx = b * MLSTM_HEADS + h
            q = qk[:, h * HEAD_DIM:(h + 1) * HEAD_DIM]
            k = qk[:, (MLSTM_HEADS + h) * HEAD_DIM:(MLSTM_HEADS + h + 1) * HEAD_DIM] * HEAD_DIM ** -0.5
            v = vb[:, h * HEAD_DIM:(h + 1) * HEAD_DIM]
            qb, kb = q.astype(BF16), k.astype(BF16)
            b_col = bcum[:, SM_FG + h:SM_FG + h + 1]
            b_row = bcum_t[SM_FG + h:SM_FG + h + 1, :]
            ig_col = gates[:, SM_IG + h:SM_IG + h + 1]
            ig_row = gates_t[SM_IG + h:SM_IG + h + 1, :]
            m_prev = mst_ref[s_idx]
            cmat = cmat_ref[s_idx]
            nvec = nvec_ref[s_idx]
            dlog = jnp.where(causal, b_col - b_row + ig_row, NEG_INF)
            inter = b_col + m_prev
            mj = jnp.maximum(inter, jnp.max(dlog, axis=-1, keepdims=True))
            dw = jnp.exp(dlog - mj)
            iw = jnp.exp(inter - mj)
            sc = _dot_nt(qb, kb) * dw
            num = iw * _dot(qb, cmat.astype(BF16)) + _dot(sc.astype(BF16), v)
            den = iw * jnp.sum(q * nvec, axis=-1, keepdims=True) + jnp.sum(sc, axis=-1, keepdims=True)
            hj = num / jnp.maximum(jnp.abs(den), jnp.exp(-mj))
            bl = b_col[L - 1:L, :]
            dec_row = bl - b_row + ig_row
            dec_col = bl - b_col + ig_col
            m_new = jnp.maximum(bl + m_prev, jnp.max(dec_row, axis=-1, keepdims=True))
            wc = jnp.exp(bl + m_prev - m_new)
            ws = jnp.exp(dec_col - m_new)
            kw = k * ws
            cmat_ref[s_idx] = wc * cmat + _dot(kw.T.astype(BF16), v)
            nvec_ref[s_idx] = wc * nvec + jnp.sum(kw, axis=0, keepdims=True)
            mst_ref[s_idx] = m_new
            mu = jnp.mean(hj, axis=-1, keepdims=True)
            dh = hj - mu
            var = jnp.mean(dh * dh, axis=-1, keepdims=True)
            hn = dh * lax.rsqrt(var + LN_EPS) * normg_ref[:, h * HEAD_DIM:(h + 1) * HEAD_DIM]
            o = _sigmoid(og[:, h * HEAD_DIM:(h + 1) * HEAD_DIM]) * hn
            o_ref[b, :, h * HEAD_DIM:(h + 1) * HEAD_DIM] = o.astype(BF16)


def _mlstm(c_qk, c_v, c_o, small, conv_w, conv_b, gate_bias, norm_g, bn, t):
    nc = t // CHUNK
    nstate = bn * MLSTM_HEADS
    blk = lambda w: pl.BlockSpec((bn, CHUNK, w), lambda c: (0, c, 0))
    full = lambda a: pl.BlockSpec(a.shape, lambda c: (0,) * a.ndim)
    return pl.pallas_call(
        functools.partial(_mlstm_kernel, bn=bn),
        grid=(nc,),
        in_specs=[blk(512), blk(256), blk(256), blk(128), full(conv_w), full(conv_b), full(gate_bias), full(norm_g)],
        out_specs=blk(256),
        out_shape=jax.ShapeDtypeStruct((bn, t, 256), BF16),
        scratch_shapes=[pltpu.VMEM((bn, 8, 512), F32), pltpu.VMEM((nstate, HEAD_DIM, HEAD_DIM), F32),
                        pltpu.VMEM((nstate, 1, HEAD_DIM), F32), pltpu.VMEM((nstate, 1, 1), F32)],
        compiler_params=_cparams(("arbitrary",)),
        name="mlstm",
    )(c_qk.reshape(bn, t, 512), c_v.reshape(bn, t, 256), c_o.reshape(bn, t, 256), small.reshape(bn, t, 128),
      conv_w, conv_b, gate_bias, norm_g)


def _s5_kernel(u_ref, perm_ref, bbre_ref, bbim_ref, pwre_ref, pwim_ref, cre_ref, cim_ref, dskip_ref, gluw_ref,
               glub_ref, o_ref, sre_ref, sim_ref, stre_ref, stim_ref, yp_ref):
    @pl.when(pl.program_id(1) == 0)
    def _():
        stre_ref[...] = jnp.zeros_like(stre_ref)
        stim_ref[...] = jnp.zeros_like(stim_ref)

    seg_len = S5_TT // S5_SEGS
    u = u_ref[...]
    ub = _dot(perm_ref[...], u.astype(BF16)).astype(BF16)
    sre_ref[...] = _dot(ub, bbre_ref[...])
    sim_ref[...] = _dot(ub, bbim_ref[...])
    a_re = jnp.broadcast_to(pwre_ref[0:1, :], (S5_SEGS, S5_LANES))
    a_im = jnp.broadcast_to(pwim_ref[0:1, :], (S5_SEGS, S5_LANES))

    def local_step(i, carry):
        s_re, s_im = carry
        rows = pl.ds(pl.multiple_of(i * S5_SEGS, S5_SEGS), S5_SEGS)
        n_re = a_re * s_re - a_im * s_im + sre_ref[rows, :]
        n_im = a_re * s_im + a_im * s_re + sim_ref[rows, :]
        sre_ref[rows, :] = n_re
        sim_ref[rows, :] = n_im
        return n_re, n_im

    zeros = jnp.zeros((S5_SEGS, S5_LANES), F32)
    e_re, e_im = lax.fori_loop(0, seg_len, local_step, (zeros, zeros))

    al_re, al_im = pwre_ref[seg_len - 1:seg_len, :], pwim_ref[seg_len - 1:seg_len, :]
    c_re, c_im = stre_ref[0:1, :], stim_ref[0:1, :]
    cs_re, cs_im = [], []
    for k in range(S5_SEGS):
        cs_re.append(c_re)
        cs_im.append(c_im)
        c_re, c_im = (e_re[k:k + 1, :] + al_re * c_re - al_im * c_im,
                      e_im[k:k + 1, :] + al_re * c_im + al_im * c_re)
    stre_ref[...] = jnp.broadcast_to(c_re, stre_ref.shape)
    stim_ref[...] = jnp.broadcast_to(c_im, stim_ref.shape)
    cin_re = jnp.concatenate(cs_re, axis=0)
    cin_im = jnp.concatenate(cs_im, axis=0)

    def correct_step(i, carry):
        rows = pl.ds(pl.multiple_of(i * S5_SEGS, S5_SEGS), S5_SEGS)
        p_re, p_im = pwre_ref[pl.ds(i, 1), :], pwim_ref[pl.ds(i, 1), :]
        sre_ref[rows, :] = sre_ref[rows, :] + p_re * cin_re - p_im * cin_im
        sim_ref[rows, :] = sim_ref[rows, :] + p_re * cin_im + p_im * cin_re
        return carry

    lax.fori_loop(0, seg_len, correct_step, 0)
    ycs = _dot(sre_ref[...].astype(BF16), cre_ref[...]) - _dot(sim_ref[...].astype(BF16), cim_ref[...])
    ngrp = GROUP_WIDTH // 128
    for g in range(ngrp):
        yp_ref[g] = ycs[:, g * 128:(g + 1) * 128]
    blocks = []
    for k in range(S5_SEGS):
        for i0 in range(0, seg_len, 8):
            rows = pl.ds(i0 * S5_SEGS + k, 8, stride=S5_SEGS)
            blocks.append(jnp.concatenate([yp_ref[g, rows, :] for g in range(ngrp)], axis=1))
    y = jnp.concatenate(blocks, axis=0) + dskip_ref[...] * u
    y = 0.5 * y * (1.0 + jnp.tanh(math.sqrt(2.0 / math.pi) * (y + 0.044715 * (y * y * y))))
    z = _dot(y.astype(BF16), gluw_ref[...]) + glub_ref[...]
    o_ref[...] = (y * _sigmoid(z)).astype(BF16)


def _s5(d_u, bb_re, bb_im, pw_re, pw_im, c_re_t, c_im_t, d_skip, glu_w, glu_b, bn, t):
    nt = t // S5_TT
    full = lambda a: pl.BlockSpec(a.shape, lambda b, i: (0,) * a.ndim)
    r = jnp.arange(S5_TT)
    perm = (r[None, :] == ((r % S5_SEGS) * (S5_TT // S5_SEGS) + r // S5_SEGS)[:, None]).astype(BF16)
    args = (perm, bb_re, bb_im, pw_re, pw_im, c_re_t, c_im_t, d_skip, glu_w, glu_b)
    return pl.pallas_call(
        _s5_kernel,
        grid=(bn, nt),
        in_specs=[pl.BlockSpec((S5_TT, 256), lambda b, i: (b * nt + i, 0))] + [full(a) for a in args],
        out_specs=pl.BlockSpec((S5_TT, 256), lambda b, i: (b * nt + i, 0)),
        out_shape=jax.ShapeDtypeStruct((bn * t, 256), BF16),
        scratch_shapes=[pltpu.VMEM((S5_TT, S5_LANES), F32), pltpu.VMEM((S5_TT, S5_LANES), F32),
                        pltpu.VMEM((8, S5_LANES), F32), pltpu.VMEM((8, S5_LANES), F32),
                        pltpu.VMEM((GROUP_WIDTH // 128, S5_TT, 128), F32)],
        compiler_params=_cparams(("arbitrary", "arbitrary")),
        name="s5_glu",
    )(d_u, *args)


def _outproj_kernel(x_ref, mod_ref, oa_ref, ob_ref, oc_ref, od_ref, w_ref, lng_ref, lnb_ref, o_ref):
    x = x_ref[...]
    gate = mod_ref[2:3, :]
    y = _dot(oa_ref[...], w_ref[0])
    y = y + _dot(ob_ref[...], w_ref[1])
    y = y + _dot(oc_ref[...], w_ref[2])
    y = y + _dot(od_ref[...], w_ref[3])
    o_ref[...] = _residual_layer_norm(x, gate * y, lng_ref[...], lnb_ref[...])


def _outproj(x2, mod, o_a, o_b, o_c, o_d, w_out4, ln_g, ln_b, rows_per_batch):
    n = x2.shape[0]
    tm = ROW_TILE
    tiles_per_batch = rows_per_batch // tm
    mix = pl.BlockSpec((tm, GROUP_WIDTH), lambda i: (i, 0))
    return pl.pallas_call(
        _outproj_kernel,
        grid=(n // tm,),
        in_specs=[
            pl.BlockSpec((tm, D_MODEL), lambda i: (i, 0)),
            pl.BlockSpec((None, 8, D_MODEL), lambda i: (i // tiles_per_batch, 0, 0)),
            mix, mix, mix, mix,
            pl.BlockSpec(w_out4.shape, lambda i: (0, 0, 0), pipeline_mode=pl.Buffered(1)),
            pl.BlockSpec((1, D_MODEL), lambda i: (0, 0)),
            pl.BlockSpec((1, D_MODEL), lambda i: (0, 0)),
        ],
        out_specs=pl.BlockSpec((tm, D_MODEL), lambda i: (i, 0)),
        out_shape=jax.ShapeDtypeStruct((n, D_MODEL), F32),
        compiler_params=_cparams(("arbitrary",)),
        name="mixer_outproj",
    )(x2, mod, o_a, o_b, o_c, o_d, w_out4, ln_g.reshape(1, -1), ln_b.reshape(1, -1))


def _permute_w_in(w_in):
    off = {}
    o = 0
    for name, s in (("qa", 256), ("ka", 128), ("va", 128), ("cq", 128), ("kb", 64), ("vb", 64), ("ki", 32),
                    ("wi", 4), ("qkc", 512), ("vc", 256), ("ig", 4), ("fg", 4), ("oc", 256), ("ud", 256)):
        off[name] = (o, o + s)
        o += s
    col = lambda n: w_in[:, off[n][0]:off[n][1]]
    small = jnp.concatenate([col("wi"), col("ig"), col("fg"),
                             jnp.zeros((w_in.shape[0], 128 - 12), w_in.dtype)], axis=1)
    parts = [col("qa"), col("ka"), col("va"), col("cq"), col("kb"), col("vb")] + [col("ki")] * 4 + [
        col("qkc"), col("vc"), col("oc"), col("ud"), small]
    return jnp.concatenate(parts, axis=1).astype(BF16)


def _s5_params(lam_re, lam_im, log_step, b_re, b_im, c_re, c_im):
    dt = jnp.exp(log_step)[:, None]
    mag = jnp.exp(lam_re * dt)
    a_re, a_im = mag * jnp.cos(lam_im * dt), mag * jnp.sin(lam_im * dt)
    den = lam_re * lam_re + lam_im * lam_im
    kap_re = ((a_re - 1.0) * lam_re + a_im * lam_im) / den
    kap_im = (a_im * lam_re - (a_re - 1.0) * lam_im) / den
    bb_re = kap_re[..., None] * b_re - kap_im[..., None] * b_im
    bb_im = kap_re[..., None] * b_im + kap_im[..., None] * b_re
    eye = jnp.eye(S5_GROUPS, dtype=F32)

    def in_mat(bb):
        return jnp.einsum("gph,gk->ghkp", bb, eye).reshape(S5_GROUPS * S5_GROUP_CH, S5_LANES).astype(BF16)

    def out_mat(cc):
        return jnp.einsum("gop,gk->gpko", cc, eye).reshape(S5_LANES, S5_GROUPS * S5_GROUP_CH).astype(BF16)

    n = jnp.arange(1, S5_TT // S5_SEGS + 1, dtype=F32)[:, None, None]
    pw_mag = jnp.exp(n * (lam_re * dt))
    pw_re = (pw_mag * jnp.cos(n * (lam_im * dt))).at[0].set(a_re).reshape(-1, S5_LANES)
    pw_im = (pw_mag * jnp.sin(n * (lam_im * dt))).at[0].set(a_im).reshape(-1, S5_LANES)
    return in_mat(bb_re), in_mat(bb_im), pw_re, pw_im, out_mat(c_re), out_mat(c_im)


def kernel(x, c, ada_w, ada_b, ln_g, ln_b, ffn_w13, ffn_w2, w_in, w_out, sinks, w_uq, w_iq, conv_w, conv_b, ig_b,
           fg_b, mh_norm_g, lam_re, lam_im, log_step, b_re, b_im, c_re, c_im, d_skip, glu_w, glu_b):
    bn, t, d = x.shape
    assert d == D_MODEL and t % max(ROW_TILE, DSA_TKA, S5_TT) == 0 and DSA_TKA % DSA_TK == 0 and bn <= 8
    n = bn * t
    nl = ada_w.shape[0]
    c_pad = jnp.zeros((8, d), F32).at[:bn].set(c)
    mod_all = _ada_mod(c_pad, ada_w, ada_b)
    mod_all = mod_all[:, :bn].reshape(nl, bn, N_SUB, 3, d).transpose(0, 2, 1, 3, 4)
    mod_all = jnp.pad(mod_all, ((0, 0), (0, 0), (0, 0), (0, 5), (0, 0)))

    x2 = x.reshape(n, d)
    for l in range(nl):
        w13c = ffn_w13[l].astype(BF16).reshape(2, d, 2, FFN_NCHUNK, FFN_TF).transpose(0, 2, 3, 1, 4)
        w2c = ffn_w2[l].astype(BF16).reshape(2, FFN_NCHUNK, FFN_TF, d)
        x2 = _ffn_sublayer(x2, mod_all[l, 0], w13c[0], w2c[0], ln_g[l, 0], ln_b[l, 0], t)
        (a_q, a_kv, b_q, b_qi, b_kv, b_ki, c_qk, c_v, c_o, d_u, small) = _inproj(
            x2, mod_all[l, 1], _permute_w_in(w_in[l]), w_uq[l].astype(BF16), w_iq[l].astype(BF16), t)
        o_a = _swa(sinks[l], a_q, a_kv, bn, t)
        o_b = _dsa(b_q, b_qi, small, b_kv, b_ki, bn, t)
        gate_bias = jnp.zeros((1, 128), F32).at[0, SM_IG:SM_IG + 4].set(ig_b[l]).at[0, SM_FG:SM_FG + 4].set(fg_b[l])
        o_c = _mlstm(c_qk, c_v, c_o, small, conv_w[l], conv_b[l].reshape(1, -1), gate_bias,
                     mh_norm_g[l].reshape(1, -1), bn, t).reshape(n, GROUP_WIDTH)
        s5p = _s5_params(lam_re[l], lam_im[l], log_step[l], b_re[l], b_im[l], c_re[l], c_im[l])
        o_d = _s5(d_u, *s5p, d_skip[l].reshape(1, -1), glu_w[l].astype(BF16), glu_b[l].reshape(1, -1), bn, t)
        x2 = _outproj(x2, mod_all[l, 1], o_a, o_b, o_c, o_d,
                      w_out[l].astype(BF16).reshape(4, GROUP_WIDTH, d), ln_g[l, 1], ln_b[l, 1], t)
        x2 = _ffn_sublayer(x2, mod_all[l, 2], w13c[1], w2c[1], ln_g[l, 2], ln_b[l, 2], t)
    return x2.reshape(bn, t, d)
```

```python
import functools
import math

import jax
import jax.numpy as jnp
from jax import lax
from jax.experimental import pallas as pl
from jax.experimental.pallas import tpu as pltpu

F32 = jnp.float32
BF16 = jnp.bfloat16
I32 = jnp.int32

D_MODEL = 1024
DEPTH = 2
CHUNK = 64
HEAD_DIM = 64
GROUP_WIDTH = 256
SWA_HEADS = 4
SWA_KV_HEADS = 2
SWA_WIN_CHUNKS = 2
DSA_HEADS = 4
DSA_Q_RANK = 128
IDX_HEADS = 4
IDX_DIM = 32
DSA_TOPK = 256
MLSTM_HEADS = 4
MLSTM_CONV = 4
S5_GROUP_CH = 16
S5_GROUPS = 16
S5_STATE = 64
S5_LANES = S5_GROUPS * S5_STATE
D_FF = 2816
N_SUB = 3
ALPHA = (2 * DEPTH) ** 0.25
LN_EPS = 1e-5
NEG_INF = -1e30
INT_MIN = -(2 ** 31)

SLOPES_A = tuple(2.0 ** -(i + 1) for i in range(0, 8, 2))
SLOPES_B = tuple(2.0 ** -(i + 1) for i in range(1, 8, 2))

VMEM_LIMIT_BYTES = 56 * 1024 * 1024

FFN_TF = 256
FFN_NCHUNK = D_FF // FFN_TF
ROW_TILE = 512
SWA_TQ = 128
DSA_TQ = 128
DSA_TK = 512
DSA_TKA = 512
DSA_UNROLL = 1
S5_TT = 512
S5_SEGS = 8

ZC_QA, ZC_KVA, ZC_CQ, ZC_KVB, ZC_KI = 0, 256, 512, 640, 768
ZC_QKC, ZC_VC, ZC_OC, ZC_UD, ZC_SMALL, Z_WIDTH = 896, 1408, 1664, 1920, 2176, 2304
SM_WI, SM_IG, SM_FG = 0, 4, 8


def _cparams(sem):
    return pltpu.CompilerParams(dimension_semantics=sem, vmem_limit_bytes=VMEM_LIMIT_BYTES)


def _dot(a, b):
    return jnp.dot(a, b, preferred_element_type=F32)


def _dot_nt(a, b):
    return lax.dot_general(a, b, (((1,), (1,)), ((), ())), preferred_element_type=F32)


def _sigmoid(x):
    return 1.0 / (1.0 + jnp.exp(-x))


def _residual_layer_norm(x, y, g, b):
    v = ALPHA * x + y
    mu = jnp.mean(v, axis=-1, keepdims=True)
    d = v - mu
    var = jnp.mean(d * d, axis=-1, keepdims=True)
    return d * lax.rsqrt(var + LN_EPS) * g + b


def _ada_kernel(c_ref, w_ref, b_ref, o_ref):
    c = c_ref[...]
    cs = c * _sigmoid(c)
    o_ref[...] = jnp.dot(cs, w_ref[...], preferred_element_type=F32,
                         precision=lax.Precision.HIGHEST) + b_ref[...]


def _ada_mod(c_pad, ada_w, ada_b):
    nl = ada_w.shape[0]
    ncol = ada_w.shape[2] // D_MODEL
    return pl.pallas_call(
        _ada_kernel,
        grid=(nl, ncol),
        in_specs=[
            pl.BlockSpec((8, D_MODEL), lambda l, j: (0, 0)),
            pl.BlockSpec((None, D_MODEL, D_MODEL), lambda l, j: (l, 0, j)),
            pl.BlockSpec((None, 1, D_MODEL), lambda l, j: (l, 0, j)),
        ],
        out_specs=pl.BlockSpec((None, 8, D_MODEL), lambda l, j: (l, 0, j)),
        out_shape=jax.ShapeDtypeStruct((nl, 8, ada_w.shape[2]), F32),
        compiler_params=_cparams(("arbitrary", "arbitrary")),
        name="ada_mod",
    )(c_pad, ada_w, ada_b.reshape(nl, 1, -1))


def _ffn_kernel(x_ref, mod_ref, w13_ref, w2_ref, lng_ref, lnb_ref, o_ref, h_ref):
    x = x_ref[...]
    shift, scale, gate = mod_ref[0:1, :], mod_ref[1:2, :], mod_ref[2:3, :]
    u = (x * (1.0 + scale) + shift).astype(BF16)

    for j in range(FFN_NCHUNK):
        a = _dot(u, w13_ref[0, j])
        g = _dot(u, w13_ref[1, j])
        h_ref[:, j * FFN_TF:(j + 1) * FFN_TF] = (a * _sigmoid(a) * g).astype(BF16)
    y = _dot(h_ref[...], w2_ref[...])
    o_ref[...] = _residual_layer_norm(x, 0.5 * gate * y, lng_ref[...], lnb_ref[...])


def _ffn_sublayer(x2, mod, w13c, w2c, ln_g, ln_b, rows_per_batch):
    n = x2.shape[0]
    tm = ROW_TILE
    tiles_per_batch = rows_per_batch // tm
    return pl.pallas_call(
        _ffn_kernel,
        grid=(n // tm,),
        in_specs=[
            pl.BlockSpec((tm, D_MODEL), lambda i: (i, 0)),
            pl.BlockSpec((None, 8, D_MODEL), lambda i: (i // tiles_per_batch, 0, 0)),
            pl.BlockSpec(w13c.shape, lambda i: (0, 0, 0, 0), pipeline_mode=pl.Buffered(1)),
            pl.BlockSpec(w2c.shape, lambda i: (0, 0), pipeline_mode=pl.Buffered(1)),
            pl.BlockSpec((1, D_MODEL), lambda i: (0, 0)),
            pl.BlockSpec((1, D_MODEL), lambda i: (0, 0)),
        ],
        out_specs=pl.BlockSpec((tm, D_MODEL), lambda i: (i, 0)),
        out_shape=jax.ShapeDtypeStruct((n, D_MODEL), F32),
        scratch_shapes=[pltpu.VMEM((tm, D_FF), BF16)],
        compiler_params=_cparams(("arbitrary",)),
        name="ffn_sublayer",
    )(x2, mod, w13c, w2c, ln_g.reshape(1, -1), ln_b.reshape(1, -1))


def _inproj_kernel(x_ref, mod_ref, w_ref, wuq_ref, wiq_ref,
                   aq_ref, akv_ref, bq_ref, bqi_ref, bkv_ref, bki_ref,
                   cqk_ref, cv_ref, co_ref, du_ref, sm_ref):
    x = x_ref[...]
    shift, scale = mod_ref[0:1, :], mod_ref[1:2, :]
    u = (x * (1.0 + scale) + shift).astype(BF16)
    z = _dot(u, w_ref[...])
    aq_ref[...] = z[:, ZC_QA:ZC_KVA].astype(BF16)
    akv_ref[...] = z[:, ZC_KVA:ZC_CQ].astype(BF16)
    cq = z[:, ZC_CQ:ZC_KVB].astype(BF16)
    bq_ref[...] = (_dot(cq, wuq_ref[...]) * HEAD_DIM ** -0.5).astype(BF16)
    bqi_ref[...] = _dot(cq, wiq_ref[...]).astype(BF16)
    bkv_ref[...] = z[:, ZC_KVB:ZC_KI].astype(BF16)
    bki_ref[...] = z[:, ZC_KI:ZC_QKC].astype(BF16)
    cqk_ref[...] = z[:, ZC_QKC:ZC_VC]
    cv_ref[...] = z[:, ZC_VC:ZC_OC].astype(BF16)
    co_ref[...] = z[:, ZC_OC:ZC_UD]
    du_ref[...] = z[:, ZC_UD:ZC_SMALL]
    sm_ref[...] = z[:, ZC_SMALL:Z_WIDTH]


def _inproj(x2, mod, w_perm, w_uq, w_iq, rows_per_batch):
    n = x2.shape[0]
    tm = ROW_TILE
    tiles_per_batch = rows_per_batch // tm
    widths = [(256, BF16), (256, BF16), (256, BF16), (128, BF16), (128, BF16), (128, BF16),
              (512, F32), (256, BF16), (256, F32), (256, F32), (128, F32)]
    return pl.pallas_call(
        _inproj_kernel,
        grid=(n // tm,),
        in_specs=[
            pl.BlockSpec((tm, D_MODEL), lambda i: (i, 0)),
            pl.BlockSpec((None, 8, D_MODEL), lambda i: (i // tiles_per_batch, 0, 0)),
            pl.BlockSpec(w_perm.shape, lambda i: (0, 0), pipeline_mode=pl.Buffered(1)),
            pl.BlockSpec(w_uq.shape, lambda i: (0, 0), pipeline_mode=pl.Buffered(1)),
            pl.BlockSpec(w_iq.shape, lambda i: (0, 0), pipeline_mode=pl.Buffered(1)),
        ],
        out_specs=[pl.BlockSpec((tm, w), lambda i: (i, 0)) for w, _ in widths],
        out_shape=[jax.ShapeDtypeStruct((n, w), dt) for w, dt in widths],
        compiler_params=_cparams(("arbitrary",)),
        name="mixer_inproj",
    )(x2, mod, w_perm, w_uq, w_iq)


def _swa_kernel(sink_ref, q_ref, kvc_ref, kvp_ref, o_ref):
    i = pl.program_id(1)
    tq = SWA_TQ
    q = q_ref[...]
    kv = jnp.concatenate([kvp_ref[...], kvc_ref[...]], axis=0)
    qpos = i * tq + lax.broadcasted_iota(I32, (tq, 2 * tq), 0)
    kpos = (i - 1) * tq + lax.broadcasted_iota(I32, (tq, 2 * tq), 1)
    qchunk = qpos // CHUNK
    kchunk = (kpos + tq) // CHUNK - tq // CHUNK
    valid = (kpos >= 0) & (kchunk <= qchunk) & (kchunk >= qchunk - SWA_WIN_CHUNKS)
    dist = jnp.abs(qpos - kpos).astype(F32)
    rep = SWA_HEADS // SWA_KV_HEADS
    for h in range(SWA_HEADS):
        g = h // rep
        qh = q[:, h * HEAD_DIM:(h + 1) * HEAD_DIM]
        kg = kv[:, g * HEAD_DIM:(g + 1) * HEAD_DIM]
        vg = kv[:, (SWA_KV_HEADS + g) * HEAD_DIM:(SWA_KV_HEADS + g + 1) * HEAD_DIM]
        s = _dot_nt(qh, kg) * HEAD_DIM ** -0.5 - SLOPES_A[h] * dist
        s = jnp.where(valid, s, NEG_INF)
        sink = sink_ref[h]
        m = jnp.maximum(jnp.max(s, axis=-1, keepdims=True), sink)
        p = jnp.exp(s - m)
        denom = jnp.sum(p, axis=-1, keepdims=True) + jnp.exp(sink - m)
        o = _dot(p.astype(BF16), vg) / denom
        o_ref[:, h * HEAD_DIM:(h + 1) * HEAD_DIM] = o.astype(BF16)


def _swa(sinks, a_q, a_kv, bn, t):
    nt = t // SWA_TQ
    return pl.pallas_call(
        _swa_kernel,
        grid=(bn, nt),
        in_specs=[
            pl.BlockSpec(memory_space=pltpu.SMEM),
            pl.BlockSpec((SWA_TQ, 256), lambda b, i: (b * nt + i, 0)),
            pl.BlockSpec((SWA_TQ, 256), lambda b, i: (b * nt + i, 0)),
            pl.BlockSpec((SWA_TQ, 256), lambda b, i: (b * nt + jnp.maximum(i - 1, 0), 0)),
        ],
        out_specs=pl.BlockSpec((SWA_TQ, 256), lambda b, i: (b * nt + i, 0)),
        out_shape=jax.ShapeDtypeStruct((bn * t, 256), BF16),
        compiler_params=_cparams(("arbitrary", "arbitrary")),
        name="swa_attention",
    )(sinks, a_q, a_kv, a_kv)


def _sortable_key(x):
    bits = lax.bitcast_convert_type(x, I32)
    return bits ^ ((bits >> 31) & 0x7FFFFFFF)


def _dsa_kernel(q_ref, qi_ref, sm_ref, kv_ref, vt_ref, ki_ref, o_ref, key_ref):
    i = pl.program_id(1)
    tq, tk, tka = DSA_TQ, DSA_TK, DSA_TKA
    nblk = (i * tq + tq + tk - 1) // tk
    span_a = tka * DSA_UNROLL
    nblk_a = (i * tq + tq + span_a - 1) // span_a
    qpos = i * tq + lax.broadcasted_iota(I32, (1, tq), 1)
    qchunk = qpos // CHUNK
    row_k = lax.broadcasted_iota(I32, (tk, tq), 0)

    qi = qi_ref[...]
    lane_i = lax.broadcasted_iota(I32, (1, IDX_HEADS * IDX_DIM), 1) // IDX_DIM
    qi_stack = jnp.concatenate([jnp.where(lane_i == h, qi, jnp.zeros_like(qi)) for h in range(IDX_HEADS)], axis=0)
    sm_t = sm_ref[...].T
    w_idx = [sm_t[SM_WI + h:SM_WI + h + 1, :] for h in range(IDX_HEADS)]
    idx_scale = (IDX_DIM * IDX_HEADS) ** -0.5

    def score_block(j, carry):
        kib = ki_ref[pl.ds(pl.multiple_of(j * tk, tk), tk), :]
        d = _dot_nt(kib, qi_stack)
        acc = w_idx[0] * jnp.maximum(d[:, 0:tq], 0.0)
        for h in range(1, IDX_HEADS):
            acc = acc + w_idx[h] * jnp.maximum(d[:, h * tq:(h + 1) * tq], 0.0)
        sc = acc * idx_scale
        kpos = j * tk + row_k
        sc = jnp.where(kpos // CHUNK <= qchunk, sc, NEG_INF)
        key_ref[j] = _sortable_key(sc)
        return carry

    lax.fori_loop(0, nblk_a * (span_a // tk), score_block, 0)

    k_eff = jnp.minimum(DSA_TOPK, (qchunk + 1) * CHUNK)

    def count(pred):
        def blk(j, c):
            m = pred(key_ref[j], j * tk + row_k).astype(I32)
            return c + jnp.sum(m.reshape(tk // 8, 8, tq), axis=0)
        c = lax.fori_loop(0, nblk, blk, jnp.zeros((8, tq), I32))
        return jnp.sum(c, axis=0, keepdims=True)

    def bit_step(bi, prefix):
        cand_u = prefix | (jnp.int32(1) << (31 - bi))
        cand_s = cand_u ^ INT_MIN
        cnt = count(lambda key, kpos: key >= cand_s)
        return jnp.where(cnt >= k_eff, cand_u, prefix)

    prefix = lax.fori_loop(0, 32, bit_step, jnp.zeros((1, tq), I32))
    thr = prefix ^ INT_MIN

    c_gt = count(lambda key, kpos: key > thr)
    need = (k_eff - c_gt).astype(F32)
    tril = (lax.broadcasted_iota(I32, (tk, tk), 0) >= lax.broadcasted_iota(I32, (tk, tk), 1)).astype(BF16)

    q = q_ref[...]
    q_stack = jnp.concatenate([q[:, h * HEAD_DIM:(h + 1) * HEAD_DIM] for h in range(DSA_HEADS)], axis=0)
    qpos_f = qpos.astype(F32)

    row_a = lax.broadcasted_iota(I32, (tka, tq), 0)

    def att_block(j, carry):
        m_run, l_run, acc, ties_seen = carry
        kb = kv_ref[pl.ds(pl.multiple_of(j * tka, tka), tka), 0:HEAD_DIM]
        s = _dot_nt(kb, q_stack)
        kpos = j * tka + row_a
        keys, ranks = [], []
        for r in range(tka // tk):
            key_r = key_ref[j * (tka // tk) + r]
            rank_r = _dot(tril, jnp.where(key_r == thr, 1.0, 0.0).astype(BF16)) + ties_seen
            ties_seen = rank_r[tk - 1:tk, :]
            keys.append(key_r)
            ranks.append(rank_r)
        key = jnp.concatenate(keys, axis=0)
        sel = (key > thr) | ((key == thr) & (jnp.concatenate(ranks, axis=0) <= need))
        dist = jnp.abs(kpos.astype(F32) - qpos_f)
        ms, ls, ps, alphas = [], [], [], []
        for h in range(DSA_HEADS):
            sh = jnp.where(sel, s[:, h * tq:(h + 1) * tq] - SLOPES_B[h] * dist, -jnp.inf)
            m_old = m_run[:, h * tq:(h + 1) * tq]
            m_new = jnp.maximum(m_old, jnp.max(sh, axis=0, keepdims=True))
            alpha = jnp.exp(m_old - m_new)
            p = jnp.exp(sh - m_new)
            ms.append(m_new)
            ls.append(alpha * l_run[:, h * tq:(h + 1) * tq] + jnp.sum(p, axis=0, keepdims=True))
            ps.append(p.astype(BF16))
            alphas.append(alpha)
        pv = _dot(vt_ref[j], jnp.concatenate(ps, axis=1))
        acc = jnp.concatenate(alphas, axis=1) * acc + pv
        return jnp.concatenate(ms, axis=1), jnp.concatenate(ls, axis=1), acc, ties_seen

    def att_blocks(jj, carry):
        for r in range(DSA_UNROLL):
            carry = att_block(jj * DSA_UNROLL + r, carry)
        return carry

    init = (jnp.full((1, DSA_HEADS * tq), NEG_INF, F32), jnp.zeros((1, DSA_HEADS * tq), F32),
            jnp.zeros((HEAD_DIM, DSA_HEADS * tq), F32), jnp.zeros((1, tq), F32))
    _, l_run, acc, _ = lax.fori_loop(0, nblk_a, att_blocks, init)
    out = acc / l_run
    for h in range(DSA_HEADS):
        o_ref[:, h * HEAD_DIM:(h + 1) * HEAD_DIM] = out[:, h * tq:(h + 1) * tq].T.astype(BF16)


def _dsa(b_q, b_qi, small, b_kv, b_ki, bn, t):
    nt = t // DSA_TQ
    nkb = t // DSA_TK
    nka = t // DSA_TKA
    v_t = b_kv[:, HEAD_DIM:].reshape(bn * nka, DSA_TKA, HEAD_DIM).transpose(0, 2, 1)
    return pl.pallas_call(
        _dsa_kernel,
        grid=(bn, nt),
        in_specs=[
            pl.BlockSpec((DSA_TQ, 256), lambda b, i: (b * nt + i, 0)),
            pl.BlockSpec((DSA_TQ, 128), lambda b, i: (b * nt + i, 0)),
            pl.BlockSpec((DSA_TQ, 128), lambda b, i: (b * nt + i, 0)),
            pl.BlockSpec((t, 128), lambda b, i: (b, 0)),
            pl.BlockSpec((nka, HEAD_DIM, DSA_TKA), lambda b, i: (b, 0, 0)),
            pl.BlockSpec((t, 128), lambda b, i: (b, 0)),
        ],
        out_specs=pl.BlockSpec((DSA_TQ, 256), lambda b, i: (b * nt + i, 0)),
        out_shape=jax.ShapeDtypeStruct((bn * t, 256), BF16),
        scratch_shapes=[pltpu.VMEM((nkb, DSA_TK, DSA_TQ), I32)],
        compiler_params=_cparams(("arbitrary", "arbitrary")),
        name="dsa_attention",
    )(b_q, b_qi, small, b_kv, v_t, b_ki)


def _mlstm_kernel(qk_ref, v_ref, og_ref, sm_ref, convw_ref, convb_ref, gbias_ref, normg_ref,
                  o_ref, tail_ref, cmat_ref, nvec_ref, mst_ref, *, bn):
    c = pl.program_id(0)
    L = CHUNK

    @pl.when(c == 0)
    def _():
        tail_ref[...] = jnp.zeros_like(tail_ref)
        cmat_ref[...] = jnp.zeros_like(cmat_ref)
        nvec_ref[...] = jnp.zeros_like(nvec_ref)
        mst_ref[...] = jnp.zeros_like(mst_ref)

    row = lax.broadcasted_iota(I32, (L, L), 0)
    col = lax.broadcasted_iota(I32, (L, L), 1)
    causal = col <= row
    tril = causal.astype(F32)
    convw = convw_ref[...]
    for b in range(bn):
        cur = qk_ref[b]
        ext = jnp.concatenate([tail_ref[b], cur], axis=0)
        tail_ref[b] = cur[L - 8:L, :]
        y = convb_ref[...] + convw[MLSTM_CONV - 1:MLSTM_CONV, :] * cur
        for k in range(MLSTM_CONV - 1):
            off = 8 - (MLSTM_CONV - 1) + k
            y = y + convw[k:k + 1, :] * ext[off:off + L, :]
        qk = y * _sigmoid(y)
        gates = sm_ref[b] + gbias_ref[...]
        lf = jnp.minimum(gates, 0.0) - jnp.log(1.0 + jnp.exp(-jnp.abs(gates)))
        bcum = jnp.dot(tril, lf, preferred_element_type=F32, precision=lax.Precision.HIGHEST)
        gates_t = gates.T
        bcum_t = bcum.T
        vb = v_ref[b]
        og = og_ref[b]
        for h in range(MLSTM_HEADS):
            s_idx = b * MLSTM_HEADS + h
            q = qk[:, h * HEAD_DIM:(h + 1) * HEAD_DIM]
            k = qk[:, (MLSTM_HEADS + h) * HEAD_DIM:(MLSTM_HEADS + h + 1) * HEAD_DIM] * HEAD_DIM ** -0.5
            v = vb[:, h * HEAD_DIM:(h + 1) * HEAD_DIM]
            qb, kb = q.astype(BF16), k.astype(BF16)
            b_col = bcum[:, SM_FG + h:SM_FG + h + 1]
            b_row = bcum_t[SM_FG + h:SM_FG + h + 1, :]
            ig_col = gates[:, SM_IG + h:SM_IG + h + 1]
            ig_row = gates_t[SM_IG + h:SM_IG + h + 1, :]
            m_prev = mst_ref[s_idx]
            cmat = cmat_ref[s_idx]
            nvec = nvec_ref[s_idx]
            dlog = jnp.where(causal, b_col - b_row + ig_row, NEG_INF)
            inter = b_col + m_prev
            mj = jnp.maximum(inter, jnp.max(dlog, axis=-1, keepdims=True))
            dw = jnp.exp(dlog - mj)
            iw = jnp.exp(inter - mj)
            sc = _dot_nt(qb, kb) * dw
            num = iw * _dot(qb, cmat.astype(BF16)) + _dot(sc.astype(BF16), v)
            den = iw * jnp.sum(q * nvec, axis=-1, keepdims=True) + jnp.sum(sc, axis=-1, keepdims=True)
            hj = num / jnp.maximum(jnp.abs(den), jnp.exp(-mj))
            bl = b_col[L - 1:L, :]
            dec_row = bl - b_row + ig_row
            dec_col = bl - b_col + ig_col
            m_new = jnp.maximum(bl + m_prev, jnp.max(dec_row, axis=-1, keepdims=True))
            wc = jnp.exp(bl + m_prev - m_new)
            ws = jnp.exp(dec_col - m_new)
            kw = k * ws
            cmat_ref[s_idx] = wc * cmat + _dot(kw.T.astype(BF16), v)
            nvec_ref[s_idx] = wc * nvec + jnp.sum(kw, axis=0, keepdims=True)
            mst_ref[s_idx] = m_new
            mu = jnp.mean(hj, axis=-1, keepdims=True)
            dh = hj - mu
            var = jnp.mean(dh * dh, axis=-1, keepdims=True)
            hn = dh * lax.rsqrt(var + LN_EPS) * normg_ref[:, h * HEAD_DIM:(h + 1) * HEAD_DIM]
            o = _sigmoid(og[:, h * HEAD_DIM:(h + 1) * HEAD_DIM]) * hn
            o_ref[b, :, h * HEAD_DIM:(h + 1) * HEAD_DIM] = o.astype(BF16)


def _mlstm(c_qk, c_v, c_o, small, conv_w, conv_b, gate_bias, norm_g, bn, t):
    nc = t // CHUNK
    nstate = bn * MLSTM_HEADS
    blk = lambda w: pl.BlockSpec((bn, CHUNK, w), lambda c: (0, c, 0))
    full = lambda a: pl.BlockSpec(a.shape, lambda c: (0,) * a.ndim)
    return pl.pallas_call(
        functools.partial(_mlstm_kernel, bn=bn),
        grid=(nc,),
        in_specs=[blk(512), blk(256), blk(256), blk(128), full(conv_w), full(conv_b), full(gate_bias), full(norm_g)],
        out_specs=blk(256),
        out_shape=jax.ShapeDtypeStruct((bn, t, 256), BF16),
        scratch_shapes=[pltpu.VMEM((bn, 8, 512), F32), pltpu.VMEM((nstate, HEAD_DIM, HEAD_DIM), F32),
                        pltpu.VMEM((nstate, 1, HEAD_DIM), F32), pltpu.VMEM((nstate, 1, 1), F32)],
        compiler_params=_cparams(("arbitrary",)),
        name="mlstm",
    )(c_qk.reshape(bn, t, 512), c_v.reshape(bn, t, 256), c_o.reshape(bn, t, 256), small.reshape(bn, t, 128),
      conv_w, conv_b, gate_bias, norm_g)


def _s5_kernel(u_ref, perm_ref, bbre_ref, bbim_ref, pwre_ref, pwim_ref, cre_ref, cim_ref, dskip_ref, gluw_ref,
               glub_ref, o_ref, sre_ref, sim_ref, stre_ref, stim_ref, yp_ref):
    @pl.when(pl.program_id(1) == 0)
    def _():
        stre_ref[...] = jnp.zeros_like(stre_ref)
        stim_ref[...] = jnp.zeros_like(stim_ref)

    seg_len = S5_TT // S5_SEGS
    u = u_ref[...]
    ub = _dot(perm_ref[...], u.astype(BF16)).astype(BF16)
    sre_ref[...] = _dot(ub, bbre_ref[...])
    sim_ref[...] = _dot(ub, bbim_ref[...])
    a_re = jnp.broadcast_to(pwre_ref[0:1, :], (S5_SEGS, S5_LANES))
    a_im = jnp.broadcast_to(pwim_ref[0:1, :], (S5_SEGS, S5_LANES))

    def local_step(i, carry):
        s_re, s_im = carry
        rows = pl.ds(pl.multiple_of(i * S5_SEGS, S5_SEGS), S5_SEGS)
        n_re = a_re * s_re - a_im * s_im + sre_ref[rows, :]
        n_im = a_re * s_im + a_im * s_re + sim_ref[rows, :]
        sre_ref[rows, :] = n_re
        sim_ref[rows, :] = n_im
        return n_re, n_im

    zeros = jnp.zeros((S5_SEGS, S5_LANES), F32)
    e_re, e_im = lax.fori_loop(0, seg_len, local_step, (zeros, zeros))

    al_re, al_im = pwre_ref[seg_len - 1:seg_len, :], pwim_ref[seg_len - 1:seg_len, :]
    c_re, c_im = stre_ref[0:1, :], stim_ref[0:1, :]
    cs_re, cs_im = [], []
    for k in range(S5_SEGS):
        cs_re.append(c_re)
        cs_im.append(c_im)
        c_re, c_im = (e_re[k:k + 1, :] + al_re * c_re - al_im * c_im,
                      e_im[k:k + 1, :] + al_re * c_im + al_im * c_re)
    stre_ref[...] = jnp.broadcast_to(c_re, stre_ref.shape)
    stim_ref[...] = jnp.broadcast_to(c_im, stim_ref.shape)
    cin_re = jnp.concatenate(cs_re, axis=0)
    cin_im = jnp.concatenate(cs_im, axis=0)

    def correct_step(i, carry):
        rows = pl.ds(pl.multiple_of(i * S5_SEGS, S5_SEGS), S5_SEGS)
        p_re, p_im = pwre_ref[pl.ds(i, 1), :], pwim_ref[pl.ds(i, 1), :]
        sre_ref[rows, :] = sre_ref[rows, :] + p_re * cin_re - p_im * cin_im
        sim_ref[rows, :] = sim_ref[rows, :] + p_re * cin_im + p_im * cin_re
        return carry

    lax.fori_loop(0, seg_len, correct_step, 0)
    ycs = _dot(sre_ref[...].astype(BF16), cre_ref[...]) - _dot(sim_ref[...].astype(BF16), cim_ref[...])
    ngrp = GROUP_WIDTH // 128
    for g in range(ngrp):
        yp_ref[g] = ycs[:, g * 128:(g + 1) * 128]
    blocks = []
    for k in range(S5_SEGS):
        for i0 in range(0, seg_len, 8):
            rows = pl.ds(i0 * S5_SEGS + k, 8, stride=S5_SEGS)
            blocks.append(jnp.concatenate([yp_ref[g, rows, :] for g in range(ngrp)], axis=1))
    y = jnp.concatenate(blocks, axis=0) + dskip_ref[...] * u
    y = 0.5 * y * (1.0 + jnp.tanh(math.sqrt(2.0 / math.pi) * (y + 0.044715 * (y * y * y))))
    z = _dot(y.astype(BF16), gluw_ref[...]) + glub_ref[...]
    o_ref[...] = (y * _sigmoid(z)).astype(BF16)


def _s5(d_u, bb_re, bb_im, pw_re, pw_im, c_re_t, c_im_t, d_skip, glu_w, glu_b, bn, t):
    nt = t // S5_TT
    full = lambda a: pl.BlockSpec(a.shape, lambda b, i: (0,) * a.ndim)
    r = jnp.arange(S5_TT)
    perm = (r[None, :] == ((r % S5_SEGS) * (S5_TT // S5_SEGS) + r // S5_SEGS)[:, None]).astype(BF16)
    args = (perm, bb_re, bb_im, pw_re, pw_im, c_re_t, c_im_t, d_skip, glu_w, glu_b)
    return pl.pallas_call(
        _s5_kernel,
        grid=(bn, nt),
        in_specs=[pl.BlockSpec((S5_TT, 256), lambda b, i: (b * nt + i, 0))] + [full(a) for a in args],
        out_specs=pl.BlockSpec((S5_TT, 256), lambda b, i: (b * nt + i, 0)),
        out_shape=jax.ShapeDtypeStruct((bn * t, 256), BF16),
        scratch_shapes=[pltpu.VMEM((S5_TT, S5_LANES), F32), pltpu.VMEM((S5_TT, S5_LANES), F32),
                        pltpu.VMEM((8, S5_LANES), F32), pltpu.VMEM((8, S5_LANES), F32),
                        pltpu.VMEM((GROUP_WIDTH // 128, S5_TT, 128), F32)],
        compiler_params=_cparams(("arbitrary", "arbitrary")),
        name="s5_glu",
    )(d_u, *args)


def _outproj_kernel(x_ref, mod_ref, oa_ref, ob_ref, oc_ref, od_ref, w_ref, lng_ref, lnb_ref, o_ref):
    x = x_ref[...]
    gate = mod_ref[2:3, :]
    y = _dot(oa_ref[...], w_ref[0])
    y = y + _dot(ob_ref[...], w_ref[1])
    y = y + _dot(oc_ref[...], w_ref[2])
    y = y + _dot(od_ref[...], w_ref[3])
    o_ref[...] = _residual_layer_norm(x, gate * y, lng_ref[...], lnb_ref[...])


def _outproj(x2, mod, o_a, o_b, o_c, o_d, w_out4, ln_g, ln_b, rows_per_batch):
    n = x2.shape[0]
    tm = ROW_TILE
    tiles_per_batch = rows_per_batch // tm
    mix = pl.BlockSpec((tm, GROUP_WIDTH), lambda i: (i, 0))
    return pl.pallas_call(
        _outproj_kernel,
        grid=(n // tm,),
        in_specs=[
            pl.BlockSpec((tm, D_MODEL), lambda i: (i, 0)),
            pl.BlockSpec((None, 8, D_MODEL), lambda i: (i // tiles_per_batch, 0, 0)),
            mix, mix, mix, mix,
            pl.BlockSpec(w_out4.shape, lambda i: (0, 0, 0), pipeline_mode=pl.Buffered(1)),
            pl.BlockSpec((1, D_MODEL), lambda i: (0, 0)),
            pl.BlockSpec((1, D_MODEL), lambda i: (0, 0)),
        ],
        out_specs=pl.BlockSpec((tm, D_MODEL), lambda i: (i, 0)),
        out_shape=jax.ShapeDtypeStruct((n, D_MODEL), F32),
        compiler_params=_cparams(("arbitrary",)),
        name="mixer_outproj",
    )(x2, mod, o_a, o_b, o_c, o_d, w_out4, ln_g.reshape(1, -1), ln_b.reshape(1, -1))


def _permute_w_in(w_in):
    off = {}
    o = 0
    for name, s in (("qa", 256), ("ka", 128), ("va", 128), ("cq", 128), ("kb", 64), ("vb", 64), ("ki", 32),
                    ("wi", 4), ("qkc", 512), ("vc", 256), ("ig", 4), ("fg", 4), ("oc", 256), ("ud", 256)):
        off[name] = (o, o + s)
        o += s
    col = lambda n: w_in[:, off[n][0]:off[n][1]]
    small = jnp.concatenate([col("wi"), col("ig"), col("fg"),
                             jnp.zeros((w_in.shape[0], 128 - 12), w_in.dtype)], axis=1)
    parts = [col("qa"), col("ka"), col("va"), col("cq"), col("kb"), col("vb")] + [col("ki")] * 4 + [
        col("qkc"), col("vc"), col("oc"), col("ud"), small]
    return jnp.concatenate(parts, axis=1).astype(BF16)


def _s5_params(lam_re, lam_im, log_step, b_re, b_im, c_re, c_im):
    dt = jnp.exp(log_step)[:, None]
    mag = jnp.exp(lam_re * dt)
    a_re, a_im = mag * jnp.cos(lam_im * dt), mag * jnp.sin(lam_im * dt)
    den = lam_re * lam_re + lam_im * lam_im
    kap_re = ((a_re - 1.0) * lam_re + a_im * lam_im) / den
    kap_im = (a_im * lam_re - (a_re - 1.0) * lam_im) / den
    bb_re = kap_re[..., None] * b_re - kap_im[..., None] * b_im
    bb_im = kap_re[..., None] * b_im + kap_im[..., None] * b_re
    eye = jnp.eye(S5_GROUPS, dtype=F32)

    def in_mat(bb):
        return jnp.einsum("gph,gk->ghkp", bb, eye).reshape(S5_GROUPS * S5_GROUP_CH, S5_LANES).astype(BF16)

    def out_mat(cc):
        return jnp.einsum("gop,gk->gpko", cc, eye).reshape(S5_LANES, S5_GROUPS * S5_GROUP_CH).astype(BF16)

    n = jnp.arange(1, S5_TT // S5_SEGS + 1, dtype=F32)[:, None, None]
    pw_mag = jnp.exp(n * (lam_re * dt))
    pw_re = (pw_mag * jnp.cos(n * (lam_im * dt))).at[0].set(a_re).reshape(-1, S5_LANES)
    pw_im = (pw_mag * jnp.sin(n * (lam_im * dt))).at[0].set(a_im).reshape(-1, S5_LANES)
    return in_mat(bb_re), in_mat(bb_im), pw_re, pw_im, out_mat(c_re), out_mat(c_im)


def kernel(x, c, ada_w, ada_b, ln_g, ln_b, ffn_w13, ffn_w2, w_in, w_out, sinks, w_uq, w_iq, conv_w, conv_b, ig_b,
           fg_b, mh_norm_g, lam_re, lam_im, log_step, b_re, b_im, c_re, c_im, d_skip, glu_w, glu_b):
    bn, t, d = x.shape
    assert d == D_MODEL and t % max(ROW_TILE, DSA_TKA * DSA_UNROLL, S5_TT) == 0 and DSA_TKA % DSA_TK == 0 and bn <= 8
    n = bn * t
    nl = ada_w.shape[0]
    c_pad = jnp.zeros((8, d), F32).at[:bn].set(c)
    mod_all = _ada_mod(c_pad, ada_w, ada_b)
    mod_all = mod_all[:, :bn].reshape(nl, bn, N_SUB, 3, d).transpose(0, 2, 1, 3, 4)
    mod_all = jnp.pad(mod_all, ((0, 0), (0, 0), (0, 0), (0, 5), (0, 0)))

    x2 = x.reshape(n, d)
    for l in range(nl):
        w13c = ffn_w13[l].astype(BF16).reshape(2, d, 2, FFN_NCHUNK, FFN_TF).transpose(0, 2, 3, 1, 4)
        w2c = ffn_w2[l].astype(BF16)
        x2 = _ffn_sublayer(x2, mod_all[l, 0], w13c[0], w2c[0], ln_g[l, 0], ln_b[l, 0], t)
        (a_q, a_kv, b_q, b_qi, b_kv, b_ki, c_qk, c_v, c_o, d_u, small) = _inproj(
            x2, mod_all[l, 1], _permute_w_in(w_in[l]), w_uq[l].astype(BF16), w_iq[l].astype(BF16), t)
        o_a = _swa(sinks[l], a_q, a_kv, bn, t)
        o_b = _dsa(b_q, b_qi, small, b_kv, b_ki, bn, t)
        gate_bias = jnp.zeros((1, 128), F32).at[0, SM_IG:SM_IG + 4].set(ig_b[l]).at[0, SM_FG:SM_FG + 4].set(fg_b[l])
        o_c = _mlstm(c_qk, c_v, c_o, small, conv_w[l], conv_b[l].reshape(1, -1), gate_bias,
                     mh_norm_g[l].reshape(1, -1), bn, t).reshape(n, GROUP_WIDTH)
        s5p = _s5_params(lam_re[l], lam_im[l], log_step[l], b_re[l], b_im[l], c_re[l], c_im[l])
        o_d = _s5(d_u, *s5p, d_skip[l].reshape(1, -1), glu_w[l].astype(BF16), glu_b[l].reshape(1, -1), bn, t)
        x2 = _outproj(x2, mod_all[l, 1], o_a, o_b, o_c, o_d,
                      w_out[l].astype(BF16).reshape(4, GROUP_WIDTH, d), ln_g[l, 1], ln_b[l, 1], t)
        x2 = _ffn_sublayer(x2, mod_all[l, 2], w13c[1], w2c[1], ln_g[l, 2], ln_b[l, 2], t)
    return x2.reshape(bn, t, d)
```

```python
import functools
import math

import jax
import jax.numpy as jnp
from jax import lax
from jax.experimental import pallas as pl
from jax.experimental.pallas import tpu as pltpu

F32 = jnp.float32
BF16 = jnp.bfloat16
I32 = jnp.int32

D_MODEL = 1024
DEPTH = 2
CHUNK = 64
HEAD_DIM = 64
GROUP_WIDTH = 256
SWA_HEADS = 4
SWA_KV_HEADS = 2
SWA_WIN_CHUNKS = 2
DSA_HEADS = 4
DSA_Q_RANK = 128
IDX_HEADS = 4
IDX_DIM = 32
DSA_TOPK = 256
MLSTM_HEADS = 4
MLSTM_CONV = 4
S5_GROUP_CH = 16
S5_GROUPS = 16
S5_STATE = 64
S5_LANES = S5_GROUPS * S5_STATE
D_FF = 2816
N_SUB = 3
ALPHA = (2 * DEPTH) ** 0.25
LN_EPS = 1e-5
NEG_INF = -1e30
INT_MIN = -(2 ** 31)

SLOPES_A = tuple(2.0 ** -(i + 1) for i in range(0, 8, 2))
SLOPES_B = tuple(2.0 ** -(i + 1) for i in range(1, 8, 2))

VMEM_LIMIT_BYTES = 56 * 1024 * 1024

FFN_TF = 256
FFN_NCHUNK = D_FF // FFN_TF
ROW_TILE = 512
SWA_TQ = 128
DSA_TQ = 128
DSA_TK = 512
S5_TT = 512
S5_SEGS = 8

ZC_QA, ZC_KVA, ZC_CQ, ZC_KVB, ZC_KI = 0, 256, 512, 640, 768
ZC_QKC, ZC_VC, ZC_OC, ZC_UD, ZC_SMALL, Z_WIDTH = 896, 1408, 1664, 1920, 2176, 2304
SM_WI, SM_IG, SM_FG = 0, 4, 8


def _cparams(sem):
    return pltpu.CompilerParams(dimension_semantics=sem, vmem_limit_bytes=VMEM_LIMIT_BYTES)


def _dot(a, b):
    return jnp.dot(a, b, preferred_element_type=F32)


def _dot_nt(a, b):
    return lax.dot_general(a, b, (((1,), (1,)), ((), ())), preferred_element_type=F32)


def _sigmoid(x):
    return 1.0 / (1.0 + jnp.exp(-x))


def _residual_layer_norm(x, y, g, b):
    v = ALPHA * x + y
    mu = jnp.mean(v, axis=-1, keepdims=True)
    d = v - mu
    var = jnp.mean(d * d, axis=-1, keepdims=True)
    return d * lax.rsqrt(var + LN_EPS) * g + b


def _ada_kernel(c_ref, w_ref, b_ref, o_ref):
    c = c_ref[...]
    cs = c * _sigmoid(c)
    o_ref[...] = jnp.dot(cs, w_ref[...], preferred_element_type=F32,
                         precision=lax.Precision.HIGHEST) + b_ref[...]


def _ada_mod(c_pad, ada_w, ada_b):
    nl = ada_w.shape[0]
    ncol = ada_w.shape[2] // D_MODEL
    return pl.pallas_call(
        _ada_kernel,
        grid=(nl, ncol),
        in_specs=[
            pl.BlockSpec((8, D_MODEL), lambda l, j: (0, 0)),
            pl.BlockSpec((None, D_MODEL, D_MODEL), lambda l, j: (l, 0, j)),
            pl.BlockSpec((None, 1, D_MODEL), lambda l, j: (l, 0, j)),
        ],
        out_specs=pl.BlockSpec((None, 8, D_MODEL), lambda l, j: (l, 0, j)),
        out_shape=jax.ShapeDtypeStruct((nl, 8, ada_w.shape[2]), F32),
        compiler_params=_cparams(("arbitrary", "arbitrary")),
        name="ada_mod",
    )(c_pad, ada_w, ada_b.reshape(nl, 1, -1))


def _ffn_kernel(x_ref, mod_ref, w13_ref, w2_ref, lng_ref, lnb_ref, o_ref, h_ref):
    x = x_ref[...]
    shift, scale, gate = mod_ref[0:1, :], mod_ref[1:2, :], mod_ref[2:3, :]
    u = (x * (1.0 + scale) + shift).astype(BF16)

    for j in range(FFN_NCHUNK):
        a = _dot(u, w13_ref[0, j])
        g = _dot(u, w13_ref[1, j])
        h_ref[:, j * FFN_TF:(j + 1) * FFN_TF] = (a * _sigmoid(a) * g).astype(BF16)
    y = _dot(h_ref[...], w2_ref[...])
    o_ref[...] = _residual_layer_norm(x, 0.5 * gate * y, lng_ref[...], lnb_ref[...])


def _ffn_sublayer(x2, mod, w13c, w2c, ln_g, ln_b, rows_per_batch):
    n = x2.shape[0]
    tm = ROW_TILE
    tiles_per_batch = rows_per_batch // tm
    return pl.pallas_call(
        _ffn_kernel,
        grid=(n // tm,),
        in_specs=[
            pl.BlockSpec((tm, D_MODEL), lambda i: (i, 0)),
            pl.BlockSpec((None, 8, D_MODEL), lambda i: (i // tiles_per_batch, 0, 0)),
            pl.BlockSpec(w13c.shape, lambda i: (0, 0, 0, 0), pipeline_mode=pl.Buffered(1)),
            pl.BlockSpec(w2c.shape, lambda i: (0, 0), pipeline_mode=pl.Buffered(1)),
            pl.BlockSpec((1, D_MODEL), lambda i: (0, 0)),
            pl.BlockSpec((1, D_MODEL), lambda i: (0, 0)),
        ],
        out_specs=pl.BlockSpec((tm, D_MODEL), lambda i: (i, 0)),
        out_shape=jax.ShapeDtypeStruct((n, D_MODEL), F32),
        scratch_shapes=[pltpu.VMEM((tm, D_FF), BF16)],
        compiler_params=_cparams(("arbitrary",)),
        name="ffn_sublayer",
    )(x2, mod, w13c, w2c, ln_g.reshape(1, -1), ln_b.reshape(1, -1))


def _inproj_kernel(x_ref, mod_ref, w_ref, wuq_ref, wiq_ref,
                   aq_ref, akv_ref, bq_ref, bqi_ref, bkv_ref, bki_ref,
                   cqk_ref, cv_ref, co_ref, du_ref, sm_ref):
    x = x_ref[...]
    shift, scale = mod_ref[0:1, :], mod_ref[1:2, :]
    u = (x * (1.0 + scale) + shift).astype(BF16)
    z = _dot(u, w_ref[...])
    aq_ref[...] = z[:, ZC_QA:ZC_KVA].astype(BF16)
    akv_ref[...] = z[:, ZC_KVA:ZC_CQ].astype(BF16)
    cq = z[:, ZC_CQ:ZC_KVB].astype(BF16)
    bq_ref[...] = (_dot(cq, wuq_ref[...]) * HEAD_DIM ** -0.5).astype(BF16)
    bqi_ref[...] = _dot(cq, wiq_ref[...]).astype(BF16)
    bkv_ref[...] = z[:, ZC_KVB:ZC_KI].astype(BF16)
    bki_ref[...] = z[:, ZC_KI:ZC_QKC].astype(BF16)
    cqk_ref[...] = z[:, ZC_QKC:ZC_VC]
    cv_ref[...] = z[:, ZC_VC:ZC_OC].astype(BF16)
    co_ref[...] = z[:, ZC_OC:ZC_UD]
    du_ref[...] = z[:, ZC_UD:ZC_SMALL]
    sm_ref[...] = z[:, ZC_SMALL:Z_WIDTH]


def _inproj(x2, mod, w_perm, w_uq, w_iq, rows_per_batch):
    n = x2.shape[0]
    tm = ROW_TILE
    tiles_per_batch = rows_per_batch // tm
    widths = [(256, BF16), (256, BF16), (256, BF16), (128, BF16), (128, BF16), (128, BF16),
              (512, F32), (256, BF16), (256, F32), (256, F32), (128, F32)]
    return pl.pallas_call(
        _inproj_kernel,
        grid=(n // tm,),
        in_specs=[
            pl.BlockSpec((tm, D_MODEL), lambda i: (i, 0)),
            pl.BlockSpec((None, 8, D_MODEL), lambda i: (i // tiles_per_batch, 0, 0)),
            pl.BlockSpec(w_perm.shape, lambda i: (0, 0), pipeline_mode=pl.Buffered(1)),
            pl.BlockSpec(w_uq.shape, lambda i: (0, 0), pipeline_mode=pl.Buffered(1)),
            pl.BlockSpec(w_iq.shape, lambda i: (0, 0), pipeline_mode=pl.Buffered(1)),
        ],
        out_specs=[pl.BlockSpec((tm, w), lambda i: (i, 0)) for w, _ in widths],
        out_shape=[jax.ShapeDtypeStruct((n, w), dt) for w, dt in widths],
        compiler_params=_cparams(("arbitrary",)),
        name="mixer_inproj",
    )(x2, mod, w_perm, w_uq, w_iq)


def _swa_kernel(sink_ref, q_ref, kvc_ref, kvp_ref, o_ref):
    i = pl.program_id(1)
    tq = SWA_TQ
    q = q_ref[...]
    kv = jnp.concatenate([kvp_ref[...], kvc_ref[...]], axis=0)
    qpos = i * tq + lax.broadcasted_iota(I32, (tq, 2 * tq), 0)
    kpos = (i - 1) * tq + lax.broadcasted_iota(I32, (tq, 2 * tq), 1)
    qchunk = qpos // CHUNK
    kchunk = (kpos + tq) // CHUNK - tq // CHUNK
    valid = (kpos >= 0) & (kchunk <= qchunk) & (kchunk >= qchunk - SWA_WIN_CHUNKS)
    dist = jnp.abs(qpos - kpos).astype(F32)
    rep = SWA_HEADS // SWA_KV_HEADS
    for h in range(SWA_HEADS):
        g = h // rep
        qh = q[:, h * HEAD_DIM:(h + 1) * HEAD_DIM]
        kg = kv[:, g * HEAD_DIM:(g + 1) * HEAD_DIM]
        vg = kv[:, (SWA_KV_HEADS + g) * HEAD_DIM:(SWA_KV_HEADS + g + 1) * HEAD_DIM]
        s = _dot_nt(qh, kg) * HEAD_DIM ** -0.5 - SLOPES_A[h] * dist
        s = jnp.where(valid, s, NEG_INF)
        sink = sink_ref[h]
        m = jnp.maximum(jnp.max(s, axis=-1, keepdims=True), sink)
        p = jnp.exp(s - m)
        denom = jnp.sum(p, axis=-1, keepdims=True) + jnp.exp(sink - m)
        o = _dot(p.astype(BF16), vg) / denom
        o_ref[:, h * HEAD_DIM:(h + 1) * HEAD_DIM] = o.astype(BF16)


def _swa(sinks, a_q, a_kv, bn, t):
    nt = t // SWA_TQ
    return pl.pallas_call(
        _swa_kernel,
        grid=(bn, nt),
        in_specs=[
            pl.BlockSpec(memory_space=pltpu.SMEM),
            pl.BlockSpec((SWA_TQ, 256), lambda b, i: (b * nt + i, 0)),
            pl.BlockSpec((SWA_TQ, 256), lambda b, i: (b * nt + i, 0)),
            pl.BlockSpec((SWA_TQ, 256), lambda b, i: (b * nt + jnp.maximum(i - 1, 0), 0)),
        ],
        out_specs=pl.BlockSpec((SWA_TQ, 256), lambda b, i: (b * nt + i, 0)),
        out_shape=jax.ShapeDtypeStruct((bn * t, 256), BF16),
        compiler_params=_cparams(("arbitrary", "arbitrary")),
        name="swa_attention",
    )(sinks, a_q, a_kv, a_kv)


def _sortable_key(x):
    bits = lax.bitcast_convert_type(x, I32)
    return bits ^ ((bits >> 31) & 0x7FFFFFFF)


def _dsa_kernel(q_ref, qi_ref, sm_ref, kv_ref, vt_ref, ki_ref, o_ref, key_ref, s_ref, p_ref):
    i = pl.program_id(1)
    tq, tk = DSA_TQ, DSA_TK
    nblk = (i * tq + tq + tk - 1) // tk
    qpos = i * tq + lax.broadcasted_iota(I32, (1, tq), 1)
    qchunk = qpos // CHUNK
    row_k = lax.broadcasted_iota(I32, (tk, tq), 0)

    qi = qi_ref[...]
    lane_i = lax.broadcasted_iota(I32, (1, IDX_HEADS * IDX_DIM), 1) // IDX_DIM
    qi_stack = jnp.concatenate([jnp.where(lane_i == h, qi, jnp.zeros_like(qi)) for h in range(IDX_HEADS)], axis=0)
    sm_t = sm_ref[...].T
    w_idx = [sm_t[SM_WI + h:SM_WI + h + 1, :] for h in range(IDX_HEADS)]
    idx_scale = (IDX_DIM * IDX_HEADS) ** -0.5
    q = q_ref[...]
    q_stack = jnp.concatenate([q[:, h * HEAD_DIM:(h + 1) * HEAD_DIM] for h in range(DSA_HEADS)], axis=0)

    def score_block(j, mask_inadmissible):
        rows = pl.ds(pl.multiple_of(j * tk, tk), tk)
        s_ref[j] = _dot_nt(kv_ref[rows, 0:HEAD_DIM], q_stack)
        d = _dot_nt(ki_ref[rows, :], qi_stack)
        acc = w_idx[0] * jnp.maximum(d[:, 0:tq], 0.0)
        for h in range(1, IDX_HEADS):
            acc = acc + w_idx[h] * jnp.maximum(d[:, h * tq:(h + 1) * tq], 0.0)
        sc = acc * idx_scale
        if mask_inadmissible:
            sc = jnp.where((j * tk + row_k) // CHUNK <= qchunk, sc, NEG_INF)
        key_ref[j] = _sortable_key(sc)

    def full_block(j, carry):
        score_block(j, False)
        return carry

    lax.fori_loop(0, nblk - 1, full_block, 0)
    score_block(nblk - 1, True)

    k_eff = jnp.minimum(DSA_TOPK, (qchunk + 1) * CHUNK)

    def count(pred):
        def blk(j, c):
            m = pred(key_ref[j], j * tk + row_k).astype(I32)
            return c + jnp.sum(m.reshape(tk // 8, 8, tq), axis=0)
        c = lax.fori_loop(0, nblk, blk, jnp.zeros((8, tq), I32))
        return jnp.sum(c, axis=0, keepdims=True)

    def bit_step(bi, prefix):
        cand_u = prefix | (jnp.int32(1) << (31 - bi))
        cand_s = cand_u ^ INT_MIN
        cnt = count(lambda key, kpos: key >= cand_s)
        return jnp.where(cnt >= k_eff, cand_u, prefix)

    prefix = lax.fori_loop(0, 32, bit_step, jnp.zeros((1, tq), I32))
    thr = prefix ^ INT_MIN

    c_gt = count(lambda key, kpos: key > thr)
    need = (k_eff - c_gt).astype(F32)
    tril = (lax.broadcasted_iota(I32, (tk, tk), 0) >= lax.broadcasted_iota(I32, (tk, tk), 1)).astype(BF16)

    qpos_f = qpos.astype(F32)
    p_ref[1] = jnp.zeros(p_ref.shape[1:], BF16)

    def att_block(j, carry):
        m_run, l_run, acc_part, ties_seen = carry
        slot = j % 2
        acc = acc_part + _dot(vt_ref[jnp.maximum(j - 1, 0)], p_ref[1 - slot])
        key = key_ref[j]
        tie = key == thr
        tie_rank = _dot(tril, jnp.where(tie, 1.0, 0.0).astype(BF16)) + ties_seen
        sel = (key > thr) | (tie & (tie_rank <= need))
        dist = jnp.abs((j * tk + row_k).astype(F32) - qpos_f)
        ms, ls, alphas = [], [], []
        for h in range(DSA_HEADS):
            cols = slice(h * tq, (h + 1) * tq)
            sh = jnp.where(sel, s_ref[j, :, cols] - SLOPES_B[h] * dist, -jnp.inf)
            m_old = m_run[:, cols]
            m_new = jnp.maximum(m_old, jnp.max(sh, axis=0, keepdims=True))
            alpha = jnp.exp(m_old - m_new)
            p = jnp.exp(sh - m_new)
            p_ref[slot, :, cols] = p.astype(BF16)
            ms.append(m_new)
            ls.append(alpha * l_run[:, cols] + jnp.sum(p, axis=0, keepdims=True))
            alphas.append(alpha)
        return (jnp.concatenate(ms, axis=1), jnp.concatenate(ls, axis=1), jnp.concatenate(alphas, axis=1) * acc,
                tie_rank[tk - 1:tk, :])

    init = (jnp.full((1, DSA_HEADS * tq), NEG_INF, F32), jnp.zeros((1, DSA_HEADS * tq), F32),
            jnp.zeros((HEAD_DIM, DSA_HEADS * tq), F32), jnp.zeros((1, tq), F32))
    _, l_run, acc_part, _ = lax.fori_loop(0, nblk, att_block, init)
    acc = acc_part + _dot(vt_ref[nblk - 1], p_ref[(nblk - 1) % 2])
    out = acc / l_run
    for h in range(DSA_HEADS):
        o_ref[:, h * HEAD_DIM:(h + 1) * HEAD_DIM] = out[:, h * tq:(h + 1) * tq].T.astype(BF16)


def _dsa(b_q, b_qi, small, b_kv, b_ki, bn, t):
    nt = t // DSA_TQ
    nkb = t // DSA_TK
    v_t = b_kv[:, HEAD_DIM:].reshape(bn * nkb, DSA_TK, HEAD_DIM).transpose(0, 2, 1)
    return pl.pallas_call(
        _dsa_kernel,
        grid=(bn, nt),
        in_specs=[
            pl.BlockSpec((DSA_TQ, 256), lambda b, i: (b * nt + i, 0)),
            pl.BlockSpec((DSA_TQ, 128), lambda b, i: (b * nt + i, 0)),
            pl.BlockSpec((DSA_TQ, 128), lambda b, i: (b * nt + i, 0)),
            pl.BlockSpec((t, 128), lambda b, i: (b, 0)),
            pl.BlockSpec((nkb, HEAD_DIM, DSA_TK), lambda b, i: (b, 0, 0)),
            pl.BlockSpec((t, 128), lambda b, i: (b, 0)),
        ],
        out_specs=pl.BlockSpec((DSA_TQ, 256), lambda b, i: (b * nt + i, 0)),
        out_shape=jax.ShapeDtypeStruct((bn * t, 256), BF16),
        scratch_shapes=[pltpu.VMEM((nkb, DSA_TK, DSA_TQ), I32),
                        pltpu.VMEM((nkb, DSA_TK, DSA_HEADS * DSA_TQ), F32),
                        pltpu.VMEM((2, DSA_TK, DSA_HEADS * DSA_TQ), BF16)],
        compiler_params=_cparams(("arbitrary", "arbitrary")),
        name="dsa_attention",
    )(b_q, b_qi, small, b_kv, v_t, b_ki)


def _mlstm_kernel(qk_ref, v_ref, og_ref, sm_ref, convw_ref, convb_ref, gbias_ref, normg_ref,
                  o_ref, tail_ref, cmat_ref, nvec_ref, mst_ref, *, bn):
    c = pl.program_id(0)
    L = CHUNK

    @pl.when(c == 0)
    def _():
        tail_ref[...] = jnp.zeros_like(tail_ref)
        cmat_ref[...] = jnp.zeros_like(cmat_ref)
        nvec_ref[...] = jnp.zeros_like(nvec_ref)
        mst_ref[...] = jnp.zeros_like(mst_ref)

    row = lax.broadcasted_iota(I32, (L, L), 0)
    col = lax.broadcasted_iota(I32, (L, L), 1)
    causal = col <= row
    tril = causal.astype(F32)
    convw = convw_ref[...]
    for b in range(bn):
        cur = qk_ref[b]
        ext = jnp.concatenate([tail_ref[b], cur], axis=0)
        tail_ref[b] = cur[L - 8:L, :]
        y = convb_ref[...] + convw[MLSTM_CONV - 1:MLSTM_CONV, :] * cur
        for k in range(MLSTM_CONV - 1):
            off = 8 - (MLSTM_CONV - 1) + k
            y = y + convw[k:k + 1, :] * ext[off:off + L, :]
        qk = y * _sigmoid(y)
        gates = sm_ref[b] + gbias_ref[...]
        lf = jnp.minimum(gates, 0.0) - jnp.log(1.0 + jnp.exp(-jnp.abs(gates)))
        bcum = jnp.dot(tril, lf, preferred_element_type=F32, precision=lax.Precision.HIGHEST)
        gates_t = gates.T
        bcum_t = bcum.T
        vb = v_ref[b]
        og = og_ref[b]
        for h in range(MLSTM_HEADS):
            s_idx = b * MLSTM_HEADS + h
            q = qk[:, h * HEAD_DIM:(h + 1) * HEAD_DIM]
            k = qk[:, (MLSTM_HEADS + h) * HEAD_DIM:(MLSTM_HEADS + h + 1) * HEAD_DIM] * HEAD_DIM ** -0.5
            v = vb[:, h * HEAD_DIM:(h + 1) * HEAD_DIM]
            qb, kb = q.astype(BF16), k.astype(BF16)
            b_col = bcum[:, SM_FG + h:SM_FG + h + 1]
            b_row = bcum_t[SM_FG + h:SM_FG + h + 1, :]
            ig_col = gates[:, SM_IG + h:SM_IG + h + 1]
            ig_row = gates_t[SM_IG + h:SM_IG + h + 1, :]
            m_prev = mst_ref[s_idx]
            cmat = cmat_ref[s_idx]
            nvec = nvec_ref[s_idx]
            dlog = jnp.where(causal, b_col - b_row + ig_row, NEG_INF)
            inter = b_col + m_prev
            mj = jnp.maximum(inter, jnp.max(dlog, axis=-1, keepdims=True))
            dw = jnp.exp(dlog - mj)
            iw = jnp.exp(inter - mj)
            sc = _dot_nt(qb, kb) * dw
            num = iw * _dot(qb, cmat.astype(BF16)) + _dot(sc.astype(BF16), v)
            den = iw * jnp.sum(q * nvec, axis=-1, keepdims=True) + jnp.sum(sc, axis=-1, keepdims=True)
            hj = num / jnp.maximum(jnp.abs(den), jnp.exp(-mj))
            bl = b_col[L - 1:L, :]
            dec_row = bl - b_row + ig_row
            dec_col = bl - b_col + ig_col
            m_new = jnp.maximum(bl + m_prev, jnp.max(dec_row, axis=-1, keepdims=True))
            wc = jnp.exp(bl + m_prev - m_new)
            ws = jnp.exp(dec_col - m_new)
            kw = k * ws
            cmat_ref[s_idx] = wc * cmat + _dot(kw.T.astype(BF16), v)
            nvec_ref[s_idx] = wc * nvec + jnp.sum(kw, axis=0, keepdims=True)
            mst_ref[s_idx] = m_new
            mu = jnp.mean(hj, axis=-1, keepdims=True)
            dh = hj - mu
            var = jnp.mean(dh * dh, axis=-1, keepdims=True)
            hn = dh * lax.rsqrt(var + LN_EPS) * normg_ref[:, h * HEAD_DIM:(h + 1) * HEAD_DIM]
            o = _sigmoid(og[:, h * HEAD_DIM:(h + 1) * HEAD_DIM]) * hn
            o_ref[b, :, h * HEAD_DIM:(h + 1) * HEAD_DIM] = o.astype(BF16)


def _mlstm(c_qk, c_v, c_o, small, conv_w, conv_b, gate_bias, norm_g, bn, t):
    nc = t // CHUNK
    nstate = bn * MLSTM_HEADS
    blk = lambda w: pl.BlockSpec((bn, CHUNK, w), lambda c: (0, c, 0))
    full = lambda a: pl.BlockSpec(a.shape, lambda c: (0,) * a.ndim)
    return pl.pallas_call(
        functools.partial(_mlstm_kernel, bn=bn),
        grid=(nc,),
        in_specs=[blk(512), blk(256), blk(256), blk(128), full(conv_w), full(conv_b), full(gate_bias), full(norm_g)],
        out_specs=blk(256),
        out_shape=jax.ShapeDtypeStruct((bn, t, 256), BF16),
        scratch_shapes=[pltpu.VMEM((bn, 8, 512), F32), pltpu.VMEM((nstate, HEAD_DIM, HEAD_DIM), F32),
                        pltpu.VMEM((nstate, 1, HEAD_DIM), F32), pltpu.VMEM((nstate, 1, 1), F32)],
        compiler_params=_cparams(("arbitrary",)),
        name="mlstm",
    )(c_qk.reshape(bn, t, 512), c_v.reshape(bn, t, 256), c_o.reshape(bn, t, 256), small.reshape(bn, t, 128),
      conv_w, conv_b, gate_bias, norm_g)


def _s5_kernel(u_ref, perm_ref, bbre_ref, bbim_ref, pwre_ref, pwim_ref, cre_ref, cim_ref, dskip_ref, gluw_ref,
               glub_ref, o_ref, sre_ref, sim_ref, stre_ref, stim_ref, yp_ref):
    @pl.when(pl.program_id(1) == 0)
    def _():
        stre_ref[...] = jnp.zeros_like(stre_ref)
        stim_ref[...] = jnp.zeros_like(stim_ref)

    seg_len = S5_TT // S5_SEGS
    u = u_ref[...]
    ub = _dot(perm_ref[...], u.astype(BF16)).astype(BF16)
    sre_ref[...] = _dot(ub, bbre_ref[...])
    sim_ref[...] = _dot(ub, bbim_ref[...])
    a_re = jnp.broadcast_to(pwre_ref[0:1, :], (S5_SEGS, S5_LANES))
    a_im = jnp.broadcast_to(pwim_ref[0:1, :], (S5_SEGS, S5_LANES))

    def local_step(i, carry):
        s_re, s_im = carry
        rows = pl.ds(pl.multiple_of(i * S5_SEGS, S5_SEGS), S5_SEGS)
        n_re = a_re * s_re - a_im * s_im + sre_ref[rows, :]
        n_im = a_re * s_im + a_im * s_re + sim_ref[rows, :]
        sre_ref[rows, :] = n_re
        sim_ref[rows, :] = n_im
        return n_re, n_im

    zeros = jnp.zeros((S5_SEGS, S5_LANES), F32)
    e_re, e_im = lax.fori_loop(0, seg_len, local_step, (zeros, zeros))

    al_re, al_im = pwre_ref[seg_len - 1:seg_len, :], pwim_ref[seg_len - 1:seg_len, :]
    c_re, c_im = stre_ref[0:1, :], stim_ref[0:1, :]
    cs_re, cs_im = [], []
    for k in range(S5_SEGS):
        cs_re.append(c_re)
        cs_im.append(c_im)
        c_re, c_im = (e_re[k:k + 1, :] + al_re * c_re - al_im * c_im,
                      e_im[k:k + 1, :] + al_re * c_im + al_im * c_re)
    stre_ref[...] = jnp.broadcast_to(c_re, stre_ref.shape)
    stim_ref[...] = jnp.broadcast_to(c_im, stim_ref.shape)
    cin_re = jnp.concatenate(cs_re, axis=0)
    cin_im = jnp.concatenate(cs_im, axis=0)

    def correct_step(i, carry):
        rows = pl.ds(pl.multiple_of(i * S5_SEGS, S5_SEGS), S5_SEGS)
        p_re, p_im = pwre_ref[pl.ds(i, 1), :], pwim_ref[pl.ds(i, 1), :]
        sre_ref[rows, :] = sre_ref[rows, :] + p_re * cin_re - p_im * cin_im
        sim_ref[rows, :] = sim_ref[rows, :] + p_re * cin_im + p_im * cin_re
        return carry

    lax.fori_loop(0, seg_len, correct_step, 0)
    ycs = _dot(sre_ref[...].astype(BF16), cre_ref[...]) - _dot(sim_ref[...].astype(BF16), cim_ref[...])
    ngrp = GROUP_WIDTH // 128
    for g in range(ngrp):
        yp_ref[g] = ycs[:, g * 128:(g + 1) * 128]
    blocks = []
    for k in range(S5_SEGS):
        for i0 in range(0, seg_len, 8):
            rows = pl.ds(i0 * S5_SEGS + k, 8, stride=S5_SEGS)
            blocks.append(jnp.concatenate([yp_ref[g, rows, :] for g in range(ngrp)], axis=1))
    y = jnp.concatenate(blocks, axis=0) + dskip_ref[...] * u
    y = 0.5 * y * (1.0 + jnp.tanh(math.sqrt(2.0 / math.pi) * (y + 0.044715 * (y * y * y))))
    z = _dot(y.astype(BF16), gluw_ref[...]) + glub_ref[...]
    o_ref[...] = (y * _sigmoid(z)).astype(BF16)


def _s5(d_u, bb_re, bb_im, pw_re, pw_im, c_re_t, c_im_t, d_skip, glu_w, glu_b, bn, t):
    nt = t // S5_TT
    full = lambda a: pl.BlockSpec(a.shape, lambda b, i: (0,) * a.ndim)
    r = jnp.arange(S5_TT)
    perm = (r[None, :] == ((r % S5_SEGS) * (S5_TT // S5_SEGS) + r // S5_SEGS)[:, None]).astype(BF16)
    args = (perm, bb_re, bb_im, pw_re, pw_im, c_re_t, c_im_t, d_skip, glu_w, glu_b)
    return pl.pallas_call(
        _s5_kernel,
        grid=(bn, nt),
        in_specs=[pl.BlockSpec((S5_TT, 256), lambda b, i: (b * nt + i, 0))] + [full(a) for a in args],
        out_specs=pl.BlockSpec((S5_TT, 256), lambda b, i: (b * nt + i, 0)),
        out_shape=jax.ShapeDtypeStruct((bn * t, 256), BF16),
        scratch_shapes=[pltpu.VMEM((S5_TT, S5_LANES), F32), pltpu.VMEM((S5_TT, S5_LANES), F32),
                        pltpu.VMEM((8, S5_LANES), F32), pltpu.VMEM((8, S5_LANES), F32),
                        pltpu.VMEM((GROUP_WIDTH // 128, S5_TT, 128), F32)],
        compiler_params=_cparams(("arbitrary", "arbitrary")),
        name="s5_glu",
    )(d_u, *args)


def _outproj_kernel(x_ref, mod_ref, oa_ref, ob_ref, oc_ref, od_ref, w_ref, lng_ref, lnb_ref, o_ref):
    x = x_ref[...]
    gate = mod_ref[2:3, :]
    y = _dot(oa_ref[...], w_ref[0])
    y = y + _dot(ob_ref[...], w_ref[1])
    y = y + _dot(oc_ref[...], w_ref[2])
    y = y + _dot(od_ref[...], w_ref[3])
    o_ref[...] = _residual_layer_norm(x, gate * y, lng_ref[...], lnb_ref[...])


def _outproj(x2, mod, o_a, o_b, o_c, o_d, w_out4, ln_g, ln_b, rows_per_batch):
    n = x2.shape[0]
    tm = ROW_TILE
    tiles_per_batch = rows_per_batch // tm
    mix = pl.BlockSpec((tm, GROUP_WIDTH), lambda i: (i, 0))
    return pl.pallas_call(
        _outproj_kernel,
        grid=(n // tm,),
        in_specs=[
            pl.BlockSpec((tm, D_MODEL), lambda i: (i, 0)),
            pl.BlockSpec((None, 8, D_MODEL), lambda i: (i // tiles_per_batch, 0, 0)),
            mix, mix, mix, mix,
            pl.BlockSpec(w_out4.shape, lambda i: (0, 0, 0), pipeline_mode=pl.Buffered(1)),
            pl.BlockSpec((1, D_MODEL), lambda i: (0, 0)),
            pl.BlockSpec((1, D_MODEL), lambda i: (0, 0)),
        ],
        out_specs=pl.BlockSpec((tm, D_MODEL), lambda i: (i, 0)),
        out_shape=jax.ShapeDtypeStruct((n, D_MODEL), F32),
        compiler_params=_cparams(("arbitrary",)),
        name="mixer_outproj",
    )(x2, mod, o_a, o_b, o_c, o_d, w_out4, ln_g.reshape(1, -1), ln_b.reshape(1, -1))


def _permute_w_in(w_in):
    off = {}
    o = 0
    for name, s in (("qa", 256), ("ka", 128), ("va", 128), ("cq", 128), ("kb", 64), ("vb", 64), ("ki", 32),
                    ("wi", 4), ("qkc", 512), ("vc", 256), ("ig", 4), ("fg", 4), ("oc", 256), ("ud", 256)):
        off[name] = (o, o + s)
        o += s
    col = lambda n: w_in[:, off[n][0]:off[n][1]]
    small = jnp.concatenate([col("wi"), col("ig"), col("fg"),
                             jnp.zeros((w_in.shape[0], 128 - 12), w_in.dtype)], axis=1)
    parts = [col("qa"), col("ka"), col("va"), col("cq"), col("kb"), col("vb")] + [col("ki")] * 4 + [
        col("qkc"), col("vc"), col("oc"), col("ud"), small]
    return jnp.concatenate(parts, axis=1).astype(BF16)


def _s5_params(lam_re, lam_im, log_step, b_re, b_im, c_re, c_im):
    dt = jnp.exp(log_step)[:, None]
    mag = jnp.exp(lam_re * dt)
    a_re, a_im = mag * jnp.cos(lam_im * dt), mag * jnp.sin(lam_im * dt)
    den = lam_re * lam_re + lam_im * lam_im
    kap_re = ((a_re - 1.0) * lam_re + a_im * lam_im) / den
    kap_im = (a_im * lam_re - (a_re - 1.0) * lam_im) / den
    bb_re = kap_re[..., None] * b_re - kap_im[..., None] * b_im
    bb_im = kap_re[..., None] * b_im + kap_im[..., None] * b_re
    eye = jnp.eye(S5_GROUPS, dtype=F32)

    def in_mat(bb):
        return jnp.einsum("gph,gk->ghkp", bb, eye).reshape(S5_GROUPS * S5_GROUP_CH, S5_LANES).astype(BF16)

    def out_mat(cc):
        return jnp.einsum("gop,gk->gpko", cc, eye).reshape(S5_LANES, S5_GROUPS * S5_GROUP_CH).astype(BF16)

    n = jnp.arange(1, S5_TT // S5_SEGS + 1, dtype=F32)[:, None, None]
    pw_mag = jnp.exp(n * (lam_re * dt))
    pw_re = (pw_mag * jnp.cos(n * (lam_im * dt))).at[0].set(a_re).reshape(-1, S5_LANES)
    pw_im = (pw_mag * jnp.sin(n * (lam_im * dt))).at[0].set(a_im).reshape(-1, S5_LANES)
    return in_mat(bb_re), in_mat(bb_im), pw_re, pw_im, out_mat(c_re), out_mat(c_im)


def kernel(x, c, ada_w, ada_b, ln_g, ln_b, ffn_w13, ffn_w2, w_in, w_out, sinks, w_uq, w_iq, conv_w, conv_b, ig_b,
           fg_b, mh_norm_g, lam_re, lam_im, log_step, b_re, b_im, c_re, c_im, d_skip, glu_w, glu_b):
    bn, t, d = x.shape
    assert d == D_MODEL and t % max(ROW_TILE, DSA_TK, S5_TT) == 0 and bn <= 8
    n = bn * t
    nl = ada_w.shape[0]
    c_pad = jnp.zeros((8, d), F32).at[:bn].set(c)
    mod_all = _ada_mod(c_pad, ada_w, ada_b)
    mod_all = mod_all[:, :bn].reshape(nl, bn, N_SUB, 3, d).transpose(0, 2, 1, 3, 4)
    mod_all = jnp.pad(mod_all, ((0, 0), (0, 0), (0, 0), (0, 5), (0, 0)))

    x2 = x.reshape(n, d)
    for l in range(nl):
        w13c = ffn_w13[l].astype(BF16).reshape(2, d, 2, FFN_NCHUNK, FFN_TF).transpose(0, 2, 3, 1, 4)
        w2c = ffn_w2[l].astype(BF16)
        x2 = _ffn_sublayer(x2, mod_all[l, 0], w13c[0], w2c[0], ln_g[l, 0], ln_b[l, 0], t)
        (a_q, a_kv, b_q, b_qi, b_kv, b_ki, c_qk, c_v, c_o, d_u, small) = _inproj(
            x2, mod_all[l, 1], _permute_w_in(w_in[l]), w_uq[l].astype(BF16), w_iq[l].astype(BF16), t)
        o_a = _swa(sinks[l], a_q, a_kv, bn, t)
        o_b = _dsa(b_q, b_qi, small, b_kv, b_ki, bn, t)
        gate_bias = jnp.zeros((1, 128), F32).at[0, SM_IG:SM_IG + 4].set(ig_b[l]).at[0, SM_FG:SM_FG + 4].set(fg_b[l])
        o_c = _mlstm(c_qk, c_v, c_o, small, conv_w[l], conv_b[l].reshape(1, -1), gate_bias,
                     mh_norm_g[l].reshape(1, -1), bn, t).reshape(n, GROUP_WIDTH)
        s5p = _s5_params(lam_re[l], lam_im[l], log_step[l], b_re[l], b_im[l], c_re[l], c_im[l])
        o_d = _s5(d_u, *s5p, d_skip[l].reshape(1, -1), glu_w[l].astype(BF16), glu_b[l].reshape(1, -1), bn, t)
        x2 = _outproj(x2, mod_all[l, 1], o_a, o_b, o_c, o_d,
                      w_out[l].astype(BF16).reshape(4, GROUP_WIDTH, d), ln_g[l, 1], ln_b[l, 1], t)
        x2 = _ffn_sublayer(x2, mod_all[l, 2], w13c[1], w2c[1], ln_g[l, 2], ln_b[l, 2], t)
    return x2.reshape(bn, t, d)
```

```python
import functools
import math

import jax
import jax.numpy as jnp
from jax import lax
from jax.experimental import pallas as pl
from jax.experimental.pallas import tpu as pltpu

F32 = jnp.float32
BF16 = jnp.bfloat16
I32 = jnp.int32

D_MODEL = 1024
DEPTH = 2
CHUNK = 64
HEAD_DIM = 64
GROUP_WIDTH = 256
SWA_HEADS = 4
SWA_KV_HEADS = 2
SWA_WIN_CHUNKS = 2
DSA_HEADS = 4
DSA_Q_RANK = 128
IDX_HEADS = 4
IDX_DIM = 32
DSA_TOPK = 256
MLSTM_HEADS = 4
MLSTM_CONV = 4
S5_GROUP_CH = 16
S5_GROUPS = 16
S5_STATE = 64
S5_LANES = S5_GROUPS * S5_STATE
D_FF = 2816
N_SUB = 3
ALPHA = (2 * DEPTH) ** 0.25
LN_EPS = 1e-5
NEG_INF = -1e30
INT_MIN = -(2 ** 31)

SLOPES_A = tuple(2.0 ** -(i + 1) for i in range(0, 8, 2))
SLOPES_B = tuple(2.0 ** -(i + 1) for i in range(1, 8, 2))

VMEM_LIMIT_BYTES = 56 * 1024 * 1024

FFN_TF = 256
FFN_NCHUNK = D_FF // FFN_TF
ROW_TILE = 512
SWA_TQ = 128
DSA_TQ = 128
DSA_TK = 512
S5_TT = 512
S5_SEGS = 8

ZC_QA, ZC_KVA, ZC_CQ, ZC_KVB, ZC_KI = 0, 256, 512, 640, 768
ZC_QKC, ZC_VC, ZC_OC, ZC_UD, ZC_SMALL, Z_WIDTH = 896, 1408, 1664, 1920, 2176, 2304
SM_WI, SM_IG, SM_FG = 0, 4, 8


def _cparams(sem):
    return pltpu.CompilerParams(dimension_semantics=sem, vmem_limit_bytes=VMEM_LIMIT_BYTES)


def _dot(a, b):
    return jnp.dot(a, b, preferred_element_type=F32)


def _dot_nt(a, b):
    return lax.dot_general(a, b, (((1,), (1,)), ((), ())), preferred_element_type=F32)


def _sigmoid(x):
    return 1.0 / (1.0 + jnp.exp(-x))


def _residual_layer_norm(x, y, g, b):
    v = ALPHA * x + y
    mu = jnp.mean(v, axis=-1, keepdims=True)
    d = v - mu
    var = jnp.mean(d * d, axis=-1, keepdims=True)
    return d * lax.rsqrt(var + LN_EPS) * g + b


def _ada_kernel(c_ref, w_ref, b_ref, o_ref):
    c = c_ref[...]
    cs = c * _sigmoid(c)
    o_ref[...] = jnp.dot(cs, w_ref[...], preferred_element_type=F32,
                         precision=lax.Precision.HIGHEST) + b_ref[...]


def _ada_mod(c_pad, ada_w, ada_b):
    nl = ada_w.shape[0]
    ncol = ada_w.shape[2] // D_MODEL
    return pl.pallas_call(
        _ada_kernel,
        grid=(nl, ncol),
        in_specs=[
            pl.BlockSpec((8, D_MODEL), lambda l, j: (0, 0)),
            pl.BlockSpec((None, D_MODEL, D_MODEL), lambda l, j: (l, 0, j)),
            pl.BlockSpec((None, 1, D_MODEL), lambda l, j: (l, 0, j)),
        ],
        out_specs=pl.BlockSpec((None, 8, D_MODEL), lambda l, j: (l, 0, j)),
        out_shape=jax.ShapeDtypeStruct((nl, 8, ada_w.shape[2]), F32),
        compiler_params=_cparams(("arbitrary", "arbitrary")),
        name="ada_mod",
    )(c_pad, ada_w, ada_b.reshape(nl, 1, -1))


def _ffn_kernel(x_ref, mod_ref, w13_ref, w2_ref, lng_ref, lnb_ref, o_ref, h_ref):
    x = x_ref[...]
    shift, scale, gate = mod_ref[0:1, :], mod_ref[1:2, :], mod_ref[2:3, :]
    u = (x * (1.0 + scale) + shift).astype(BF16)

    for j in range(FFN_NCHUNK):
        a = _dot(u, w13_ref[0, j])
        g = _dot(u, w13_ref[1, j])
        h_ref[:, j * FFN_TF:(j + 1) * FFN_TF] = (a * _sigmoid(a) * g).astype(BF16)
    y = _dot(h_ref[...], w2_ref[...])
    o_ref[...] = _residual_layer_norm(x, 0.5 * gate * y, lng_ref[...], lnb_ref[...])


def _ffn_sublayer(x2, mod, w13c, w2c, ln_g, ln_b, rows_per_batch):
    n = x2.shape[0]
    tm = ROW_TILE
    tiles_per_batch = rows_per_batch // tm
    return pl.pallas_call(
        _ffn_kernel,
        grid=(n // tm,),
        in_specs=[
            pl.BlockSpec((tm, D_MODEL), lambda i: (i, 0)),
            pl.BlockSpec((None, 8, D_MODEL), lambda i: (i // tiles_per_batch, 0, 0)),
            pl.BlockSpec(w13c.shape, lambda i: (0, 0, 0, 0), pipeline_mode=pl.Buffered(1)),
            pl.BlockSpec(w2c.shape, lambda i: (0, 0), pipeline_mode=pl.Buffered(1)),
            pl.BlockSpec((1, D_MODEL), lambda i: (0, 0)),
            pl.BlockSpec((1, D_MODEL), lambda i: (0, 0)),
        ],
        out_specs=pl.BlockSpec((tm, D_MODEL), lambda i: (i, 0)),
        out_shape=jax.ShapeDtypeStruct((n, D_MODEL), F32),
        scratch_shapes=[pltpu.VMEM((tm, D_FF), BF16)],
        compiler_params=_cparams(("arbitrary",)),
        name="ffn_sublayer",
    )(x2, mod, w13c, w2c, ln_g.reshape(1, -1), ln_b.reshape(1, -1))


def _inproj_kernel(x_ref, mod_ref, w_ref, wuq_ref, wiq_ref,
                   aq_ref, akv_ref, bq_ref, bqi_ref, bkv_ref, bki_ref,
                   cqk_ref, cv_ref, co_ref, du_ref, sm_ref):
    x = x_ref[...]
    shift, scale = mod_ref[0:1, :], mod_ref[1:2, :]
    u = (x * (1.0 + scale) + shift).astype(BF16)
    z = _dot(u, w_ref[...])
    aq_ref[...] = z[:, ZC_QA:ZC_KVA].astype(BF16)
    akv_ref[...] = z[:, ZC_KVA:ZC_CQ].astype(BF16)
    cq = z[:, ZC_CQ:ZC_KVB].astype(BF16)
    bq_ref[...] = (_dot(cq, wuq_ref[...]) * HEAD_DIM ** -0.5).astype(BF16)
    bqi_ref[...] = _dot(cq, wiq_ref[...]).astype(BF16)
    bkv_ref[...] = z[:, ZC_KVB:ZC_KI].astype(BF16)
    bki_ref[...] = z[:, ZC_KI:ZC_QKC].astype(BF16)
    cqk_ref[...] = z[:, ZC_QKC:ZC_VC]
    cv_ref[...] = z[:, ZC_VC:ZC_OC].astype(BF16)
    co_ref[...] = z[:, ZC_OC:ZC_UD]
    du_ref[...] = z[:, ZC_UD:ZC_SMALL]
    sm_ref[...] = z[:, ZC_SMALL:Z_WIDTH]


def _inproj(x2, mod, w_perm, w_uq, w_iq, rows_per_batch):
    n = x2.shape[0]
    tm = ROW_TILE
    tiles_per_batch = rows_per_batch // tm
    widths = [(256, BF16), (256, BF16), (256, BF16), (128, BF16), (128, BF16), (128, BF16),
              (512, F32), (256, BF16), (256, F32), (256, F32), (128, F32)]
    return pl.pallas_call(
        _inproj_kernel,
        grid=(n // tm,),
        in_specs=[
            pl.BlockSpec((tm, D_MODEL), lambda i: (i, 0)),
            pl.BlockSpec((None, 8, D_MODEL), lambda i: (i // tiles_per_batch, 0, 0)),
            pl.BlockSpec(w_perm.shape, lambda i: (0, 0), pipeline_mode=pl.Buffered(1)),
            pl.BlockSpec(w_uq.shape, lambda i: (0, 0), pipeline_mode=pl.Buffered(1)),
            pl.BlockSpec(w_iq.shape, lambda i: (0, 0), pipeline_mode=pl.Buffered(1)),
        ],
        out_specs=[pl.BlockSpec((tm, w), lambda i: (i, 0)) for w, _ in widths],
        out_shape=[jax.ShapeDtypeStruct((n, w), dt) for w, dt in widths],
        compiler_params=_cparams(("arbitrary",)),
        name="mixer_inproj",
    )(x2, mod, w_perm, w_uq, w_iq)


def _swa_kernel(sink_ref, q_ref, kvc_ref, kvp_ref, o_ref):
    i = pl.program_id(1)
    tq = SWA_TQ
    q = q_ref[...]
    kv = jnp.concatenate([kvp_ref[...], kvc_ref[...]], axis=0)
    qpos = i * tq + lax.broadcasted_iota(I32, (tq, 2 * tq), 0)
    kpos = (i - 1) * tq + lax.broadcasted_iota(I32, (tq, 2 * tq), 1)
    qchunk = qpos // CHUNK
    kchunk = (kpos + tq) // CHUNK - tq // CHUNK
    valid = (kpos >= 0) & (kchunk <= qchunk) & (kchunk >= qchunk - SWA_WIN_CHUNKS)
    dist = jnp.abs(qpos - kpos).astype(F32)
    rep = SWA_HEADS // SWA_KV_HEADS
    for h in range(SWA_HEADS):
        g = h // rep
        qh = q[:, h * HEAD_DIM:(h + 1) * HEAD_DIM]
        kg = kv[:, g * HEAD_DIM:(g + 1) * HEAD_DIM]
        vg = kv[:, (SWA_KV_HEADS + g) * HEAD_DIM:(SWA_KV_HEADS + g + 1) * HEAD_DIM]
        s = _dot_nt(qh, kg) * HEAD_DIM ** -0.5 - SLOPES_A[h] * dist
        s = jnp.where(valid, s, NEG_INF)
        sink = sink_ref[h]
        m = jnp.maximum(jnp.max(s, axis=-1, keepdims=True), sink)
        p = jnp.exp(s - m)
        denom = jnp.sum(p, axis=-1, keepdims=True) + jnp.exp(sink - m)
        o = _dot(p.astype(BF16), vg) / denom
        o_ref[:, h * HEAD_DIM:(h + 1) * HEAD_DIM] = o.astype(BF16)


def _swa(sinks, a_q, a_kv, bn, t):
    nt = t // SWA_TQ
    return pl.pallas_call(
        _swa_kernel,
        grid=(bn, nt),
        in_specs=[
            pl.BlockSpec(memory_space=pltpu.SMEM),
            pl.BlockSpec((SWA_TQ, 256), lambda b, i: (b * nt + i, 0)),
            pl.BlockSpec((SWA_TQ, 256), lambda b, i: (b * nt + i, 0)),
            pl.BlockSpec((SWA_TQ, 256), lambda b, i: (b * nt + jnp.maximum(i - 1, 0), 0)),
        ],
        out_specs=pl.BlockSpec((SWA_TQ, 256), lambda b, i: (b * nt + i, 0)),
        out_shape=jax.ShapeDtypeStruct((bn * t, 256), BF16),
        compiler_params=_cparams(("arbitrary", "arbitrary")),
        name="swa_attention",
    )(sinks, a_q, a_kv, a_kv)


def _sortable_key(x):
    bits = lax.bitcast_convert_type(x, I32)
    return bits ^ ((bits >> 31) & 0x7FFFFFFF)


def _dsa_kernel(q_ref, qi_ref, sm_ref, kv_ref, vt_ref, ki_ref, o_ref, key_ref, s_ref, p_ref):
    i = pl.program_id(1)
    tq, tk = DSA_TQ, DSA_TK
    nblk = (i * tq + tq + tk - 1) // tk
    qpos = i * tq + lax.broadcasted_iota(I32, (1, tq), 1)
    qchunk = qpos // CHUNK
    row_k = lax.broadcasted_iota(I32, (tk, tq), 0)

    qi = qi_ref[...]
    lane_i = lax.broadcasted_iota(I32, (1, IDX_HEADS * IDX_DIM), 1) // IDX_DIM
    qi_stack = jnp.concatenate([jnp.where(lane_i == h, qi, jnp.zeros_like(qi)) for h in range(IDX_HEADS)], axis=0)
    sm_t = sm_ref[...].T
    w_idx = [sm_t[SM_WI + h:SM_WI + h + 1, :] for h in range(IDX_HEADS)]
    idx_scale = (IDX_DIM * IDX_HEADS) ** -0.5
    q = q_ref[...]
    q_stack = jnp.concatenate([q[:, h * HEAD_DIM:(h + 1) * HEAD_DIM] for h in range(DSA_HEADS)], axis=0)

    def score_block(j, mask_inadmissible):
        rows = pl.ds(pl.multiple_of(j * tk, tk), tk)
        s_ref[j] = _dot_nt(kv_ref[rows, 0:HEAD_DIM], q_stack)
        d = _dot_nt(ki_ref[rows, :], qi_stack)
        acc = w_idx[0] * jnp.maximum(d[:, 0:tq], 0.0)
        for h in range(1, IDX_HEADS):
            acc = acc + w_idx[h] * jnp.maximum(d[:, h * tq:(h + 1) * tq], 0.0)
        sc = acc * idx_scale
        if mask_inadmissible:
            sc = jnp.where((j * tk + row_k) // CHUNK <= qchunk, sc, NEG_INF)
        key_ref[j] = _sortable_key(sc)

    def full_block(j, carry):
        score_block(j, False)
        return carry

    lax.fori_loop(0, nblk - 1, full_block, 0)
    score_block(nblk - 1, True)

    k_eff = jnp.minimum(DSA_TOPK, (qchunk + 1) * CHUNK)

    def count(pred):
        def blk(j, c):
            m = pred(key_ref[j], j * tk + row_k).astype(I32)
            return c + jnp.sum(m.reshape(tk // 8, 8, tq), axis=0)
        c = lax.fori_loop(0, nblk, blk, jnp.zeros((8, tq), I32))
        return jnp.sum(c, axis=0, keepdims=True)

    def bit_step(bi, prefix):
        cand_u = prefix | (jnp.int32(1) << (31 - bi))
        cand_s = cand_u ^ INT_MIN
        cnt = count(lambda key, kpos: key >= cand_s)
        return jnp.where(cnt >= k_eff, cand_u, prefix)

    prefix = lax.fori_loop(0, 32, bit_step, jnp.zeros((1, tq), I32))
    thr = prefix ^ INT_MIN

    c_gt = count(lambda key, kpos: key > thr)
    need = (k_eff - c_gt).astype(F32)
    tril = (lax.broadcasted_iota(I32, (tk, tk), 0) >= lax.broadcasted_iota(I32, (tk, tk), 1)).astype(BF16)

    qpos_f = qpos.astype(F32)
    p_ref[1] = jnp.zeros(p_ref.shape[1:], BF16)

    def att_block(j, carry):
        m_run, l_run, acc_part, ties_seen = carry
        slot = j % 2
        acc = acc_part + _dot(vt_ref[jnp.maximum(j - 1, 0)], p_ref[1 - slot])
        key = key_ref[j]
        tie = key == thr
        tie_rank = _dot(tril, jnp.where(tie, 1.0, 0.0).astype(BF16)) + ties_seen
        sel = (key > thr) | (tie & (tie_rank <= need))
        dist = jnp.abs((j * tk + row_k).astype(F32) - qpos_f)
        ms, ls, alphas = [], [], []
        for h in range(DSA_HEADS):
            cols = slice(h * tq, (h + 1) * tq)
            sh = jnp.where(sel, s_ref[j, :, cols] - SLOPES_B[h] * dist, -jnp.inf)
            m_old = m_run[:, cols]
            m_new = jnp.maximum(m_old, jnp.max(sh, axis=0, keepdims=True))
            alpha = jnp.exp(m_old - m_new)
            p = jnp.exp(sh - m_new)
            p_ref[slot, :, cols] = p.astype(BF16)
            ms.append(m_new)
            ls.append(alpha * l_run[:, cols] + jnp.sum(p, axis=0, keepdims=True))
            alphas.append(alpha)
        return (jnp.concatenate(ms, axis=1), jnp.concatenate(ls, axis=1), jnp.concatenate(alphas, axis=1) * acc,
                tie_rank[tk - 1:tk, :])

    init = (jnp.full((1, DSA_HEADS * tq), NEG_INF, F32), jnp.zeros((1, DSA_HEADS * tq), F32),
            jnp.zeros((HEAD_DIM, DSA_HEADS * tq), F32), jnp.zeros((1, tq), F32))
    _, l_run, acc_part, _ = lax.fori_loop(0, nblk, att_block, init)
    acc = acc_part + _dot(vt_ref[nblk - 1], p_ref[(nblk - 1) % 2])
    out = acc / l_run
    for h in range(DSA_HEADS):
        o_ref[:, h * HEAD_DIM:(h + 1) * HEAD_DIM] = out[:, h * tq:(h + 1) * tq].T.astype(BF16)


def _dsa(b_q, b_qi, small, b_kv, b_ki, bn, t):
    nt = t // DSA_TQ
    nkb = t // DSA_TK
    v_t = b_kv[:, HEAD_DIM:].reshape(bn * nkb, DSA_TK, HEAD_DIM).transpose(0, 2, 1)
    return pl.pallas_call(
        _dsa_kernel,
        grid=(bn, nt),
        in_specs=[
            pl.BlockSpec((DSA_TQ, 256), lambda b, i: (b * nt + i, 0)),
            pl.BlockSpec((DSA_TQ, 128), lambda b, i: (b * nt + i, 0)),
            pl.BlockSpec((DSA_TQ, 128), lambda b, i: (b * nt + i, 0)),
            pl.BlockSpec((t, 128), lambda b, i: (b, 0)),
            pl.BlockSpec((nkb, HEAD_DIM, DSA_TK), lambda b, i: (b, 0, 0)),
            pl.BlockSpec((t, 128), lambda b, i: (b, 0)),
        ],
        out_specs=pl.BlockSpec((DSA_TQ, 256), lambda b, i: (b * nt + i, 0)),
        out_shape=jax.ShapeDtypeStruct((bn * t, 256), BF16),
        scratch_shapes=[pltpu.VMEM((nkb, DSA_TK, DSA_TQ), I32),
                        pltpu.VMEM((nkb, DSA_TK, DSA_HEADS * DSA_TQ), F32),
                        pltpu.VMEM((2, DSA_TK, DSA_HEADS * DSA_TQ), BF16)],
        compiler_params=_cparams(("arbitrary", "arbitrary")),
        name="dsa_attention",
    )(b_q, b_qi, small, b_kv, v_t, b_ki)


def _mlstm_kernel(qk_ref, v_ref, og_ref, sm_ref, convw_ref, convb_ref, gbias_ref, normgt_ref,
                  o_ref, tail_ref, ct_ref, nvec_ref, mst_ref, *, bn):
    c = pl.program_id(0)
    L = CHUNK
    nh, dh, width = MLSTM_HEADS, HEAD_DIM, MLSTM_HEADS * HEAD_DIM

    @pl.when(c == 0)
    def _():
        tail_ref[...] = jnp.zeros_like(tail_ref)
        ct_ref[...] = jnp.zeros_like(ct_ref)
        nvec_ref[...] = jnp.zeros_like(nvec_ref)
        mst_ref[...] = jnp.zeros_like(mst_ref)

    srow = lax.broadcasted_iota(I32, (L, width), 0)
    lane = lax.broadcasted_iota(I32, (L, width), 1)
    jlane = lane % dh
    causal_t = srow <= jlane
    diag_t = srow == jlane
    head_of_lane = lax.broadcasted_iota(I32, (1, width), 1) // dh
    tril = (lax.broadcasted_iota(I32, (L, L), 1) <= lax.broadcasted_iota(I32, (L, L), 0)).astype(F32)
    erow = lax.broadcasted_iota(I32, (128, width), 0)
    ecol_head = lax.broadcasted_iota(I32, (128, width), 1) // dh
    expand_ig = (erow == SM_IG + ecol_head).astype(F32)
    expand_fg = (erow == SM_FG + ecol_head).astype(F32)
    exact = dict(preferred_element_type=F32, precision=lax.Precision.HIGHEST)

    def head_blocks(a):
        out = jnp.where(head_of_lane == 0, a[0:dh], 0.0)
        for h in range(1, nh):
            out = out + jnp.where(head_of_lane == h, a[h * dh:(h + 1) * dh], 0.0)
        return out

    convw = convw_ref[...]
    for b in range(bn):
        cur = qk_ref[b]
        ext = jnp.concatenate([tail_ref[b], cur], axis=0)
        tail_ref[b] = cur[L - 8:L, :]
        y = convb_ref[...] + convw[MLSTM_CONV - 1:MLSTM_CONV, :] * cur
        for k in range(MLSTM_CONV - 1):
            off = 8 - (MLSTM_CONV - 1) + k
            y = y + convw[k:k + 1, :] * ext[off:off + L, :]
        qk = y * _sigmoid(y)
        q_all = qk[:, 0:width]
        k_all = qk[:, width:2 * width] * dh ** -0.5
        q_stack = jnp.concatenate([jnp.where(head_of_lane == h, q_all, 0.0) for h in range(nh)], axis=0).astype(BF16)
        v_all = v_ref[b]
        v_t = v_all.astype(F32).T.astype(BF16)

        gates = sm_ref[b] + gbias_ref[...]
        lf = jnp.minimum(gates, 0.0) - jnp.log(1.0 + jnp.exp(-jnp.abs(gates)))
        bcum = jnp.dot(tril, lf, **exact)
        ig_x = jnp.dot(gates, expand_ig, **exact)
        b_x = jnp.dot(bcum, expand_fg, **exact)
        b_q = jnp.sum(jnp.where(diag_t, b_x, 0.0), axis=0, keepdims=True)
        b_last = b_x[L - 1:L, :]
        m_prev = mst_ref[b]
        ct = ct_ref[b]
        nvec = nvec_ref[b]

        dlog = jnp.where(causal_t, b_q - b_x + ig_x, NEG_INF)
        inter = b_q + m_prev
        mj = jnp.maximum(inter, jnp.max(dlog, axis=0, keepdims=True))
        dw = jnp.exp(dlog - mj)
        iw = jnp.exp(inter - mj)
        sc = _dot_nt(k_all.astype(BF16), q_stack) * dw
        qn = _dot_nt(jnp.broadcast_to(nvec, (8, width)).astype(BF16), q_stack)[0:1, :]
        q_c = _dot_nt(ct.astype(BF16), q_stack)
        num = iw * q_c + head_blocks(_dot(v_t, sc.astype(BF16)))
        den = iw * qn + jnp.sum(sc, axis=0, keepdims=True)
        hj = num / jnp.maximum(jnp.abs(den), jnp.exp(-mj))

        dec = b_last - b_x + ig_x
        m_new = jnp.maximum(b_last + m_prev, jnp.max(dec, axis=0, keepdims=True))
        wc = jnp.exp(b_last + m_prev - m_new)
        kw = k_all * jnp.exp(dec - m_new)
        ct_ref[b] = wc * ct + head_blocks(_dot(v_t, kw.astype(BF16)))
        nvec_ref[b] = wc * nvec + jnp.sum(kw, axis=0, keepdims=True)
        mst_ref[b] = m_new

        mu = jnp.mean(hj, axis=0, keepdims=True)
        dev = hj - mu
        var = jnp.mean(dev * dev, axis=0, keepdims=True)
        hn_t = (dev * lax.rsqrt(var + LN_EPS) * normgt_ref[...]).T
        hn = jnp.concatenate([hn_t[h * dh:(h + 1) * dh, :] for h in range(nh)], axis=1)
        o_ref[b] = (_sigmoid(og_ref[b]) * hn).astype(BF16)


def _mlstm(c_qk, c_v, c_o, small, conv_w, conv_b, gate_bias, norm_g, bn, t):
    nc = t // CHUNK
    width = MLSTM_HEADS * HEAD_DIM
    norm_g_t = jnp.repeat(norm_g.reshape(MLSTM_HEADS, HEAD_DIM).T, HEAD_DIM, axis=1)
    blk = lambda w: pl.BlockSpec((bn, CHUNK, w), lambda c: (0, c, 0))
    full = lambda a: pl.BlockSpec(a.shape, lambda c: (0,) * a.ndim)
    return pl.pallas_call(
        functools.partial(_mlstm_kernel, bn=bn),
        grid=(nc,),
        in_specs=[blk(512), blk(256), blk(256), blk(128), full(conv_w), full(conv_b), full(gate_bias), full(norm_g_t)],
        out_specs=blk(256),
        out_shape=jax.ShapeDtypeStruct((bn, t, 256), BF16),
        scratch_shapes=[pltpu.VMEM((bn, 8, 512), F32), pltpu.VMEM((bn, HEAD_DIM, width), F32),
                        pltpu.VMEM((bn, 1, width), F32), pltpu.VMEM((bn, 1, width), F32)],
        compiler_params=_cparams(("arbitrary",)),
        name="mlstm",
    )(c_qk.reshape(bn, t, 512), c_v.reshape(bn, t, 256), c_o.reshape(bn, t, 256), small.reshape(bn, t, 128),
      conv_w, conv_b, gate_bias, norm_g_t)


def _s5_kernel(u_ref, perm_ref, bbre_ref, bbim_ref, pwre_ref, pwim_ref, cre_ref, cim_ref, dskip_ref, gluw_ref,
               glub_ref, o_ref, sre_ref, sim_ref, stre_ref, stim_ref, yp_ref):
    @pl.when(pl.program_id(1) == 0)
    def _():
        stre_ref[...] = jnp.zeros_like(stre_ref)
        stim_ref[...] = jnp.zeros_like(stim_ref)

    seg_len = S5_TT // S5_SEGS
    u = u_ref[...]
    ub = _dot(perm_ref[...], u.astype(BF16)).astype(BF16)
    sre_ref[...] = _dot(ub, bbre_ref[...])
    sim_ref[...] = _dot(ub, bbim_ref[...])
    a_re = jnp.broadcast_to(pwre_ref[0:1, :], (S5_SEGS, S5_LANES))
    a_im = jnp.broadcast_to(pwim_ref[0:1, :], (S5_SEGS, S5_LANES))

    def local_step(i, carry):
        s_re, s_im = carry
        rows = pl.ds(pl.multiple_of(i * S5_SEGS, S5_SEGS), S5_SEGS)
        n_re = a_re * s_re - a_im * s_im + sre_ref[rows, :]
        n_im = a_re * s_im + a_im * s_re + sim_ref[rows, :]
        sre_ref[rows, :] = n_re
        sim_ref[rows, :] = n_im
        return n_re, n_im

    zeros = jnp.zeros((S5_SEGS, S5_LANES), F32)
    e_re, e_im = lax.fori_loop(0, seg_len, local_step, (zeros, zeros))

    al_re, al_im = pwre_ref[seg_len - 1:seg_len, :], pwim_ref[seg_len - 1:seg_len, :]
    c_re, c_im = stre_ref[0:1, :], stim_ref[0:1, :]
    cs_re, cs_im = [], []
    for k in range(S5_SEGS):
        cs_re.append(c_re)
        cs_im.append(c_im)
        c_re, c_im = (e_re[k:k + 1, :] + al_re * c_re - al_im * c_im,
                      e_im[k:k + 1, :] + al_re * c_im + al_im * c_re)
    stre_ref[...] = jnp.broadcast_to(c_re, stre_ref.shape)
    stim_ref[...] = jnp.broadcast_to(c_im, stim_ref.shape)
    cin_re = jnp.concatenate(cs_re, axis=0)
    cin_im = jnp.concatenate(cs_im, axis=0)

    def correct_step(i, carry):
        rows = pl.ds(pl.multiple_of(i * S5_SEGS, S5_SEGS), S5_SEGS)
        p_re, p_im = pwre_ref[pl.ds(i, 1), :], pwim_ref[pl.ds(i, 1), :]
        sre_ref[rows, :] = sre_ref[rows, :] + p_re * cin_re - p_im * cin_im
        sim_ref[rows, :] = sim_ref[rows, :] + p_re * cin_im + p_im * cin_re
        return carry

    lax.fori_loop(0, seg_len, correct_step, 0)
    ycs = _dot(sre_ref[...].astype(BF16), cre_ref[...]) - _dot(sim_ref[...].astype(BF16), cim_ref[...])
    ngrp = GROUP_WIDTH // 128
    for g in range(ngrp):
        yp_ref[g] = ycs[:, g * 128:(g + 1) * 128]
    blocks = []
    for k in range(S5_SEGS):
        for i0 in range(0, seg_len, 8):
            rows = pl.ds(i0 * S5_SEGS + k, 8, stride=S5_SEGS)
            blocks.append(jnp.concatenate([yp_ref[g, rows, :] for g in range(ngrp)], axis=1))
    y = jnp.concatenate(blocks, axis=0) + dskip_ref[...] * u
    y = 0.5 * y * (1.0 + jnp.tanh(math.sqrt(2.0 / math.pi) * (y + 0.044715 * (y * y * y))))
    z = _dot(y.astype(BF16), gluw_ref[...]) + glub_ref[...]
    o_ref[...] = (y * _sigmoid(z)).astype(BF16)


def _s5(d_u, bb_re, bb_im, pw_re, pw_im, c_re_t, c_im_t, d_skip, glu_w, glu_b, bn, t):
    nt = t // S5_TT
    full = lambda a: pl.BlockSpec(a.shape, lambda b, i: (0,) * a.ndim)
    r = jnp.arange(S5_TT)
    perm = (r[None, :] == ((r % S5_SEGS) * (S5_TT // S5_SEGS) + r // S5_SEGS)[:, None]).astype(BF16)
    args = (perm, bb_re, bb_im, pw_re, pw_im, c_re_t, c_im_t, d_skip, glu_w, glu_b)
    return pl.pallas_call(
        _s5_kernel,
        grid=(bn, nt),
        in_specs=[pl.BlockSpec((S5_TT, 256), lambda b, i: (b * nt + i, 0))] + [full(a) for a in args],
        out_specs=pl.BlockSpec((S5_TT, 256), lambda b, i: (b * nt + i, 0)),
        out_shape=jax.ShapeDtypeStruct((bn * t, 256), BF16),
        scratch_shapes=[pltpu.VMEM((S5_TT, S5_LANES), F32), pltpu.VMEM((S5_TT, S5_LANES), F32),
                        pltpu.VMEM((8, S5_LANES), F32), pltpu.VMEM((8, S5_LANES), F32),
                        pltpu.VMEM((GROUP_WIDTH // 128, S5_TT, 128), F32)],
        compiler_params=_cparams(("arbitrary", "arbitrary")),
        name="s5_glu",
    )(d_u, *args)


def _outproj_kernel(x_ref, mod_ref, oa_ref, ob_ref, oc_ref, od_ref, w_ref, lng_ref, lnb_ref, o_ref):
    x = x_ref[...]
    gate = mod_ref[2:3, :]
    y = _dot(oa_ref[...], w_ref[0])
    y = y + _dot(ob_ref[...], w_ref[1])
    y = y + _dot(oc_ref[...], w_ref[2])
    y = y + _dot(od_ref[...], w_ref[3])
    o_ref[...] = _residual_layer_norm(x, gate * y, lng_ref[...], lnb_ref[...])


def _outproj(x2, mod, o_a, o_b, o_c, o_d, w_out4, ln_g, ln_b, rows_per_batch):
    n = x2.shape[0]
    tm = ROW_TILE
    tiles_per_batch = rows_per_batch // tm
    mix = pl.BlockSpec((tm, GROUP_WIDTH), lambda i: (i, 0))
    return pl.pallas_call(
        _outproj_kernel,
        grid=(n // tm,),
        in_specs=[
            pl.BlockSpec((tm, D_MODEL), lambda i: (i, 0)),
            pl.BlockSpec((None, 8, D_MODEL), lambda i: (i // tiles_per_batch, 0, 0)),
            mix, mix, mix, mix,
            pl.BlockSpec(w_out4.shape, lambda i: (0, 0, 0), pipeline_mode=pl.Buffered(1)),
            pl.BlockSpec((1, D_MODEL), lambda i: (0, 0)),
            pl.BlockSpec((1, D_MODEL), lambda i: (0, 0)),
        ],
        out_specs=pl.BlockSpec((tm, D_MODEL), lambda i: (i, 0)),
        out_shape=jax.ShapeDtypeStruct((n, D_MODEL), F32),
        compiler_params=_cparams(("arbitrary",)),
        name="mixer_outproj",
    )(x2, mod, o_a, o_b, o_c, o_d, w_out4, ln_g.reshape(1, -1), ln_b.reshape(1, -1))


def _permute_w_in(w_in):
    off = {}
    o = 0
    for name, s in (("qa", 256), ("ka", 128), ("va", 128), ("cq", 128), ("kb", 64), ("vb", 64), ("ki", 32),
                    ("wi", 4), ("qkc", 512), ("vc", 256), ("ig", 4), ("fg", 4), ("oc", 256), ("ud", 256)):
        off[name] = (o, o + s)
        o += s
    col = lambda n: w_in[:, off[n][0]:off[n][1]]
    small = jnp.concatenate([col("wi"), col("ig"), col("fg"),
                             jnp.zeros((w_in.shape[0], 128 - 12), w_in.dtype)], axis=1)
    parts = [col("qa"), col("ka"), col("va"), col("cq"), col("kb"), col("vb")] + [col("ki")] * 4 + [
        col("qkc"), col("vc"), col("oc"), col("ud"), small]
    return jnp.concatenate(parts, axis=1).astype(BF16)


def _s5_params(lam_re, lam_im, log_step, b_re, b_im, c_re, c_im):
    dt = jnp.exp(log_step)[:, None]
    mag = jnp.exp(lam_re * dt)
    a_re, a_im = mag * jnp.cos(lam_im * dt), mag * jnp.sin(lam_im * dt)
    den = lam_re * lam_re + lam_im * lam_im
    kap_re = ((a_re - 1.0) * lam_re + a_im * lam_im) / den
    kap_im = (a_im * lam_re - (a_re - 1.0) * lam_im) / den
    bb_re = kap_re[..., None] * b_re - kap_im[..., None] * b_im
    bb_im = kap_re[..., None] * b_im + kap_im[..., None] * b_re
    eye = jnp.eye(S5_GROUPS, dtype=F32)

    def in_mat(bb):
        return jnp.einsum("gph,gk->ghkp", bb, eye).reshape(S5_GROUPS * S5_GROUP_CH, S5_LANES).astype(BF16)

    def out_mat(cc):
        return jnp.einsum("gop,gk->gpko", cc, eye).reshape(S5_LANES, S5_GROUPS * S5_GROUP_CH).astype(BF16)

    n = jnp.arange(1, S5_TT // S5_SEGS + 1, dtype=F32)[:, None, None]
    pw_mag = jnp.exp(n * (lam_re * dt))
    pw_re = (pw_mag * jnp.cos(n * (lam_im * dt))).at[0].set(a_re).reshape(-1, S5_LANES)
    pw_im = (pw_mag * jnp.sin(n * (lam_im * dt))).at[0].set(a_im).reshape(-1, S5_LANES)
    return in_mat(bb_re), in_mat(bb_im), pw_re, pw_im, out_mat(c_re), out_mat(c_im)


def kernel(x, c, ada_w, ada_b, ln_g, ln_b, ffn_w13, ffn_w2, w_in, w_out, sinks, w_uq, w_iq, conv_w, conv_b, ig_b,
           fg_b, mh_norm_g, lam_re, lam_im, log_step, b_re, b_im, c_re, c_im, d_skip, glu_w, glu_b):
    bn, t, d = x.shape
    assert d == D_MODEL and t % max(ROW_TILE, DSA_TK, S5_TT) == 0 and bn <= 8
    n = bn * t
    nl = ada_w.shape[0]
    c_pad = jnp.zeros((8, d), F32).at[:bn].set(c)
    mod_all = _ada_mod(c_pad, ada_w, ada_b)
    mod_all = mod_all[:, :bn].reshape(nl, bn, N_SUB, 3, d).transpose(0, 2, 1, 3, 4)
    mod_all = jnp.pad(mod_all, ((0, 0), (0, 0), (0, 0), (0, 5), (0, 0)))

    x2 = x.reshape(n, d)
    for l in range(nl):
        w13c = ffn_w13[l].astype(BF16).reshape(2, d, 2, FFN_NCHUNK, FFN_TF).transpose(0, 2, 3, 1, 4)
        w2c = ffn_w2[l].astype(BF16)
        x2 = _ffn_sublayer(x2, mod_all[l, 0], w13c[0], w2c[0], ln_g[l, 0], ln_b[l, 0], t)
        (a_q, a_kv, b_q, b_qi, b_kv, b_ki, c_qk, c_v, c_o, d_u, small) = _inproj(
            x2, mod_all[l, 1], _permute_w_in(w_in[l]), w_uq[l].astype(BF16), w_iq[l].astype(BF16), t)
        o_a = _swa(sinks[l], a_q, a_kv, bn, t)
        o_b = _dsa(b_q, b_qi, small, b_kv, b_ki, bn, t)
        gate_bias = jnp.zeros((1, 128), F32).at[0, SM_IG:SM_IG + 4].set(ig_b[l]).at[0, SM_FG:SM_FG + 4].set(fg_b[l])
        o_c = _mlstm(c_qk, c_v, c_o, small, conv_w[l], conv_b[l].reshape(1, -1), gate_bias,
                     mh_norm_g[l].reshape(1, -1), bn, t).reshape(n, GROUP_WIDTH)
        s5p = _s5_params(lam_re[l], lam_im[l], log_step[l], b_re[l], b_im[l], c_re[l], c_im[l])
        o_d = _s5(d_u, *s5p, d_skip[l].reshape(1, -1), glu_w[l].astype(BF16), glu_b[l].reshape(1, -1), bn, t)
        x2 = _outproj(x2, mod_all[l, 1], o_a, o_b, o_c, o_d,
                      w_out[l].astype(BF16).reshape(4, GROUP_WIDTH, d), ln_g[l, 1], ln_b[l, 1], t)
        x2 = _ffn_sublayer(x2, mod_all[l, 2], w13c[1], w2c[1], ln_g[l, 2], ln_b[l, 2], t)
    return x2.reshape(bn, t, d)
```

```python
import functools
import math

import jax
import jax.numpy as jnp
from jax import lax
from jax.experimental import pallas as pl
from jax.experimental.pallas import tpu as pltpu

F32 = jnp.float32
BF16 = jnp.bfloat16
I32 = jnp.int32

D_MODEL = 1024
DEPTH = 2
CHUNK = 64
HEAD_DIM = 64
GROUP_WIDTH = 256
SWA_HEADS = 4
SWA_KV_HEADS = 2
SWA_WIN_CHUNKS = 2
DSA_HEADS = 4
DSA_Q_RANK = 128
IDX_HEADS = 4
IDX_DIM = 32
DSA_TOPK = 256
MLSTM_HEADS = 4
MLSTM_CONV = 4
S5_GROUP_CH = 16
S5_GROUPS = 16
S5_STATE = 64
S5_LANES = S5_GROUPS * S5_STATE
D_FF = 2816
N_SUB = 3
ALPHA = (2 * DEPTH) ** 0.25
LN_EPS = 1e-5
NEG_INF = -1e30
INT_MIN = -(2 ** 31)
FIELD_BITS = 15
FIELD_GUARD = -(2 ** 31) + 2 ** 15

SLOPES_A = tuple(2.0 ** -(i + 1) for i in range(0, 8, 2))
SLOPES_B = tuple(2.0 ** -(i + 1) for i in range(1, 8, 2))

VMEM_LIMIT_BYTES = 56 * 1024 * 1024

FFN_TF = 256
FFN_NCHUNK = D_FF // FFN_TF
ROW_TILE = 512
SWA_TQ = 128
DSA_TQ = 128
DSA_TK = 512
S5_TT = 512
S5_SEGS = 8

ZC_QA, ZC_KVA, ZC_CQ, ZC_KVB, ZC_KI = 0, 256, 512, 640, 768
ZC_QKC, ZC_VC, ZC_OC, ZC_UD, ZC_SMALL, Z_WIDTH = 896, 1408, 1664, 1920, 2176, 2304
SM_WI, SM_IG, SM_FG = 0, 4, 8


def _cparams(sem):
    return pltpu.CompilerParams(dimension_semantics=sem, vmem_limit_bytes=VMEM_LIMIT_BYTES)


def _dot(a, b):
    return jnp.dot(a, b, preferred_element_type=F32)


def _dot_nt(a, b):
    return lax.dot_general(a, b, (((1,), (1,)), ((), ())), preferred_element_type=F32)


def _sigmoid(x):
    return 1.0 / (1.0 + jnp.exp(-x))


def _residual_layer_norm(x, y, g, b):
    v = ALPHA * x + y
    mu = jnp.mean(v, axis=-1, keepdims=True)
    d = v - mu
    var = jnp.mean(d * d, axis=-1, keepdims=True)
    return d * lax.rsqrt(var + LN_EPS) * g + b


def _ada_kernel(c_ref, w_ref, b_ref, o_ref):
    c = c_ref[...]
    cs = c * _sigmoid(c)
    o_ref[...] = jnp.dot(cs, w_ref[...], preferred_element_type=F32,
                         precision=lax.Precision.HIGHEST) + b_ref[...]


def _ada_mod(c_pad, ada_w, ada_b):
    nl = ada_w.shape[0]
    ncol = ada_w.shape[2] // D_MODEL
    return pl.pallas_call(
        _ada_kernel,
        grid=(nl, ncol),
        in_specs=[
            pl.BlockSpec((8, D_MODEL), lambda l, j: (0, 0)),
            pl.BlockSpec((None, D_MODEL, D_MODEL), lambda l, j: (l, 0, j)),
            pl.BlockSpec((None, 1, D_MODEL), lambda l, j: (l, 0, j)),
        ],
        out_specs=pl.BlockSpec((None, 8, D_MODEL), lambda l, j: (l, 0, j)),
        out_shape=jax.ShapeDtypeStruct((nl, 8, ada_w.shape[2]), F32),
        compiler_params=_cparams(("arbitrary", "arbitrary")),
        name="ada_mod",
    )(c_pad, ada_w, ada_b.reshape(nl, 1, -1))


def _ffn_kernel(x_ref, mod_ref, w13_ref, w2_ref, lng_ref, lnb_ref, o_ref, h_ref):
    x = x_ref[...]
    shift, scale, gate = mod_ref[0:1, :], mod_ref[1:2, :], mod_ref[2:3, :]
    u = (x * (1.0 + scale) + shift).astype(BF16)

    for j in range(FFN_NCHUNK):
        a = _dot(u, w13_ref[0, j])
        g = _dot(u, w13_ref[1, j])
        h_ref[:, j * FFN_TF:(j + 1) * FFN_TF] = (a * _sigmoid(a) * g).astype(BF16)
    y = _dot(h_ref[...], w2_ref[...])
    o_ref[...] = _residual_layer_norm(x, 0.5 * gate * y, lng_ref[...], lnb_ref[...])


def _ffn_sublayer(x2, mod, w13c, w2c, ln_g, ln_b, rows_per_batch):
    n = x2.shape[0]
    tm = ROW_TILE
    tiles_per_batch = rows_per_batch // tm
    return pl.pallas_call(
        _ffn_kernel,
        grid=(n // tm,),
        in_specs=[
            pl.BlockSpec((tm, D_MODEL), lambda i: (i, 0)),
            pl.BlockSpec((None, 8, D_MODEL), lambda i: (i // tiles_per_batch, 0, 0)),
            pl.BlockSpec(w13c.shape, lambda i: (0, 0, 0, 0), pipeline_mode=pl.Buffered(1)),
            pl.BlockSpec(w2c.shape, lambda i: (0, 0), pipeline_mode=pl.Buffered(1)),
            pl.BlockSpec((1, D_MODEL), lambda i: (0, 0)),
            pl.BlockSpec((1, D_MODEL), lambda i: (0, 0)),
        ],
        out_specs=pl.BlockSpec((tm, D_MODEL), lambda i: (i, 0)),
        out_shape=jax.ShapeDtypeStruct((n, D_MODEL), F32),
        scratch_shapes=[pltpu.VMEM((tm, D_FF), BF16)],
        compiler_params=_cparams(("arbitrary",)),
        name="ffn_sublayer",
    )(x2, mod, w13c, w2c, ln_g.reshape(1, -1), ln_b.reshape(1, -1))


def _inproj_kernel(x_ref, mod_ref, w_ref, wuq_ref, wiq_ref,
                   aq_ref, akv_ref, bq_ref, bqi_ref, bkv_ref, bki_ref,
                   cqk_ref, cv_ref, co_ref, du_ref, sm_ref):
    x = x_ref[...]
    shift, scale = mod_ref[0:1, :], mod_ref[1:2, :]
    u = (x * (1.0 + scale) + shift).astype(BF16)
    z = _dot(u, w_ref[...])
    aq_ref[...] = z[:, ZC_QA:ZC_KVA].astype(BF16)
    akv_ref[...] = z[:, ZC_KVA:ZC_CQ].astype(BF16)
    cq = z[:, ZC_CQ:ZC_KVB].astype(BF16)
    bq_ref[...] = (_dot(cq, wuq_ref[...]) * HEAD_DIM ** -0.5).astype(BF16)
    bqi_ref[...] = _dot(cq, wiq_ref[...]).astype(BF16)
    bkv_ref[...] = z[:, ZC_KVB:ZC_KI].astype(BF16)
    bki_ref[...] = z[:, ZC_KI:ZC_QKC].astype(BF16)
    cqk_ref[...] = z[:, ZC_QKC:ZC_VC]
    cv_ref[...] = z[:, ZC_VC:ZC_OC].astype(BF16)
    co_ref[...] = z[:, ZC_OC:ZC_UD]
    du_ref[...] = z[:, ZC_UD:ZC_SMALL]
    sm_ref[...] = z[:, ZC_SMALL:Z_WIDTH]


def _inproj(x2, mod, w_perm, w_uq, w_iq, rows_per_batch):
    n = x2.shape[0]
    tm = ROW_TILE
    tiles_per_batch = rows_per_batch // tm
    widths = [(256, BF16), (256, BF16), (256, BF16), (128, BF16), (128, BF16), (128, BF16),
              (512, F32), (256, BF16), (256, F32), (256, F32), (128, F32)]
    return pl.pallas_call(
        _inproj_kernel,
        grid=(n // tm,),
        in_specs=[
            pl.BlockSpec((tm, D_MODEL), lambda i: (i, 0)),
            pl.BlockSpec((None, 8, D_MODEL), lambda i: (i // tiles_per_batch, 0, 0)),
            pl.BlockSpec(w_perm.shape, lambda i: (0, 0), pipeline_mode=pl.Buffered(1)),
            pl.BlockSpec(w_uq.shape, lambda i: (0, 0), pipeline_mode=pl.Buffered(1)),
            pl.BlockSpec(w_iq.shape, lambda i: (0, 0), pipeline_mode=pl.Buffered(1)),
        ],
        out_specs=[pl.BlockSpec((tm, w), lambda i: (i, 0)) for w, _ in widths],
        out_shape=[jax.ShapeDtypeStruct((n, w), dt) for w, dt in widths],
        compiler_params=_cparams(("arbitrary",)),
        name="mixer_inproj",
    )(x2, mod, w_perm, w_uq, w_iq)


def _swa_kernel(sink_ref, q_ref, kvc_ref, kvp_ref, o_ref):
    i = pl.program_id(1)
    tq = SWA_TQ
    q = q_ref[...]
    kv = jnp.concatenate([kvp_ref[...], kvc_ref[...]], axis=0)
    qpos = i * tq + lax.broadcasted_iota(I32, (tq, 2 * tq), 0)
    kpos = (i - 1) * tq + lax.broadcasted_iota(I32, (tq, 2 * tq), 1)
    qchunk = qpos // CHUNK
    kchunk = (kpos + tq) // CHUNK - tq // CHUNK
    valid = (kpos >= 0) & (kchunk <= qchunk) & (kchunk >= qchunk - SWA_WIN_CHUNKS)
    dist = jnp.abs(qpos - kpos).astype(F32)
    rep = SWA_HEADS // SWA_KV_HEADS
    for h in range(SWA_HEADS):
        g = h // rep
        qh = q[:, h * HEAD_DIM:(h + 1) * HEAD_DIM]
        kg = kv[:, g * HEAD_DIM:(g + 1) * HEAD_DIM]
        vg = kv[:, (SWA_KV_HEADS + g) * HEAD_DIM:(SWA_KV_HEADS + g + 1) * HEAD_DIM]
        s = _dot_nt(qh, kg) * HEAD_DIM ** -0.5 - SLOPES_A[h] * dist
        s = jnp.where(valid, s, NEG_INF)
        sink = sink_ref[h]
        m = jnp.maximum(jnp.max(s, axis=-1, keepdims=True), sink)
        p = jnp.exp(s - m)
        denom = jnp.sum(p, axis=-1, keepdims=True) + jnp.exp(sink - m)
        o = _dot(p.astype(BF16), vg) / denom
        o_ref[:, h * HEAD_DIM:(h + 1) * HEAD_DIM] = o.astype(BF16)


def _swa(sinks, a_q, a_kv, bn, t):
    nt = t // SWA_TQ
    return pl.pallas_call(
        _swa_kernel,
        grid=(bn, nt),
        in_specs=[
            pl.BlockSpec(memory_space=pltpu.SMEM),
            pl.BlockSpec((SWA_TQ, 256), lambda b, i: (b * nt + i, 0)),
            pl.BlockSpec((SWA_TQ, 256), lambda b, i: (b * nt + i, 0)),
            pl.BlockSpec((SWA_TQ, 256), lambda b, i: (b * nt + jnp.maximum(i - 1, 0), 0)),
        ],
        out_specs=pl.BlockSpec((SWA_TQ, 256), lambda b, i: (b * nt + i, 0)),
        out_shape=jax.ShapeDtypeStruct((bn * t, 256), BF16),
        compiler_params=_cparams(("arbitrary", "arbitrary")),
        name="swa_attention",
    )(sinks, a_q, a_kv, a_kv)


def _sortable_key(x):
    bits = lax.bitcast_convert_type(x, I32)
    return bits ^ ((bits >> 31) & 0x7FFFFFFF)


def _dsa_kernel(q_ref, qi_ref, sm_ref, kv_ref, vt_ref, ki_ref, o_ref, key_ref, s_ref, p_ref, w1_ref, w2_ref):
    i = pl.program_id(1)
    tq, tk = DSA_TQ, DSA_TK
    nblk = (i * tq + tq + tk - 1) // tk
    qpos = i * tq + lax.broadcasted_iota(I32, (1, tq), 1)
    qchunk = qpos // CHUNK
    row_k = lax.broadcasted_iota(I32, (tk, tq), 0)

    qi = qi_ref[...]
    lane_i = lax.broadcasted_iota(I32, (1, IDX_HEADS * IDX_DIM), 1) // IDX_DIM
    qi_stack = jnp.concatenate([jnp.where(lane_i == h, qi, jnp.zeros_like(qi)) for h in range(IDX_HEADS)], axis=0)
    sm_t = sm_ref[...].T
    w_idx = [sm_t[SM_WI + h:SM_WI + h + 1, :] for h in range(IDX_HEADS)]
    idx_scale = (IDX_DIM * IDX_HEADS) ** -0.5
    q = q_ref[...]
    q_stack = jnp.concatenate([q[:, h * HEAD_DIM:(h + 1) * HEAD_DIM] for h in range(DSA_HEADS)], axis=0)

    half = tk // 2
    guard = jnp.int32(FIELD_GUARD)

    def pack_fields(f):
        return (f[0:half] << 16) | f[half:tk] | guard

    def score_block(j, mask_inadmissible):
        rows = pl.ds(pl.multiple_of(j * tk, tk), tk)
        s_ref[j] = _dot_nt(kv_ref[rows, 0:HEAD_DIM], q_stack)
        d = _dot_nt(ki_ref[rows, :], qi_stack)
        acc = w_idx[0] * jnp.maximum(d[:, 0:tq], 0.0)
        for h in range(1, IDX_HEADS):
            acc = acc + w_idx[h] * jnp.maximum(d[:, h * tq:(h + 1) * tq], 0.0)
        sc = acc * idx_scale
        if mask_inadmissible:
            sc = jnp.where((j * tk + row_k) // CHUNK <= qchunk, sc, NEG_INF)
        key = _sortable_key(sc)
        key_ref[j] = key
        w1_ref[j] = pack_fields(lax.shift_right_logical(key ^ INT_MIN, 32 - FIELD_BITS))

    def full_block(j, carry):
        score_block(j, False)
        return carry

    lax.fori_loop(0, nblk - 1, full_block, 0)
    score_block(nblk - 1, True)

    k_eff = jnp.minimum(DSA_TOPK, (qchunk + 1) * CHUNK)

    def count(pred):
        def blk(j, c):
            m = pred(key_ref[j], j * tk + row_k).astype(I32)
            return c + jnp.sum(m.reshape(tk // 8, 8, tq), axis=0)
        c = lax.fori_loop(0, nblk, blk, jnp.zeros((8, tq), I32))
        return jnp.sum(c, axis=0, keepdims=True)

    def count_fields(w_ref, cand):
        cand2 = (cand << 16) | cand
        def blk(j, c):
            hit = ((w_ref[j] - cand2) >> 15) & 0x00010001
            return c + jnp.sum(hit.reshape(half // 8, 8, tq), axis=0)
        c = lax.fori_loop(0, nblk, blk, jnp.zeros((8, tq), I32))
        return jnp.sum((c & 0xFFFF) + (c >> 16), axis=0, keepdims=True)

    def field_search(w_ref, k_want):
        def step(bi, prefix):
            cand = prefix | (jnp.int32(1) << (FIELD_BITS - 1 - bi))
            return jnp.where(count_fields(w_ref, cand) >= k_want, cand, prefix)
        return lax.fori_loop(0, FIELD_BITS, step, jnp.zeros((1, tq), I32))

    top = field_search(w1_ref, k_eff)
    field_max = (1 << FIELD_BITS) - 1
    above = jnp.where(top == field_max, 0, count_fields(w1_ref, jnp.minimum(top + 1, field_max)))

    def pack_mid(j, carry):
        ukey = key_ref[j] ^ INT_MIN
        member = lax.shift_right_logical(ukey, 32 - FIELD_BITS) == top
        mid = lax.shift_right_logical(ukey, 32 - 2 * FIELD_BITS) & field_max
        w2_ref[j] = pack_fields(jnp.where(member, mid, 0))
        return carry

    lax.fori_loop(0, nblk, pack_mid, 0)
    mid = field_search(w2_ref, k_eff - above)

    def bit_step(bi, prefix):
        cand_u = prefix | (jnp.int32(1) << (31 - 2 * FIELD_BITS - bi))
        cand_s = cand_u ^ INT_MIN
        cnt = count(lambda key, kpos: key >= cand_s)
        return jnp.where(cnt >= k_eff, cand_u, prefix)

    prefix = lax.fori_loop(0, 32 - 2 * FIELD_BITS, bit_step,
                           (top << (32 - FIELD_BITS)) | (mid << (32 - 2 * FIELD_BITS)))
    thr = prefix ^ INT_MIN

    c_gt = count(lambda key, kpos: key > thr)
    need = (k_eff - c_gt).astype(F32)
    tril = (lax.broadcasted_iota(I32, (tk, tk), 0) >= lax.broadcasted_iota(I32, (tk, tk), 1)).astype(BF16)

    qpos_f = qpos.astype(F32)
    p_ref[1] = jnp.zeros(p_ref.shape[1:], BF16)

    def att_block(j, carry):
        m_run, l_run, acc_part, ties_seen = carry
        slot = j % 2
        acc = acc_part + _dot(vt_ref[jnp.maximum(j - 1, 0)], p_ref[1 - slot])
        key = key_ref[j]
        tie = key == thr
        tie_rank = _dot(tril, jnp.where(tie, 1.0, 0.0).astype(BF16)) + ties_seen
        sel = (key > thr) | (tie & (tie_rank <= need))
        dist = jnp.abs((j * tk + row_k).astype(F32) - qpos_f)
        ms, ls, alphas = [], [], []
        for h in range(DSA_HEADS):
            cols = slice(h * tq, (h + 1) * tq)
            sh = jnp.where(sel, s_ref[j, :, cols] - SLOPES_B[h] * dist, -jnp.inf)
            m_old = m_run[:, cols]
            m_new = jnp.maximum(m_old, jnp.max(sh, axis=0, keepdims=True))
            alpha = jnp.exp(m_old - m_new)
            p = jnp.exp(sh - m_new)
            p_ref[slot, :, cols] = p.astype(BF16)
            ms.append(m_new)
            ls.append(alpha * l_run[:, cols] + jnp.sum(p, axis=0, keepdims=True))
            alphas.append(alpha)
        return (jnp.concatenate(ms, axis=1), jnp.concatenate(ls, axis=1), jnp.concatenate(alphas, axis=1) * acc,
                tie_rank[tk - 1:tk, :])

    init = (jnp.full((1, DSA_HEADS * tq), NEG_INF, F32), jnp.zeros((1, DSA_HEADS * tq), F32),
            jnp.zeros((HEAD_DIM, DSA_HEADS * tq), F32), jnp.zeros((1, tq), F32))
    _, l_run, acc_part, _ = lax.fori_loop(0, nblk, att_block, init)
    acc = acc_part + _dot(vt_ref[nblk - 1], p_ref[(nblk - 1) % 2])
    out = acc / l_run
    for h in range(DSA_HEADS):
        o_ref[:, h * HEAD_DIM:(h + 1) * HEAD_DIM] = out[:, h * tq:(h + 1) * tq].T.astype(BF16)


def _dsa(b_q, b_qi, small, b_kv, b_ki, bn, t):
    nt = t // DSA_TQ
    nkb = t // DSA_TK
    v_t = b_kv[:, HEAD_DIM:].reshape(bn * nkb, DSA_TK, HEAD_DIM).transpose(0, 2, 1)
    return pl.pallas_call(
        _dsa_kernel,
        grid=(bn, nt),
        in_specs=[
            pl.BlockSpec((DSA_TQ, 256), lambda b, i: (b * nt + i, 0)),
            pl.BlockSpec((DSA_TQ, 128), lambda b, i: (b * nt + i, 0)),
            pl.BlockSpec((DSA_TQ, 128), lambda b, i: (b * nt + i, 0)),
            pl.BlockSpec((t, 128), lambda b, i: (b, 0)),
            pl.BlockSpec((nkb, HEAD_DIM, DSA_TK), lambda b, i: (b, 0, 0)),
            pl.BlockSpec((t, 128), lambda b, i: (b, 0)),
        ],
        out_specs=pl.BlockSpec((DSA_TQ, 256), lambda b, i: (b * nt + i, 0)),
        out_shape=jax.ShapeDtypeStruct((bn * t, 256), BF16),
        scratch_shapes=[pltpu.VMEM((nkb, DSA_TK, DSA_TQ), I32),
                        pltpu.VMEM((nkb, DSA_TK, DSA_HEADS * DSA_TQ), F32),
                        pltpu.VMEM((2, DSA_TK, DSA_HEADS * DSA_TQ), BF16),
                        pltpu.VMEM((nkb, DSA_TK // 2, DSA_TQ), I32), pltpu.VMEM((nkb, DSA_TK // 2, DSA_TQ), I32)],
        compiler_params=_cparams(("arbitrary", "arbitrary")),
        name="dsa_attention",
    )(b_q, b_qi, small, b_kv, v_t, b_ki)


def _mlstm_kernel(qk_ref, v_ref, og_ref, sm_ref, convw_ref, convb_ref, gbias_ref, normgt_ref,
                  o_ref, tail_ref, ct_ref, nvec_ref, mst_ref, *, bn):
    c = pl.program_id(0)
    L = CHUNK
    nh, dh, width = MLSTM_HEADS, HEAD_DIM, MLSTM_HEADS * HEAD_DIM

    @pl.when(c == 0)
    def _():
        tail_ref[...] = jnp.zeros_like(tail_ref)
        ct_ref[...] = jnp.zeros_like(ct_ref)
        nvec_ref[...] = jnp.zeros_like(nvec_ref)
        mst_ref[...] = jnp.zeros_like(mst_ref)

    srow = lax.broadcasted_iota(I32, (L, width), 0)
    lane = lax.broadcasted_iota(I32, (L, width), 1)
    jlane = lane % dh
    causal_t = srow <= jlane
    diag_t = srow == jlane
    head_of_lane = lax.broadcasted_iota(I32, (1, width), 1) // dh
    tril = (lax.broadcasted_iota(I32, (L, L), 1) <= lax.broadcasted_iota(I32, (L, L), 0)).astype(F32)
    erow = lax.broadcasted_iota(I32, (128, width), 0)
    ecol_head = lax.broadcasted_iota(I32, (128, width), 1) // dh
    expand_ig = (erow == SM_IG + ecol_head).astype(F32)
    expand_fg = (erow == SM_FG + ecol_head).astype(F32)
    exact = dict(preferred_element_type=F32, precision=lax.Precision.HIGHEST)

    def head_blocks(a):
        out = jnp.where(head_of_lane == 0, a[0:dh], 0.0)
        for h in range(1, nh):
            out = out + jnp.where(head_of_lane == h, a[h * dh:(h + 1) * dh], 0.0)
        return out

    convw = convw_ref[...]
    for b in range(bn):
        cur = qk_ref[b]
        ext = jnp.concatenate([tail_ref[b], cur], axis=0)
        tail_ref[b] = cur[L - 8:L, :]
        y = convb_ref[...] + convw[MLSTM_CONV - 1:MLSTM_CONV, :] * cur
        for k in range(MLSTM_CONV - 1):
            off = 8 - (MLSTM_CONV - 1) + k
            y = y + convw[k:k + 1, :] * ext[off:off + L, :]
        qk = y * _sigmoid(y)
        q_all = qk[:, 0:width]
        k_all = qk[:, width:2 * width] * dh ** -0.5
        q_stack = jnp.concatenate([jnp.where(head_of_lane == h, q_all, 0.0) for h in range(nh)], axis=0).astype(BF16)
        v_all = v_ref[b]
        v_t = v_all.astype(F32).T.astype(BF16)

        gates = sm_ref[b] + gbias_ref[...]
        lf = jnp.minimum(gates, 0.0) - jnp.log(1.0 + jnp.exp(-jnp.abs(gates)))
        bcum = jnp.dot(tril, lf, **exact)
        ig_x = jnp.dot(gates, expand_ig, **exact)
        b_x = jnp.dot(bcum, expand_fg, **exact)
        b_q = jnp.sum(jnp.where(diag_t, b_x, 0.0), axis=0, keepdims=True)
        b_last = b_x[L - 1:L, :]
        m_prev = mst_ref[b]
        ct = ct_ref[b]
        nvec = nvec_ref[b]

        dlog = jnp.where(causal_t, b_q - b_x + ig_x, NEG_INF)
        inter = b_q + m_prev
        mj = jnp.maximum(inter, jnp.max(dlog, axis=0, keepdims=True))
        dw = jnp.exp(dlog - mj)
        iw = jnp.exp(inter - mj)
        sc = _dot_nt(k_all.astype(BF16), q_stack) * dw
        qn = _dot_nt(jnp.broadcast_to(nvec, (8, width)).astype(BF16), q_stack)[0:1, :]
        q_c = _dot_nt(ct.astype(BF16), q_stack)
        num = iw * q_c + head_blocks(_dot(v_t, sc.astype(BF16)))
        den = iw * qn + jnp.sum(sc, axis=0, keepdims=True)
        hj = num / jnp.maximum(jnp.abs(den), jnp.exp(-mj))

        dec = b_last - b_x + ig_x
        m_new = jnp.maximum(b_last + m_prev, jnp.max(dec, axis=0, keepdims=True))
        wc = jnp.exp(b_last + m_prev - m_new)
        kw = k_all * jnp.exp(dec - m_new)
        ct_ref[b] = wc * ct + head_blocks(_dot(v_t, kw.astype(BF16)))
        nvec_ref[b] = wc * nvec + jnp.sum(kw, axis=0, keepdims=True)
        mst_ref[b] = m_new

        mu = jnp.mean(hj, axis=0, keepdims=True)
        dev = hj - mu
        var = jnp.mean(dev * dev, axis=0, keepdims=True)
        hn_t = (dev * lax.rsqrt(var + LN_EPS) * normgt_ref[...]).T
        hn = jnp.concatenate([hn_t[h * dh:(h + 1) * dh, :] for h in range(nh)], axis=1)
        o_ref[b] = (_sigmoid(og_ref[b]) * hn).astype(BF16)


def _mlstm(c_qk, c_v, c_o, small, conv_w, conv_b, gate_bias, norm_g, bn, t):
    nc = t // CHUNK
    width = MLSTM_HEADS * HEAD_DIM
    norm_g_t = jnp.repeat(norm_g.reshape(MLSTM_HEADS, HEAD_DIM).T, HEAD_DIM, axis=1)
    blk = lambda w: pl.BlockSpec((bn, CHUNK, w), lambda c: (0, c, 0))
    full = lambda a: pl.BlockSpec(a.shape, lambda c: (0,) * a.ndim)
    return pl.pallas_call(
        functools.partial(_mlstm_kernel, bn=bn),
        grid=(nc,),
        in_specs=[blk(512), blk(256), blk(256), blk(128), full(conv_w), full(conv_b), full(gate_bias), full(norm_g_t)],
        out_specs=blk(256),
        out_shape=jax.ShapeDtypeStruct((bn, t, 256), BF16),
        scratch_shapes=[pltpu.VMEM((bn, 8, 512), F32), pltpu.VMEM((bn, HEAD_DIM, width), F32),
                        pltpu.VMEM((bn, 1, width), F32), pltpu.VMEM((bn, 1, width), F32)],
        compiler_params=_cparams(("arbitrary",)),
        name="mlstm",
    )(c_qk.reshape(bn, t, 512), c_v.reshape(bn, t, 256), c_o.reshape(bn, t, 256), small.reshape(bn, t, 128),
      conv_w, conv_b, gate_bias, norm_g_t)


def _s5_kernel(u_ref, perm_ref, bbre_ref, bbim_ref, pwre_ref, pwim_ref, cre_ref, cim_ref, dskip_ref, gluw_ref,
               glub_ref, o_ref, sre_ref, sim_ref, stre_ref, stim_ref, yp_ref):
    @pl.when(pl.program_id(1) == 0)
    def _():
        stre_ref[...] = jnp.zeros_like(stre_ref)
        stim_ref[...] = jnp.zeros_like(stim_ref)

    seg_len = S5_TT // S5_SEGS
    u = u_ref[...]
    ub = _dot(perm_ref[...], u.astype(BF16)).astype(BF16)
    sre_ref[...] = _dot(ub, bbre_ref[...])
    sim_ref[...] = _dot(ub, bbim_ref[...])
    a_re = jnp.broadcast_to(pwre_ref[0:1, :], (S5_SEGS, S5_LANES))
    a_im = jnp.broadcast_to(pwim_ref[0:1, :], (S5_SEGS, S5_LANES))

    def local_step(i, carry):
        s_re, s_im = carry
        rows = pl.ds(pl.multiple_of(i * S5_SEGS, S5_SEGS), S5_SEGS)
        n_re = a_re * s_re - a_im * s_im + sre_ref[rows, :]
        n_im = a_re * s_im + a_im * s_re + sim_ref[rows, :]
        sre_ref[rows, :] = n_re
        sim_ref[rows, :] = n_im
        return n_re, n_im

    zeros = jnp.zeros((S5_SEGS, S5_LANES), F32)
    e_re, e_im = lax.fori_loop(0, seg_len, local_step, (zeros, zeros))

    al_re, al_im = pwre_ref[seg_len - 1:seg_len, :], pwim_ref[seg_len - 1:seg_len, :]
    c_re, c_im = stre_ref[0:1, :], stim_ref[0:1, :]
    cs_re, cs_im = [], []
    for k in range(S5_SEGS):
        cs_re.append(c_re)
        cs_im.append(c_im)
        c_re, c_im = (e_re[k:k + 1, :] + al_re * c_re - al_im * c_im,
                      e_im[k:k + 1, :] + al_re * c_im + al_im * c_re)
    stre_ref[...] = jnp.broadcast_to(c_re, stre_ref.shape)
    stim_ref[...] = jnp.broadcast_to(c_im, stim_ref.shape)
    cin_re = jnp.concatenate(cs_re, axis=0)
    cin_im = jnp.concatenate(cs_im, axis=0)

    def correct_step(i, carry):
        rows = pl.ds(pl.multiple_of(i * S5_SEGS, S5_SEGS), S5_SEGS)
        p_re, p_im = pwre_ref[pl.ds(i, 1), :], pwim_ref[pl.ds(i, 1), :]
        sre_ref[rows, :] = sre_ref[rows, :] + p_re * cin_re - p_im * cin_im
        sim_ref[rows, :] = sim_ref[rows, :] + p_re * cin_im + p_im * cin_re
        return carry

    lax.fori_loop(0, seg_len, correct_step, 0)
    ycs = _dot(sre_ref[...].astype(BF16), cre_ref[...]) - _dot(sim_ref[...].astype(BF16), cim_ref[...])
    ngrp = GROUP_WIDTH // 128
    for g in range(ngrp):
        yp_ref[g] = ycs[:, g * 128:(g + 1) * 128]
    blocks = []
    for k in range(S5_SEGS):
        for i0 in range(0, seg_len, 8):
            rows = pl.ds(i0 * S5_SEGS + k, 8, stride=S5_SEGS)
            blocks.append(jnp.concatenate([yp_ref[g, rows, :] for g in range(ngrp)], axis=1))
    y = jnp.concatenate(blocks, axis=0) + dskip_ref[...] * u
    y = 0.5 * y * (1.0 + jnp.tanh(math.sqrt(2.0 / math.pi) * (y + 0.044715 * (y * y * y))))
    z = _dot(y.astype(BF16), gluw_ref[...]) + glub_ref[...]
    o_ref[...] = (y * _sigmoid(z)).astype(BF16)


def _s5(d_u, bb_re, bb_im, pw_re, pw_im, c_re_t, c_im_t, d_skip, glu_w, glu_b, bn, t):
    nt = t // S5_TT
    full = lambda a: pl.BlockSpec(a.shape, lambda b, i: (0,) * a.ndim)
    r = jnp.arange(S5_TT)
    perm = (r[None, :] == ((r % S5_SEGS) * (S5_TT // S5_SEGS) + r // S5_SEGS)[:, None]).astype(BF16)
    args = (perm, bb_re, bb_im, pw_re, pw_im, c_re_t, c_im_t, d_skip, glu_w, glu_b)
    return pl.pallas_call(
        _s5_kernel,
        grid=(bn, nt),
        in_specs=[pl.BlockSpec((S5_TT, 256), lambda b, i: (b * nt + i, 0))] + [full(a) for a in args],
        out_specs=pl.BlockSpec((S5_TT, 256), lambda b, i: (b * nt + i, 0)),
        out_shape=jax.ShapeDtypeStruct((bn * t, 256), BF16),
        scratch_shapes=[pltpu.VMEM((S5_TT, S5_LANES), F32), pltpu.VMEM((S5_TT, S5_LANES), F32),
                        pltpu.VMEM((8, S5_LANES), F32), pltpu.VMEM((8, S5_LANES), F32),
                        pltpu.VMEM((GROUP_WIDTH // 128, S5_TT, 128), F32)],
        compiler_params=_cparams(("arbitrary", "arbitrary")),
        name="s5_glu",
    )(d_u, *args)


def _outproj_kernel(x_ref, mod_ref, oa_ref, ob_ref, oc_ref, od_ref, w_ref, lng_ref, lnb_ref, o_ref):
    x = x_ref[...]
    gate = mod_ref[2:3, :]
    y = _dot(oa_ref[...], w_ref[0])
    y = y + _dot(ob_ref[...], w_ref[1])
    y = y + _dot(oc_ref[...], w_ref[2])
    y = y + _dot(od_ref[...], w_ref[3])
    o_ref[...] = _residual_layer_norm(x, gate * y, lng_ref[...], lnb_ref[...])


def _outproj(x2, mod, o_a, o_b, o_c, o_d, w_out4, ln_g, ln_b, rows_per_batch):
    n = x2.shape[0]
    tm = ROW_TILE
    tiles_per_batch = rows_per_batch // tm
    mix = pl.BlockSpec((tm, GROUP_WIDTH), lambda i: (i, 0))
    return pl.pallas_call(
        _outproj_kernel,
        grid=(n // tm,),
        in_specs=[
            pl.BlockSpec((tm, D_MODEL), lambda i: (i, 0)),
            pl.BlockSpec((None, 8, D_MODEL), lambda i: (i // tiles_per_batch, 0, 0)),
            mix, mix, mix, mix,
            pl.BlockSpec(w_out4.shape, lambda i: (0, 0, 0), pipeline_mode=pl.Buffered(1)),
            pl.BlockSpec((1, D_MODEL), lambda i: (0, 0)),
            pl.BlockSpec((1, D_MODEL), lambda i: (0, 0)),
        ],
        out_specs=pl.BlockSpec((tm, D_MODEL), lambda i: (i, 0)),
        out_shape=jax.ShapeDtypeStruct((n, D_MODEL), F32),
        compiler_params=_cparams(("arbitrary",)),
        name="mixer_outproj",
    )(x2, mod, o_a, o_b, o_c, o_d, w_out4, ln_g.reshape(1, -1), ln_b.reshape(1, -1))


def _permute_w_in(w_in):
    off = {}
    o = 0
    for name, s in (("qa", 256), ("ka", 128), ("va", 128), ("cq", 128), ("kb", 64), ("vb", 64), ("ki", 32),
                    ("wi", 4), ("qkc", 512), ("vc", 256), ("ig", 4), ("fg", 4), ("oc", 256), ("ud", 256)):
        off[name] = (o, o + s)
        o += s
    col = lambda n: w_in[:, off[n][0]:off[n][1]]
    small = jnp.concatenate([col("wi"), col("ig"), col("fg"),
                             jnp.zeros((w_in.shape[0], 128 - 12), w_in.dtype)], axis=1)
    parts = [col("qa"), col("ka"), col("va"), col("cq"), col("kb"), col("vb")] + [col("ki")] * 4 + [
        col("qkc"), col("vc"), col("oc"), col("ud"), small]
    return jnp.concatenate(parts, axis=1).astype(BF16)


def _s5_params(lam_re, lam_im, log_step, b_re, b_im, c_re, c_im):
    dt = jnp.exp(log_step)[:, None]
    mag = jnp.exp(lam_re * dt)
    a_re, a_im = mag * jnp.cos(lam_im * dt), mag * jnp.sin(lam_im * dt)
    den = lam_re * lam_re + lam_im * lam_im
    kap_re = ((a_re - 1.0) * lam_re + a_im * lam_im) / den
    kap_im = (a_im * lam_re - (a_re - 1.0) * lam_im) / den
    bb_re = kap_re[..., None] * b_re - kap_im[..., None] * b_im
    bb_im = kap_re[..., None] * b_im + kap_im[..., None] * b_re
    eye = jnp.eye(S5_GROUPS, dtype=F32)

    def in_mat(bb):
        return jnp.einsum("gph,gk->ghkp", bb, eye).reshape(S5_GROUPS * S5_GROUP_CH, S5_LANES).astype(BF16)

    def out_mat(cc):
        return jnp.einsum("gop,gk->gpko", cc, eye).reshape(S5_LANES, S5_GROUPS * S5_GROUP_CH).astype(BF16)

    n = jnp.arange(1, S5_TT // S5_SEGS + 1, dtype=F32)[:, None, None]
    pw_mag = jnp.exp(n * (lam_re * dt))
    pw_re = (pw_mag * jnp.cos(n * (lam_im * dt))).at[0].set(a_re).reshape(-1, S5_LANES)
    pw_im = (pw_mag * jnp.sin(n * (lam_im * dt))).at[0].set(a_im).reshape(-1, S5_LANES)
    return in_mat(bb_re), in_mat(bb_im), pw_re, pw_im, out_mat(c_re), out_mat(c_im)


def kernel(x, c, ada_w, ada_b, ln_g, ln_b, ffn_w13, ffn_w2, w_in, w_out, sinks, w_uq, w_iq, conv_w, conv_b, ig_b,
           fg_b, mh_norm_g, lam_re, lam_im, log_step, b_re, b_im, c_re, c_im, d_skip, glu_w, glu_b):
    bn, t, d = x.shape
    assert d == D_MODEL and t % max(ROW_TILE, DSA_TK, S5_TT) == 0 and bn <= 8
    n = bn * t
    nl = ada_w.shape[0]
    c_pad = jnp.zeros((8, d), F32).at[:bn].set(c)
    mod_all = _ada_mod(c_pad, ada_w, ada_b)
    mod_all = mod_all[:, :bn].reshape(nl, bn, N_SUB, 3, d).transpose(0, 2, 1, 3, 4)
    mod_all = jnp.pad(mod_all, ((0, 0), (0, 0), (0, 0), (0, 5), (0, 0)))

    x2 = x.reshape(n, d)
    for l in range(nl):
        w13c = ffn_w13[l].astype(BF16).reshape(2, d, 2, FFN_NCHUNK, FFN_TF).transpose(0, 2, 3, 1, 4)
        w2c = ffn_w2[l].astype(BF16)
        x2 = _ffn_sublayer(x2, mod_all[l, 0], w13c[0], w2c[0], ln_g[l, 0], ln_b[l, 0], t)
        (a_q, a_kv, b_q, b_qi, b_kv, b_ki, c_qk, c_v, c_o, d_u, small) = _inproj(
            x2, mod_all[l, 1], _permute_w_in(w_in[l]), w_uq[l].astype(BF16), w_iq[l].astype(BF16), t)
        o_a = _swa(sinks[l], a_q, a_kv, bn, t)
        o_b = _dsa(b_q, b_qi, small, b_kv, b_ki, bn, t)
        gate_bias = jnp.zeros((1, 128), F32).at[0, SM_IG:SM_IG + 4].set(ig_b[l]).at[0, SM_FG:SM_FG + 4].set(fg_b[l])
        o_c = _mlstm(c_qk, c_v, c_o, small, conv_w[l], conv_b[l].reshape(1, -1), gate_bias,
                     mh_norm_g[l].reshape(1, -1), bn, t).reshape(n, GROUP_WIDTH)
        s5p = _s5_params(lam_re[l], lam_im[l], log_step[l], b_re[l], b_im[l], c_re[l], c_im[l])
        o_d = _s5(d_u, *s5p, d_skip[l].reshape(1, -1), glu_w[l].astype(BF16), glu_b[l].reshape(1, -1), bn, t)
        x2 = _outproj(x2, mod_all[l, 1], o_a, o_b, o_c, o_d,
                      w_out[l].astype(BF16).reshape(4, GROUP_WIDTH, d), ln_g[l, 1], ln_b[l, 1], t)
        x2 = _ffn_sublayer(x2, mod_all[l, 2], w13c[1], w2c[1], ln_g[l, 2], ln_b[l, 2], t)
    return x2.reshape(bn, t, d)
```

```python
import functools
import math

import jax
import jax.numpy as jnp
from jax import lax
from jax.experimental import pallas as pl
from jax.experimental.pallas import tpu as pltpu

F32 = jnp.float32
BF16 = jnp.bfloat16
I32 = jnp.int32

D_MODEL = 1024
DEPTH = 2
CHUNK = 64
HEAD_DIM = 64
GROUP_WIDTH = 256
SWA_HEADS = 4
SWA_KV_HEADS = 2
SWA_WIN_CHUNKS = 2
DSA_HEADS = 4
DSA_Q_RANK = 128
IDX_HEADS = 4
IDX_DIM = 32
DSA_TOPK = 256
MLSTM_HEADS = 4
MLSTM_CONV = 4
S5_GROUP_CH = 16
S5_GROUPS = 16
S5_STATE = 64
S5_LANES = S5_GROUPS * S5_STATE
D_FF = 2816
N_SUB = 3
ALPHA = (2 * DEPTH) ** 0.25
LN_EPS = 1e-5
NEG_INF = -1e30
INT_MIN = -(2 ** 31)
FIELD_BITS = 15
FIELD_GUARD = -(2 ** 31) + 2 ** 15

SLOPES_A = tuple(2.0 ** -(i + 1) for i in range(0, 8, 2))
SLOPES_B = tuple(2.0 ** -(i + 1) for i in range(1, 8, 2))

VMEM_LIMIT_BYTES = 56 * 1024 * 1024

FFN_TF = 256
FFN_NCHUNK = D_FF // FFN_TF
ROW_TILE = 512
SWA_TQ = 128
DSA_TQ = 128
DSA_TK = 512
S5_TT = 512
S5_SEGS = 8

ZC_QA, ZC_KVA, ZC_CQ, ZC_KVB, ZC_KI = 0, 256, 512, 640, 768
ZC_QKC, ZC_VC, ZC_OC, ZC_UD, ZC_SMALL, Z_WIDTH = 896, 1408, 1664, 1920, 2176, 2304
SM_WI, SM_IG, SM_FG = 0, 4, 8


def _cparams(sem):
    return pltpu.CompilerParams(dimension_semantics=sem, vmem_limit_bytes=VMEM_LIMIT_BYTES)


def _dot(a, b):
    return jnp.dot(a, b, preferred_element_type=F32)


def _dot_nt(a, b):
    return lax.dot_general(a, b, (((1,), (1,)), ((), ())), preferred_element_type=F32)


def _sigmoid(x):
    return 1.0 / (1.0 + jnp.exp(-x))


def _residual_layer_norm(x, y, g, b):
    v = ALPHA * x + y
    mu = jnp.mean(v, axis=-1, keepdims=True)
    d = v - mu
    var = jnp.mean(d * d, axis=-1, keepdims=True)
    return d * lax.rsqrt(var + LN_EPS) * g + b


def _ada_kernel(c_ref, w_ref, b_ref, o_ref):
    c = c_ref[...]
    cs = c * _sigmoid(c)
    o_ref[...] = jnp.dot(cs, w_ref[...], preferred_element_type=F32,
                         precision=lax.Precision.HIGHEST) + b_ref[...]


def _ada_mod(c_pad, ada_w, ada_b):
    nl = ada_w.shape[0]
    ncol = ada_w.shape[2] // D_MODEL
    return pl.pallas_call(
        _ada_kernel,
        grid=(nl, ncol),
        in_specs=[
            pl.BlockSpec((8, D_MODEL), lambda l, j: (0, 0)),
            pl.BlockSpec((None, D_MODEL, D_MODEL), lambda l, j: (l, 0, j)),
            pl.BlockSpec((None, 1, D_MODEL), lambda l, j: (l, 0, j)),
        ],
        out_specs=pl.BlockSpec((None, 8, D_MODEL), lambda l, j: (l, 0, j)),
        out_shape=jax.ShapeDtypeStruct((nl, 8, ada_w.shape[2]), F32),
        compiler_params=_cparams(("arbitrary", "arbitrary")),
        name="ada_mod",
    )(c_pad, ada_w, ada_b.reshape(nl, 1, -1))


def _ffn_kernel(x_ref, mod_ref, w13_ref, w2_ref, lng_ref, lnb_ref, o_ref, h_ref):
    x = x_ref[...]
    shift, scale, gate = mod_ref[0:1, :], mod_ref[1:2, :], mod_ref[2:3, :]
    u = (x * (1.0 + scale) + shift).astype(BF16)

    for j in range(FFN_NCHUNK):
        a = _dot(u, w13_ref[0, j])
        g = _dot(u, w13_ref[1, j])
        h_ref[:, j * FFN_TF:(j + 1) * FFN_TF] = (a * _sigmoid(a) * g).astype(BF16)
    y = _dot(h_ref[...], w2_ref[...])
    o_ref[...] = _residual_layer_norm(x, 0.5 * gate * y, lng_ref[...], lnb_ref[...])


def _ffn_sublayer(x2, mod, w13c, w2c, ln_g, ln_b, rows_per_batch):
    n = x2.shape[0]
    tm = ROW_TILE
    tiles_per_batch = rows_per_batch // tm
    return pl.pallas_call(
        _ffn_kernel,
        grid=(n // tm,),
        in_specs=[
            pl.BlockSpec((tm, D_MODEL), lambda i: (i, 0)),
            pl.BlockSpec((None, 8, D_MODEL), lambda i: (i // tiles_per_batch, 0, 0)),
            pl.BlockSpec(w13c.shape, lambda i: (0, 0, 0, 0), pipeline_mode=pl.Buffered(1)),
            pl.BlockSpec(w2c.shape, lambda i: (0, 0), pipeline_mode=pl.Buffered(1)),
            pl.BlockSpec((1, D_MODEL), lambda i: (0, 0)),
            pl.BlockSpec((1, D_MODEL), lambda i: (0, 0)),
        ],
        out_specs=pl.BlockSpec((tm, D_MODEL), lambda i: (i, 0)),
        out_shape=jax.ShapeDtypeStruct((n, D_MODEL), F32),
        scratch_shapes=[pltpu.VMEM((tm, D_FF), BF16)],
        compiler_params=_cparams(("arbitrary",)),
        name="ffn_sublayer",
    )(x2, mod, w13c, w2c, ln_g.reshape(1, -1), ln_b.reshape(1, -1))


def _inproj_kernel(x_ref, mod_ref, w_ref, wuq_ref, wiq_ref,
                   aq_ref, akv_ref, bq_ref, bqi_ref, bkv_ref, bki_ref,
                   cqk_ref, cv_ref, co_ref, du_ref, sm_ref):
    x = x_ref[...]
    shift, scale = mod_ref[0:1, :], mod_ref[1:2, :]
    u = (x * (1.0 + scale) + shift).astype(BF16)
    z = _dot(u, w_ref[...])
    aq_ref[...] = z[:, ZC_QA:ZC_KVA].astype(BF16)
    akv_ref[...] = z[:, ZC_KVA:ZC_CQ].astype(BF16)
    cq = z[:, ZC_CQ:ZC_KVB].astype(BF16)
    bq_ref[...] = (_dot(cq, wuq_ref[...]) * HEAD_DIM ** -0.5).astype(BF16)
    bqi_ref[...] = _dot(cq, wiq_ref[...]).astype(BF16)
    bkv_ref[...] = z[:, ZC_KVB:ZC_KI].astype(BF16)
    bki_ref[...] = z[:, ZC_KI:ZC_QKC].astype(BF16)
    cqk_ref[...] = z[:, ZC_QKC:ZC_VC]
    cv_ref[...] = z[:, ZC_VC:ZC_OC].astype(BF16)
    co_ref[...] = z[:, ZC_OC:ZC_UD]
    du_ref[...] = z[:, ZC_UD:ZC_SMALL]
    sm_ref[...] = z[:, ZC_SMALL:Z_WIDTH]


def _inproj(x2, mod, w_perm, w_uq, w_iq, rows_per_batch):
    n = x2.shape[0]
    tm = ROW_TILE
    tiles_per_batch = rows_per_batch // tm
    widths = [(256, BF16), (256, BF16), (256, BF16), (128, BF16), (128, BF16), (128, BF16),
              (512, F32), (256, BF16), (256, F32), (256, F32), (128, F32)]
    return pl.pallas_call(
        _inproj_kernel,
        grid=(n // tm,),
        in_specs=[
            pl.BlockSpec((tm, D_MODEL), lambda i: (i, 0)),
            pl.BlockSpec((None, 8, D_MODEL), lambda i: (i // tiles_per_batch, 0, 0)),
            pl.BlockSpec(w_perm.shape, lambda i: (0, 0), pipeline_mode=pl.Buffered(1)),
            pl.BlockSpec(w_uq.shape, lambda i: (0, 0), pipeline_mode=pl.Buffered(1)),
            pl.BlockSpec(w_iq.shape, lambda i: (0, 0), pipeline_mode=pl.Buffered(1)),
        ],
        out_specs=[pl.BlockSpec((tm, w), lambda i: (i, 0)) for w, _ in widths],
        out_shape=[jax.ShapeDtypeStruct((n, w), dt) for w, dt in widths],
        compiler_params=_cparams(("arbitrary",)),
        name="mixer_inproj",
    )(x2, mod, w_perm, w_uq, w_iq)


def _swa_kernel(sink_ref, q_ref, kvc_ref, kvp_ref, o_ref):
    i = pl.program_id(1)
    tq = SWA_TQ
    q = q_ref[...]
    kv = jnp.concatenate([kvp_ref[...], kvc_ref[...]], axis=0)
    qpos = i * tq + lax.broadcasted_iota(I32, (tq, 2 * tq), 0)
    kpos = (i - 1) * tq + lax.broadcasted_iota(I32, (tq, 2 * tq), 1)
    qchunk = qpos // CHUNK
    kchunk = (kpos + tq) // CHUNK - tq // CHUNK
    valid = (kpos >= 0) & (kchunk <= qchunk) & (kchunk >= qchunk - SWA_WIN_CHUNKS)
    dist = jnp.abs(qpos - kpos).astype(F32)
    rep = SWA_HEADS // SWA_KV_HEADS
    for h in range(SWA_HEADS):
        g = h // rep
        qh = q[:, h * HEAD_DIM:(h + 1) * HEAD_DIM]
        kg = kv[:, g * HEAD_DIM:(g + 1) * HEAD_DIM]
        vg = kv[:, (SWA_KV_HEADS + g) * HEAD_DIM:(SWA_KV_HEADS + g + 1) * HEAD_DIM]
        s = _dot_nt(qh, kg) * HEAD_DIM ** -0.5 - SLOPES_A[h] * dist
        s = jnp.where(valid, s, NEG_INF)
        sink = sink_ref[h]
        m = jnp.maximum(jnp.max(s, axis=-1, keepdims=True), sink)
        p = jnp.exp(s - m)
        denom = jnp.sum(p, axis=-1, keepdims=True) + jnp.exp(sink - m)
        o = _dot(p.astype(BF16), vg) / denom
        o_ref[:, h * HEAD_DIM:(h + 1) * HEAD_DIM] = o.astype(BF16)


def _swa(sinks, a_q, a_kv, bn, t):
    nt = t // SWA_TQ
    return pl.pallas_call(
        _swa_kernel,
        grid=(bn, nt),
        in_specs=[
            pl.BlockSpec(memory_space=pltpu.SMEM),
            pl.BlockSpec((SWA_TQ, 256), lambda b, i: (b * nt + i, 0)),
            pl.BlockSpec((SWA_TQ, 256), lambda b, i: (b * nt + i, 0)),
            pl.BlockSpec((SWA_TQ, 256), lambda b, i: (b * nt + jnp.maximum(i - 1, 0), 0)),
        ],
        out_specs=pl.BlockSpec((SWA_TQ, 256), lambda b, i: (b * nt + i, 0)),
        out_shape=jax.ShapeDtypeStruct((bn * t, 256), BF16),
        compiler_params=_cparams(("arbitrary", "arbitrary")),
        name="swa_attention",
    )(sinks, a_q, a_kv, a_kv)


def _sortable_key(x):
    bits = lax.bitcast_convert_type(x, I32)
    return bits ^ ((bits >> 31) & 0x7FFFFFFF)


def _dsa_kernel(q_ref, qi_ref, sm_ref, kv_ref, vt_ref, ki_ref, o_ref, key_ref, s_ref, p_ref, w1_ref, w2_ref):
    i = pl.program_id(1)
    tq, tk = DSA_TQ, DSA_TK
    nblk = (i * tq + tq + tk - 1) // tk
    qpos = i * tq + lax.broadcasted_iota(I32, (1, tq), 1)
    qchunk = qpos // CHUNK
    row_k = lax.broadcasted_iota(I32, (tk, tq), 0)

    qi = qi_ref[...]
    lane_i = lax.broadcasted_iota(I32, (1, IDX_HEADS * IDX_DIM), 1) // IDX_DIM
    qi_stack = jnp.concatenate([jnp.where(lane_i == h, qi, jnp.zeros_like(qi)) for h in range(IDX_HEADS)], axis=0)
    sm_t = sm_ref[...].T
    w_idx = [sm_t[SM_WI + h:SM_WI + h + 1, :] for h in range(IDX_HEADS)]
    idx_scale = (IDX_DIM * IDX_HEADS) ** -0.5
    q = q_ref[...]
    q_stack = jnp.concatenate([q[:, h * HEAD_DIM:(h + 1) * HEAD_DIM] for h in range(DSA_HEADS)], axis=0)

    half = tk // 2
    guard = jnp.int32(FIELD_GUARD)

    def pack_fields(f):
        return (f[0:half] << 16) | f[half:tk] | guard

    def score_block(j, mask_inadmissible):
        rows = pl.ds(pl.multiple_of(j * tk, tk), tk)
        s_ref[j] = _dot_nt(kv_ref[rows, 0:HEAD_DIM], q_stack)
        d = _dot_nt(ki_ref[rows, :], qi_stack)
        acc = w_idx[0] * jnp.maximum(d[:, 0:tq], 0.0)
        for h in range(1, IDX_HEADS):
            acc = acc + w_idx[h] * jnp.maximum(d[:, h * tq:(h + 1) * tq], 0.0)
        sc = acc * idx_scale
        if mask_inadmissible:
            sc = jnp.where((j * tk + row_k) // CHUNK <= qchunk, sc, NEG_INF)
        key = _sortable_key(sc)
        key_ref[j] = key
        w1_ref[j] = pack_fields(lax.shift_right_logical(key ^ INT_MIN, 32 - FIELD_BITS))

    def full_block(j, carry):
        score_block(j, False)
        return carry

    lax.fori_loop(0, nblk - 1, full_block, 0)
    score_block(nblk - 1, True)

    k_eff = jnp.minimum(DSA_TOPK, (qchunk + 1) * CHUNK)

    def over_blocks(body, init):
        c = lax.fori_loop(0, nblk // 2, lambda jj, c: body(2 * jj + 1, body(2 * jj, c)), init)
        return lax.cond(nblk % 2 == 1, lambda c: body(nblk - 1, c), lambda c: c, c)

    def count(pred):
        def blk(j, c):
            m = pred(key_ref[j], j * tk + row_k).astype(I32)
            return c + jnp.sum(m.reshape(tk // 8, 8, tq), axis=0)
        return jnp.sum(over_blocks(blk, jnp.zeros((8, tq), I32)), axis=0, keepdims=True)

    def count_fields(w_ref, cand):
        cand2 = (cand << 16) | cand
        def blk(j, c):
            hit = ((w_ref[j] - cand2) >> 15) & 0x00010001
            return c + jnp.sum(hit.reshape(half // 8, 8, tq), axis=0)
        c = over_blocks(blk, jnp.zeros((8, tq), I32))
        return jnp.sum((c & 0xFFFF) + (c >> 16), axis=0, keepdims=True)

    def field_search(w_ref, k_want):
        def step(bi, prefix):
            cand = prefix | (jnp.int32(1) << (FIELD_BITS - 1 - bi))
            return jnp.where(count_fields(w_ref, cand) >= k_want, cand, prefix)
        return lax.fori_loop(0, FIELD_BITS, step, jnp.zeros((1, tq), I32))

    top = field_search(w1_ref, k_eff)
    field_max = (1 << FIELD_BITS) - 1
    above = jnp.where(top == field_max, 0, count_fields(w1_ref, jnp.minimum(top + 1, field_max)))

    def pack_mid(j, carry):
        ukey = key_ref[j] ^ INT_MIN
        member = lax.shift_right_logical(ukey, 32 - FIELD_BITS) == top
        mid = lax.shift_right_logical(ukey, 32 - 2 * FIELD_BITS) & field_max
        w2_ref[j] = pack_fields(jnp.where(member, mid, 0))
        return carry

    lax.fori_loop(0, nblk, pack_mid, 0)
    mid = field_search(w2_ref, k_eff - above)

    def bit_step(bi, prefix):
        cand_u = prefix | (jnp.int32(1) << (31 - 2 * FIELD_BITS - bi))
        cand_s = cand_u ^ INT_MIN
        cnt = count(lambda key, kpos: key >= cand_s)
        return jnp.where(cnt >= k_eff, cand_u, prefix)

    prefix = lax.fori_loop(0, 32 - 2 * FIELD_BITS, bit_step,
                           (top << (32 - FIELD_BITS)) | (mid << (32 - 2 * FIELD_BITS)))
    thr = prefix ^ INT_MIN

    c_gt = count(lambda key, kpos: key > thr)
    need = (k_eff - c_gt).astype(F32)
    tril = (lax.broadcasted_iota(I32, (tk, tk), 0) >= lax.broadcasted_iota(I32, (tk, tk), 1)).astype(BF16)

    offs = (row_k - qpos).astype(F32)
    p_ref[1] = jnp.zeros(p_ref.shape[1:], BF16)

    def att_block(j, carry):
        m_run, l_run, acc_part, ties_seen = carry
        slot = j % 2
        acc = acc_part + _dot(vt_ref[jnp.maximum(j - 1, 0)], p_ref[1 - slot])
        key = key_ref[j]
        tie = key == thr
        tie_rank = _dot(tril, jnp.where(tie, 1.0, 0.0).astype(BF16)) + ties_seen
        sel = (key > thr) | (tie & (tie_rank <= need))
        dist = jnp.where(sel, jnp.abs(offs + (j * tk).astype(F32)), jnp.inf)
        ms, ls, alphas = [], [], []
        for h in range(DSA_HEADS):
            cols = slice(h * tq, (h + 1) * tq)
            sh = s_ref[j, :, cols] - SLOPES_B[h] * dist
            m_old = m_run[:, cols]
            m_new = jnp.maximum(m_old, jnp.max(sh, axis=0, keepdims=True))
            alpha = jnp.exp(m_old - m_new)
            p = jnp.exp(sh - m_new)
            p_ref[slot, :, cols] = p.astype(BF16)
            ms.append(m_new)
            ls.append(alpha * l_run[:, cols] + jnp.sum(p, axis=0, keepdims=True))
            alphas.append(alpha)
        return (jnp.concatenate(ms, axis=1), jnp.concatenate(ls, axis=1), jnp.concatenate(alphas, axis=1) * acc,
                tie_rank[tk - 1:tk, :])

    init = (jnp.full((1, DSA_HEADS * tq), NEG_INF, F32), jnp.zeros((1, DSA_HEADS * tq), F32),
            jnp.zeros((HEAD_DIM, DSA_HEADS * tq), F32), jnp.zeros((1, tq), F32))
    _, l_run, acc_part, _ = lax.fori_loop(0, nblk, att_block, init)
    acc = acc_part + _dot(vt_ref[nblk - 1], p_ref[(nblk - 1) % 2])
    out = acc / l_run
    for h in range(DSA_HEADS):
        o_ref[:, h * HEAD_DIM:(h + 1) * HEAD_DIM] = out[:, h * tq:(h + 1) * tq].T.astype(BF16)


def _dsa(b_q, b_qi, small, b_kv, b_ki, bn, t):
    nt = t // DSA_TQ
    nkb = t // DSA_TK
    v_t = b_kv[:, HEAD_DIM:].reshape(bn * nkb, DSA_TK, HEAD_DIM).transpose(0, 2, 1)
    return pl.pallas_call(
        _dsa_kernel,
        grid=(bn, nt),
        in_specs=[
            pl.BlockSpec((DSA_TQ, 256), lambda b, i: (b * nt + i, 0)),
            pl.BlockSpec((DSA_TQ, 128), lambda b, i: (b * nt + i, 0)),
            pl.BlockSpec((DSA_TQ, 128), lambda b, i: (b * nt + i, 0)),
            pl.BlockSpec((t, 128), lambda b, i: (b, 0)),
            pl.BlockSpec((nkb, HEAD_DIM, DSA_TK), lambda b, i: (b, 0, 0)),
            pl.BlockSpec((t, 128), lambda b, i: (b, 0)),
        ],
        out_specs=pl.BlockSpec((DSA_TQ, 256), lambda b, i: (b * nt + i, 0)),
        out_shape=jax.ShapeDtypeStruct((bn * t, 256), BF16),
        scratch_shapes=[pltpu.VMEM((nkb, DSA_TK, DSA_TQ), I32),
                        pltpu.VMEM((nkb, DSA_TK, DSA_HEADS * DSA_TQ), F32),
                        pltpu.VMEM((2, DSA_TK, DSA_HEADS * DSA_TQ), BF16),
                        pltpu.VMEM((nkb, DSA_TK // 2, DSA_TQ), I32), pltpu.VMEM((nkb, DSA_TK // 2, DSA_TQ), I32)],
        compiler_params=_cparams(("arbitrary", "arbitrary")),
        name="dsa_attention",
    )(b_q, b_qi, small, b_kv, v_t, b_ki)


def _mlstm_kernel(qk_ref, v_ref, og_ref, sm_ref, convw_ref, convb_ref, gbias_ref, normgt_ref,
                  o_ref, tail_ref, ct_ref, nvec_ref, mst_ref, *, bn):
    c = pl.program_id(0)
    L = CHUNK
    nh, dh, width = MLSTM_HEADS, HEAD_DIM, MLSTM_HEADS * HEAD_DIM

    @pl.when(c == 0)
    def _():
        tail_ref[...] = jnp.zeros_like(tail_ref)
        ct_ref[...] = jnp.zeros_like(ct_ref)
        nvec_ref[...] = jnp.zeros_like(nvec_ref)
        mst_ref[...] = jnp.zeros_like(mst_ref)

    srow = lax.broadcasted_iota(I32, (L, width), 0)
    lane = lax.broadcasted_iota(I32, (L, width), 1)
    jlane = lane % dh
    causal_t = srow <= jlane
    diag_t = srow == jlane
    head_of_lane = lax.broadcasted_iota(I32, (1, width), 1) // dh
    tril = (lax.broadcasted_iota(I32, (L, L), 1) <= lax.broadcasted_iota(I32, (L, L), 0)).astype(F32)
    erow = lax.broadcasted_iota(I32, (128, width), 0)
    ecol_head = lax.broadcasted_iota(I32, (128, width), 1) // dh
    expand_ig = (erow == SM_IG + ecol_head).astype(F32)
    expand_fg = (erow == SM_FG + ecol_head).astype(F32)
    exact = dict(preferred_element_type=F32, precision=lax.Precision.HIGHEST)

    def head_blocks(a):
        out = jnp.where(head_of_lane == 0, a[0:dh], 0.0)
        for h in range(1, nh):
            out = out + jnp.where(head_of_lane == h, a[h * dh:(h + 1) * dh], 0.0)
        return out

    convw = convw_ref[...]
    for b in range(bn):
        cur = qk_ref[b]
        ext = jnp.concatenate([tail_ref[b], cur], axis=0)
        tail_ref[b] = cur[L - 8:L, :]
        y = convb_ref[...] + convw[MLSTM_CONV - 1:MLSTM_CONV, :] * cur
        for k in range(MLSTM_CONV - 1):
            off = 8 - (MLSTM_CONV - 1) + k
            y = y + convw[k:k + 1, :] * ext[off:off + L, :]
        qk = y * _sigmoid(y)
        q_all = qk[:, 0:width]
        k_all = qk[:, width:2 * width] * dh ** -0.5
        q_stack = jnp.concatenate([jnp.where(head_of_lane == h, q_all, 0.0) for h in range(nh)], axis=0).astype(BF16)
        v_all = v_ref[b]
        v_t = v_all.astype(F32).T.astype(BF16)

        gates = sm_ref[b] + gbias_ref[...]
        lf = jnp.minimum(gates, 0.0) - jnp.log(1.0 + jnp.exp(-jnp.abs(gates)))
        bcum = jnp.dot(tril, lf, **exact)
        ig_x = jnp.dot(gates, expand_ig, **exact)
        b_x = jnp.dot(bcum, expand_fg, **exact)
        b_q = jnp.sum(jnp.where(diag_t, b_x, 0.0), axis=0, keepdims=True)
        b_last = b_x[L - 1:L, :]
        m_prev = mst_ref[b]
        ct = ct_ref[b]
        nvec = nvec_ref[b]

        dlog = jnp.where(causal_t, b_q - b_x + ig_x, NEG_INF)
        inter = b_q + m_prev
        mj = jnp.maximum(inter, jnp.max(dlog, axis=0, keepdims=True))
        dw = jnp.exp(dlog - mj)
        iw = jnp.exp(inter - mj)
        sc = _dot_nt(k_all.astype(BF16), q_stack) * dw
        qn = _dot_nt(jnp.broadcast_to(nvec, (8, width)).astype(BF16), q_stack)[0:1, :]
        q_c = _dot_nt(ct.astype(BF16), q_stack)
        num = iw * q_c + head_blocks(_dot(v_t, sc.astype(BF16)))
        den = iw * qn + jnp.sum(sc, axis=0, keepdims=True)
        hj = num / jnp.maximum(jnp.abs(den), jnp.exp(-mj))

        dec = b_last - b_x + ig_x
        m_new = jnp.maximum(b_last + m_prev, jnp.max(dec, axis=0, keepdims=True))
        wc = jnp.exp(b_last + m_prev - m_new)
        kw = k_all * jnp.exp(dec - m_new)
        ct_ref[b] = wc * ct + head_blocks(_dot(v_t, kw.astype(BF16)))
        nvec_ref[b] = wc * nvec + jnp.sum(kw, axis=0, keepdims=True)
        mst_ref[b] = m_new

        mu = jnp.mean(hj, axis=0, keepdims=True)
        dev = hj - mu
        var = jnp.mean(dev * dev, axis=0, keepdims=True)
        hn_t = (dev * lax.rsqrt(var + LN_EPS) * normgt_ref[...]).T
        hn = jnp.concatenate([hn_t[h * dh:(h + 1) * dh, :] for h in range(nh)], axis=1)
        o_ref[b] = (_sigmoid(og_ref[b]) * hn).astype(BF16)


def _mlstm(c_qk, c_v, c_o, small, conv_w, conv_b, gate_bias, norm_g, bn, t):
    nc = t // CHUNK
    width = MLSTM_HEADS * HEAD_DIM
    norm_g_t = jnp.repeat(norm_g.reshape(MLSTM_HEADS, HEAD_DIM).T, HEAD_DIM, axis=1)
    blk = lambda w: pl.BlockSpec((bn, CHUNK, w), lambda c: (0, c, 0))
    full = lambda a: pl.BlockSpec(a.shape, lambda c: (0,) * a.ndim)
    return pl.pallas_call(
        functools.partial(_mlstm_kernel, bn=bn),
        grid=(nc,),
        in_specs=[blk(512), blk(256), blk(256), blk(128), full(conv_w), full(conv_b), full(gate_bias), full(norm_g_t)],
        out_specs=blk(256),
        out_shape=jax.ShapeDtypeStruct((bn, t, 256), BF16),
        scratch_shapes=[pltpu.VMEM((bn, 8, 512), F32), pltpu.VMEM((bn, HEAD_DIM, width), F32),
                        pltpu.VMEM((bn, 1, width), F32), pltpu.VMEM((bn, 1, width), F32)],
        compiler_params=_cparams(("arbitrary",)),
        name="mlstm",
    )(c_qk.reshape(bn, t, 512), c_v.reshape(bn, t, 256), c_o.reshape(bn, t, 256), small.reshape(bn, t, 128),
      conv_w, conv_b, gate_bias, norm_g_t)


def _s5_kernel(u_ref, perm_ref, bbre_ref, bbim_ref, pwre_ref, pwim_ref, cre_ref, cim_ref, dskip_ref, gluw_ref,
               glub_ref, o_ref, sre_ref, sim_ref, stre_ref, stim_ref, yp_ref):
    @pl.when(pl.program_id(1) == 0)
    def _():
        stre_ref[...] = jnp.zeros_like(stre_ref)
        stim_ref[...] = jnp.zeros_like(stim_ref)

    seg_len = S5_TT // S5_SEGS
    u = u_ref[...]
    ub = _dot(perm_ref[...], u.astype(BF16)).astype(BF16)
    sre_ref[...] = _dot(ub, bbre_ref[...])
    sim_ref[...] = _dot(ub, bbim_ref[...])
    a_re = jnp.broadcast_to(pwre_ref[0:1, :], (S5_SEGS, S5_LANES))
    a_im = jnp.broadcast_to(pwim_ref[0:1, :], (S5_SEGS, S5_LANES))

    def local_step(i, carry):
        s_re, s_im = carry
        rows = pl.ds(pl.multiple_of(i * S5_SEGS, S5_SEGS), S5_SEGS)
        n_re = a_re * s_re - a_im * s_im + sre_ref[rows, :]
        n_im = a_re * s_im + a_im * s_re + sim_ref[rows, :]
        sre_ref[rows, :] = n_re
        sim_ref[rows, :] = n_im
        return n_re, n_im

    zeros = jnp.zeros((S5_SEGS, S5_LANES), F32)
    e_re, e_im = lax.fori_loop(0, seg_len, local_step, (zeros, zeros))

    al_re, al_im = pwre_ref[seg_len - 1:seg_len, :], pwim_ref[seg_len - 1:seg_len, :]
    c_re, c_im = stre_ref[0:1, :], stim_ref[0:1, :]
    cs_re, cs_im = [], []
    for k in range(S5_SEGS):
        cs_re.append(c_re)
        cs_im.append(c_im)
        c_re, c_im = (e_re[k:k + 1, :] + al_re * c_re - al_im * c_im,
                      e_im[k:k + 1, :] + al_re * c_im + al_im * c_re)
    stre_ref[...] = jnp.broadcast_to(c_re, stre_ref.shape)
    stim_ref[...] = jnp.broadcast_to(c_im, stim_ref.shape)
    cin_re = jnp.concatenate(cs_re, axis=0)
    cin_im = jnp.concatenate(cs_im, axis=0)

    def correct_step(i, carry):
        rows = pl.ds(pl.multiple_of(i * S5_SEGS, S5_SEGS), S5_SEGS)
        p_re, p_im = pwre_ref[pl.ds(i, 1), :], pwim_ref[pl.ds(i, 1), :]
        sre_ref[rows, :] = sre_ref[rows, :] + p_re * cin_re - p_im * cin_im
        sim_ref[rows, :] = sim_ref[rows, :] + p_re * cin_im + p_im * cin_re
        return carry

    lax.fori_loop(0, seg_len, correct_step, 0)
    ycs = _dot(sre_ref[...].astype(BF16), cre_ref[...]) - _dot(sim_ref[...].astype(BF16), cim_ref[...])
    ngrp = GROUP_WIDTH // 128
    for g in range(ngrp):
        yp_ref[g] = ycs[:, g * 128:(g + 1) * 128]
    blocks = []
    for k in range(S5_SEGS):
        for i0 in range(0, seg_len, 8):
            rows = pl.ds(i0 * S5_SEGS + k, 8, stride=S5_SEGS)
            blocks.append(jnp.concatenate([yp_ref[g, rows, :] for g in range(ngrp)], axis=1))
    y = jnp.concatenate(blocks, axis=0) + dskip_ref[...] * u
    y = 0.5 * y * (1.0 + jnp.tanh(math.sqrt(2.0 / math.pi) * (y + 0.044715 * (y * y * y))))
    z = _dot(y.astype(BF16), gluw_ref[...]) + glub_ref[...]
    o_ref[...] = (y * _sigmoid(z)).astype(BF16)


def _s5(d_u, bb_re, bb_im, pw_re, pw_im, c_re_t, c_im_t, d_skip, glu_w, glu_b, bn, t):
    nt = t // S5_TT
    full = lambda a: pl.BlockSpec(a.shape, lambda b, i: (0,) * a.ndim)
    r = jnp.arange(S5_TT)
    perm = (r[None, :] == ((r % S5_SEGS) * (S5_TT // S5_SEGS) + r // S5_SEGS)[:, None]).astype(BF16)
    args = (perm, bb_re, bb_im, pw_re, pw_im, c_re_t, c_im_t, d_skip, glu_w, glu_b)
    return pl.pallas_call(
        _s5_kernel,
        grid=(bn, nt),
        in_specs=[pl.BlockSpec((S5_TT, 256), lambda b, i: (b * nt + i, 0))] + [full(a) for a in args],
        out_specs=pl.BlockSpec((S5_TT, 256), lambda b, i: (b * nt + i, 0)),
        out_shape=jax.ShapeDtypeStruct((bn * t, 256), BF16),
        scratch_shapes=[pltpu.VMEM((S5_TT, S5_LANES), F32), pltpu.VMEM((S5_TT, S5_LANES), F32),
                        pltpu.VMEM((8, S5_LANES), F32), pltpu.VMEM((8, S5_LANES), F32),
                        pltpu.VMEM((GROUP_WIDTH // 128, S5_TT, 128), F32)],
        compiler_params=_cparams(("arbitrary", "arbitrary")),
        name="s5_glu",
    )(d_u, *args)


def _outproj_kernel(x_ref, mod_ref, oa_ref, ob_ref, oc_ref, od_ref, w_ref, lng_ref, lnb_ref, o_ref):
    x = x_ref[...]
    gate = mod_ref[2:3, :]
    y = _dot(oa_ref[...], w_ref[0])
    y = y + _dot(ob_ref[...], w_ref[1])
    y = y + _dot(oc_ref[...], w_ref[2])
    y = y + _dot(od_ref[...], w_ref[3])
    o_ref[...] = _residual_layer_norm(x, gate * y, lng_ref[...], lnb_ref[...])


def _outproj(x2, mod, o_a, o_b, o_c, o_d, w_out4, ln_g, ln_b, rows_per_batch):
    n = x2.shape[0]
    tm = ROW_TILE
    tiles_per_batch = rows_per_batch // tm
    mix = pl.BlockSpec((tm, GROUP_WIDTH), lambda i: (i, 0))
    return pl.pallas_call(
        _outproj_kernel,
        grid=(n // tm,),
        in_specs=[
            pl.BlockSpec((tm, D_MODEL), lambda i: (i, 0)),
            pl.BlockSpec((None, 8, D_MODEL), lambda i: (i // tiles_per_batch, 0, 0)),
            mix, mix, mix, mix,
            pl.BlockSpec(w_out4.shape, lambda i: (0, 0, 0), pipeline_mode=pl.Buffered(1)),
            pl.BlockSpec((1, D_MODEL), lambda i: (0, 0)),
            pl.BlockSpec((1, D_MODEL), lambda i: (0, 0)),
        ],
        out_specs=pl.BlockSpec((tm, D_MODEL), lambda i: (i, 0)),
        out_shape=jax.ShapeDtypeStruct((n, D_MODEL), F32),
        compiler_params=_cparams(("arbitrary",)),
        name="mixer_outproj",
    )(x2, mod, o_a, o_b, o_c, o_d, w_out4, ln_g.reshape(1, -1), ln_b.reshape(1, -1))


def _permute_w_in(w_in):
    off = {}
    o = 0
    for name, s in (("qa", 256), ("ka", 128), ("va", 128), ("cq", 128), ("kb", 64), ("vb", 64), ("ki", 32),
                    ("wi", 4), ("qkc", 512), ("vc", 256), ("ig", 4), ("fg", 4), ("oc", 256), ("ud", 256)):
        off[name] = (o, o + s)
        o += s
    col = lambda n: w_in[:, off[n][0]:off[n][1]]
    small = jnp.concatenate([col("wi"), col("ig"), col("fg"),
                             jnp.zeros((w_in.shape[0], 128 - 12), w_in.dtype)], axis=1)
    parts = [col("qa"), col("ka"), col("va"), col("cq"), col("kb"), col("vb")] + [col("ki")] * 4 + [
        col("qkc"), col("vc"), col("oc"), col("ud"), small]
    return jnp.concatenate(parts, axis=1).astype(BF16)


def _s5_params(lam_re, lam_im, log_step, b_re, b_im, c_re, c_im):
    dt = jnp.exp(log_step)[:, None]
    mag = jnp.exp(lam_re * dt)
    a_re, a_im = mag * jnp.cos(lam_im * dt), mag * jnp.sin(lam_im * dt)
    den = lam_re * lam_re + lam_im * lam_im
    kap_re = ((a_re - 1.0) * lam_re + a_im * lam_im) / den
    kap_im = (a_im * lam_re - (a_re - 1.0) * lam_im) / den
    bb_re = kap_re[..., None] * b_re - kap_im[..., None] * b_im
    bb_im = kap_re[..., None] * b_im + kap_im[..., None] * b_re
    eye = jnp.eye(S5_GROUPS, dtype=F32)

    def in_mat(bb):
        return jnp.einsum("gph,gk->ghkp", bb, eye).reshape(S5_GROUPS * S5_GROUP_CH, S5_LANES).astype(BF16)

    def out_mat(cc):
        return jnp.einsum("gop,gk->gpko", cc, eye).reshape(S5_LANES, S5_GROUPS * S5_GROUP_CH).astype(BF16)

    n = jnp.arange(1, S5_TT // S5_SEGS + 1, dtype=F32)[:, None, None]
    pw_mag = jnp.exp(n * (lam_re * dt))
    pw_re = (pw_mag * jnp.cos(n * (lam_im * dt))).at[0].set(a_re).reshape(-1, S5_LANES)
    pw_im = (pw_mag * jnp.sin(n * (lam_im * dt))).at[0].set(a_im).reshape(-1, S5_LANES)
    return in_mat(bb_re), in_mat(bb_im), pw_re, pw_im, out_mat(c_re), out_mat(c_im)


def kernel(x, c, ada_w, ada_b, ln_g, ln_b, ffn_w13, ffn_w2, w_in, w_out, sinks, w_uq, w_iq, conv_w, conv_b, ig_b,
           fg_b, mh_norm_g, lam_re, lam_im, log_step, b_re, b_im, c_re, c_im, d_skip, glu_w, glu_b):
    bn, t, d = x.shape
    assert d == D_MODEL and t % max(ROW_TILE, DSA_TK, S5_TT) == 0 and bn <= 8
    n = bn * t
    nl = ada_w.shape[0]
    c_pad = jnp.zeros((8, d), F32).at[:bn].set(c)
    mod_all = _ada_mod(c_pad, ada_w, ada_b)
    mod_all = mod_all[:, :bn].reshape(nl, bn, N_SUB, 3, d).transpose(0, 2, 1, 3, 4)
    mod_all = jnp.pad(mod_all, ((0, 0), (0, 0), (0, 0), (0, 5), (0, 0)))

    x2 = x.reshape(n, d)
    for l in range(nl):
        w13c = ffn_w13[l].astype(BF16).reshape(2, d, 2, FFN_NCHUNK, FFN_TF).transpose(0, 2, 3, 1, 4)
        w2c = ffn_w2[l].astype(BF16)
        x2 = _ffn_sublayer(x2, mod_all[l, 0], w13c[0], w2c[0], ln_g[l, 0], ln_b[l, 0], t)
        (a_q, a_kv, b_q, b_qi, b_kv, b_ki, c_qk, c_v, c_o, d_u, small) = _inproj(
            x2, mod_all[l, 1], _permute_w_in(w_in[l]), w_uq[l].astype(BF16), w_iq[l].astype(BF16), t)
        o_a = _swa(sinks[l], a_q, a_kv, bn, t)
        o_b = _dsa(b_q, b_qi, small, b_kv, b_ki, bn, t)
        gate_bias = jnp.zeros((1, 128), F32).at[0, SM_IG:SM_IG + 4].set(ig_b[l]).at[0, SM_FG:SM_FG + 4].set(fg_b[l])
        o_c = _mlstm(c_qk, c_v, c_o, small, conv_w[l], conv_b[l].reshape(1, -1), gate_bias,
                     mh_norm_g[l].reshape(1, -1), bn, t).reshape(n, GROUP_WIDTH)
        s5p = _s5_params(lam_re[l], lam_im[l], log_step[l], b_re[l], b_im[l], c_re[l], c_im[l])
        o_d = _s5(d_u, *s5p, d_skip[l].reshape(1, -1), glu_w[l].astype(BF16), glu_b[l].reshape(1, -1), bn, t)
        x2 = _outproj(x2, mod_all[l, 1], o_a, o_b, o_c, o_d,
                      w_out[l].astype(BF16).reshape(4, GROUP_WIDTH, d), ln_g[l, 1], ln_b[l, 1], t)
        x2 = _ffn_sublayer(x2, mod_all[l, 2], w13c[1], w2c[1], ln_g[l, 2], ln_b[l, 2], t)
    return x2.reshape(bn, t, d)
```

```python
import functools
import math

import jax
import jax.numpy as jnp
from jax import lax
from jax.experimental import pallas as pl
from jax.experimental.pallas import tpu as pltpu

F32 = jnp.float32
BF16 = jnp.bfloat16
I32 = jnp.int32

D_MODEL = 1024
DEPTH = 2
CHUNK = 64
HEAD_DIM = 64
GROUP_WIDTH = 256
SWA_HEADS = 4
SWA_KV_HEADS = 2
SWA_WIN_CHUNKS = 2
DSA_HEADS = 4
DSA_Q_RANK = 128
IDX_HEADS = 4
IDX_DIM = 32
DSA_TOPK = 256
MLSTM_HEADS = 4
MLSTM_CONV = 4
S5_GROUP_CH = 16
S5_GROUPS = 16
S5_STATE = 64
S5_LANES = S5_GROUPS * S5_STATE
D_FF = 2816
N_SUB = 3
ALPHA = (2 * DEPTH) ** 0.25
LN_EPS = 1e-5
NEG_INF = -1e30
INT_MIN = -(2 ** 31)
FIELD_BITS = 15
FIELD_GUARD = -(2 ** 31) + 2 ** 15

SLOPES_A = tuple(2.0 ** -(i + 1) for i in range(0, 8, 2))
SLOPES_B = tuple(2.0 ** -(i + 1) for i in range(1, 8, 2))

VMEM_LIMIT_BYTES = 56 * 1024 * 1024

FFN_TF = 256
FFN_NCHUNK = D_FF // FFN_TF
ROW_TILE = 512
SWA_TQ = 128
DSA_TQ = 128
DSA_TK = 512
S5_TT = 512
S5_SEGS = 8

ZC_QA, ZC_KVA, ZC_CQ, ZC_KVB, ZC_KI = 0, 256, 512, 640, 768
ZC_QKC, ZC_VC, ZC_OC, ZC_UD, ZC_SMALL, Z_WIDTH = 896, 1408, 1664, 1920, 2176, 2304
SM_WI, SM_IG, SM_FG = 0, 4, 8


def _cparams(sem):
    return pltpu.CompilerParams(dimension_semantics=sem, vmem_limit_bytes=VMEM_LIMIT_BYTES)


def _dot(a, b):
    return jnp.dot(a, b, preferred_element_type=F32)


def _dot_nt(a, b):
    return lax.dot_general(a, b, (((1,), (1,)), ((), ())), preferred_element_type=F32)


def _sigmoid(x):
    return 1.0 / (1.0 + jnp.exp(-x))


def _residual_layer_norm(x, y, g, b):
    v = ALPHA * x + y
    mu = jnp.mean(v, axis=-1, keepdims=True)
    d = v - mu
    var = jnp.mean(d * d, axis=-1, keepdims=True)
    return d * lax.rsqrt(var + LN_EPS) * g + b


def _ada_kernel(c_ref, w_ref, b_ref, o_ref):
    c = c_ref[...]
    cs = c * _sigmoid(c)
    o_ref[...] = jnp.dot(cs, w_ref[...], preferred_element_type=F32,
                         precision=lax.Precision.HIGHEST) + b_ref[...]


def _ada_mod(c_pad, ada_w, ada_b):
    nl = ada_w.shape[0]
    ncol = ada_w.shape[2] // D_MODEL
    return pl.pallas_call(
        _ada_kernel,
        grid=(nl, ncol),
        in_specs=[
            pl.BlockSpec((8, D_MODEL), lambda l, j: (0, 0)),
            pl.BlockSpec((None, D_MODEL, D_MODEL), lambda l, j: (l, 0, j)),
            pl.BlockSpec((None, 1, D_MODEL), lambda l, j: (l, 0, j)),
        ],
        out_specs=pl.BlockSpec((None, 8, D_MODEL), lambda l, j: (l, 0, j)),
        out_shape=jax.ShapeDtypeStruct((nl, 8, ada_w.shape[2]), F32),
        compiler_params=_cparams(("arbitrary", "arbitrary")),
        name="ada_mod",
    )(c_pad, ada_w, ada_b.reshape(nl, 1, -1))


def _ffn_kernel(x_ref, mod_ref, w13_ref, w2_ref, lng_ref, lnb_ref, o_ref, h_ref):
    x = x_ref[...]
    shift, scale, gate = mod_ref[0:1, :], mod_ref[1:2, :], mod_ref[2:3, :]
    u = (x * (1.0 + scale) + shift).astype(BF16)

    for j in range(FFN_NCHUNK):
        a = _dot(u, w13_ref[0, j])
        g = _dot(u, w13_ref[1, j])
        h_ref[:, j * FFN_TF:(j + 1) * FFN_TF] = (a * _sigmoid(a) * g).astype(BF16)
    y = _dot(h_ref[...], w2_ref[...])
    o_ref[...] = _residual_layer_norm(x, 0.5 * gate * y, lng_ref[...], lnb_ref[...])


def _ffn_sublayer(x2, mod, w13c, w2c, ln_g, ln_b, rows_per_batch):
    n = x2.shape[0]
    tm = ROW_TILE
    tiles_per_batch = rows_per_batch // tm
    return pl.pallas_call(
        _ffn_kernel,
        grid=(n // tm,),
        in_specs=[
            pl.BlockSpec((tm, D_MODEL), lambda i: (i, 0)),
            pl.BlockSpec((None, 8, D_MODEL), lambda i: (i // tiles_per_batch, 0, 0)),
            pl.BlockSpec(w13c.shape, lambda i: (0, 0, 0, 0), pipeline_mode=pl.Buffered(1)),
            pl.BlockSpec(w2c.shape, lambda i: (0, 0), pipeline_mode=pl.Buffered(1)),
            pl.BlockSpec((1, D_MODEL), lambda i: (0, 0)),
            pl.BlockSpec((1, D_MODEL), lambda i: (0, 0)),
        ],
        out_specs=pl.BlockSpec((tm, D_MODEL), lambda i: (i, 0)),
        out_shape=jax.ShapeDtypeStruct((n, D_MODEL), F32),
        scratch_shapes=[pltpu.VMEM((tm, D_FF), BF16)],
        compiler_params=_cparams(("arbitrary",)),
        name="ffn_sublayer",
    )(x2, mod, w13c, w2c, ln_g.reshape(1, -1), ln_b.reshape(1, -1))


def _inproj_kernel(x_ref, mod_ref, w_ref, wuq_ref, wiq_ref,
                   aq_ref, akv_ref, bq_ref, bqi_ref, bkv_ref, bki_ref,
                   cqk_ref, cv_ref, co_ref, du_ref, sm_ref):
    x = x_ref[...]
    shift, scale = mod_ref[0:1, :], mod_ref[1:2, :]
    u = (x * (1.0 + scale) + shift).astype(BF16)
    z = _dot(u, w_ref[...])
    aq_ref[...] = z[:, ZC_QA:ZC_KVA].astype(BF16)
    akv_ref[...] = z[:, ZC_KVA:ZC_CQ].astype(BF16)
    cq = z[:, ZC_CQ:ZC_KVB].astype(BF16)
    bq_ref[...] = (_dot(cq, wuq_ref[...]) * HEAD_DIM ** -0.5).astype(BF16)
    bqi_ref[...] = _dot(cq, wiq_ref[...]).astype(BF16)
    bkv_ref[...] = z[:, ZC_KVB:ZC_KI].astype(BF16)
    bki_ref[...] = z[:, ZC_KI:ZC_QKC].astype(BF16)
    cqk_ref[...] = z[:, ZC_QKC:ZC_VC]
    cv_ref[...] = z[:, ZC_VC:ZC_OC].astype(BF16)
    co_ref[...] = z[:, ZC_OC:ZC_UD]
    du_ref[...] = z[:, ZC_UD:ZC_SMALL]
    sm_ref[...] = z[:, ZC_SMALL:Z_WIDTH]


def _inproj(x2, mod, w_perm, w_uq, w_iq, rows_per_batch):
    n = x2.shape[0]
    tm = ROW_TILE
    tiles_per_batch = rows_per_batch // tm
    widths = [(256, BF16), (256, BF16), (256, BF16), (128, BF16), (128, BF16), (128, BF16),
              (512, F32), (256, BF16), (256, F32), (256, F32), (128, F32)]
    return pl.pallas_call(
        _inproj_kernel,
        grid=(n // tm,),
        in_specs=[
            pl.BlockSpec((tm, D_MODEL), lambda i: (i, 0)),
            pl.BlockSpec((None, 8, D_MODEL), lambda i: (i // tiles_per_batch, 0, 0)),
            pl.BlockSpec(w_perm.shape, lambda i: (0, 0), pipeline_mode=pl.Buffered(1)),
            pl.BlockSpec(w_uq.shape, lambda i: (0, 0), pipeline_mode=pl.Buffered(1)),
            pl.BlockSpec(w_iq.shape, lambda i: (0, 0), pipeline_mode=pl.Buffered(1)),
        ],
        out_specs=[pl.BlockSpec((tm, w), lambda i: (i, 0)) for w, _ in widths],
        out_shape=[jax.ShapeDtypeStruct((n, w), dt) for w, dt in widths],
        compiler_params=_cparams(("arbitrary",)),
        name="mixer_inproj",
    )(x2, mod, w_perm, w_uq, w_iq)


def _swa_kernel(sink_ref, q_ref, kvc_ref, kvp_ref, o_ref):
    i = pl.program_id(1)
    tq = SWA_TQ
    q = q_ref[...]
    kv = jnp.concatenate([kvp_ref[...], kvc_ref[...]], axis=0)
    qpos = i * tq + lax.broadcasted_iota(I32, (tq, 2 * tq), 0)
    kpos = (i - 1) * tq + lax.broadcasted_iota(I32, (tq, 2 * tq), 1)
    qchunk = qpos // CHUNK
    kchunk = (kpos + tq) // CHUNK - tq // CHUNK
    valid = (kpos >= 0) & (kchunk <= qchunk) & (kchunk >= qchunk - SWA_WIN_CHUNKS)
    dist = jnp.abs(qpos - kpos).astype(F32)
    rep = SWA_HEADS // SWA_KV_HEADS
    for h in range(SWA_HEADS):
        g = h // rep
        qh = q[:, h * HEAD_DIM:(h + 1) * HEAD_DIM]
        kg = kv[:, g * HEAD_DIM:(g + 1) * HEAD_DIM]
        vg = kv[:, (SWA_KV_HEADS + g) * HEAD_DIM:(SWA_KV_HEADS + g + 1) * HEAD_DIM]
        s = _dot_nt(qh, kg) * HEAD_DIM ** -0.5 - SLOPES_A[h] * dist
        s = jnp.where(valid, s, NEG_INF)
        sink = sink_ref[h]
        m = jnp.maximum(jnp.max(s, axis=-1, keepdims=True), sink)
        p = jnp.exp(s - m)
        denom = jnp.sum(p, axis=-1, keepdims=True) + jnp.exp(sink - m)
        o = _dot(p.astype(BF16), vg) / denom
        o_ref[:, h * HEAD_DIM:(h + 1) * HEAD_DIM] = o.astype(BF16)


def _swa(sinks, a_q, a_kv, bn, t):
    nt = t // SWA_TQ
    return pl.pallas_call(
        _swa_kernel,
        grid=(bn, nt),
        in_specs=[
            pl.BlockSpec(memory_space=pltpu.SMEM),
            pl.BlockSpec((SWA_TQ, 256), lambda b, i: (b * nt + i, 0)),
            pl.BlockSpec((SWA_TQ, 256), lambda b, i: (b * nt + i, 0)),
            pl.BlockSpec((SWA_TQ, 256), lambda b, i: (b * nt + jnp.maximum(i - 1, 0), 0)),
        ],
        out_specs=pl.BlockSpec((SWA_TQ, 256), lambda b, i: (b * nt + i, 0)),
        out_shape=jax.ShapeDtypeStruct((bn * t, 256), BF16),
        compiler_params=_cparams(("arbitrary", "arbitrary")),
        name="swa_attention",
    )(sinks, a_q, a_kv, a_kv)


def _sortable_key(x):
    bits = lax.bitcast_convert_type(x, I32)
    return bits ^ ((bits >> 31) & 0x7FFFFFFF)


def _dsa_kernel(q_ref, qi_ref, sm_ref, kv_ref, vt_ref, ki_ref, tril_ref, o_ref, key_ref, s_ref, p_ref, w1_ref, w2_ref):
    i = pl.program_id(1)
    tq, tk = DSA_TQ, DSA_TK
    nblk = (i * tq + tq + tk - 1) // tk
    qpos = i * tq + lax.broadcasted_iota(I32, (1, tq), 1)
    qchunk = qpos // CHUNK
    row_k = lax.broadcasted_iota(I32, (tk, tq), 0)

    qi = qi_ref[...]
    lane_i = lax.broadcasted_iota(I32, (1, IDX_HEADS * IDX_DIM), 1) // IDX_DIM
    qi_stack = jnp.concatenate([jnp.where(lane_i == h, qi, jnp.zeros_like(qi)) for h in range(IDX_HEADS)], axis=0)
    sm_t = sm_ref[...].T
    w_idx = [sm_t[SM_WI + h:SM_WI + h + 1, :] for h in range(IDX_HEADS)]
    idx_scale = (IDX_DIM * IDX_HEADS) ** -0.5
    q = q_ref[...]
    q_stack = jnp.concatenate([q[:, h * HEAD_DIM:(h + 1) * HEAD_DIM] for h in range(DSA_HEADS)], axis=0)

    half = tk // 2
    guard = jnp.int32(FIELD_GUARD)

    def pack_fields(f):
        return (f[0:half] << 16) | f[half:tk] | guard

    def score_block(j, mask_inadmissible):
        rows = pl.ds(pl.multiple_of(j * tk, tk), tk)
        s_ref[j] = _dot_nt(kv_ref[rows, 0:HEAD_DIM], q_stack)
        d = _dot_nt(ki_ref[rows, :], qi_stack)
        acc = w_idx[0] * jnp.maximum(d[:, 0:tq], 0.0)
        for h in range(1, IDX_HEADS):
            acc = acc + w_idx[h] * jnp.maximum(d[:, h * tq:(h + 1) * tq], 0.0)
        sc = acc * idx_scale
        if mask_inadmissible:
            sc = jnp.where((j * tk + row_k) // CHUNK <= qchunk, sc, NEG_INF)
        key = _sortable_key(sc)
        key_ref[j] = key
        w1_ref[j] = pack_fields(lax.shift_right_logical(key ^ INT_MIN, 32 - FIELD_BITS))

    def full_block(j, carry):
        score_block(j, False)
        return carry

    lax.fori_loop(0, nblk - 1, full_block, 0)
    score_block(nblk - 1, True)

    k_eff = jnp.minimum(DSA_TOPK, (qchunk + 1) * CHUNK)

    def over_blocks(body, init):
        c = lax.fori_loop(0, nblk // 2, lambda jj, c: body(2 * jj + 1, body(2 * jj, c)), init)
        return lax.cond(nblk % 2 == 1, lambda c: body(nblk - 1, c), lambda c: c, c)

    def count(pred):
        def blk(j, c):
            m = pred(key_ref[j], j * tk + row_k).astype(I32)
            return c + jnp.sum(m.reshape(tk // 8, 8, tq), axis=0)
        return jnp.sum(over_blocks(blk, jnp.zeros((8, tq), I32)), axis=0, keepdims=True)

    def count_fields(w_ref, cand):
        cand2 = (cand << 16) | cand
        def blk(j, c):
            hit = ((w_ref[j] - cand2) >> 15) & 0x00010001
            return c + jnp.sum(hit.reshape(half // 8, 8, tq), axis=0)
        c = over_blocks(blk, jnp.zeros((8, tq), I32))
        return jnp.sum((c & 0xFFFF) + (c >> 16), axis=0, keepdims=True)

    def field_search(w_ref, k_want):
        def step(bi, carry):
            prefix, above = carry
            cand = prefix | (jnp.int32(1) << (FIELD_BITS - 1 - bi))
            cnt = count_fields(w_ref, cand)
            ok = cnt >= k_want
            return jnp.where(ok, cand, prefix), jnp.where(ok, above, cnt)
        return lax.fori_loop(0, FIELD_BITS, step, (jnp.zeros((1, tq), I32), jnp.zeros((1, tq), I32)))

    top, above = field_search(w1_ref, k_eff)
    field_max = (1 << FIELD_BITS) - 1

    def pack_mid(j, carry):
        ukey = key_ref[j] ^ INT_MIN
        member = lax.shift_right_logical(ukey, 32 - FIELD_BITS) == top
        mid = lax.shift_right_logical(ukey, 32 - 2 * FIELD_BITS) & field_max
        w2_ref[j] = pack_fields(jnp.where(member, mid, 0))
        return carry

    lax.fori_loop(0, nblk, pack_mid, 0)
    mid, above_mid = field_search(w2_ref, k_eff - above)

    def bit_step(bi, carry):
        prefix, c_gt = carry
        cand_u = prefix | (jnp.int32(1) << (31 - 2 * FIELD_BITS - bi))
        cand_s = cand_u ^ INT_MIN
        cnt = count(lambda key, kpos: key >= cand_s)
        ok = cnt >= k_eff
        return jnp.where(ok, cand_u, prefix), jnp.where(ok, c_gt, cnt)

    prefix, c_gt = lax.fori_loop(0, 32 - 2 * FIELD_BITS, bit_step,
                                 ((top << (32 - FIELD_BITS)) | (mid << (32 - 2 * FIELD_BITS)), above + above_mid))
    thr = prefix ^ INT_MIN

    need = (k_eff - c_gt).astype(F32)
    tril = tril_ref[...]

    offs = (row_k - qpos).astype(F32)
    p_ref[1] = jnp.zeros(p_ref.shape[1:], BF16)

    def att_block(j, carry):
        m_run, l_run, acc_part, ties_seen = carry
        slot = j % 2
        acc = acc_part + _dot(vt_ref[jnp.maximum(j - 1, 0)], p_ref[1 - slot])
        key = key_ref[j]
        tie = key == thr
        tie_rank = _dot(tril, jnp.where(tie, 1.0, 0.0).astype(BF16)) + ties_seen
        sel = (key > thr) | (tie & (tie_rank <= need))
        dist = jnp.where(sel, jnp.abs(offs + (j * tk).astype(F32)), jnp.inf)
        ms, ls, alphas = [], [], []
        for h in range(DSA_HEADS):
            cols = slice(h * tq, (h + 1) * tq)
            sh = s_ref[j, :, cols] - SLOPES_B[h] * dist
            m_old = m_run[:, cols]
            m_new = jnp.maximum(m_old, jnp.max(sh, axis=0, keepdims=True))
            alpha = jnp.exp(m_old - m_new)
            p = jnp.exp(sh - m_new)
            p_ref[slot, :, cols] = p.astype(BF16)
            ms.append(m_new)
            ls.append(alpha * l_run[:, cols] + jnp.sum(p, axis=0, keepdims=True))
            alphas.append(alpha)
        return (jnp.concatenate(ms, axis=1), jnp.concatenate(ls, axis=1), jnp.concatenate(alphas, axis=1) * acc,
                tie_rank[tk - 1:tk, :])

    init = (jnp.full((1, DSA_HEADS * tq), NEG_INF, F32), jnp.zeros((1, DSA_HEADS * tq), F32),
            jnp.zeros((HEAD_DIM, DSA_HEADS * tq), F32), jnp.zeros((1, tq), F32))
    _, l_run, acc_part, _ = lax.fori_loop(0, nblk, att_block, init)
    acc = acc_part + _dot(vt_ref[nblk - 1], p_ref[(nblk - 1) % 2])
    out = acc / l_run
    for h in range(DSA_HEADS):
        o_ref[:, h * HEAD_DIM:(h + 1) * HEAD_DIM] = out[:, h * tq:(h + 1) * tq].T.astype(BF16)


def _dsa(b_q, b_qi, small, b_kv, b_ki, bn, t):
    nt = t // DSA_TQ
    nkb = t // DSA_TK
    v_t = b_kv[:, HEAD_DIM:].reshape(bn * nkb, DSA_TK, HEAD_DIM).transpose(0, 2, 1)
    return pl.pallas_call(
        _dsa_kernel,
        grid=(bn, nt),
        in_specs=[
            pl.BlockSpec((DSA_TQ, 256), lambda b, i: (b * nt + i, 0)),
            pl.BlockSpec((DSA_TQ, 128), lambda b, i: (b * nt + i, 0)),
            pl.BlockSpec((DSA_TQ, 128), lambda b, i: (b * nt + i, 0)),
            pl.BlockSpec((t, 128), lambda b, i: (b, 0)),
            pl.BlockSpec((nkb, HEAD_DIM, DSA_TK), lambda b, i: (b, 0, 0)),
            pl.BlockSpec((t, 128), lambda b, i: (b, 0)),
            pl.BlockSpec((DSA_TK, DSA_TK), lambda b, i: (0, 0)),
        ],
        out_specs=pl.BlockSpec((DSA_TQ, 256), lambda b, i: (b * nt + i, 0)),
        out_shape=jax.ShapeDtypeStruct((bn * t, 256), BF16),
        scratch_shapes=[pltpu.VMEM((nkb, DSA_TK, DSA_TQ), I32),
                        pltpu.VMEM((nkb, DSA_TK, DSA_HEADS * DSA_TQ), F32),
                        pltpu.VMEM((2, DSA_TK, DSA_HEADS * DSA_TQ), BF16),
                        pltpu.VMEM((nkb, DSA_TK // 2, DSA_TQ), I32), pltpu.VMEM((nkb, DSA_TK // 2, DSA_TQ), I32)],
        compiler_params=_cparams(("arbitrary", "arbitrary")),
        name="dsa_attention",
    )(b_q, b_qi, small, b_kv, v_t, b_ki, jnp.tril(jnp.ones((DSA_TK, DSA_TK), BF16)))


def _mlstm_kernel(qk_ref, v_ref, og_ref, sm_ref, convw_ref, convb_ref, gbias_ref, normgt_ref,
                  o_ref, tail_ref, ct_ref, nvec_ref, mst_ref, *, bn):
    c = pl.program_id(0)
    L = CHUNK
    nh, dh, width = MLSTM_HEADS, HEAD_DIM, MLSTM_HEADS * HEAD_DIM

    @pl.when(c == 0)
    def _():
        tail_ref[...] = jnp.zeros_like(tail_ref)
        ct_ref[...] = jnp.zeros_like(ct_ref)
        nvec_ref[...] = jnp.zeros_like(nvec_ref)
        mst_ref[...] = jnp.zeros_like(mst_ref)

    srow = lax.broadcasted_iota(I32, (L, width), 0)
    lane = lax.broadcasted_iota(I32, (L, width), 1)
    jlane = lane % dh
    causal_t = srow <= jlane
    diag_t = srow == jlane
    head_of_lane = lax.broadcasted_iota(I32, (1, width), 1) // dh
    tril = (lax.broadcasted_iota(I32, (L, L), 1) <= lax.broadcasted_iota(I32, (L, L), 0)).astype(F32)
    erow = lax.broadcasted_iota(I32, (128, width), 0)
    ecol_head = lax.broadcasted_iota(I32, (128, width), 1) // dh
    expand_ig = (erow == SM_IG + ecol_head).astype(F32)
    expand_fg = (erow == SM_FG + ecol_head).astype(F32)
    exact = dict(preferred_element_type=F32, precision=lax.Precision.HIGHEST)

    def head_blocks(a):
        out = jnp.where(head_of_lane == 0, a[0:dh], 0.0)
        for h in range(1, nh):
            out = out + jnp.where(head_of_lane == h, a[h * dh:(h + 1) * dh], 0.0)
        return out

    convw = convw_ref[...]
    for b in range(bn):
        cur = qk_ref[b]
        ext = jnp.concatenate([tail_ref[b], cur], axis=0)
        tail_ref[b] = cur[L - 8:L, :]
        y = convb_ref[...] + convw[MLSTM_CONV - 1:MLSTM_CONV, :] * cur
        for k in range(MLSTM_CONV - 1):
            off = 8 - (MLSTM_CONV - 1) + k
            y = y + convw[k:k + 1, :] * ext[off:off + L, :]
        qk = y * _sigmoid(y)
        q_all = qk[:, 0:width]
        k_all = qk[:, width:2 * width] * dh ** -0.5
        q_stack = jnp.concatenate([jnp.where(head_of_lane == h, q_all, 0.0) for h in range(nh)], axis=0).astype(BF16)
        v_all = v_ref[b]
        v_t = v_all.astype(F32).T.astype(BF16)

        gates = sm_ref[b] + gbias_ref[...]
        lf = jnp.minimum(gates, 0.0) - jnp.log(1.0 + jnp.exp(-jnp.abs(gates)))
        bcum = jnp.dot(tril, lf, **exact)
        ig_x = jnp.dot(gates, expand_ig, **exact)
        b_x = jnp.dot(bcum, expand_fg, **exact)
        b_q = jnp.sum(jnp.where(diag_t, b_x, 0.0), axis=0, keepdims=True)
        b_last = b_x[L - 1:L, :]
        m_prev = mst_ref[b]
        ct = ct_ref[b]
        nvec = nvec_ref[b]

        dlog = jnp.where(causal_t, b_q - b_x + ig_x, NEG_INF)
        inter = b_q + m_prev
        mj = jnp.maximum(inter, jnp.max(dlog, axis=0, keepdims=True))
        dw = jnp.exp(dlog - mj)
        iw = jnp.exp(inter - mj)
        sc = _dot_nt(k_all.astype(BF16), q_stack) * dw
        qn = _dot_nt(jnp.broadcast_to(nvec, (8, width)).astype(BF16), q_stack)[0:1, :]
        q_c = _dot_nt(ct.astype(BF16), q_stack)
        num = iw * q_c + head_blocks(_dot(v_t, sc.astype(BF16)))
        den = iw * qn + jnp.sum(sc, axis=0, keepdims=True)
        hj = num / jnp.maximum(jnp.abs(den), jnp.exp(-mj))

        dec = b_last - b_x + ig_x
        m_new = jnp.maximum(b_last + m_prev, jnp.max(dec, axis=0, keepdims=True))
        wc = jnp.exp(b_last + m_prev - m_new)
        kw = k_all * jnp.exp(dec - m_new)
        ct_ref[b] = wc * ct + head_blocks(_dot(v_t, kw.astype(BF16)))
        nvec_ref[b] = wc * nvec + jnp.sum(kw, axis=0, keepdims=True)
        mst_ref[b] = m_new

        mu = jnp.mean(hj, axis=0, keepdims=True)
        dev = hj - mu
        var = jnp.mean(dev * dev, axis=0, keepdims=True)
        hn_t = (dev * lax.rsqrt(var + LN_EPS) * normgt_ref[...]).T
        hn = jnp.concatenate([hn_t[h * dh:(h + 1) * dh, :] for h in range(nh)], axis=1)
        o_ref[b] = (_sigmoid(og_ref[b]) * hn).astype(BF16)


def _mlstm(c_qk, c_v, c_o, small, conv_w, conv_b, gate_bias, norm_g, bn, t):
    nc = t // CHUNK
    width = MLSTM_HEADS * HEAD_DIM
    norm_g_t = jnp.repeat(norm_g.reshape(MLSTM_HEADS, HEAD_DIM).T, HEAD_DIM, axis=1)
    blk = lambda w: pl.BlockSpec((bn, CHUNK, w), lambda c: (0, c, 0))
    full = lambda a: pl.BlockSpec(a.shape, lambda c: (0,) * a.ndim)
    return pl.pallas_call(
        functools.partial(_mlstm_kernel, bn=bn),
        grid=(nc,),
        in_specs=[blk(512), blk(256), blk(256), blk(128), full(conv_w), full(conv_b), full(gate_bias), full(norm_g_t)],
        out_specs=blk(256),
        out_shape=jax.ShapeDtypeStruct((bn, t, 256), BF16),
        scratch_shapes=[pltpu.VMEM((bn, 8, 512), F32), pltpu.VMEM((bn, HEAD_DIM, width), F32),
                        pltpu.VMEM((bn, 1, width), F32), pltpu.VMEM((bn, 1, width), F32)],
        compiler_params=_cparams(("arbitrary",)),
        name="mlstm",
    )(c_qk.reshape(bn, t, 512), c_v.reshape(bn, t, 256), c_o.reshape(bn, t, 256), small.reshape(bn, t, 128),
      conv_w, conv_b, gate_bias, norm_g_t)


def _s5_kernel(u_ref, perm_ref, bbre_ref, bbim_ref, pwre_ref, pwim_ref, cre_ref, cim_ref, dskip_ref, gluw_ref,
               glub_ref, o_ref, sre_ref, sim_ref, stre_ref, stim_ref, yp_ref):
    @pl.when(pl.program_id(1) == 0)
    def _():
        stre_ref[...] = jnp.zeros_like(stre_ref)
        stim_ref[...] = jnp.zeros_like(stim_ref)

    seg_len = S5_TT // S5_SEGS
    u = u_ref[...]
    ub = _dot(perm_ref[...], u.astype(BF16)).astype(BF16)
    sre_ref[...] = _dot(ub, bbre_ref[...])
    sim_ref[...] = _dot(ub, bbim_ref[...])
    a_re = jnp.broadcast_to(pwre_ref[0:1, :], (S5_SEGS, S5_LANES))
    a_im = jnp.broadcast_to(pwim_ref[0:1, :], (S5_SEGS, S5_LANES))

    def local_step(i, carry):
        s_re, s_im = carry
        rows = pl.ds(pl.multiple_of(i * S5_SEGS, S5_SEGS), S5_SEGS)
        n_re = a_re * s_re - a_im * s_im + sre_ref[rows, :]
        n_im = a_re * s_im + a_im * s_re + sim_ref[rows, :]
        sre_ref[rows, :] = n_re
        sim_ref[rows, :] = n_im
        return n_re, n_im

    zeros = jnp.zeros((S5_SEGS, S5_LANES), F32)
    e_re, e_im = lax.fori_loop(0, seg_len, local_step, (zeros, zeros), unroll=4)

    al_re, al_im = pwre_ref[seg_len - 1:seg_len, :], pwim_ref[seg_len - 1:seg_len, :]
    c_re, c_im = stre_ref[0:1, :], stim_ref[0:1, :]
    cs_re, cs_im = [], []
    for k in range(S5_SEGS):
        cs_re.append(c_re)
        cs_im.append(c_im)
        c_re, c_im = (e_re[k:k + 1, :] + al_re * c_re - al_im * c_im,
                      e_im[k:k + 1, :] + al_re * c_im + al_im * c_re)
    stre_ref[...] = jnp.broadcast_to(c_re, stre_ref.shape)
    stim_ref[...] = jnp.broadcast_to(c_im, stim_ref.shape)
    cin_re = jnp.concatenate(cs_re, axis=0)
    cin_im = jnp.concatenate(cs_im, axis=0)

    def correct_step(i, carry):
        rows = pl.ds(pl.multiple_of(i * S5_SEGS, S5_SEGS), S5_SEGS)
        p_re, p_im = pwre_ref[pl.ds(i, 1), :], pwim_ref[pl.ds(i, 1), :]
        sre_ref[rows, :] = sre_ref[rows, :] + p_re * cin_re - p_im * cin_im
        sim_ref[rows, :] = sim_ref[rows, :] + p_re * cin_im + p_im * cin_re
        return carry

    lax.fori_loop(0, seg_len, correct_step, 0, unroll=4)
    ycs = _dot(sre_ref[...].astype(BF16), cre_ref[...]) - _dot(sim_ref[...].astype(BF16), cim_ref[...])
    ngrp = GROUP_WIDTH // 128
    for g in range(ngrp):
        yp_ref[g] = ycs[:, g * 128:(g + 1) * 128]
    blocks = []
    for k in range(S5_SEGS):
        for i0 in range(0, seg_len, 8):
            rows = pl.ds(i0 * S5_SEGS + k, 8, stride=S5_SEGS)
            blocks.append(jnp.concatenate([yp_ref[g, rows, :] for g in range(ngrp)], axis=1))
    y = jnp.concatenate(blocks, axis=0) + dskip_ref[...] * u
    y = 0.5 * y * (1.0 + jnp.tanh(math.sqrt(2.0 / math.pi) * (y + 0.044715 * (y * y * y))))
    z = _dot(y.astype(BF16), gluw_ref[...]) + glub_ref[...]
    o_ref[...] = (y * _sigmoid(z)).astype(BF16)


def _s5(d_u, bb_re, bb_im, pw_re, pw_im, c_re_t, c_im_t, d_skip, glu_w, glu_b, bn, t):
    nt = t // S5_TT
    full = lambda a: pl.BlockSpec(a.shape, lambda b, i: (0,) * a.ndim)
    r = jnp.arange(S5_TT)
    perm = (r[None, :] == ((r % S5_SEGS) * (S5_TT // S5_SEGS) + r // S5_SEGS)[:, None]).astype(BF16)
    args = (perm, bb_re, bb_im, pw_re, pw_im, c_re_t, c_im_t, d_skip, glu_w, glu_b)
    return pl.pallas_call(
        _s5_kernel,
        grid=(bn, nt),
        in_specs=[pl.BlockSpec((S5_TT, 256), lambda b, i: (b * nt + i, 0))] + [full(a) for a in args],
        out_specs=pl.BlockSpec((S5_TT, 256), lambda b, i: (b * nt + i, 0)),
        out_shape=jax.ShapeDtypeStruct((bn * t, 256), BF16),
        scratch_shapes=[pltpu.VMEM((S5_TT, S5_LANES), F32), pltpu.VMEM((S5_TT, S5_LANES), F32),
                        pltpu.VMEM((8, S5_LANES), F32), pltpu.VMEM((8, S5_LANES), F32),
                        pltpu.VMEM((GROUP_WIDTH // 128, S5_TT, 128), F32)],
        compiler_params=_cparams(("arbitrary", "arbitrary")),
        name="s5_glu",
    )(d_u, *args)


def _outproj_kernel(x_ref, mod_ref, oa_ref, ob_ref, oc_ref, od_ref, w_ref, lng_ref, lnb_ref, o_ref):
    x = x_ref[...]
    gate = mod_ref[2:3, :]
    y = _dot(oa_ref[...], w_ref[0])
    y = y + _dot(ob_ref[...], w_ref[1])
    y = y + _dot(oc_ref[...], w_ref[2])
    y = y + _dot(od_ref[...], w_ref[3])
    o_ref[...] = _residual_layer_norm(x, gate * y, lng_ref[...], lnb_ref[...])


def _outproj(x2, mod, o_a, o_b, o_c, o_d, w_out4, ln_g, ln_b, rows_per_batch):
    n = x2.shape[0]
    tm = ROW_TILE
    tiles_per_batch = rows_per_batch // tm
    mix = pl.BlockSpec((tm, GROUP_WIDTH), lambda i: (i, 0))
    return pl.pallas_call(
        _outproj_kernel,
        grid=(n // tm,),
        in_specs=[
            pl.BlockSpec((tm, D_MODEL), lambda i: (i, 0)),
            pl.BlockSpec((None, 8, D_MODEL), lambda i: (i // tiles_per_batch, 0, 0)),
            mix, mix, mix, mix,
            pl.BlockSpec(w_out4.shape, lambda i: (0, 0, 0), pipeline_mode=pl.Buffered(1)),
            pl.BlockSpec((1, D_MODEL), lambda i: (0, 0)),
            pl.BlockSpec((1, D_MODEL), lambda i: (0, 0)),
        ],
        out_specs=pl.BlockSpec((tm, D_MODEL), lambda i: (i, 0)),
        out_shape=jax.ShapeDtypeStruct((n, D_MODEL), F32),
        compiler_params=_cparams(("arbitrary",)),
        name="mixer_outproj",
    )(x2, mod, o_a, o_b, o_c, o_d, w_out4, ln_g.reshape(1, -1), ln_b.reshape(1, -1))


def _permute_w_in(w_in):
    off = {}
    o = 0
    for name, s in (("qa", 256), ("ka", 128), ("va", 128), ("cq", 128), ("kb", 64), ("vb", 64), ("ki", 32),
                    ("wi", 4), ("qkc", 512), ("vc", 256), ("ig", 4), ("fg", 4), ("oc", 256), ("ud", 256)):
        off[name] = (o, o + s)
        o += s
    col = lambda n: w_in[:, off[n][0]:off[n][1]]
    small = jnp.concatenate([col("wi"), col("ig"), col("fg"),
                             jnp.zeros((w_in.shape[0], 128 - 12), w_in.dtype)], axis=1)
    parts = [col("qa"), col("ka"), col("va"), col("cq"), col("kb"), col("vb")] + [col("ki")] * 4 + [
        col("qkc"), col("vc"), col("oc"), col("ud"), small]
    return jnp.concatenate(parts, axis=1).astype(BF16)


def _s5_params(lam_re, lam_im, log_step, b_re, b_im, c_re, c_im):
    dt = jnp.exp(log_step)[:, None]
    mag = jnp.exp(lam_re * dt)
    a_re, a_im = mag * jnp.cos(lam_im * dt), mag * jnp.sin(lam_im * dt)
    den = lam_re * lam_re + lam_im * lam_im
    kap_re = ((a_re - 1.0) * lam_re + a_im * lam_im) / den
    kap_im = (a_im * lam_re - (a_re - 1.0) * lam_im) / den
    bb_re = kap_re[..., None] * b_re - kap_im[..., None] * b_im
    bb_im = kap_re[..., None] * b_im + kap_im[..., None] * b_re
    eye = jnp.eye(S5_GROUPS, dtype=F32)

    def in_mat(bb):
        return jnp.einsum("gph,gk->ghkp", bb, eye).reshape(S5_GROUPS * S5_GROUP_CH, S5_LANES).astype(BF16)

    def out_mat(cc):
        return jnp.einsum("gop,gk->gpko", cc, eye).reshape(S5_LANES, S5_GROUPS * S5_GROUP_CH).astype(BF16)

    n = jnp.arange(1, S5_TT // S5_SEGS + 1, dtype=F32)[:, None, None]
    pw_mag = jnp.exp(n * (lam_re * dt))
    pw_re = (pw_mag * jnp.cos(n * (lam_im * dt))).at[0].set(a_re).reshape(-1, S5_LANES)
    pw_im = (pw_mag * jnp.sin(n * (lam_im * dt))).at[0].set(a_im).reshape(-1, S5_LANES)
    return in_mat(bb_re), in_mat(bb_im), pw_re, pw_im, out_mat(c_re), out_mat(c_im)


def kernel(x, c, ada_w, ada_b, ln_g, ln_b, ffn_w13, ffn_w2, w_in, w_out, sinks, w_uq, w_iq, conv_w, conv_b, ig_b,
           fg_b, mh_norm_g, lam_re, lam_im, log_step, b_re, b_im, c_re, c_im, d_skip, glu_w, glu_b):
    bn, t, d = x.shape
    assert d == D_MODEL and t % max(ROW_TILE, DSA_TK, S5_TT) == 0 and bn <= 8
    n = bn * t
    nl = ada_w.shape[0]
    c_pad = jnp.zeros((8, d), F32).at[:bn].set(c)
    mod_all = _ada_mod(c_pad, ada_w, ada_b)
    mod_all = mod_all[:, :bn].reshape(nl, bn, N_SUB, 3, d).transpose(0, 2, 1, 3, 4)
    mod_all = jnp.pad(mod_all, ((0, 0), (0, 0), (0, 0), (0, 5), (0, 0)))

    x2 = x.reshape(n, d)
    for l in range(nl):
        w13c = ffn_w13[l].astype(BF16).reshape(2, d, 2, FFN_NCHUNK, FFN_TF).transpose(0, 2, 3, 1, 4)
        w2c = ffn_w2[l].astype(BF16)
        x2 = _ffn_sublayer(x2, mod_all[l, 0], w13c[0], w2c[0], ln_g[l, 0], ln_b[l, 0], t)
        (a_q, a_kv, b_q, b_qi, b_kv, b_ki, c_qk, c_v, c_o, d_u, small) = _inproj(
            x2, mod_all[l, 1], _permute_w_in(w_in[l]), w_uq[l].astype(BF16), w_iq[l].astype(BF16), t)
        o_a = _swa(sinks[l], a_q, a_kv, bn, t)
        o_b = _dsa(b_q, b_qi, small, b_kv, b_ki, bn, t)
        gate_bias = jnp.zeros((1, 128), F32).at[0, SM_IG:SM_IG + 4].set(ig_b[l]).at[0, SM_FG:SM_FG + 4].set(fg_b[l])
        o_c = _mlstm(c_qk, c_v, c_o, small, conv_w[l], conv_b[l].reshape(1, -1), gate_bias,
                     mh_norm_g[l].reshape(1, -1), bn, t).reshape(n, GROUP_WIDTH)
        s5p = _s5_params(lam_re[l], lam_im[l], log_step[l], b_re[l], b_im[l], c_re[l], c_im[l])
        o_d = _s5(d_u, *s5p, d_skip[l].reshape(1, -1), glu_w[l].astype(BF16), glu_b[l].reshape(1, -1), bn, t)
        x2 = _outproj(x2, mod_all[l, 1], o_a, o_b, o_c, o_d,
                      w_out[l].astype(BF16).reshape(4, GROUP_WIDTH, d), ln_g[l, 1], ln_b[l, 1], t)
        x2 = _ffn_sublayer(x2, mod_all[l, 2], w13c[1], w2c[1], ln_g[l, 2], ln_b[l, 2], t)
    return x2.reshape(bn, t, d)
```

```python
import functools
import math

import jax
import jax.numpy as jnp
from jax import lax
from jax.experimental import pallas as pl
from jax.experimental.pallas import tpu as pltpu

F32 = jnp.float32
BF16 = jnp.bfloat16
I32 = jnp.int32

D_MODEL = 1024
DEPTH = 2
CHUNK = 64
HEAD_DIM = 64
GROUP_WIDTH = 256
SWA_HEADS = 4
SWA_KV_HEADS = 2
SWA_WIN_CHUNKS = 2
DSA_HEADS = 4
DSA_Q_RANK = 128
IDX_HEADS = 4
IDX_DIM = 32
DSA_TOPK = 256
MLSTM_HEADS = 4
MLSTM_CONV = 4
S5_GROUP_CH = 16
S5_GROUPS = 16
S5_STATE = 64
S5_LANES = S5_GROUPS * S5_STATE
D_FF = 2816
N_SUB = 3
ALPHA = (2 * DEPTH) ** 0.25
LN_EPS = 1e-5
NEG_INF = -1e30
INT_MIN = -(2 ** 31)
FIELD_BITS = 15
FIELD_GUARD = -(2 ** 31) + 2 ** 15

SLOPES_A = tuple(2.0 ** -(i + 1) for i in range(0, 8, 2))
SLOPES_B = tuple(2.0 ** -(i + 1) for i in range(1, 8, 2))

VMEM_LIMIT_BYTES = 56 * 1024 * 1024

FFN_TF = 256
FFN_NCHUNK = D_FF // FFN_TF
ROW_TILE = 512
SWA_TQ = 128
DSA_TQ = 128
DSA_TK = 512
S5_TT = 512
S5_SEGS = 8

ZC_QA, ZC_KVA, ZC_CQ, ZC_KVB, ZC_KI = 0, 256, 512, 640, 768
ZC_QKC, ZC_VC, ZC_OC, ZC_UD, ZC_SMALL, Z_WIDTH = 896, 1408, 1664, 1920, 2176, 2304
SM_WI, SM_IG, SM_FG = 0, 4, 8


def _cparams(sem):
    return pltpu.CompilerParams(dimension_semantics=sem, vmem_limit_bytes=VMEM_LIMIT_BYTES)


def _dot(a, b):
    return jnp.dot(a, b, preferred_element_type=F32)


def _dot_nt(a, b):
    return lax.dot_general(a, b, (((1,), (1,)), ((), ())), preferred_element_type=F32)


def _sigmoid(x):
    return 1.0 / (1.0 + jnp.exp(-x))


def _residual_layer_norm(x, y, g, b):
    v = ALPHA * x + y
    mu = jnp.mean(v, axis=-1, keepdims=True)
    d = v - mu
    var = jnp.mean(d * d, axis=-1, keepdims=True)
    return d * lax.rsqrt(var + LN_EPS) * g + b


def _ada_kernel(c_ref, w_ref, b_ref, o_ref):
    c = c_ref[...]
    cs = c * _sigmoid(c)
    o_ref[...] = jnp.dot(cs, w_ref[...], preferred_element_type=F32,
                         precision=lax.Precision.HIGHEST) + b_ref[...]


def _ada_mod(c_pad, ada_w, ada_b):
    nl = ada_w.shape[0]
    ncol = ada_w.shape[2] // D_MODEL
    return pl.pallas_call(
        _ada_kernel,
        grid=(nl, ncol),
        in_specs=[
            pl.BlockSpec((8, D_MODEL), lambda l, j: (0, 0)),
            pl.BlockSpec((None, D_MODEL, D_MODEL), lambda l, j: (l, 0, j)),
            pl.BlockSpec((None, 1, D_MODEL), lambda l, j: (l, 0, j)),
        ],
        out_specs=pl.BlockSpec((None, 8, D_MODEL), lambda l, j: (l, 0, j)),
        out_shape=jax.ShapeDtypeStruct((nl, 8, ada_w.shape[2]), F32),
        compiler_params=_cparams(("arbitrary", "arbitrary")),
        name="ada_mod",
    )(c_pad, ada_w, ada_b.reshape(nl, 1, -1))


def _ffn_tile(x, mod_ref, w13_ref, w2_ref, lng_ref, lnb_ref, h_ref):
    shift, scale, gate = mod_ref[0:1, :], mod_ref[1:2, :], mod_ref[2:3, :]
    u = (x * (1.0 + scale) + shift).astype(BF16)
    for j in range(FFN_NCHUNK):
        a = _dot(u, w13_ref[0, j])
        g = _dot(u, w13_ref[1, j])
        h_ref[:, j * FFN_TF:(j + 1) * FFN_TF] = (a * _sigmoid(a) * g).astype(BF16)
    y = _dot(h_ref[...], w2_ref[...])
    return _residual_layer_norm(x, 0.5 * gate * y, lng_ref[...], lnb_ref[...])


def _ffn_kernel(x_ref, mod_ref, w13_ref, w2_ref, lng_ref, lnb_ref, o_ref, h_ref):
    o_ref[...] = _ffn_tile(x_ref[...], mod_ref, w13_ref, w2_ref, lng_ref, lnb_ref, h_ref)


def _ffn_sublayer(x2, mod, w13c, w2c, ln_g, ln_b, rows_per_batch):
    n = x2.shape[0]
    tm = ROW_TILE
    tiles_per_batch = rows_per_batch // tm
    return pl.pallas_call(
        _ffn_kernel,
        grid=(n // tm,),
        in_specs=[
            pl.BlockSpec((tm, D_MODEL), lambda i: (i, 0)),
            pl.BlockSpec((None, 8, D_MODEL), lambda i: (i // tiles_per_batch, 0, 0)),
            pl.BlockSpec(w13c.shape, lambda i: (0, 0, 0, 0), pipeline_mode=pl.Buffered(1)),
            pl.BlockSpec(w2c.shape, lambda i: (0, 0), pipeline_mode=pl.Buffered(1)),
            pl.BlockSpec((1, D_MODEL), lambda i: (0, 0)),
            pl.BlockSpec((1, D_MODEL), lambda i: (0, 0)),
        ],
        out_specs=pl.BlockSpec((tm, D_MODEL), lambda i: (i, 0)),
        out_shape=jax.ShapeDtypeStruct((n, D_MODEL), F32),
        scratch_shapes=[pltpu.VMEM((tm, D_FF), BF16)],
        compiler_params=_cparams(("arbitrary",)),
        name="ffn_sublayer",
    )(x2, mod, w13c, w2c, ln_g.reshape(1, -1), ln_b.reshape(1, -1))


def _inproj_kernel(x_ref, mod_ref, w_ref, wuq_ref, wiq_ref,
                   aq_ref, akv_ref, bq_ref, bqi_ref, bkv_ref, bki_ref,
                   cqk_ref, cv_ref, co_ref, du_ref, sm_ref):
    x = x_ref[...]
    shift, scale = mod_ref[0:1, :], mod_ref[1:2, :]
    u = (x * (1.0 + scale) + shift).astype(BF16)
    z = _dot(u, w_ref[...])
    aq_ref[...] = z[:, ZC_QA:ZC_KVA].astype(BF16)
    akv_ref[...] = z[:, ZC_KVA:ZC_CQ].astype(BF16)
    cq = z[:, ZC_CQ:ZC_KVB].astype(BF16)
    bq_ref[...] = (_dot(cq, wuq_ref[...]) * HEAD_DIM ** -0.5).astype(BF16)
    bqi_ref[...] = _dot(cq, wiq_ref[...]).astype(BF16)
    bkv_ref[...] = z[:, ZC_KVB:ZC_KI].astype(BF16)
    bki_ref[...] = z[:, ZC_KI:ZC_QKC].astype(BF16)
    cqk_ref[...] = z[:, ZC_QKC:ZC_VC]
    cv_ref[...] = z[:, ZC_VC:ZC_OC].astype(BF16)
    co_ref[...] = z[:, ZC_OC:ZC_UD]
    du_ref[...] = z[:, ZC_UD:ZC_SMALL]
    sm_ref[...] = z[:, ZC_SMALL:Z_WIDTH]


def _inproj(x2, mod, w_perm, w_uq, w_iq, rows_per_batch):
    n = x2.shape[0]
    tm = ROW_TILE
    tiles_per_batch = rows_per_batch // tm
    widths = [(256, BF16), (256, BF16), (256, BF16), (128, BF16), (128, BF16), (128, BF16),
              (512, F32), (256, BF16), (256, F32), (256, F32), (128, F32)]
    return pl.pallas_call(
        _inproj_kernel,
        grid=(n // tm,),
        in_specs=[
            pl.BlockSpec((tm, D_MODEL), lambda i: (i, 0)),
            pl.BlockSpec((None, 8, D_MODEL), lambda i: (i // tiles_per_batch, 0, 0)),
            pl.BlockSpec(w_perm.shape, lambda i: (0, 0), pipeline_mode=pl.Buffered(1)),
            pl.BlockSpec(w_uq.shape, lambda i: (0, 0), pipeline_mode=pl.Buffered(1)),
            pl.BlockSpec(w_iq.shape, lambda i: (0, 0), pipeline_mode=pl.Buffered(1)),
        ],
        out_specs=[pl.BlockSpec((tm, w), lambda i: (i, 0)) for w, _ in widths],
        out_shape=[jax.ShapeDtypeStruct((n, w), dt) for w, dt in widths],
        compiler_params=_cparams(("arbitrary",)),
        name="mixer_inproj",
    )(x2, mod, w_perm, w_uq, w_iq)


def _swa_kernel(sink_ref, q_ref, kvc_ref, kvp_ref, o_ref):
    i = pl.program_id(1)
    tq = SWA_TQ
    q = q_ref[...]
    kv = jnp.concatenate([kvp_ref[...], kvc_ref[...]], axis=0)
    qpos = i * tq + lax.broadcasted_iota(I32, (tq, 2 * tq), 0)
    kpos = (i - 1) * tq + lax.broadcasted_iota(I32, (tq, 2 * tq), 1)
    qchunk = qpos // CHUNK
    kchunk = (kpos + tq) // CHUNK - tq // CHUNK
    valid = (kpos >= 0) & (kchunk <= qchunk) & (kchunk >= qchunk - SWA_WIN_CHUNKS)
    dist = jnp.abs(qpos - kpos).astype(F32)
    rep = SWA_HEADS // SWA_KV_HEADS
    for h in range(SWA_HEADS):
        g = h // rep
        qh = q[:, h * HEAD_DIM:(h + 1) * HEAD_DIM]
        kg = kv[:, g * HEAD_DIM:(g + 1) * HEAD_DIM]
        vg = kv[:, (SWA_KV_HEADS + g) * HEAD_DIM:(SWA_KV_HEADS + g + 1) * HEAD_DIM]
        s = _dot_nt(qh, kg) * HEAD_DIM ** -0.5 - SLOPES_A[h] * dist
        s = jnp.where(valid, s, NEG_INF)
        sink = sink_ref[h]
        m = jnp.maximum(jnp.max(s, axis=-1, keepdims=True), sink)
        p = jnp.exp(s - m)
        denom = jnp.sum(p, axis=-1, keepdims=True) + jnp.exp(sink - m)
        o = _dot(p.astype(BF16), vg) / denom
        o_ref[:, h * HEAD_DIM:(h + 1) * HEAD_DIM] = o.astype(BF16)


def _swa(sinks, a_q, a_kv, bn, t):
    nt = t // SWA_TQ
    return pl.pallas_call(
        _swa_kernel,
        grid=(bn, nt),
        in_specs=[
            pl.BlockSpec(memory_space=pltpu.SMEM),
            pl.BlockSpec((SWA_TQ, 256), lambda b, i: (b * nt + i, 0)),
            pl.BlockSpec((SWA_TQ, 256), lambda b, i: (b * nt + i, 0)),
            pl.BlockSpec((SWA_TQ, 256), lambda b, i: (b * nt + jnp.maximum(i - 1, 0), 0)),
        ],
        out_specs=pl.BlockSpec((SWA_TQ, 256), lambda b, i: (b * nt + i, 0)),
        out_shape=jax.ShapeDtypeStruct((bn * t, 256), BF16),
        compiler_params=_cparams(("arbitrary", "arbitrary")),
        name="swa_attention",
    )(sinks, a_q, a_kv, a_kv)


def _sortable_key(x):
    bits = lax.bitcast_convert_type(x, I32)
    return bits ^ ((bits >> 31) & 0x7FFFFFFF)


def _dsa_kernel(q_ref, qi_ref, sm_ref, kv_ref, vt_ref, ki_ref, tril_ref, o_ref, key_ref, s_ref, p_ref, w1_ref, w2_ref):
    i = pl.program_id(1)
    tq, tk = DSA_TQ, DSA_TK
    nblk = (i * tq + tq + tk - 1) // tk
    qpos = i * tq + lax.broadcasted_iota(I32, (1, tq), 1)
    qchunk = qpos // CHUNK
    row_k = lax.broadcasted_iota(I32, (tk, tq), 0)

    qi = qi_ref[...]
    lane_i = lax.broadcasted_iota(I32, (1, IDX_HEADS * IDX_DIM), 1) // IDX_DIM
    qi_stack = jnp.concatenate([jnp.where(lane_i == h, qi, jnp.zeros_like(qi)) for h in range(IDX_HEADS)], axis=0)
    sm_t = sm_ref[...].T
    w_idx = [sm_t[SM_WI + h:SM_WI + h + 1, :] for h in range(IDX_HEADS)]
    idx_scale = (IDX_DIM * IDX_HEADS) ** -0.5
    q = q_ref[...]
    q_stack = jnp.concatenate([q[:, h * HEAD_DIM:(h + 1) * HEAD_DIM] for h in range(DSA_HEADS)], axis=0)

    half = tk // 2
    guard = jnp.int32(FIELD_GUARD)

    def pack_fields(f):
        return (f[0:half] << 16) | f[half:tk] | guard

    def score_block(j, mask_inadmissible):
        rows = pl.ds(pl.multiple_of(j * tk, tk), tk)
        s_ref[j] = _dot_nt(kv_ref[rows, 0:HEAD_DIM], q_stack)
        d = _dot_nt(ki_ref[rows, :], qi_stack)
        acc = w_idx[0] * jnp.maximum(d[:, 0:tq], 0.0)
        for h in range(1, IDX_HEADS):
            acc = acc + w_idx[h] * jnp.maximum(d[:, h * tq:(h + 1) * tq], 0.0)
        sc = acc * idx_scale
        if mask_inadmissible:
            sc = jnp.where((j * tk + row_k) // CHUNK <= qchunk, sc, NEG_INF)
        key = _sortable_key(sc)
        key_ref[j] = key
        w1_ref[j] = pack_fields(lax.shift_right_logical(key ^ INT_MIN, 32 - FIELD_BITS))

    def full_block(j, carry):
        score_block(j, False)
        return carry

    lax.fori_loop(0, nblk - 1, full_block, 0)
    score_block(nblk - 1, True)

    k_eff = jnp.minimum(DSA_TOPK, (qchunk + 1) * CHUNK)

    def over_blocks(body, init):
        c = lax.fori_loop(0, nblk // 2, lambda jj, c: body(2 * jj + 1, body(2 * jj, c)), init)
        return lax.cond(nblk % 2 == 1, lambda c: body(nblk - 1, c), lambda c: c, c)

    def count(pred):
        def blk(j, c):
            m = pred(key_ref[j], j * tk + row_k).astype(I32)
            return c + jnp.sum(m.reshape(tk // 8, 8, tq), axis=0)
        return jnp.sum(over_blocks(blk, jnp.zeros((8, tq), I32)), axis=0, keepdims=True)

    def count_fields(w_ref, cand):
        cand2 = (cand << 16) | cand
        def blk(j, c):
            hit = ((w_ref[j] - cand2) >> 15) & 0x00010001
            return c + jnp.sum(hit.reshape(half // 8, 8, tq), axis=0)
        c = over_blocks(blk, jnp.zeros((8, tq), I32))
        return jnp.sum((c & 0xFFFF) + (c >> 16), axis=0, keepdims=True)

    def field_search(w_ref, k_want):
        def step(bi, carry):
            prefix, above = carry
            cand = prefix | (jnp.int32(1) << (FIELD_BITS - 1 - bi))
            cnt = count_fields(w_ref, cand)
            ok = cnt >= k_want
            return jnp.where(ok, cand, prefix), jnp.where(ok, above, cnt)
        return lax.fori_loop(0, FIELD_BITS, step, (jnp.zeros((1, tq), I32), jnp.zeros((1, tq), I32)))

    top, above = field_search(w1_ref, k_eff)
    field_max = (1 << FIELD_BITS) - 1

    def pack_mid(j, carry):
        ukey = key_ref[j] ^ INT_MIN
        member = lax.shift_right_logical(ukey, 32 - FIELD_BITS) == top
        mid = lax.shift_right_logical(ukey, 32 - 2 * FIELD_BITS) & field_max
        w2_ref[j] = pack_fields(jnp.where(member, mid, 0))
        return carry

    lax.fori_loop(0, nblk, pack_mid, 0)
    mid, above_mid = field_search(w2_ref, k_eff - above)

    def bit_step(bi, carry):
        prefix, c_gt = carry
        cand_u = prefix | (jnp.int32(1) << (31 - 2 * FIELD_BITS - bi))
        cand_s = cand_u ^ INT_MIN
        cnt = count(lambda key, kpos: key >= cand_s)
        ok = cnt >= k_eff
        return jnp.where(ok, cand_u, prefix), jnp.where(ok, c_gt, cnt)

    prefix, c_gt = lax.fori_loop(0, 32 - 2 * FIELD_BITS, bit_step,
                                 ((top << (32 - FIELD_BITS)) | (mid << (32 - 2 * FIELD_BITS)), above + above_mid))
    thr = prefix ^ INT_MIN

    need = (k_eff - c_gt).astype(F32)
    tril = tril_ref[...]

    offs = (row_k - qpos).astype(F32)
    p_ref[1] = jnp.zeros(p_ref.shape[1:], BF16)

    def att_block(j, carry):
        m_run, l_run, acc_part, ties_seen = carry
        slot = j % 2
        acc = acc_part + _dot(vt_ref[jnp.maximum(j - 1, 0)], p_ref[1 - slot])
        key = key_ref[j]
        tie = key == thr
        tie_rank = _dot(tril, jnp.where(tie, 1.0, 0.0).astype(BF16)) + ties_seen
        sel = (key > thr) | (tie & (tie_rank <= need))
        dist = jnp.where(sel, jnp.abs(offs + (j * tk).astype(F32)), jnp.inf)
        ms, ls, alphas = [], [], []
        for h in range(DSA_HEADS):
            cols = slice(h * tq, (h + 1) * tq)
            sh = s_ref[j, :, cols] - SLOPES_B[h] * dist
            m_old = m_run[:, cols]
            m_new = jnp.maximum(m_old, jnp.max(sh, axis=0, keepdims=True))
            alpha = jnp.exp(m_old - m_new)
            p = jnp.exp(sh - m_new)
            p_ref[slot, :, cols] = p.astype(BF16)
            ms.append(m_new)
            ls.append(alpha * l_run[:, cols] + jnp.sum(p, axis=0, keepdims=True))
            alphas.append(alpha)
        return (jnp.concatenate(ms, axis=1), jnp.concatenate(ls, axis=1), jnp.concatenate(alphas, axis=1) * acc,
                tie_rank[tk - 1:tk, :])

    init = (jnp.full((1, DSA_HEADS * tq), NEG_INF, F32), jnp.zeros((1, DSA_HEADS * tq), F32),
            jnp.zeros((HEAD_DIM, DSA_HEADS * tq), F32), jnp.zeros((1, tq), F32))
    _, l_run, acc_part, _ = lax.fori_loop(0, nblk, att_block, init)
    acc = acc_part + _dot(vt_ref[nblk - 1], p_ref[(nblk - 1) % 2])
    out = acc / l_run
    for h in range(DSA_HEADS):
        o_ref[:, h * HEAD_DIM:(h + 1) * HEAD_DIM] = out[:, h * tq:(h + 1) * tq].T.astype(BF16)


def _dsa(b_q, b_qi, small, b_kv, b_ki, bn, t):
    nt = t // DSA_TQ
    nkb = t // DSA_TK
    v_t = b_kv[:, HEAD_DIM:].reshape(bn * nkb, DSA_TK, HEAD_DIM).transpose(0, 2, 1)
    return pl.pallas_call(
        _dsa_kernel,
        grid=(bn, nt),
        in_specs=[
            pl.BlockSpec((DSA_TQ, 256), lambda b, i: (b * nt + i, 0)),
            pl.BlockSpec((DSA_TQ, 128), lambda b, i: (b * nt + i, 0)),
            pl.BlockSpec((DSA_TQ, 128), lambda b, i: (b * nt + i, 0)),
            pl.BlockSpec((t, 128), lambda b, i: (b, 0)),
            pl.BlockSpec((nkb, HEAD_DIM, DSA_TK), lambda b, i: (b, 0, 0)),
            pl.BlockSpec((t, 128), lambda b, i: (b, 0)),
            pl.BlockSpec((DSA_TK, DSA_TK), lambda b, i: (0, 0)),
        ],
        out_specs=pl.BlockSpec((DSA_TQ, 256), lambda b, i: (b * nt + i, 0)),
        out_shape=jax.ShapeDtypeStruct((bn * t, 256), BF16),
        scratch_shapes=[pltpu.VMEM((nkb, DSA_TK, DSA_TQ), I32),
                        pltpu.VMEM((nkb, DSA_TK, DSA_HEADS * DSA_TQ), F32),
                        pltpu.VMEM((2, DSA_TK, DSA_HEADS * DSA_TQ), BF16),
                        pltpu.VMEM((nkb, DSA_TK // 2, DSA_TQ), I32), pltpu.VMEM((nkb, DSA_TK // 2, DSA_TQ), I32)],
        compiler_params=_cparams(("arbitrary", "arbitrary")),
        name="dsa_attention",
    )(b_q, b_qi, small, b_kv, v_t, b_ki, jnp.tril(jnp.ones((DSA_TK, DSA_TK), BF16)))


def _mlstm_kernel(qk_ref, v_ref, og_ref, sm_ref, convw_ref, convb_ref, gbias_ref, normgt_ref,
                  o_ref, tail_ref, ct_ref, nvec_ref, mst_ref, *, bn):
    c = pl.program_id(0)
    L = CHUNK
    nh, dh, width = MLSTM_HEADS, HEAD_DIM, MLSTM_HEADS * HEAD_DIM

    @pl.when(c == 0)
    def _():
        tail_ref[...] = jnp.zeros_like(tail_ref)
        ct_ref[...] = jnp.zeros_like(ct_ref)
        nvec_ref[...] = jnp.zeros_like(nvec_ref)
        mst_ref[...] = jnp.zeros_like(mst_ref)

    srow = lax.broadcasted_iota(I32, (L, width), 0)
    lane = lax.broadcasted_iota(I32, (L, width), 1)
    jlane = lane % dh
    causal_t = srow <= jlane
    diag_t = srow == jlane
    head_of_lane = lax.broadcasted_iota(I32, (1, width), 1) // dh
    tril = (lax.broadcasted_iota(I32, (L, L), 1) <= lax.broadcasted_iota(I32, (L, L), 0)).astype(F32)
    erow = lax.broadcasted_iota(I32, (128, width), 0)
    ecol_head = lax.broadcasted_iota(I32, (128, width), 1) // dh
    expand_ig = (erow == SM_IG + ecol_head).astype(F32)
    expand_fg = (erow == SM_FG + ecol_head).astype(F32)
    exact = dict(preferred_element_type=F32, precision=lax.Precision.HIGHEST)

    def head_blocks(a):
        out = jnp.where(head_of_lane == 0, a[0:dh], 0.0)
        for h in range(1, nh):
            out = out + jnp.where(head_of_lane == h, a[h * dh:(h + 1) * dh], 0.0)
        return out

    convw = convw_ref[...]
    for b in range(bn):
        cur = qk_ref[b]
        ext = jnp.concatenate([tail_ref[b], cur], axis=0)
        tail_ref[b] = cur[L - 8:L, :]
        y = convb_ref[...] + convw[MLSTM_CONV - 1:MLSTM_CONV, :] * cur
        for k in range(MLSTM_CONV - 1):
            off = 8 - (MLSTM_CONV - 1) + k
            y = y + convw[k:k + 1, :] * ext[off:off + L, :]
        qk = y * _sigmoid(y)
        q_all = qk[:, 0:width]
        k_all = qk[:, width:2 * width] * dh ** -0.5
        q_stack = jnp.concatenate([jnp.where(head_of_lane == h, q_all, 0.0) for h in range(nh)], axis=0).astype(BF16)
        v_all = v_ref[b]
        v_t = v_all.astype(F32).T.astype(BF16)

        gates = sm_ref[b] + gbias_ref[...]
        lf = jnp.minimum(gates, 0.0) - jnp.log(1.0 + jnp.exp(-jnp.abs(gates)))
        bcum = jnp.dot(tril, lf, **exact)
        ig_x = jnp.dot(gates, expand_ig, **exact)
        b_x = jnp.dot(bcum, expand_fg, **exact)
        b_q = jnp.sum(jnp.where(diag_t, b_x, 0.0), axis=0, keepdims=True)
        b_last = b_x[L - 1:L, :]
        m_prev = mst_ref[b]
        ct = ct_ref[b]
        nvec = nvec_ref[b]

        dlog = jnp.where(causal_t, b_q - b_x + ig_x, NEG_INF)
        inter = b_q + m_prev
        mj = jnp.maximum(inter, jnp.max(dlog, axis=0, keepdims=True))
        dw = jnp.exp(dlog - mj)
        iw = jnp.exp(inter - mj)
        sc = _dot_nt(k_all.astype(BF16), q_stack) * dw
        qn = _dot_nt(jnp.broadcast_to(nvec, (8, width)).astype(BF16), q_stack)[0:1, :]
        q_c = _dot_nt(ct.astype(BF16), q_stack)
        num = iw * q_c + head_blocks(_dot(v_t, sc.astype(BF16)))
        den = iw * qn + jnp.sum(sc, axis=0, keepdims=True)
        hj = num / jnp.maximum(jnp.abs(den), jnp.exp(-mj))

        dec = b_last - b_x + ig_x
        m_new = jnp.maximum(b_last + m_prev, jnp.max(dec, axis=0, keepdims=True))
        wc = jnp.exp(b_last + m_prev - m_new)
        kw = k_all * jnp.exp(dec - m_new)
        ct_ref[b] = wc * ct + head_blocks(_dot(v_t, kw.astype(BF16)))
        nvec_ref[b] = wc * nvec + jnp.sum(kw, axis=0, keepdims=True)
        mst_ref[b] = m_new

        mu = jnp.mean(hj, axis=0, keepdims=True)
        dev = hj - mu
        var = jnp.mean(dev * dev, axis=0, keepdims=True)
        hn_t = (dev * lax.rsqrt(var + LN_EPS) * normgt_ref[...]).T
        hn = jnp.concatenate([hn_t[h * dh:(h + 1) * dh, :] for h in range(nh)], axis=1)
        o_ref[b] = (_sigmoid(og_ref[b]) * hn).astype(BF16)


def _mlstm(c_qk, c_v, c_o, small, conv_w, conv_b, gate_bias, norm_g, bn, t):
    nc = t // CHUNK
    width = MLSTM_HEADS * HEAD_DIM
    norm_g_t = jnp.repeat(norm_g.reshape(MLSTM_HEADS, HEAD_DIM).T, HEAD_DIM, axis=1)
    blk = lambda w: pl.BlockSpec((bn, CHUNK, w), lambda c: (0, c, 0))
    full = lambda a: pl.BlockSpec(a.shape, lambda c: (0,) * a.ndim)
    return pl.pallas_call(
        functools.partial(_mlstm_kernel, bn=bn),
        grid=(nc,),
        in_specs=[blk(512), blk(256), blk(256), blk(128), full(conv_w), full(conv_b), full(gate_bias), full(norm_g_t)],
        out_specs=blk(256),
        out_shape=jax.ShapeDtypeStruct((bn, t, 256), BF16),
        scratch_shapes=[pltpu.VMEM((bn, 8, 512), F32), pltpu.VMEM((bn, HEAD_DIM, width), F32),
                        pltpu.VMEM((bn, 1, width), F32), pltpu.VMEM((bn, 1, width), F32)],
        compiler_params=_cparams(("arbitrary",)),
        name="mlstm",
    )(c_qk.reshape(bn, t, 512), c_v.reshape(bn, t, 256), c_o.reshape(bn, t, 256), small.reshape(bn, t, 128),
      conv_w, conv_b, gate_bias, norm_g_t)


def _s5_kernel(u_ref, perm_ref, bbre_ref, bbim_ref, pwre_ref, pwim_ref, cre_ref, cim_ref, dskip_ref, gluw_ref,
               glub_ref, o_ref, sre_ref, sim_ref, stre_ref, stim_ref, yp_ref):
    @pl.when(pl.program_id(1) == 0)
    def _():
        stre_ref[...] = jnp.zeros_like(stre_ref)
        stim_ref[...] = jnp.zeros_like(stim_ref)

    seg_len = S5_TT // S5_SEGS
    u = u_ref[...]
    ub = _dot(perm_ref[...], u.astype(BF16)).astype(BF16)
    sre_ref[...] = _dot(ub, bbre_ref[...])
    sim_ref[...] = _dot(ub, bbim_ref[...])
    a_re = jnp.broadcast_to(pwre_ref[0:1, :], (S5_SEGS, S5_LANES))
    a_im = jnp.broadcast_to(pwim_ref[0:1, :], (S5_SEGS, S5_LANES))

    def local_step(i, carry):
        s_re, s_im = carry
        rows = pl.ds(pl.multiple_of(i * S5_SEGS, S5_SEGS), S5_SEGS)
        n_re = a_re * s_re - a_im * s_im + sre_ref[rows, :]
        n_im = a_re * s_im + a_im * s_re + sim_ref[rows, :]
        sre_ref[rows, :] = n_re
        sim_ref[rows, :] = n_im
        return n_re, n_im

    zeros = jnp.zeros((S5_SEGS, S5_LANES), F32)
    e_re, e_im = lax.fori_loop(0, seg_len, local_step, (zeros, zeros), unroll=4)

    al_re, al_im = pwre_ref[seg_len - 1:seg_len, :], pwim_ref[seg_len - 1:seg_len, :]
    c_re, c_im = stre_ref[0:1, :], stim_ref[0:1, :]
    cs_re, cs_im = [], []
    for k in range(S5_SEGS):
        cs_re.append(c_re)
        cs_im.append(c_im)
        c_re, c_im = (e_re[k:k + 1, :] + al_re * c_re - al_im * c_im,
                      e_im[k:k + 1, :] + al_re * c_im + al_im * c_re)
    stre_ref[...] = jnp.broadcast_to(c_re, stre_ref.shape)
    stim_ref[...] = jnp.broadcast_to(c_im, stim_ref.shape)
    cin_re = jnp.concatenate(cs_re, axis=0)
    cin_im = jnp.concatenate(cs_im, axis=0)

    def correct_step(i, carry):
        rows = pl.ds(pl.multiple_of(i * S5_SEGS, S5_SEGS), S5_SEGS)
        p_re, p_im = pwre_ref[pl.ds(i, 1), :], pwim_ref[pl.ds(i, 1), :]
        sre_ref[rows, :] = sre_ref[rows, :] + p_re * cin_re - p_im * cin_im
        sim_ref[rows, :] = sim_ref[rows, :] + p_re * cin_im + p_im * cin_re
        return carry

    lax.fori_loop(0, seg_len, correct_step, 0, unroll=4)
    ycs = _dot(sre_ref[...].astype(BF16), cre_ref[...]) - _dot(sim_ref[...].astype(BF16), cim_ref[...])
    ngrp = GROUP_WIDTH // 128
    for g in range(ngrp):
        yp_ref[g] = ycs[:, g * 128:(g + 1) * 128]
    blocks = []
    for k in range(S5_SEGS):
        for i0 in range(0, seg_len, 8):
            rows = pl.ds(i0 * S5_SEGS + k, 8, stride=S5_SEGS)
            blocks.append(jnp.concatenate([yp_ref[g, rows, :] for g in range(ngrp)], axis=1))
    y = jnp.concatenate(blocks, axis=0) + dskip_ref[...] * u
    y = 0.5 * y * (1.0 + jnp.tanh(math.sqrt(2.0 / math.pi) * (y + 0.044715 * (y * y * y))))
    z = _dot(y.astype(BF16), gluw_ref[...]) + glub_ref[...]
    o_ref[...] = (y * _sigmoid(z)).astype(BF16)


def _s5(d_u, bb_re, bb_im, pw_re, pw_im, c_re_t, c_im_t, d_skip, glu_w, glu_b, bn, t):
    nt = t // S5_TT
    full = lambda a: pl.BlockSpec(a.shape, lambda b, i: (0,) * a.ndim)
    r = jnp.arange(S5_TT)
    perm = (r[None, :] == ((r % S5_SEGS) * (S5_TT // S5_SEGS) + r // S5_SEGS)[:, None]).astype(BF16)
    args = (perm, bb_re, bb_im, pw_re, pw_im, c_re_t, c_im_t, d_skip, glu_w, glu_b)
    return pl.pallas_call(
        _s5_kernel,
        grid=(bn, nt),
        in_specs=[pl.BlockSpec((S5_TT, 256), lambda b, i: (b * nt + i, 0))] + [full(a) for a in args],
        out_specs=pl.BlockSpec((S5_TT, 256), lambda b, i: (b * nt + i, 0)),
        out_shape=jax.ShapeDtypeStruct((bn * t, 256), BF16),
        scratch_shapes=[pltpu.VMEM((S5_TT, S5_LANES), F32), pltpu.VMEM((S5_TT, S5_LANES), F32),
                        pltpu.VMEM((8, S5_LANES), F32), pltpu.VMEM((8, S5_LANES), F32),
                        pltpu.VMEM((GROUP_WIDTH // 128, S5_TT, 128), F32)],
        compiler_params=_cparams(("arbitrary", "arbitrary")),
        name="s5_glu",
    )(d_u, *args)


def _outproj_ffn_kernel(x_ref, modm_ref, oa_ref, ob_ref, oc_ref, od_ref, wout_ref, lngm_ref, lnbm_ref,
                        modf_ref, w13_ref, w2_ref, lngf_ref, lnbf_ref, o_ref, h_ref):
    y = _dot(oa_ref[...], wout_ref[0])
    y = y + _dot(ob_ref[...], wout_ref[1])
    y = y + _dot(oc_ref[...], wout_ref[2])
    y = y + _dot(od_ref[...], wout_ref[3])
    x_mid = _residual_layer_norm(x_ref[...], modm_ref[2:3, :] * y, lngm_ref[...], lnbm_ref[...])
    o_ref[...] = _ffn_tile(x_mid, modf_ref, w13_ref, w2_ref, lngf_ref, lnbf_ref, h_ref)


def _outproj_ffn(x2, mod_mix, o_a, o_b, o_c, o_d, w_out4, ln_g_mix, ln_b_mix, mod_ffn, w13c, w2c, ln_g_ffn, ln_b_ffn,
                 rows_per_batch):
    n = x2.shape[0]
    tm = ROW_TILE
    tiles_per_batch = rows_per_batch // tm
    row = pl.BlockSpec((tm, D_MODEL), lambda i: (i, 0))
    mod = pl.BlockSpec((None, 8, D_MODEL), lambda i: (i // tiles_per_batch, 0, 0))
    mix = pl.BlockSpec((tm, GROUP_WIDTH), lambda i: (i, 0))
    vec = pl.BlockSpec((1, D_MODEL), lambda i: (0, 0))
    resident = lambda a: pl.BlockSpec(a.shape, lambda i: (0,) * a.ndim, pipeline_mode=pl.Buffered(1))
    return pl.pallas_call(
        _outproj_ffn_kernel,
        grid=(n // tm,),
        in_specs=[row, mod, mix, mix, mix, mix, resident(w_out4), vec, vec, mod, resident(w13c), resident(w2c), vec, vec],
        out_specs=row,
        out_shape=jax.ShapeDtypeStruct((n, D_MODEL), F32),
        scratch_shapes=[pltpu.VMEM((tm, D_FF), BF16)],
        compiler_params=_cparams(("arbitrary",)),
        name="outproj_ffn",
    )(x2, mod_mix, o_a, o_b, o_c, o_d, w_out4, ln_g_mix.reshape(1, -1), ln_b_mix.reshape(1, -1),
      mod_ffn, w13c, w2c, ln_g_ffn.reshape(1, -1), ln_b_ffn.reshape(1, -1))


def _permute_w_in(w_in):
    off = {}
    o = 0
    for name, s in (("qa", 256), ("ka", 128), ("va", 128), ("cq", 128), ("kb", 64), ("vb", 64), ("ki", 32),
                    ("wi", 4), ("qkc", 512), ("vc", 256), ("ig", 4), ("fg", 4), ("oc", 256), ("ud", 256)):
        off[name] = (o, o + s)
        o += s
    col = lambda n: w_in[:, off[n][0]:off[n][1]]
    small = jnp.concatenate([col("wi"), col("ig"), col("fg"),
                             jnp.zeros((w_in.shape[0], 128 - 12), w_in.dtype)], axis=1)
    parts = [col("qa"), col("ka"), col("va"), col("cq"), col("kb"), col("vb")] + [col("ki")] * 4 + [
        col("qkc"), col("vc"), col("oc"), col("ud"), small]
    return jnp.concatenate(parts, axis=1).astype(BF16)


def _s5_params(lam_re, lam_im, log_step, b_re, b_im, c_re, c_im):
    dt = jnp.exp(log_step)[:, None]
    mag = jnp.exp(lam_re * dt)
    a_re, a_im = mag * jnp.cos(lam_im * dt), mag * jnp.sin(lam_im * dt)
    den = lam_re * lam_re + lam_im * lam_im
    kap_re = ((a_re - 1.0) * lam_re + a_im * lam_im) / den
    kap_im = (a_im * lam_re - (a_re - 1.0) * lam_im) / den
    bb_re = kap_re[..., None] * b_re - kap_im[..., None] * b_im
    bb_im = kap_re[..., None] * b_im + kap_im[..., None] * b_re
    eye = jnp.eye(S5_GROUPS, dtype=F32)

    def in_mat(bb):
        return jnp.einsum("gph,gk->ghkp", bb, eye).reshape(S5_GROUPS * S5_GROUP_CH, S5_LANES).astype(BF16)

    def out_mat(cc):
        return jnp.einsum("gop,gk->gpko", cc, eye).reshape(S5_LANES, S5_GROUPS * S5_GROUP_CH).astype(BF16)

    n = jnp.arange(1, S5_TT // S5_SEGS + 1, dtype=F32)[:, None, None]
    pw_mag = jnp.exp(n * (lam_re * dt))
    pw_re = (pw_mag * jnp.cos(n * (lam_im * dt))).at[0].set(a_re).reshape(-1, S5_LANES)
    pw_im = (pw_mag * jnp.sin(n * (lam_im * dt))).at[0].set(a_im).reshape(-1, S5_LANES)
    return in_mat(bb_re), in_mat(bb_im), pw_re, pw_im, out_mat(c_re), out_mat(c_im)


def kernel(x, c, ada_w, ada_b, ln_g, ln_b, ffn_w13, ffn_w2, w_in, w_out, sinks, w_uq, w_iq, conv_w, conv_b, ig_b,
           fg_b, mh_norm_g, lam_re, lam_im, log_step, b_re, b_im, c_re, c_im, d_skip, glu_w, glu_b):
    bn, t, d = x.shape
    assert d == D_MODEL and t % max(ROW_TILE, DSA_TK, S5_TT) == 0 and bn <= 8
    n = bn * t
    nl = ada_w.shape[0]
    c_pad = jnp.zeros((8, d), F32).at[:bn].set(c)
    mod_all = _ada_mod(c_pad, ada_w, ada_b)
    mod_all = mod_all[:, :bn].reshape(nl, bn, N_SUB, 3, d).transpose(0, 2, 1, 3, 4)
    mod_all = jnp.pad(mod_all, ((0, 0), (0, 0), (0, 0), (0, 5), (0, 0)))

    x2 = x.reshape(n, d)
    for l in range(nl):
        w13c = ffn_w13[l].astype(BF16).reshape(2, d, 2, FFN_NCHUNK, FFN_TF).transpose(0, 2, 3, 1, 4)
        w2c = ffn_w2[l].astype(BF16)
        x2 = _ffn_sublayer(x2, mod_all[l, 0], w13c[0], w2c[0], ln_g[l, 0], ln_b[l, 0], t)
        (a_q, a_kv, b_q, b_qi, b_kv, b_ki, c_qk, c_v, c_o, d_u, small) = _inproj(
            x2, mod_all[l, 1], _permute_w_in(w_in[l]), w_uq[l].astype(BF16), w_iq[l].astype(BF16), t)
        o_a = _swa(sinks[l], a_q, a_kv, bn, t)
        o_b = _dsa(b_q, b_qi, small, b_kv, b_ki, bn, t)
        gate_bias = jnp.zeros((1, 128), F32).at[0, SM_IG:SM_IG + 4].set(ig_b[l]).at[0, SM_FG:SM_FG + 4].set(fg_b[l])
        o_c = _mlstm(c_qk, c_v, c_o, small, conv_w[l], conv_b[l].reshape(1, -1), gate_bias,
                     mh_norm_g[l].reshape(1, -1), bn, t).reshape(n, GROUP_WIDTH)
        s5p = _s5_params(lam_re[l], lam_im[l], log_step[l], b_re[l], b_im[l], c_re[l], c_im[l])
        o_d = _s5(d_u, *s5p, d_skip[l].reshape(1, -1), glu_w[l].astype(BF16), glu_b[l].reshape(1, -1), bn, t)
        x2 = _outproj_ffn(x2, mod_all[l, 1], o_a, o_b, o_c, o_d, w_out[l].astype(BF16).reshape(4, GROUP_WIDTH, d),
                          ln_g[l, 1], ln_b[l, 1], mod_all[l, 2], w13c[1], w2c[1], ln_g[l, 2], ln_b[l, 2], t)
    return x2.reshape(bn, t, d)
```

```python
import functools
import math

import jax
import jax.numpy as jnp
from jax import lax
from jax.experimental import pallas as pl
from jax.experimental.pallas import tpu as pltpu

F32 = jnp.float32
BF16 = jnp.bfloat16
I32 = jnp.int32

D_MODEL = 1024
DEPTH = 2
CHUNK = 64
HEAD_DIM = 64
GROUP_WIDTH = 256
SWA_HEADS = 4
SWA_KV_HEADS = 2
SWA_WIN_CHUNKS = 2
DSA_HEADS = 4
DSA_Q_RANK = 128
IDX_HEADS = 4
IDX_DIM = 32
DSA_TOPK = 256
MLSTM_HEADS = 4
MLSTM_CONV = 4
S5_GROUP_CH = 16
S5_GROUPS = 16
S5_STATE = 64
S5_LANES = S5_GROUPS * S5_STATE
D_FF = 2816
N_SUB = 3
ALPHA = (2 * DEPTH) ** 0.25
LN_EPS = 1e-5
NEG_INF = -1e30
INT_MIN = -(2 ** 31)
FIELD_BITS = 15
FIELD_GUARD = -(2 ** 31) + 2 ** 15

SLOPES_A = tuple(2.0 ** -(i + 1) for i in range(0, 8, 2))
SLOPES_B = tuple(2.0 ** -(i + 1) for i in range(1, 8, 2))

VMEM_LIMIT_BYTES = 56 * 1024 * 1024

FFN_TF = 256
FFN_NCHUNK = D_FF // FFN_TF
ROW_TILE = 512
SWA_TQ = 128
DSA_TQ = 128
DSA_TK = 512
S5_TT = 512
S5_SEGS = 8

ZC_QA, ZC_KVA, ZC_CQ, ZC_KVB, ZC_KI = 0, 256, 512, 640, 768
ZC_QKC, ZC_VC, ZC_OC, ZC_UD, ZC_SMALL, Z_WIDTH = 896, 1408, 1664, 1920, 2176, 2304
SM_WI, SM_IG, SM_FG = 0, 4, 8


def _cparams(sem):
    return pltpu.CompilerParams(dimension_semantics=sem, vmem_limit_bytes=VMEM_LIMIT_BYTES)


def _dot(a, b):
    return jnp.dot(a, b, preferred_element_type=F32)


def _dot_nt(a, b):
    return lax.dot_general(a, b, (((1,), (1,)), ((), ())), preferred_element_type=F32)


def _sigmoid(x):
    return 1.0 / (1.0 + jnp.exp(-x))


def _residual_layer_norm(x, y, g, b):
    v = ALPHA * x + y
    mu = jnp.mean(v, axis=-1, keepdims=True)
    d = v - mu
    var = jnp.mean(d * d, axis=-1, keepdims=True)
    return d * lax.rsqrt(var + LN_EPS) * g + b


def _ada_kernel(c_ref, w_ref, b_ref, o_ref):
    c = c_ref[...]
    cs = c * _sigmoid(c)
    o_ref[...] = jnp.dot(cs, w_ref[...], preferred_element_type=F32,
                         precision=lax.Precision.HIGHEST) + b_ref[...]


def _ada_mod(c_pad, ada_w, ada_b):
    nl = ada_w.shape[0]
    ncol = ada_w.shape[2] // D_MODEL
    return pl.pallas_call(
        _ada_kernel,
        grid=(nl, ncol),
        in_specs=[
            pl.BlockSpec((8, D_MODEL), lambda l, j: (0, 0)),
            pl.BlockSpec((None, D_MODEL, D_MODEL), lambda l, j: (l, 0, j)),
            pl.BlockSpec((None, 1, D_MODEL), lambda l, j: (l, 0, j)),
        ],
        out_specs=pl.BlockSpec((None, 8, D_MODEL), lambda l, j: (l, 0, j)),
        out_shape=jax.ShapeDtypeStruct((nl, 8, ada_w.shape[2]), F32),
        compiler_params=_cparams(("arbitrary", "arbitrary")),
        name="ada_mod",
    )(c_pad, ada_w, ada_b.reshape(nl, 1, -1))


def _ffn_tile(x, mod_ref, w13_ref, w2_ref, lng_ref, lnb_ref, h_ref):
    shift, scale, gate = mod_ref[0:1, :], mod_ref[1:2, :], mod_ref[2:3, :]
    u = (x * (1.0 + scale) + shift).astype(BF16)
    for j in range(FFN_NCHUNK):
        a = _dot(u, w13_ref[0, j])
        g = _dot(u, w13_ref[1, j])
        h_ref[:, j * FFN_TF:(j + 1) * FFN_TF] = (a * _sigmoid(a) * g).astype(BF16)
    y = _dot(h_ref[...], w2_ref[...])
    return _residual_layer_norm(x, 0.5 * gate * y, lng_ref[...], lnb_ref[...])


def _inproj_tile(x, mod_ref, w_ref, wuq_ref, wiq_ref, outs):
    (aq_ref, akv_ref, bq_ref, bqi_ref, bkv_ref, bki_ref, cqk_ref, cv_ref, co_ref, du_ref, sm_ref) = outs
    shift, scale = mod_ref[0:1, :], mod_ref[1:2, :]
    u = (x * (1.0 + scale) + shift).astype(BF16)
    z = _dot(u, w_ref[...])
    aq_ref[...] = z[:, ZC_QA:ZC_KVA].astype(BF16)
    akv_ref[...] = z[:, ZC_KVA:ZC_CQ].astype(BF16)
    cq = z[:, ZC_CQ:ZC_KVB].astype(BF16)
    bq_ref[...] = (_dot(cq, wuq_ref[...]) * HEAD_DIM ** -0.5).astype(BF16)
    bqi_ref[...] = _dot(cq, wiq_ref[...]).astype(BF16)
    bkv_ref[...] = z[:, ZC_KVB:ZC_KI].astype(BF16)
    bki_ref[...] = z[:, ZC_KI:ZC_QKC].astype(BF16)
    cqk_ref[...] = z[:, ZC_QKC:ZC_VC]
    cv_ref[...] = z[:, ZC_VC:ZC_OC].astype(BF16)
    co_ref[...] = z[:, ZC_OC:ZC_UD]
    du_ref[...] = z[:, ZC_UD:ZC_SMALL]
    sm_ref[...] = z[:, ZC_SMALL:Z_WIDTH]


INPROJ_OUTPUTS = ((256, BF16), (256, BF16), (256, BF16), (128, BF16), (128, BF16), (128, BF16),
                  (512, F32), (256, BF16), (256, F32), (256, F32), (128, F32))


def _ffn_inproj_kernel(x_ref, modf_ref, w13_ref, w2_ref, lng_ref, lnb_ref, modm_ref, win_ref, wuq_ref, wiq_ref,
                       o_ref, *rest):
    outs, h_ref = rest[:-1], rest[-1]
    x_new = _ffn_tile(x_ref[...], modf_ref, w13_ref, w2_ref, lng_ref, lnb_ref, h_ref)
    o_ref[...] = x_new
    _inproj_tile(x_new, modm_ref, win_ref, wuq_ref, wiq_ref, outs)


def _ffn_inproj(x2, mod_ffn, w13c, w2c, ln_g, ln_b, mod_mix, w_perm, w_uq, w_iq, rows_per_batch):
    n = x2.shape[0]
    tm = ROW_TILE
    tiles_per_batch = rows_per_batch // tm
    row = pl.BlockSpec((tm, D_MODEL), lambda i: (i, 0))
    mod = pl.BlockSpec((None, 8, D_MODEL), lambda i: (i // tiles_per_batch, 0, 0))
    vec = pl.BlockSpec((1, D_MODEL), lambda i: (0, 0))
    resident = lambda a: pl.BlockSpec(a.shape, lambda i: (0,) * a.ndim, pipeline_mode=pl.Buffered(1))
    outs = pl.pallas_call(
        _ffn_inproj_kernel,
        grid=(n // tm,),
        in_specs=[row, mod, resident(w13c), resident(w2c), vec, vec, mod, resident(w_perm), resident(w_uq),
                  resident(w_iq)],
        out_specs=[row] + [pl.BlockSpec((tm, w), lambda i: (i, 0)) for w, _ in INPROJ_OUTPUTS],
        out_shape=[jax.ShapeDtypeStruct((n, D_MODEL), F32)]
        + [jax.ShapeDtypeStruct((n, w), dt) for w, dt in INPROJ_OUTPUTS],
        scratch_shapes=[pltpu.VMEM((tm, D_FF), BF16)],
        compiler_params=_cparams(("arbitrary",)),
        name="ffn_inproj",
    )(x2, mod_ffn, w13c, w2c, ln_g.reshape(1, -1), ln_b.reshape(1, -1), mod_mix, w_perm, w_uq, w_iq)
    return outs[0], outs[1:]


def _swa_kernel(sink_ref, q_ref, kvc_ref, kvp_ref, o_ref):
    i = pl.program_id(1)
    tq = SWA_TQ
    q = q_ref[...]
    kv = jnp.concatenate([kvp_ref[...], kvc_ref[...]], axis=0)
    qpos = i * tq + lax.broadcasted_iota(I32, (tq, 2 * tq), 0)
    kpos = (i - 1) * tq + lax.broadcasted_iota(I32, (tq, 2 * tq), 1)
    qchunk = qpos // CHUNK
    kchunk = (kpos + tq) // CHUNK - tq // CHUNK
    valid = (kpos >= 0) & (kchunk <= qchunk) & (kchunk >= qchunk - SWA_WIN_CHUNKS)
    dist = jnp.abs(qpos - kpos).astype(F32)
    rep = SWA_HEADS // SWA_KV_HEADS
    for h in range(SWA_HEADS):
        g = h // rep
        qh = q[:, h * HEAD_DIM:(h + 1) * HEAD_DIM]
        kg = kv[:, g * HEAD_DIM:(g + 1) * HEAD_DIM]
        vg = kv[:, (SWA_KV_HEADS + g) * HEAD_DIM:(SWA_KV_HEADS + g + 1) * HEAD_DIM]
        s = _dot_nt(qh, kg) * HEAD_DIM ** -0.5 - SLOPES_A[h] * dist
        s = jnp.where(valid, s, NEG_INF)
        sink = sink_ref[h]
        m = jnp.maximum(jnp.max(s, axis=-1, keepdims=True), sink)
        p = jnp.exp(s - m)
        denom = jnp.sum(p, axis=-1, keepdims=True) + jnp.exp(sink - m)
        o = _dot(p.astype(BF16), vg) / denom
        o_ref[:, h * HEAD_DIM:(h + 1) * HEAD_DIM] = o.astype(BF16)


def _swa(sinks, a_q, a_kv, bn, t):
    nt = t // SWA_TQ
    return pl.pallas_call(
        _swa_kernel,
        grid=(bn, nt),
        in_specs=[
            pl.BlockSpec(memory_space=pltpu.SMEM),
            pl.BlockSpec((SWA_TQ, 256), lambda b, i: (b * nt + i, 0)),
            pl.BlockSpec((SWA_TQ, 256), lambda b, i: (b * nt + i, 0)),
            pl.BlockSpec((SWA_TQ, 256), lambda b, i: (b * nt + jnp.maximum(i - 1, 0), 0)),
        ],
        out_specs=pl.BlockSpec((SWA_TQ, 256), lambda b, i: (b * nt + i, 0)),
        out_shape=jax.ShapeDtypeStruct((bn * t, 256), BF16),
        compiler_params=_cparams(("arbitrary", "arbitrary")),
        name="swa_attention",
    )(sinks, a_q, a_kv, a_kv)


def _sortable_key(x):
    bits = lax.bitcast_convert_type(x, I32)
    return bits ^ ((bits >> 31) & 0x7FFFFFFF)


def _dsa_kernel(q_ref, qi_ref, sm_ref, kv_ref, vt_ref, ki_ref, tril_ref, o_ref, key_ref, s_ref, p_ref, w1_ref, w2_ref):
    i = pl.program_id(1)
    tq, tk = DSA_TQ, DSA_TK
    nblk = (i * tq + tq + tk - 1) // tk
    qpos = i * tq + lax.broadcasted_iota(I32, (1, tq), 1)
    qchunk = qpos // CHUNK
    row_k = lax.broadcasted_iota(I32, (tk, tq), 0)

    qi = qi_ref[...]
    lane_i = lax.broadcasted_iota(I32, (1, IDX_HEADS * IDX_DIM), 1) // IDX_DIM
    qi_stack = jnp.concatenate([jnp.where(lane_i == h, qi, jnp.zeros_like(qi)) for h in range(IDX_HEADS)], axis=0)
    sm_t = sm_ref[...].T
    w_idx = [sm_t[SM_WI + h:SM_WI + h + 1, :] for h in range(IDX_HEADS)]
    idx_scale = (IDX_DIM * IDX_HEADS) ** -0.5
    q = q_ref[...]
    q_stack = jnp.concatenate([q[:, h * HEAD_DIM:(h + 1) * HEAD_DIM] for h in range(DSA_HEADS)], axis=0)

    half = tk // 2
    guard = jnp.int32(FIELD_GUARD)

    def pack_fields(f):
        return (f[0:half] << 16) | f[half:tk] | guard

    def score_block(j, mask_inadmissible):
        rows = pl.ds(pl.multiple_of(j * tk, tk), tk)
        s_ref[j] = _dot_nt(kv_ref[rows, 0:HEAD_DIM], q_stack)
        d = _dot_nt(ki_ref[rows, :], qi_stack)
        acc = w_idx[0] * jnp.maximum(d[:, 0:tq], 0.0)
        for h in range(1, IDX_HEADS):
            acc = acc + w_idx[h] * jnp.maximum(d[:, h * tq:(h + 1) * tq], 0.0)
        sc = acc * idx_scale
        if mask_inadmissible:
            sc = jnp.where((j * tk + row_k) // CHUNK <= qchunk, sc, NEG_INF)
        key = _sortable_key(sc)
        key_ref[j] = key
        w1_ref[j] = pack_fields(lax.shift_right_logical(key ^ INT_MIN, 32 - FIELD_BITS))

    def full_block(j, carry):
        score_block(j, False)
        return carry

    lax.fori_loop(0, nblk - 1, full_block, 0)
    score_block(nblk - 1, True)

    k_eff = jnp.minimum(DSA_TOPK, (qchunk + 1) * CHUNK)

    def over_blocks(body, init):
        c = lax.fori_loop(0, nblk // 2, lambda jj, c: body(2 * jj + 1, body(2 * jj, c)), init)
        return lax.cond(nblk % 2 == 1, lambda c: body(nblk - 1, c), lambda c: c, c)

    def count(pred):
        def blk(j, c):
            m = pred(key_ref[j], j * tk + row_k).astype(I32)
            return c + jnp.sum(m.reshape(tk // 8, 8, tq), axis=0)
        return jnp.sum(over_blocks(blk, jnp.zeros((8, tq), I32)), axis=0, keepdims=True)

    def count_fields(w_ref, cand):
        cand2 = (cand << 16) | cand
        def blk(j, c):
            hit = ((w_ref[j] - cand2) >> 15) & 0x00010001
            return c + jnp.sum(hit.reshape(half // 8, 8, tq), axis=0)
        c = over_blocks(blk, jnp.zeros((8, tq), I32))
        return jnp.sum((c & 0xFFFF) + (c >> 16), axis=0, keepdims=True)

    def field_search(w_ref, k_want):
        def step(bi, carry):
            prefix, above = carry
            cand = prefix | (jnp.int32(1) << (FIELD_BITS - 1 - bi))
            cnt = count_fields(w_ref, cand)
            ok = cnt >= k_want
            return jnp.where(ok, cand, prefix), jnp.where(ok, above, cnt)
        return lax.fori_loop(0, FIELD_BITS, step, (jnp.zeros((1, tq), I32), jnp.zeros((1, tq), I32)))

    top, above = field_search(w1_ref, k_eff)
    field_max = (1 << FIELD_BITS) - 1

    def pack_mid(j, carry):
        ukey = key_ref[j] ^ INT_MIN
        member = lax.shift_right_logical(ukey, 32 - FIELD_BITS) == top
        mid = lax.shift_right_logical(ukey, 32 - 2 * FIELD_BITS) & field_max
        w2_ref[j] = pack_fields(jnp.where(member, mid, 0))
        return carry

    lax.fori_loop(0, nblk, pack_mid, 0)
    mid, above_mid = field_search(w2_ref, k_eff - above)

    def bit_step(bi, carry):
        prefix, c_gt = carry
        cand_u = prefix | (jnp.int32(1) << (31 - 2 * FIELD_BITS - bi))
        cand_s = cand_u ^ INT_MIN
        cnt = count(lambda key, kpos: key >= cand_s)
        ok = cnt >= k_eff
        return jnp.where(ok, cand_u, prefix), jnp.where(ok, c_gt, cnt)

    prefix, c_gt = lax.fori_loop(0, 32 - 2 * FIELD_BITS, bit_step,
                                 ((top << (32 - FIELD_BITS)) | (mid << (32 - 2 * FIELD_BITS)), above + above_mid))
    thr = prefix ^ INT_MIN

    need = (k_eff - c_gt).astype(F32)
    tril = tril_ref[...]

    offs = (row_k - qpos).astype(F32)
    p_ref[1] = jnp.zeros(p_ref.shape[1:], BF16)

    def att_block(j, carry):
        m_run, l_run, acc_part, ties_seen = carry
        slot = j % 2
        acc = acc_part + _dot(vt_ref[jnp.maximum(j - 1, 0)], p_ref[1 - slot])
        key = key_ref[j]
        tie = key == thr
        tie_rank = _dot(tril, jnp.where(tie, 1.0, 0.0).astype(BF16)) + ties_seen
        sel = (key > thr) | (tie & (tie_rank <= need))
        dist = jnp.where(sel, jnp.abs(offs + (j * tk).astype(F32)), jnp.inf)
        ms, ls, alphas = [], [], []
        for h in range(DSA_HEADS):
            cols = slice(h * tq, (h + 1) * tq)
            sh = s_ref[j, :, cols] - SLOPES_B[h] * dist
            m_old = m_run[:, cols]
            m_new = jnp.maximum(m_old, jnp.max(sh, axis=0, keepdims=True))
            alpha = jnp.exp(m_old - m_new)
            p = jnp.exp(sh - m_new)
            p_ref[slot, :, cols] = p.astype(BF16)
            ms.append(m_new)
            ls.append(alpha * l_run[:, cols] + jnp.sum(p, axis=0, keepdims=True))
            alphas.append(alpha)
        return (jnp.concatenate(ms, axis=1), jnp.concatenate(ls, axis=1), jnp.concatenate(alphas, axis=1) * acc,
                tie_rank[tk - 1:tk, :])

    init = (jnp.full((1, DSA_HEADS * tq), NEG_INF, F32), jnp.zeros((1, DSA_HEADS * tq), F32),
            jnp.zeros((HEAD_DIM, DSA_HEADS * tq), F32), jnp.zeros((1, tq), F32))
    _, l_run, acc_part, _ = lax.fori_loop(0, nblk, att_block, init)
    acc = acc_part + _dot(vt_ref[nblk - 1], p_ref[(nblk - 1) % 2])
    out = acc / l_run
    o_ref[...] = jnp.concatenate([out[:, h * tq:(h + 1) * tq] for h in range(DSA_HEADS)], axis=0).astype(BF16)


def _dsa(b_q, b_qi, small, b_kv, b_ki, bn, t):
    nt = t // DSA_TQ
    nkb = t // DSA_TK
    v_t = b_kv[:, HEAD_DIM:].reshape(bn * nkb, DSA_TK, HEAD_DIM).transpose(0, 2, 1)
    o_t = pl.pallas_call(
        _dsa_kernel,
        grid=(bn, nt),
        in_specs=[
            pl.BlockSpec((DSA_TQ, 256), lambda b, i: (b * nt + i, 0)),
            pl.BlockSpec((DSA_TQ, 128), lambda b, i: (b * nt + i, 0)),
            pl.BlockSpec((DSA_TQ, 128), lambda b, i: (b * nt + i, 0)),
            pl.BlockSpec((t, 128), lambda b, i: (b, 0)),
            pl.BlockSpec((nkb, HEAD_DIM, DSA_TK), lambda b, i: (b, 0, 0)),
            pl.BlockSpec((t, 128), lambda b, i: (b, 0)),
            pl.BlockSpec((DSA_TK, DSA_TK), lambda b, i: (0, 0)),
        ],
        out_specs=pl.BlockSpec((None, DSA_HEADS * HEAD_DIM, DSA_TQ), lambda b, i: (b * nt + i, 0, 0)),
        out_shape=jax.ShapeDtypeStruct((bn * nt, DSA_HEADS * HEAD_DIM, DSA_TQ), BF16),
        scratch_shapes=[pltpu.VMEM((nkb, DSA_TK, DSA_TQ), I32),
                        pltpu.VMEM((nkb, DSA_TK, DSA_HEADS * DSA_TQ), F32),
                        pltpu.VMEM((2, DSA_TK, DSA_HEADS * DSA_TQ), BF16),
                        pltpu.VMEM((nkb, DSA_TK // 2, DSA_TQ), I32), pltpu.VMEM((nkb, DSA_TK // 2, DSA_TQ), I32)],
        compiler_params=_cparams(("arbitrary", "arbitrary")),
        name="dsa_attention",
    )(b_q, b_qi, small, b_kv, v_t, b_ki, jnp.tril(jnp.ones((DSA_TK, DSA_TK), BF16)))
    return o_t.transpose(0, 2, 1).reshape(bn * t, DSA_HEADS * HEAD_DIM)


def _mlstm_kernel(qk_ref, v_ref, og_ref, sm_ref, convw_ref, convb_ref, gbias_ref, normgt_ref,
                  o_ref, tail_ref, ct_ref, nvec_ref, mst_ref, *, bn):
    c = pl.program_id(0)
    L = CHUNK
    nh, dh, width = MLSTM_HEADS, HEAD_DIM, MLSTM_HEADS * HEAD_DIM

    @pl.when(c == 0)
    def _():
        tail_ref[...] = jnp.zeros_like(tail_ref)
        ct_ref[...] = jnp.zeros_like(ct_ref)
        nvec_ref[...] = jnp.zeros_like(nvec_ref)
        mst_ref[...] = jnp.zeros_like(mst_ref)

    srow = lax.broadcasted_iota(I32, (L, width), 0)
    lane = lax.broadcasted_iota(I32, (L, width), 1)
    jlane = lane % dh
    causal_t = srow <= jlane
    diag_t = srow == jlane
    head_of_lane = lax.broadcasted_iota(I32, (1, width), 1) // dh
    tril = (lax.broadcasted_iota(I32, (L, L), 1) <= lax.broadcasted_iota(I32, (L, L), 0)).astype(F32)
    erow = lax.broadcasted_iota(I32, (128, width), 0)
    ecol_head = lax.broadcasted_iota(I32, (128, width), 1) // dh
    expand_ig = (erow == SM_IG + ecol_head).astype(F32)
    expand_fg = (erow == SM_FG + ecol_head).astype(F32)
    exact = dict(preferred_element_type=F32, precision=lax.Precision.HIGHEST)

    def head_blocks(a):
        out = jnp.where(head_of_lane == 0, a[0:dh], 0.0)
        for h in range(1, nh):
            out = out + jnp.where(head_of_lane == h, a[h * dh:(h + 1) * dh], 0.0)
        return out

    convw = convw_ref[...]
    for b in range(bn):
        cur = qk_ref[b]
        ext = jnp.concatenate([tail_ref[b], cur], axis=0)
        tail_ref[b] = cur[L - 8:L, :]
        y = convb_ref[...] + convw[MLSTM_CONV - 1:MLSTM_CONV, :] * cur
        for k in range(MLSTM_CONV - 1):
            off = 8 - (MLSTM_CONV - 1) + k
            y = y + convw[k:k + 1, :] * ext[off:off + L, :]
        qk = y * _sigmoid(y)
        q_all = qk[:, 0:width]
        k_all = qk[:, width:2 * width] * dh ** -0.5
        q_stack = jnp.concatenate([jnp.where(head_of_lane == h, q_all, 0.0) for h in range(nh)], axis=0).astype(BF16)
        v_all = v_ref[b]
        v_t = v_all.astype(F32).T.astype(BF16)

        gates = sm_ref[b] + gbias_ref[...]
        lf = jnp.minimum(gates, 0.0) - jnp.log(1.0 + jnp.exp(-jnp.abs(gates)))
        bcum = jnp.dot(tril, lf, **exact)
        ig_x = jnp.dot(gates, expand_ig, **exact)
        b_x = jnp.dot(bcum, expand_fg, **exact)
        b_q = jnp.sum(jnp.where(diag_t, b_x, 0.0), axis=0, keepdims=True)
        b_last = b_x[L - 1:L, :]
        m_prev = mst_ref[b]
        ct = ct_ref[b]
        nvec = nvec_ref[b]

        dlog = jnp.where(causal_t, b_q - b_x + ig_x, NEG_INF)
        inter = b_q + m_prev
        mj = jnp.maximum(inter, jnp.max(dlog, axis=0, keepdims=True))
        dw = jnp.exp(dlog - mj)
        iw = jnp.exp(inter - mj)
        sc = _dot_nt(k_all.astype(BF16), q_stack) * dw
        qn = _dot_nt(jnp.broadcast_to(nvec, (8, width)).astype(BF16), q_stack)[0:1, :]
        q_c = _dot_nt(ct.astype(BF16), q_stack)
        num = iw * q_c + head_blocks(_dot(v_t, sc.astype(BF16)))
        den = iw * qn + jnp.sum(sc, axis=0, keepdims=True)
        hj = num / jnp.maximum(jnp.abs(den), jnp.exp(-mj))

        dec = b_last - b_x + ig_x
        m_new = jnp.maximum(b_last + m_prev, jnp.max(dec, axis=0, keepdims=True))
        wc = jnp.exp(b_last + m_prev - m_new)
        kw = k_all * jnp.exp(dec - m_new)
        ct_ref[b] = wc * ct + head_blocks(_dot(v_t, kw.astype(BF16)))
        nvec_ref[b] = wc * nvec + jnp.sum(kw, axis=0, keepdims=True)
        mst_ref[b] = m_new

        mu = jnp.mean(hj, axis=0, keepdims=True)
        dev = hj - mu
        var = jnp.mean(dev * dev, axis=0, keepdims=True)
        hn_t = (dev * lax.rsqrt(var + LN_EPS) * normgt_ref[...]).T
        hn = jnp.concatenate([hn_t[h * dh:(h + 1) * dh, :] for h in range(nh)], axis=1)
        o_ref[b] = (_sigmoid(og_ref[b]) * hn).astype(BF16)


def _mlstm(c_qk, c_v, c_o, small, conv_w, conv_b, gate_bias, norm_g, bn, t):
    nc = t // CHUNK
    width = MLSTM_HEADS * HEAD_DIM
    norm_g_t = jnp.repeat(norm_g.reshape(MLSTM_HEADS, HEAD_DIM).T, HEAD_DIM, axis=1)
    blk = lambda w: pl.BlockSpec((bn, CHUNK, w), lambda c: (0, c, 0))
    full = lambda a: pl.BlockSpec(a.shape, lambda c: (0,) * a.ndim)
    return pl.pallas_call(
        functools.partial(_mlstm_kernel, bn=bn),
        grid=(nc,),
        in_specs=[blk(512), blk(256), blk(256), blk(128), full(conv_w), full(conv_b), full(gate_bias), full(norm_g_t)],
        out_specs=blk(256),
        out_shape=jax.ShapeDtypeStruct((bn, t, 256), BF16),
        scratch_shapes=[pltpu.VMEM((bn, 8, 512), F32), pltpu.VMEM((bn, HEAD_DIM, width), F32),
                        pltpu.VMEM((bn, 1, width), F32), pltpu.VMEM((bn, 1, width), F32)],
        compiler_params=_cparams(("arbitrary",)),
        name="mlstm",
    )(c_qk.reshape(bn, t, 512), c_v.reshape(bn, t, 256), c_o.reshape(bn, t, 256), small.reshape(bn, t, 128),
      conv_w, conv_b, gate_bias, norm_g_t)


def _s5_kernel(u_ref, perm_ref, bbre_ref, bbim_ref, pwre_ref, pwim_ref, cre_ref, cim_ref, dskip_ref, gluw_ref,
               glub_ref, o_ref, sre_ref, sim_ref, stre_ref, stim_ref, yp_ref):
    @pl.when(pl.program_id(1) == 0)
    def _():
        stre_ref[...] = jnp.zeros_like(stre_ref)
        stim_ref[...] = jnp.zeros_like(stim_ref)

    seg_len = S5_TT // S5_SEGS
    u = u_ref[...]
    ub = _dot(perm_ref[...], u.astype(BF16)).astype(BF16)
    sre_ref[...] = _dot(ub, bbre_ref[...])
    sim_ref[...] = _dot(ub, bbim_ref[...])
    a_re = jnp.broadcast_to(pwre_ref[0:1, :], (S5_SEGS, S5_LANES))
    a_im = jnp.broadcast_to(pwim_ref[0:1, :], (S5_SEGS, S5_LANES))

    def local_step(i, carry):
        s_re, s_im = carry
        rows = pl.ds(pl.multiple_of(i * S5_SEGS, S5_SEGS), S5_SEGS)
        n_re = a_re * s_re - a_im * s_im + sre_ref[rows, :]
        n_im = a_re * s_im + a_im * s_re + sim_ref[rows, :]
        sre_ref[rows, :] = n_re
        sim_ref[rows, :] = n_im
        return n_re, n_im

    zeros = jnp.zeros((S5_SEGS, S5_LANES), F32)
    e_re, e_im = lax.fori_loop(0, seg_len, local_step, (zeros, zeros), unroll=4)

    al_re, al_im = pwre_ref[seg_len - 1:seg_len, :], pwim_ref[seg_len - 1:seg_len, :]
    c_re, c_im = stre_ref[0:1, :], stim_ref[0:1, :]
    cs_re, cs_im = [], []
    for k in range(S5_SEGS):
        cs_re.append(c_re)
        cs_im.append(c_im)
        c_re, c_im = (e_re[k:k + 1, :] + al_re * c_re - al_im * c_im,
                      e_im[k:k + 1, :] + al_re * c_im + al_im * c_re)
    stre_ref[...] = jnp.broadcast_to(c_re, stre_ref.shape)
    stim_ref[...] = jnp.broadcast_to(c_im, stim_ref.shape)
    cin_re = jnp.concatenate(cs_re, axis=0)
    cin_im = jnp.concatenate(cs_im, axis=0)

    def correct_step(i, carry):
        rows = pl.ds(pl.multiple_of(i * S5_SEGS, S5_SEGS), S5_SEGS)
        p_re, p_im = pwre_ref[pl.ds(i, 1), :], pwim_ref[pl.ds(i, 1), :]
        sre_ref[rows, :] = sre_ref[rows, :] + p_re * cin_re - p_im * cin_im
        sim_ref[rows, :] = sim_ref[rows, :] + p_re * cin_im + p_im * cin_re
        return carry

    lax.fori_loop(0, seg_len, correct_step, 0, unroll=4)
    ycs = _dot(sre_ref[...].astype(BF16), cre_ref[...]) - _dot(sim_ref[...].astype(BF16), cim_ref[...])
    ngrp = GROUP_WIDTH // 128
    for g in range(ngrp):
        yp_ref[g] = ycs[:, g * 128:(g + 1) * 128]
    blocks = []
    for k in range(S5_SEGS):
        for i0 in range(0, seg_len, 8):
            rows = pl.ds(i0 * S5_SEGS + k, 8, stride=S5_SEGS)
            blocks.append(jnp.concatenate([yp_ref[g, rows, :] for g in range(ngrp)], axis=1))
    y = jnp.concatenate(blocks, axis=0) + dskip_ref[...] * u
    y = 0.5 * y * (1.0 + jnp.tanh(math.sqrt(2.0 / math.pi) * (y + 0.044715 * (y * y * y))))
    z = _dot(y.astype(BF16), gluw_ref[...]) + glub_ref[...]
    o_ref[...] = (y * _sigmoid(z)).astype(BF16)


def _s5(d_u, bb_re, bb_im, pw_re, pw_im, c_re_t, c_im_t, d_skip, glu_w, glu_b, bn, t):
    nt = t // S5_TT
    full = lambda a: pl.BlockSpec(a.shape, lambda b, i: (0,) * a.ndim)
    r = jnp.arange(S5_TT)
    perm = (r[None, :] == ((r % S5_SEGS) * (S5_TT // S5_SEGS) + r // S5_SEGS)[:, None]).astype(BF16)
    args = (perm, bb_re, bb_im, pw_re, pw_im, c_re_t, c_im_t, d_skip, glu_w, glu_b)
    return pl.pallas_call(
        _s5_kernel,
        grid=(bn, nt),
        in_specs=[pl.BlockSpec((S5_TT, 256), lambda b, i: (b * nt + i, 0))] + [full(a) for a in args],
        out_specs=pl.BlockSpec((S5_TT, 256), lambda b, i: (b * nt + i, 0)),
        out_shape=jax.ShapeDtypeStruct((bn * t, 256), BF16),
        scratch_shapes=[pltpu.VMEM((S5_TT, S5_LANES), F32), pltpu.VMEM((S5_TT, S5_LANES), F32),
                        pltpu.VMEM((8, S5_LANES), F32), pltpu.VMEM((8, S5_LANES), F32),
                        pltpu.VMEM((GROUP_WIDTH // 128, S5_TT, 128), F32)],
        compiler_params=_cparams(("arbitrary", "arbitrary")),
        name="s5_glu",
    )(d_u, *args)


def _outproj_ffn_kernel(x_ref, modm_ref, oa_ref, ob_ref, oc_ref, od_ref, wout_ref, lngm_ref, lnbm_ref,
                        modf_ref, w13_ref, w2_ref, lngf_ref, lnbf_ref, o_ref, h_ref):
    y = _dot(oa_ref[...], wout_ref[0])
    y = y + _dot(ob_ref[...], wout_ref[1])
    y = y + _dot(oc_ref[...], wout_ref[2])
    y = y + _dot(od_ref[...], wout_ref[3])
    x_mid = _residual_layer_norm(x_ref[...], modm_ref[2:3, :] * y, lngm_ref[...], lnbm_ref[...])
    o_ref[...] = _ffn_tile(x_mid, modf_ref, w13_ref, w2_ref, lngf_ref, lnbf_ref, h_ref)


def _outproj_ffn(x2, mod_mix, o_a, o_b, o_c, o_d, w_out4, ln_g_mix, ln_b_mix, mod_ffn, w13c, w2c, ln_g_ffn, ln_b_ffn,
                 rows_per_batch):
    n = x2.shape[0]
    tm = ROW_TILE
    tiles_per_batch = rows_per_batch // tm
    row = pl.BlockSpec((tm, D_MODEL), lambda i: (i, 0))
    mod = pl.BlockSpec((None, 8, D_MODEL), lambda i: (i // tiles_per_batch, 0, 0))
    mix = pl.BlockSpec((tm, GROUP_WIDTH), lambda i: (i, 0))
    vec = pl.BlockSpec((1, D_MODEL), lambda i: (0, 0))
    resident = lambda a: pl.BlockSpec(a.shape, lambda i: (0,) * a.ndim, pipeline_mode=pl.Buffered(1))
    return pl.pallas_call(
        _outproj_ffn_kernel,
        grid=(n // tm,),
        in_specs=[row, mod, mix, mix, mix, mix, resident(w_out4), vec, vec, mod, resident(w13c), resident(w2c), vec, vec],
        out_specs=row,
        out_shape=jax.ShapeDtypeStruct((n, D_MODEL), F32),
        scratch_shapes=[pltpu.VMEM((tm, D_FF), BF16)],
        compiler_params=_cparams(("arbitrary",)),
        name="outproj_ffn",
    )(x2, mod_mix, o_a, o_b, o_c, o_d, w_out4, ln_g_mix.reshape(1, -1), ln_b_mix.reshape(1, -1),
      mod_ffn, w13c, w2c, ln_g_ffn.reshape(1, -1), ln_b_ffn.reshape(1, -1))


def _permute_w_in(w_in):
    off = {}
    o = 0
    for name, s in (("qa", 256), ("ka", 128), ("va", 128), ("cq", 128), ("kb", 64), ("vb", 64), ("ki", 32),
                    ("wi", 4), ("qkc", 512), ("vc", 256), ("ig", 4), ("fg", 4), ("oc", 256), ("ud", 256)):
        off[name] = (o, o + s)
        o += s
    col = lambda n: w_in[:, off[n][0]:off[n][1]]
    small = jnp.concatenate([col("wi"), col("ig"), col("fg"),
                             jnp.zeros((w_in.shape[0], 128 - 12), w_in.dtype)], axis=1)
    parts = [col("qa"), col("ka"), col("va"), col("cq"), col("kb"), col("vb")] + [col("ki")] * 4 + [
        col("qkc"), col("vc"), col("oc"), col("ud"), small]
    return jnp.concatenate(parts, axis=1).astype(BF16)


def _s5_params(lam_re, lam_im, log_step, b_re, b_im, c_re, c_im):
    dt = jnp.exp(log_step)[:, None]
    mag = jnp.exp(lam_re * dt)
    a_re, a_im = mag * jnp.cos(lam_im * dt), mag * jnp.sin(lam_im * dt)
    den = lam_re * lam_re + lam_im * lam_im
    kap_re = ((a_re - 1.0) * lam_re + a_im * lam_im) / den
    kap_im = (a_im * lam_re - (a_re - 1.0) * lam_im) / den
    bb_re = kap_re[..., None] * b_re - kap_im[..., None] * b_im
    bb_im = kap_re[..., None] * b_im + kap_im[..., None] * b_re
    eye = jnp.eye(S5_GROUPS, dtype=F32)

    def in_mat(bb):
        return jnp.einsum("gph,gk->ghkp", bb, eye).reshape(S5_GROUPS * S5_GROUP_CH, S5_LANES).astype(BF16)

    def out_mat(cc):
        return jnp.einsum("gop,gk->gpko", cc, eye).reshape(S5_LANES, S5_GROUPS * S5_GROUP_CH).astype(BF16)

    n = jnp.arange(1, S5_TT // S5_SEGS + 1, dtype=F32)[:, None, None]
    pw_mag = jnp.exp(n * (lam_re * dt))
    pw_re = (pw_mag * jnp.cos(n * (lam_im * dt))).at[0].set(a_re).reshape(-1, S5_LANES)
    pw_im = (pw_mag * jnp.sin(n * (lam_im * dt))).at[0].set(a_im).reshape(-1, S5_LANES)
    return in_mat(bb_re), in_mat(bb_im), pw_re, pw_im, out_mat(c_re), out_mat(c_im)


def kernel(x, c, ada_w, ada_b, ln_g, ln_b, ffn_w13, ffn_w2, w_in, w_out, sinks, w_uq, w_iq, conv_w, conv_b, ig_b,
           fg_b, mh_norm_g, lam_re, lam_im, log_step, b_re, b_im, c_re, c_im, d_skip, glu_w, glu_b):
    bn, t, d = x.shape
    assert d == D_MODEL and t % max(ROW_TILE, DSA_TK, S5_TT) == 0 and bn <= 8
    n = bn * t
    nl = ada_w.shape[0]
    c_pad = jnp.zeros((8, d), F32).at[:bn].set(c)
    mod_all = _ada_mod(c_pad, ada_w, ada_b)
    mod_all = mod_all[:, :bn].reshape(nl, bn, N_SUB, 3, d).transpose(0, 2, 1, 3, 4)
    mod_all = jnp.pad(mod_all, ((0, 0), (0, 0), (0, 0), (0, 5), (0, 0)))

    x2 = x.reshape(n, d)
    for l in range(nl):
        w13c = ffn_w13[l].astype(BF16).reshape(2, d, 2, FFN_NCHUNK, FFN_TF).transpose(0, 2, 3, 1, 4)
        w2c = ffn_w2[l].astype(BF16)
        x2, (a_q, a_kv, b_q, b_qi, b_kv, b_ki, c_qk, c_v, c_o, d_u, small) = _ffn_inproj(
            x2, mod_all[l, 0], w13c[0], w2c[0], ln_g[l, 0], ln_b[l, 0], mod_all[l, 1],
            _permute_w_in(w_in[l]), w_uq[l].astype(BF16), w_iq[l].astype(BF16), t)
        o_a = _swa(sinks[l], a_q, a_kv, bn, t)
        o_b = _dsa(b_q, b_qi, small, b_kv, b_ki, bn, t)
        gate_bias = jnp.zeros((1, 128), F32).at[0, SM_IG:SM_IG + 4].set(ig_b[l]).at[0, SM_FG:SM_FG + 4].set(fg_b[l])
        o_c = _mlstm(c_qk, c_v, c_o, small, conv_w[l], conv_b[l].reshape(1, -1), gate_bias,
                     mh_norm_g[l].reshape(1, -1), bn, t).reshape(n, GROUP_WIDTH)
        s5p = _s5_params(lam_re[l], lam_im[l], log_step[l], b_re[l], b_im[l], c_re[l], c_im[l])
        o_d = _s5(d_u, *s5p, d_skip[l].reshape(1, -1), glu_w[l].astype(BF16), glu_b[l].reshape(1, -1), bn, t)
        x2 = _outproj_ffn(x2, mod_all[l, 1], o_a, o_b, o_c, o_d, w_out[l].astype(BF16).reshape(4, GROUP_WIDTH, d),
                          ln_g[l, 1], ln_b[l, 1], mod_all[l, 2], w13c[1], w2c[1], ln_g[l, 2], ln_b[l, 2], t)
    return x2.reshape(bn, t, d)
```

```python
import functools
import math

import jax
import jax.numpy as jnp
from jax import lax
from jax.experimental import pallas as pl
from jax.experimental.pallas import tpu as pltpu

F32 = jnp.float32
BF16 = jnp.bfloat16
I32 = jnp.int32

D_MODEL = 1024
DEPTH = 2
CHUNK = 64
HEAD_DIM = 64
GROUP_WIDTH = 256
SWA_HEADS = 4
SWA_KV_HEADS = 2
SWA_WIN_CHUNKS = 2
DSA_HEADS = 4
DSA_Q_RANK = 128
IDX_HEADS = 4
IDX_DIM = 32
DSA_TOPK = 256
MLSTM_HEADS = 4
MLSTM_CONV = 4
S5_GROUP_CH = 16
S5_GROUPS = 16
S5_STATE = 64
S5_LANES = S5_GROUPS * S5_STATE
D_FF = 2816
N_SUB = 3
ALPHA = (2 * DEPTH) ** 0.25
LN_EPS = 1e-5
NEG_INF = -1e30
INT_MIN = -(2 ** 31)
FIELD_BITS = 15
FIELD_GUARD = -(2 ** 31) + 2 ** 15

SLOPES_A = tuple(2.0 ** -(i + 1) for i in range(0, 8, 2))
SLOPES_B = tuple(2.0 ** -(i + 1) for i in range(1, 8, 2))

VMEM_LIMIT_BYTES = 56 * 1024 * 1024

FFN_TF = 256
FFN_NCHUNK = D_FF // FFN_TF
ROW_TILE = 512
SWA_TQ = 128
DSA_TQ = 128
DSA_TK = 512
S5_TT = 512
S5_SEGS = 8

ZC_QA, ZC_KVA, ZC_CQ, ZC_KVB, ZC_KI = 0, 256, 512, 640, 768
ZC_QKC, ZC_VC, ZC_OC, ZC_UD, ZC_SMALL, Z_WIDTH = 896, 1408, 1664, 1920, 2176, 2304
SM_WI, SM_IG, SM_FG = 0, 4, 8


def _cparams(sem):
    return pltpu.CompilerParams(dimension_semantics=sem, vmem_limit_bytes=VMEM_LIMIT_BYTES)


def _dot(a, b):
    return jnp.dot(a, b, preferred_element_type=F32)


def _dot_nt(a, b):
    return lax.dot_general(a, b, (((1,), (1,)), ((), ())), preferred_element_type=F32)


def _sigmoid(x):
    return 1.0 / (1.0 + jnp.exp(-x))


def _residual_layer_norm(x, y, g, b):
    v = ALPHA * x + y
    mu = jnp.mean(v, axis=-1, keepdims=True)
    d = v - mu
    var = jnp.mean(d * d, axis=-1, keepdims=True)
    return d * lax.rsqrt(var + LN_EPS) * g + b


def _ada_kernel(c_ref, w_ref, b_ref, o_ref):
    c = c_ref[...]
    cs = c * _sigmoid(c)
    o_ref[...] = jnp.dot(cs, w_ref[...], preferred_element_type=F32,
                         precision=lax.Precision.HIGHEST) + b_ref[...]


def _ada_mod(c_pad, ada_w, ada_b):
    nl = ada_w.shape[0]
    ncol = ada_w.shape[2] // D_MODEL
    return pl.pallas_call(
        _ada_kernel,
        grid=(nl, ncol),
        in_specs=[
            pl.BlockSpec((8, D_MODEL), lambda l, j: (0, 0)),
            pl.BlockSpec((None, D_MODEL, D_MODEL), lambda l, j: (l, 0, j)),
            pl.BlockSpec((None, 1, D_MODEL), lambda l, j: (l, 0, j)),
        ],
        out_specs=pl.BlockSpec((None, 8, D_MODEL), lambda l, j: (l, 0, j)),
        out_shape=jax.ShapeDtypeStruct((nl, 8, ada_w.shape[2]), F32),
        compiler_params=_cparams(("arbitrary", "arbitrary")),
        name="ada_mod",
    )(c_pad, ada_w, ada_b.reshape(nl, 1, -1))


def _ffn_tile(x, mod_ref, w13_ref, w2_ref, lng_ref, lnb_ref, h_ref):
    shift, scale, gate = mod_ref[0:1, :], mod_ref[1:2, :], mod_ref[2:3, :]
    u = (x * (1.0 + scale) + shift).astype(BF16)
    for j in range(FFN_NCHUNK):
        a = _dot(u, w13_ref[:, j * FFN_TF:(j + 1) * FFN_TF])
        g = _dot(u, w13_ref[:, D_FF + j * FFN_TF:D_FF + (j + 1) * FFN_TF])
        h_ref[:, j * FFN_TF:(j + 1) * FFN_TF] = (a * _sigmoid(a) * g).astype(BF16)
    y = _dot(h_ref[...], w2_ref[...])
    return _residual_layer_norm(x, 0.5 * gate * y, lng_ref[...], lnb_ref[...])


def _inproj_tile(x, mod_ref, w_ref, wuq_ref, wiq_ref, outs):
    (aq_ref, akv_ref, bq_ref, bqi_ref, bkv_ref, bki_ref, cqk_ref, cv_ref, co_ref, du_ref, sm_ref) = outs
    shift, scale = mod_ref[0:1, :], mod_ref[1:2, :]
    u = (x * (1.0 + scale) + shift).astype(BF16)
    z = _dot(u, w_ref[...])
    aq_ref[...] = z[:, ZC_QA:ZC_KVA].astype(BF16)
    akv_ref[...] = z[:, ZC_KVA:ZC_CQ].astype(BF16)
    cq = z[:, ZC_CQ:ZC_KVB].astype(BF16)
    bq_ref[...] = (_dot(cq, wuq_ref[...]) * HEAD_DIM ** -0.5).astype(BF16)
    bqi_ref[...] = _dot(cq, wiq_ref[...]).astype(BF16)
    bkv_ref[...] = z[:, ZC_KVB:ZC_KI].astype(BF16)
    bki_ref[...] = z[:, ZC_KI:ZC_QKC].astype(BF16)
    cqk_ref[...] = z[:, ZC_QKC:ZC_VC]
    cv_ref[...] = z[:, ZC_VC:ZC_OC].astype(BF16)
    co_ref[...] = z[:, ZC_OC:ZC_UD]
    du_ref[...] = z[:, ZC_UD:ZC_SMALL]
    sm_ref[...] = z[:, ZC_SMALL:Z_WIDTH]


INPROJ_OUTPUTS = ((256, BF16), (256, BF16), (256, BF16), (128, BF16), (128, BF16), (128, BF16),
                  (512, F32), (256, BF16), (256, F32), (256, F32), (128, F32))


def _ffn_inproj_kernel(x_ref, modf_ref, w13_ref, w2_ref, lng_ref, lnb_ref, modm_ref, win_ref, wuq_ref, wiq_ref,
                       o_ref, *rest):
    outs, h_ref = rest[:-1], rest[-1]
    x_new = _ffn_tile(x_ref[...], modf_ref, w13_ref, w2_ref, lng_ref, lnb_ref, h_ref)
    o_ref[...] = x_new
    _inproj_tile(x_new, modm_ref, win_ref, wuq_ref, wiq_ref, outs)


def _ffn_inproj(x2, mod_ffn, w13c, w2c, ln_g, ln_b, mod_mix, w_perm, w_uq, w_iq, rows_per_batch):
    n = x2.shape[0]
    tm = ROW_TILE
    tiles_per_batch = rows_per_batch // tm
    row = pl.BlockSpec((tm, D_MODEL), lambda i: (i, 0))
    mod = pl.BlockSpec((None, 8, D_MODEL), lambda i: (i // tiles_per_batch, 0, 0))
    vec = pl.BlockSpec((1, D_MODEL), lambda i: (0, 0))
    resident = lambda a: pl.BlockSpec(a.shape, lambda i: (0,) * a.ndim, pipeline_mode=pl.Buffered(1))
    outs = pl.pallas_call(
        _ffn_inproj_kernel,
        grid=(n // tm,),
        in_specs=[row, mod, resident(w13c), resident(w2c), vec, vec, mod, resident(w_perm), resident(w_uq),
                  resident(w_iq)],
        out_specs=[row] + [pl.BlockSpec((tm, w), lambda i: (i, 0)) for w, _ in INPROJ_OUTPUTS],
        out_shape=[jax.ShapeDtypeStruct((n, D_MODEL), F32)]
        + [jax.ShapeDtypeStruct((n, w), dt) for w, dt in INPROJ_OUTPUTS],
        scratch_shapes=[pltpu.VMEM((tm, D_FF), BF16)],
        compiler_params=_cparams(("arbitrary",)),
        name="ffn_inproj",
    )(x2, mod_ffn, w13c, w2c, ln_g.reshape(1, -1), ln_b.reshape(1, -1), mod_mix, w_perm, w_uq, w_iq)
    return outs[0], outs[1:]


def _swa_kernel(sink_ref, q_ref, kvc_ref, kvp_ref, o_ref):
    i = pl.program_id(1)
    tq = SWA_TQ
    q = q_ref[...]
    kv = jnp.concatenate([kvp_ref[...], kvc_ref[...]], axis=0)
    qpos = i * tq + lax.broadcasted_iota(I32, (tq, 2 * tq), 0)
    kpos = (i - 1) * tq + lax.broadcasted_iota(I32, (tq, 2 * tq), 1)
    qchunk = qpos // CHUNK
    kchunk = (kpos + tq) // CHUNK - tq // CHUNK
    valid = (kpos >= 0) & (kchunk <= qchunk) & (kchunk >= qchunk - SWA_WIN_CHUNKS)
    dist = jnp.abs(qpos - kpos).astype(F32)
    rep = SWA_HEADS // SWA_KV_HEADS
    for h in range(SWA_HEADS):
        g = h // rep
        qh = q[:, h * HEAD_DIM:(h + 1) * HEAD_DIM]
        kg = kv[:, g * HEAD_DIM:(g + 1) * HEAD_DIM]
        vg = kv[:, (SWA_KV_HEADS + g) * HEAD_DIM:(SWA_KV_HEADS + g + 1) * HEAD_DIM]
        s = _dot_nt(qh, kg) * HEAD_DIM ** -0.5 - SLOPES_A[h] * dist
        s = jnp.where(valid, s, NEG_INF)
        sink = sink_ref[h]
        m = jnp.maximum(jnp.max(s, axis=-1, keepdims=True), sink)
        p = jnp.exp(s - m)
        denom = jnp.sum(p, axis=-1, keepdims=True) + jnp.exp(sink - m)
        o = _dot(p.astype(BF16), vg) / denom
        o_ref[:, h * HEAD_DIM:(h + 1) * HEAD_DIM] = o.astype(BF16)


def _swa(sinks, a_q, a_kv, bn, t):
    nt = t // SWA_TQ
    return pl.pallas_call(
        _swa_kernel,
        grid=(bn, nt),
        in_specs=[
            pl.BlockSpec(memory_space=pltpu.SMEM),
            pl.BlockSpec((SWA_TQ, 256), lambda b, i: (b * nt + i, 0)),
            pl.BlockSpec((SWA_TQ, 256), lambda b, i: (b * nt + i, 0)),
            pl.BlockSpec((SWA_TQ, 256), lambda b, i: (b * nt + jnp.maximum(i - 1, 0), 0)),
        ],
        out_specs=pl.BlockSpec((SWA_TQ, 256), lambda b, i: (b * nt + i, 0)),
        out_shape=jax.ShapeDtypeStruct((bn * t, 256), BF16),
        compiler_params=_cparams(("arbitrary", "arbitrary")),
        name="swa_attention",
    )(sinks, a_q, a_kv, a_kv)


def _sortable_key(x):
    bits = lax.bitcast_convert_type(x, I32)
    return bits ^ ((bits >> 31) & 0x7FFFFFFF)


def _dsa_kernel(q_ref, qi_ref, sm_ref, kv_ref, vt_ref, ki_ref, tril_ref, o_ref, key_ref, s_ref, p_ref, w1_ref, w2_ref):
    i = pl.program_id(1)
    tq, tk = DSA_TQ, DSA_TK
    nblk = (i * tq + tq + tk - 1) // tk
    qpos = i * tq + lax.broadcasted_iota(I32, (1, tq), 1)
    qchunk = qpos // CHUNK
    row_k = lax.broadcasted_iota(I32, (tk, tq), 0)

    qi = qi_ref[...]
    lane_i = lax.broadcasted_iota(I32, (1, IDX_HEADS * IDX_DIM), 1) // IDX_DIM
    qi_stack = jnp.concatenate([jnp.where(lane_i == h, qi, jnp.zeros_like(qi)) for h in range(IDX_HEADS)], axis=0)
    sm_t = sm_ref[...].T
    w_idx = [sm_t[SM_WI + h:SM_WI + h + 1, :] for h in range(IDX_HEADS)]
    idx_scale = (IDX_DIM * IDX_HEADS) ** -0.5
    q = q_ref[...]
    q_stack = jnp.concatenate([q[:, h * HEAD_DIM:(h + 1) * HEAD_DIM] for h in range(DSA_HEADS)], axis=0)

    half = tk // 2
    guard = jnp.int32(FIELD_GUARD)

    def pack_fields(f):
        return (f[0:half] << 16) | f[half:tk] | guard

    def score_block(j, mask_inadmissible):
        rows = pl.ds(pl.multiple_of(j * tk, tk), tk)
        s_ref[j] = _dot_nt(kv_ref[rows, 0:HEAD_DIM], q_stack)
        d = _dot_nt(ki_ref[rows, :], qi_stack)
        acc = w_idx[0] * jnp.maximum(d[:, 0:tq], 0.0)
        for h in range(1, IDX_HEADS):
            acc = acc + w_idx[h] * jnp.maximum(d[:, h * tq:(h + 1) * tq], 0.0)
        sc = acc * idx_scale
        if mask_inadmissible:
            sc = jnp.where((j * tk + row_k) // CHUNK <= qchunk, sc, NEG_INF)
        key = _sortable_key(sc)
        key_ref[j] = key
        w1_ref[j] = pack_fields(lax.shift_right_logical(key ^ INT_MIN, 32 - FIELD_BITS))

    def full_block(j, carry):
        score_block(j, False)
        return carry

    lax.fori_loop(0, nblk - 1, full_block, 0)
    score_block(nblk - 1, True)

    k_eff = jnp.minimum(DSA_TOPK, (qchunk + 1) * CHUNK)

    def over_blocks(body, init):
        c = lax.fori_loop(0, nblk // 2, lambda jj, c: body(2 * jj + 1, body(2 * jj, c)), init)
        return lax.cond(nblk % 2 == 1, lambda c: body(nblk - 1, c), lambda c: c, c)

    def count(pred):
        def blk(j, c):
            m = pred(key_ref[j], j * tk + row_k).astype(I32)
            return c + jnp.sum(m.reshape(tk // 8, 8, tq), axis=0)
        return jnp.sum(over_blocks(blk, jnp.zeros((8, tq), I32)), axis=0, keepdims=True)

    def count_fields(w_ref, cand):
        cand2 = (cand << 16) | cand
        def blk(j, c):
            hit = ((w_ref[j] - cand2) >> 15) & 0x00010001
            return c + jnp.sum(hit.reshape(half // 8, 8, tq), axis=0)
        c = over_blocks(blk, jnp.zeros((8, tq), I32))
        return jnp.sum((c & 0xFFFF) + (c >> 16), axis=0, keepdims=True)

    def field_search(w_ref, k_want):
        def step(bi, carry):
            prefix, above = carry
            cand = prefix | (jnp.int32(1) << (FIELD_BITS - 1 - bi))
            cnt = count_fields(w_ref, cand)
            ok = cnt >= k_want
            return jnp.where(ok, cand, prefix), jnp.where(ok, above, cnt)
        return lax.fori_loop(0, FIELD_BITS, step, (jnp.zeros((1, tq), I32), jnp.zeros((1, tq), I32)))

    top, above = field_search(w1_ref, k_eff)
    field_max = (1 << FIELD_BITS) - 1

    def pack_mid(j, carry):
        ukey = key_ref[j] ^ INT_MIN
        member = lax.shift_right_logical(ukey, 32 - FIELD_BITS) == top
        mid = lax.shift_right_logical(ukey, 32 - 2 * FIELD_BITS) & field_max
        w2_ref[j] = pack_fields(jnp.where(member, mid, 0))
        return carry

    lax.fori_loop(0, nblk, pack_mid, 0)
    mid, above_mid = field_search(w2_ref, k_eff - above)

    def bit_step(bi, carry):
        prefix, c_gt = carry
        cand_u = prefix | (jnp.int32(1) << (31 - 2 * FIELD_BITS - bi))
        cand_s = cand_u ^ INT_MIN
        cnt = count(lambda key, kpos: key >= cand_s)
        ok = cnt >= k_eff
        return jnp.where(ok, cand_u, prefix), jnp.where(ok, c_gt, cnt)

    prefix, c_gt = lax.fori_loop(0, 32 - 2 * FIELD_BITS, bit_step,
                                 ((top << (32 - FIELD_BITS)) | (mid << (32 - 2 * FIELD_BITS)), above + above_mid))
    thr = prefix ^ INT_MIN

    need = (k_eff - c_gt).astype(F32)
    tril = tril_ref[...]

    offs = (row_k - qpos).astype(F32)
    p_ref[1] = jnp.zeros(p_ref.shape[1:], BF16)

    def att_block(j, carry):
        m_run, l_run, acc_part, ties_seen = carry
        slot = j % 2
        acc = acc_part + _dot(vt_ref[jnp.maximum(j - 1, 0)], p_ref[1 - slot])
        key = key_ref[j]
        tie = key == thr
        tie_rank = _dot(tril, jnp.where(tie, 1.0, 0.0).astype(BF16)) + ties_seen
        sel = (key > thr) | (tie & (tie_rank <= need))
        dist = jnp.where(sel, jnp.abs(offs + (j * tk).astype(F32)), jnp.inf)
        ms, ls, alphas = [], [], []
        for h in range(DSA_HEADS):
            cols = slice(h * tq, (h + 1) * tq)
            sh = s_ref[j, :, cols] - SLOPES_B[h] * dist
            m_old = m_run[:, cols]
            m_new = jnp.maximum(m_old, jnp.max(sh, axis=0, keepdims=True))
            alpha = jnp.exp(m_old - m_new)
            p = jnp.exp(sh - m_new)
            p_ref[slot, :, cols] = p.astype(BF16)
            ms.append(m_new)
            ls.append(alpha * l_run[:, cols] + jnp.sum(p, axis=0, keepdims=True))
            alphas.append(alpha)
        return (jnp.concatenate(ms, axis=1), jnp.concatenate(ls, axis=1), jnp.concatenate(alphas, axis=1) * acc,
                tie_rank[tk - 1:tk, :])

    init = (jnp.full((1, DSA_HEADS * tq), NEG_INF, F32), jnp.zeros((1, DSA_HEADS * tq), F32),
            jnp.zeros((HEAD_DIM, DSA_HEADS * tq), F32), jnp.zeros((1, tq), F32))
    _, l_run, acc_part, _ = lax.fori_loop(0, nblk, att_block, init)
    acc = acc_part + _dot(vt_ref[nblk - 1], p_ref[(nblk - 1) % 2])
    out = acc / l_run
    o_ref[...] = jnp.concatenate([out[:, h * tq:(h + 1) * tq] for h in range(DSA_HEADS)], axis=0).astype(BF16)


def _dsa(b_q, b_qi, small, b_kv, b_ki, bn, t):
    nt = t // DSA_TQ
    nkb = t // DSA_TK
    v_t = b_kv[:, HEAD_DIM:].reshape(bn * nkb, DSA_TK, HEAD_DIM).transpose(0, 2, 1)
    o_t = pl.pallas_call(
        _dsa_kernel,
        grid=(bn, nt),
        in_specs=[
            pl.BlockSpec((DSA_TQ, 256), lambda b, i: (b * nt + i, 0)),
            pl.BlockSpec((DSA_TQ, 128), lambda b, i: (b * nt + i, 0)),
            pl.BlockSpec((DSA_TQ, 128), lambda b, i: (b * nt + i, 0)),
            pl.BlockSpec((t, 128), lambda b, i: (b, 0)),
            pl.BlockSpec((nkb, HEAD_DIM, DSA_TK), lambda b, i: (b, 0, 0)),
            pl.BlockSpec((t, 128), lambda b, i: (b, 0)),
            pl.BlockSpec((DSA_TK, DSA_TK), lambda b, i: (0, 0)),
        ],
        out_specs=pl.BlockSpec((None, DSA_HEADS * HEAD_DIM, DSA_TQ), lambda b, i: (b * nt + i, 0, 0)),
        out_shape=jax.ShapeDtypeStruct((bn * nt, DSA_HEADS * HEAD_DIM, DSA_TQ), BF16),
        scratch_shapes=[pltpu.VMEM((nkb, DSA_TK, DSA_TQ), I32),
                        pltpu.VMEM((nkb, DSA_TK, DSA_HEADS * DSA_TQ), F32),
                        pltpu.VMEM((2, DSA_TK, DSA_HEADS * DSA_TQ), BF16),
                        pltpu.VMEM((nkb, DSA_TK // 2, DSA_TQ), I32), pltpu.VMEM((nkb, DSA_TK // 2, DSA_TQ), I32)],
        compiler_params=_cparams(("arbitrary", "arbitrary")),
        name="dsa_attention",
    )(b_q, b_qi, small, b_kv, v_t, b_ki, jnp.tril(jnp.ones((DSA_TK, DSA_TK), BF16)))
    return o_t.transpose(0, 2, 1).reshape(bn * t, DSA_HEADS * HEAD_DIM)


def _mlstm_kernel(qk_ref, v_ref, og_ref, sm_ref, convw_ref, convb_ref, gbias_ref, normgt_ref,
                  o_ref, tail_ref, ct_ref, nvec_ref, mst_ref, *, bn):
    c = pl.program_id(0)
    L = CHUNK
    nh, dh, width = MLSTM_HEADS, HEAD_DIM, MLSTM_HEADS * HEAD_DIM

    @pl.when(c == 0)
    def _():
        tail_ref[...] = jnp.zeros_like(tail_ref)
        ct_ref[...] = jnp.zeros_like(ct_ref)
        nvec_ref[...] = jnp.zeros_like(nvec_ref)
        mst_ref[...] = jnp.zeros_like(mst_ref)

    srow = lax.broadcasted_iota(I32, (L, width), 0)
    lane = lax.broadcasted_iota(I32, (L, width), 1)
    jlane = lane % dh
    causal_t = srow <= jlane
    diag_t = srow == jlane
    head_of_lane = lax.broadcasted_iota(I32, (1, width), 1) // dh
    tril = (lax.broadcasted_iota(I32, (L, L), 1) <= lax.broadcasted_iota(I32, (L, L), 0)).astype(F32)
    erow = lax.broadcasted_iota(I32, (128, width), 0)
    ecol_head = lax.broadcasted_iota(I32, (128, width), 1) // dh
    expand_ig = (erow == SM_IG + ecol_head).astype(F32)
    expand_fg = (erow == SM_FG + ecol_head).astype(F32)
    exact = dict(preferred_element_type=F32, precision=lax.Precision.HIGHEST)

    def head_blocks(a):
        out = jnp.where(head_of_lane == 0, a[0:dh], 0.0)
        for h in range(1, nh):
            out = out + jnp.where(head_of_lane == h, a[h * dh:(h + 1) * dh], 0.0)
        return out

    convw = convw_ref[...]
    for b in range(bn):
        cur = qk_ref[b]
        ext = jnp.concatenate([tail_ref[b], cur], axis=0)
        tail_ref[b] = cur[L - 8:L, :]
        y = convb_ref[...] + convw[MLSTM_CONV - 1:MLSTM_CONV, :] * cur
        for k in range(MLSTM_CONV - 1):
            off = 8 - (MLSTM_CONV - 1) + k
            y = y + convw[k:k + 1, :] * ext[off:off + L, :]
        qk = y * _sigmoid(y)
        q_all = qk[:, 0:width]
        k_all = qk[:, width:2 * width] * dh ** -0.5
        q_stack = jnp.concatenate([jnp.where(head_of_lane == h, q_all, 0.0) for h in range(nh)], axis=0).astype(BF16)
        v_all = v_ref[b]
        v_t = v_all.astype(F32).T.astype(BF16)

        gates = sm_ref[b] + gbias_ref[...]
        lf = jnp.minimum(gates, 0.0) - jnp.log(1.0 + jnp.exp(-jnp.abs(gates)))
        bcum = jnp.dot(tril, lf, **exact)
        ig_x = jnp.dot(gates, expand_ig, **exact)
        b_x = jnp.dot(bcum, expand_fg, **exact)
        b_q = jnp.sum(jnp.where(diag_t, b_x, 0.0), axis=0, keepdims=True)
        b_last = b_x[L - 1:L, :]
        m_prev = mst_ref[b]
        ct = ct_ref[b]
        nvec = nvec_ref[b]

        dlog = jnp.where(causal_t, b_q - b_x + ig_x, NEG_INF)
        inter = b_q + m_prev
        mj = jnp.maximum(inter, jnp.max(dlog, axis=0, keepdims=True))
        dw = jnp.exp(dlog - mj)
        iw = jnp.exp(inter - mj)
        sc = _dot_nt(k_all.astype(BF16), q_stack) * dw
        qn = _dot_nt(jnp.broadcast_to(nvec, (8, width)).astype(BF16), q_stack)[0:1, :]
        q_c = _dot_nt(ct.astype(BF16), q_stack)
        num = iw * q_c + head_blocks(_dot(v_t, sc.astype(BF16)))
        den = iw * qn + jnp.sum(sc, axis=0, keepdims=True)
        hj = num / jnp.maximum(jnp.abs(den), jnp.exp(-mj))

        dec = b_last - b_x + ig_x
        m_new = jnp.maximum(b_last + m_prev, jnp.max(dec, axis=0, keepdims=True))
        wc = jnp.exp(b_last + m_prev - m_new)
        kw = k_all * jnp.exp(dec - m_new)
        ct_ref[b] = wc * ct + head_blocks(_dot(v_t, kw.astype(BF16)))
        nvec_ref[b] = wc * nvec + jnp.sum(kw, axis=0, keepdims=True)
        mst_ref[b] = m_new

        mu = jnp.mean(hj, axis=0, keepdims=True)
        dev = hj - mu
        var = jnp.mean(dev * dev, axis=0, keepdims=True)
        hn_t = (dev * lax.rsqrt(var + LN_EPS) * normgt_ref[...]).T
        hn = jnp.concatenate([hn_t[h * dh:(h + 1) * dh, :] for h in range(nh)], axis=1)
        o_ref[b] = (_sigmoid(og_ref[b]) * hn).astype(BF16)


def _mlstm(c_qk, c_v, c_o, small, conv_w, conv_b, gate_bias, norm_g, bn, t):
    nc = t // CHUNK
    width = MLSTM_HEADS * HEAD_DIM
    norm_g_t = jnp.repeat(norm_g.reshape(MLSTM_HEADS, HEAD_DIM).T, HEAD_DIM, axis=1)
    blk = lambda w: pl.BlockSpec((bn, CHUNK, w), lambda c: (0, c, 0))
    full = lambda a: pl.BlockSpec(a.shape, lambda c: (0,) * a.ndim)
    return pl.pallas_call(
        functools.partial(_mlstm_kernel, bn=bn),
        grid=(nc,),
        in_specs=[blk(512), blk(256), blk(256), blk(128), full(conv_w), full(conv_b), full(gate_bias), full(norm_g_t)],
        out_specs=blk(256),
        out_shape=jax.ShapeDtypeStruct((bn, t, 256), BF16),
        scratch_shapes=[pltpu.VMEM((bn, 8, 512), F32), pltpu.VMEM((bn, HEAD_DIM, width), F32),
                        pltpu.VMEM((bn, 1, width), F32), pltpu.VMEM((bn, 1, width), F32)],
        compiler_params=_cparams(("arbitrary",)),
        name="mlstm",
    )(c_qk.reshape(bn, t, 512), c_v.reshape(bn, t, 256), c_o.reshape(bn, t, 256), small.reshape(bn, t, 128),
      conv_w, conv_b, gate_bias, norm_g_t)


def _s5_kernel(u_ref, perm_ref, bbre_ref, bbim_ref, pwre_ref, pwim_ref, cre_ref, cim_ref, dskip_ref, gluw_ref,
               glub_ref, o_ref, sre_ref, sim_ref, stre_ref, stim_ref, yp_ref):
    @pl.when(pl.program_id(1) == 0)
    def _():
        stre_ref[...] = jnp.zeros_like(stre_ref)
        stim_ref[...] = jnp.zeros_like(stim_ref)

    seg_len = S5_TT // S5_SEGS
    u = u_ref[...]
    ub = _dot(perm_ref[...], u.astype(BF16)).astype(BF16)
    sre_ref[...] = _dot(ub, bbre_ref[...])
    sim_ref[...] = _dot(ub, bbim_ref[...])
    a_re = jnp.broadcast_to(pwre_ref[0:1, :], (S5_SEGS, S5_LANES))
    a_im = jnp.broadcast_to(pwim_ref[0:1, :], (S5_SEGS, S5_LANES))

    def local_step(i, carry):
        s_re, s_im = carry
        rows = pl.ds(pl.multiple_of(i * S5_SEGS, S5_SEGS), S5_SEGS)
        n_re = a_re * s_re - a_im * s_im + sre_ref[rows, :]
        n_im = a_re * s_im + a_im * s_re + sim_ref[rows, :]
        sre_ref[rows, :] = n_re
        sim_ref[rows, :] = n_im
        return n_re, n_im

    zeros = jnp.zeros((S5_SEGS, S5_LANES), F32)
    e_re, e_im = lax.fori_loop(0, seg_len, local_step, (zeros, zeros), unroll=4)

    al_re, al_im = pwre_ref[seg_len - 1:seg_len, :], pwim_ref[seg_len - 1:seg_len, :]
    c_re, c_im = stre_ref[0:1, :], stim_ref[0:1, :]
    cs_re, cs_im = [], []
    for k in range(S5_SEGS):
        cs_re.append(c_re)
        cs_im.append(c_im)
        c_re, c_im = (e_re[k:k + 1, :] + al_re * c_re - al_im * c_im,
                      e_im[k:k + 1, :] + al_re * c_im + al_im * c_re)
    stre_ref[...] = jnp.broadcast_to(c_re, stre_ref.shape)
    stim_ref[...] = jnp.broadcast_to(c_im, stim_ref.shape)
    cin_re = jnp.concatenate(cs_re, axis=0)
    cin_im = jnp.concatenate(cs_im, axis=0)

    def correct_step(i, carry):
        rows = pl.ds(pl.multiple_of(i * S5_SEGS, S5_SEGS), S5_SEGS)
        p_re, p_im = pwre_ref[pl.ds(i, 1), :], pwim_ref[pl.ds(i, 1), :]
        sre_ref[rows, :] = sre_ref[rows, :] + p_re * cin_re - p_im * cin_im
        sim_ref[rows, :] = sim_ref[rows, :] + p_re * cin_im + p_im * cin_re
        return carry

    lax.fori_loop(0, seg_len, correct_step, 0, unroll=4)
    ycs = _dot(sre_ref[...].astype(BF16), cre_ref[...]) - _dot(sim_ref[...].astype(BF16), cim_ref[...])
    ngrp = GROUP_WIDTH // 128
    for g in range(ngrp):
        yp_ref[g] = ycs[:, g * 128:(g + 1) * 128]
    blocks = []
    for k in range(S5_SEGS):
        for i0 in range(0, seg_len, 8):
            rows = pl.ds(i0 * S5_SEGS + k, 8, stride=S5_SEGS)
            blocks.append(jnp.concatenate([yp_ref[g, rows, :] for g in range(ngrp)], axis=1))
    y = jnp.concatenate(blocks, axis=0) + dskip_ref[...] * u
    y = 0.5 * y * (1.0 + jnp.tanh(math.sqrt(2.0 / math.pi) * (y + 0.044715 * (y * y * y))))
    z = _dot(y.astype(BF16), gluw_ref[...]) + glub_ref[...]
    o_ref[...] = (y * _sigmoid(z)).astype(BF16)


def _s5(d_u, bb_re, bb_im, pw_re, pw_im, c_re_t, c_im_t, d_skip, glu_w, glu_b, bn, t):
    nt = t // S5_TT
    full = lambda a: pl.BlockSpec(a.shape, lambda b, i: (0,) * a.ndim)
    r = jnp.arange(S5_TT)
    perm = (r[None, :] == ((r % S5_SEGS) * (S5_TT // S5_SEGS) + r // S5_SEGS)[:, None]).astype(BF16)
    args = (perm, bb_re, bb_im, pw_re, pw_im, c_re_t, c_im_t, d_skip, glu_w, glu_b)
    return pl.pallas_call(
        _s5_kernel,
        grid=(bn, nt),
        in_specs=[pl.BlockSpec((S5_TT, 256), lambda b, i: (b * nt + i, 0))] + [full(a) for a in args],
        out_specs=pl.BlockSpec((S5_TT, 256), lambda b, i: (b * nt + i, 0)),
        out_shape=jax.ShapeDtypeStruct((bn * t, 256), BF16),
        scratch_shapes=[pltpu.VMEM((S5_TT, S5_LANES), F32), pltpu.VMEM((S5_TT, S5_LANES), F32),
                        pltpu.VMEM((8, S5_LANES), F32), pltpu.VMEM((8, S5_LANES), F32),
                        pltpu.VMEM((GROUP_WIDTH // 128, S5_TT, 128), F32)],
        compiler_params=_cparams(("arbitrary", "arbitrary")),
        name="s5_glu",
    )(d_u, *args)


def _outproj_ffn_kernel(x_ref, modm_ref, oa_ref, ob_ref, oc_ref, od_ref, wout_ref, lngm_ref, lnbm_ref,
                        modf_ref, w13_ref, w2_ref, lngf_ref, lnbf_ref, o_ref, h_ref):
    y = _dot(oa_ref[...], wout_ref[0])
    y = y + _dot(ob_ref[...], wout_ref[1])
    y = y + _dot(oc_ref[...], wout_ref[2])
    y = y + _dot(od_ref[...], wout_ref[3])
    x_mid = _residual_layer_norm(x_ref[...], modm_ref[2:3, :] * y, lngm_ref[...], lnbm_ref[...])
    o_ref[...] = _ffn_tile(x_mid, modf_ref, w13_ref, w2_ref, lngf_ref, lnbf_ref, h_ref)


def _outproj_ffn(x2, mod_mix, o_a, o_b, o_c, o_d, w_out4, ln_g_mix, ln_b_mix, mod_ffn, w13c, w2c, ln_g_ffn, ln_b_ffn,
                 rows_per_batch):
    n = x2.shape[0]
    tm = ROW_TILE
    tiles_per_batch = rows_per_batch // tm
    row = pl.BlockSpec((tm, D_MODEL), lambda i: (i, 0))
    mod = pl.BlockSpec((None, 8, D_MODEL), lambda i: (i // tiles_per_batch, 0, 0))
    mix = pl.BlockSpec((tm, GROUP_WIDTH), lambda i: (i, 0))
    vec = pl.BlockSpec((1, D_MODEL), lambda i: (0, 0))
    resident = lambda a: pl.BlockSpec(a.shape, lambda i: (0,) * a.ndim, pipeline_mode=pl.Buffered(1))
    return pl.pallas_call(
        _outproj_ffn_kernel,
        grid=(n // tm,),
        in_specs=[row, mod, mix, mix, mix, mix, resident(w_out4), vec, vec, mod, resident(w13c), resident(w2c), vec, vec],
        out_specs=row,
        out_shape=jax.ShapeDtypeStruct((n, D_MODEL), F32),
        scratch_shapes=[pltpu.VMEM((tm, D_FF), BF16)],
        compiler_params=_cparams(("arbitrary",)),
        name="outproj_ffn",
    )(x2, mod_mix, o_a, o_b, o_c, o_d, w_out4, ln_g_mix.reshape(1, -1), ln_b_mix.reshape(1, -1),
      mod_ffn, w13c, w2c, ln_g_ffn.reshape(1, -1), ln_b_ffn.reshape(1, -1))


def _permute_w_in(w_in):
    off = {}
    o = 0
    for name, s in (("qa", 256), ("ka", 128), ("va", 128), ("cq", 128), ("kb", 64), ("vb", 64), ("ki", 32),
                    ("wi", 4), ("qkc", 512), ("vc", 256), ("ig", 4), ("fg", 4), ("oc", 256), ("ud", 256)):
        off[name] = (o, o + s)
        o += s
    col = lambda n: w_in[:, off[n][0]:off[n][1]]
    small = jnp.concatenate([col("wi"), col("ig"), col("fg"),
                             jnp.zeros((w_in.shape[0], 128 - 12), w_in.dtype)], axis=1)
    parts = [col("qa"), col("ka"), col("va"), col("cq"), col("kb"), col("vb")] + [col("ki")] * 4 + [
        col("qkc"), col("vc"), col("oc"), col("ud"), small]
    return jnp.concatenate(parts, axis=1).astype(BF16)


def _s5_params(lam_re, lam_im, log_step, b_re, b_im, c_re, c_im):
    dt = jnp.exp(log_step)[:, None]
    mag = jnp.exp(lam_re * dt)
    a_re, a_im = mag * jnp.cos(lam_im * dt), mag * jnp.sin(lam_im * dt)
    den = lam_re * lam_re + lam_im * lam_im
    kap_re = ((a_re - 1.0) * lam_re + a_im * lam_im) / den
    kap_im = (a_im * lam_re - (a_re - 1.0) * lam_im) / den
    bb_re = kap_re[..., None] * b_re - kap_im[..., None] * b_im
    bb_im = kap_re[..., None] * b_im + kap_im[..., None] * b_re
    eye = jnp.eye(S5_GROUPS, dtype=F32)

    def in_mat(bb):
        return jnp.einsum("gph,gk->ghkp", bb, eye).reshape(S5_GROUPS * S5_GROUP_CH, S5_LANES).astype(BF16)

    def out_mat(cc):
        return jnp.einsum("gop,gk->gpko", cc, eye).reshape(S5_LANES, S5_GROUPS * S5_GROUP_CH).astype(BF16)

    n = jnp.arange(1, S5_TT // S5_SEGS + 1, dtype=F32)[:, None, None]
    pw_mag = jnp.exp(n * (lam_re * dt))
    pw_re = (pw_mag * jnp.cos(n * (lam_im * dt))).at[0].set(a_re).reshape(-1, S5_LANES)
    pw_im = (pw_mag * jnp.sin(n * (lam_im * dt))).at[0].set(a_im).reshape(-1, S5_LANES)
    return in_mat(bb_re), in_mat(bb_im), pw_re, pw_im, out_mat(c_re), out_mat(c_im)


def kernel(x, c, ada_w, ada_b, ln_g, ln_b, ffn_w13, ffn_w2, w_in, w_out, sinks, w_uq, w_iq, conv_w, conv_b, ig_b,
           fg_b, mh_norm_g, lam_re, lam_im, log_step, b_re, b_im, c_re, c_im, d_skip, glu_w, glu_b):
    bn, t, d = x.shape
    assert d == D_MODEL and t % max(ROW_TILE, DSA_TK, S5_TT) == 0 and bn <= 8
    n = bn * t
    nl = ada_w.shape[0]
    c_pad = jnp.zeros((8, d), F32).at[:bn].set(c)
    mod_all = _ada_mod(c_pad, ada_w, ada_b)
    mod_all = mod_all[:, :bn].reshape(nl, bn, N_SUB, 3, d).transpose(0, 2, 1, 3, 4)
    mod_all = jnp.pad(mod_all, ((0, 0), (0, 0), (0, 0), (0, 5), (0, 0)))

    x2 = x.reshape(n, d)
    for l in range(nl):
        w13c = ffn_w13[l].astype(BF16)
        w2c = ffn_w2[l].astype(BF16)
        x2, (a_q, a_kv, b_q, b_qi, b_kv, b_ki, c_qk, c_v, c_o, d_u, small) = _ffn_inproj(
            x2, mod_all[l, 0], w13c[0], w2c[0], ln_g[l, 0], ln_b[l, 0], mod_all[l, 1],
            _permute_w_in(w_in[l]), w_uq[l].astype(BF16), w_iq[l].astype(BF16), t)
        o_a = _swa(sinks[l], a_q, a_kv, bn, t)
        o_b = _dsa(b_q, b_qi, small, b_kv, b_ki, bn, t)
        gate_bias = jnp.zeros((1, 128), F32).at[0, SM_IG:SM_IG + 4].set(ig_b[l]).at[0, SM_FG:SM_FG + 4].set(fg_b[l])
        o_c = _mlstm(c_qk, c_v, c_o, small, conv_w[l], conv_b[l].reshape(1, -1), gate_bias,
                     mh_norm_g[l].reshape(1, -1), bn, t).reshape(n, GROUP_WIDTH)
        s5p = _s5_params(lam_re[l], lam_im[l], log_step[l], b_re[l], b_im[l], c_re[l], c_im[l])
        o_d = _s5(d_u, *s5p, d_skip[l].reshape(1, -1), glu_w[l].astype(BF16), glu_b[l].reshape(1, -1), bn, t)
        x2 = _outproj_ffn(x2, mod_all[l, 1], o_a, o_b, o_c, o_d, w_out[l].astype(BF16).reshape(4, GROUP_WIDTH, d),
                          ln_g[l, 1], ln_b[l, 1], mod_all[l, 2], w13c[1], w2c[1], ln_g[l, 2], ln_b[l, 2], t)
    return x2.reshape(bn, t, d)
```

```python
import functools
import math

import jax
import jax.numpy as jnp
from jax import lax
from jax.experimental import pallas as pl
from jax.experimental.pallas import tpu as pltpu

F32 = jnp.float32
BF16 = jnp.bfloat16
I32 = jnp.int32

D_MODEL = 1024
DEPTH = 2
CHUNK = 64
HEAD_DIM = 64
GROUP_WIDTH = 256
SWA_HEADS = 4
SWA_KV_HEADS = 2
SWA_WIN_CHUNKS = 2
DSA_HEADS = 4
DSA_Q_RANK = 128
IDX_HEADS = 4
IDX_DIM = 32
DSA_TOPK = 256
MLSTM_HEADS = 4
MLSTM_CONV = 4
S5_GROUP_CH = 16
S5_GROUPS = 16
S5_STATE = 64
S5_LANES = S5_GROUPS * S5_STATE
D_FF = 2816
N_SUB = 3
ALPHA = (2 * DEPTH) ** 0.25
LN_EPS = 1e-5
NEG_INF = -1e30
INT_MIN = -(2 ** 31)
FIELD_BITS = 15
FIELD_GUARD = -(2 ** 31) + 2 ** 15

SLOPES_A = tuple(2.0 ** -(i + 1) for i in range(0, 8, 2))
SLOPES_B = tuple(2.0 ** -(i + 1) for i in range(1, 8, 2))

VMEM_LIMIT_BYTES = 56 * 1024 * 1024

FFN_TF = 256
FFN_NCHUNK = D_FF // FFN_TF
ROW_TILE = 512
SWA_TQ = 128
SWA_SUB = 4
DSA_TQ = 128
DSA_TK = 512
S5_TT = 512
S5_SEGS = 8

ZC_QA, ZC_KVA, ZC_CQ, ZC_KVB, ZC_KI = 0, 256, 512, 640, 768
ZC_QKC, ZC_VC, ZC_OC, ZC_UD, ZC_SMALL, Z_WIDTH = 896, 1408, 1664, 1920, 2176, 2304
SM_WI, SM_IG, SM_FG = 0, 4, 8


def _cparams(sem):
    return pltpu.CompilerParams(dimension_semantics=sem, vmem_limit_bytes=VMEM_LIMIT_BYTES)


def _dot(a, b):
    return jnp.dot(a, b, preferred_element_type=F32)


def _dot_nt(a, b):
    return lax.dot_general(a, b, (((1,), (1,)), ((), ())), preferred_element_type=F32)


def _sigmoid(x):
    return 1.0 / (1.0 + jnp.exp(-x))


def _residual_layer_norm(x, y, g, b):
    v = ALPHA * x + y
    mu = jnp.mean(v, axis=-1, keepdims=True)
    d = v - mu
    var = jnp.mean(d * d, axis=-1, keepdims=True)
    return d * lax.rsqrt(var + LN_EPS) * g + b


def _ada_kernel(c_ref, w_ref, b_ref, o_ref):
    c = c_ref[...]
    cs = c * _sigmoid(c)
    o_ref[...] = jnp.dot(cs, w_ref[...], preferred_element_type=F32,
                         precision=lax.Precision.HIGHEST) + b_ref[...]


def _ada_mod(c_pad, ada_w, ada_b):
    nl = ada_w.shape[0]
    ncol = ada_w.shape[2] // D_MODEL
    return pl.pallas_call(
        _ada_kernel,
        grid=(nl, ncol),
        in_specs=[
            pl.BlockSpec((8, D_MODEL), lambda l, j: (0, 0)),
            pl.BlockSpec((None, D_MODEL, D_MODEL), lambda l, j: (l, 0, j)),
            pl.BlockSpec((None, 1, D_MODEL), lambda l, j: (l, 0, j)),
        ],
        out_specs=pl.BlockSpec((None, 8, D_MODEL), lambda l, j: (l, 0, j)),
        out_shape=jax.ShapeDtypeStruct((nl, 8, ada_w.shape[2]), F32),
        compiler_params=_cparams(("arbitrary", "arbitrary")),
        name="ada_mod",
    )(c_pad, ada_w, ada_b.reshape(nl, 1, -1))


def _ffn_tile(x, mod_ref, w13_ref, w2_ref, lng_ref, lnb_ref, h_ref):
    shift, scale, gate = mod_ref[0:1, :], mod_ref[1:2, :], mod_ref[2:3, :]
    u = (x * (1.0 + scale) + shift).astype(BF16)
    for j in range(FFN_NCHUNK):
        a = _dot(u, w13_ref[:, j * FFN_TF:(j + 1) * FFN_TF])
        g = _dot(u, w13_ref[:, D_FF + j * FFN_TF:D_FF + (j + 1) * FFN_TF])
        h_ref[:, j * FFN_TF:(j + 1) * FFN_TF] = (a * _sigmoid(a) * g).astype(BF16)
    y = _dot(h_ref[...], w2_ref[...])
    return _residual_layer_norm(x, 0.5 * gate * y, lng_ref[...], lnb_ref[...])


def _inproj_tile(x, mod_ref, w_ref, wuq_ref, wiq_ref, outs):
    (aq_ref, akv_ref, bq_ref, bqi_ref, bkv_ref, bki_ref, cqk_ref, cv_ref, co_ref, du_ref, sm_ref) = outs
    shift, scale = mod_ref[0:1, :], mod_ref[1:2, :]
    u = (x * (1.0 + scale) + shift).astype(BF16)
    z = _dot(u, w_ref[...])
    aq_ref[...] = z[:, ZC_QA:ZC_KVA].astype(BF16)
    akv_ref[...] = z[:, ZC_KVA:ZC_CQ].astype(BF16)
    cq = z[:, ZC_CQ:ZC_KVB].astype(BF16)
    bq_ref[...] = (_dot(cq, wuq_ref[...]) * HEAD_DIM ** -0.5).astype(BF16)
    bqi_ref[...] = _dot(cq, wiq_ref[...]).astype(BF16)
    bkv_ref[...] = z[:, ZC_KVB:ZC_KI].astype(BF16)
    bki_ref[...] = z[:, ZC_KI:ZC_QKC].astype(BF16)
    cqk_ref[...] = z[:, ZC_QKC:ZC_VC]
    cv_ref[...] = z[:, ZC_VC:ZC_OC].astype(BF16)
    co_ref[...] = z[:, ZC_OC:ZC_UD]
    du_ref[...] = z[:, ZC_UD:ZC_SMALL]
    sm_ref[...] = z[:, ZC_SMALL:Z_WIDTH]


INPROJ_OUTPUTS = ((256, BF16), (256, BF16), (256, BF16), (128, BF16), (128, BF16), (128, BF16),
                  (512, F32), (256, BF16), (256, F32), (256, F32), (128, F32))


def _ffn_inproj_kernel(x_ref, modf_ref, w13_ref, w2_ref, lng_ref, lnb_ref, modm_ref, win_ref, wuq_ref, wiq_ref,
                       o_ref, *rest):
    outs, h_ref = rest[:-1], rest[-1]
    x_new = _ffn_tile(x_ref[...], modf_ref, w13_ref, w2_ref, lng_ref, lnb_ref, h_ref)
    o_ref[...] = x_new
    _inproj_tile(x_new, modm_ref, win_ref, wuq_ref, wiq_ref, outs)


def _ffn_inproj(x2, mod_ffn, w13c, w2c, ln_g, ln_b, mod_mix, w_perm, w_uq, w_iq, rows_per_batch):
    n = x2.shape[0]
    tm = ROW_TILE
    tiles_per_batch = rows_per_batch // tm
    row = pl.BlockSpec((tm, D_MODEL), lambda i: (i, 0))
    mod = pl.BlockSpec((None, 8, D_MODEL), lambda i: (i // tiles_per_batch, 0, 0))
    vec = pl.BlockSpec((1, D_MODEL), lambda i: (0, 0))
    resident = lambda a: pl.BlockSpec(a.shape, lambda i: (0,) * a.ndim, pipeline_mode=pl.Buffered(1))
    outs = pl.pallas_call(
        _ffn_inproj_kernel,
        grid=(n // tm,),
        in_specs=[row, mod, resident(w13c), resident(w2c), vec, vec, mod, resident(w_perm), resident(w_uq),
                  resident(w_iq)],
        out_specs=[row] + [pl.BlockSpec((tm, w), lambda i: (i, 0)) for w, _ in INPROJ_OUTPUTS],
        out_shape=[jax.ShapeDtypeStruct((n, D_MODEL), F32)]
        + [jax.ShapeDtypeStruct((n, w), dt) for w, dt in INPROJ_OUTPUTS],
        scratch_shapes=[pltpu.VMEM((tm, D_FF), BF16)],
        compiler_params=_cparams(("arbitrary",)),
        name="ffn_inproj",
    )(x2, mod_ffn, w13c, w2c, ln_g.reshape(1, -1), ln_b.reshape(1, -1), mod_mix, w_perm, w_uq, w_iq)
    return outs[0], outs[1:]


def _swa_kernel(sink_ref, q_ref, kvc_ref, kvp_ref, o_ref):
    i = pl.program_id(1)
    tq = SWA_TQ
    rep = SWA_HEADS // SWA_KV_HEADS
    for r in range(SWA_SUB):
        rows = slice(r * tq, (r + 1) * tq)
        q = q_ref[rows, :]
        prev = kvp_ref[...] if r == 0 else kvc_ref[(r - 1) * tq:r * tq, :]
        kv = jnp.concatenate([prev, kvc_ref[rows, :]], axis=0)
        first = (i * SWA_SUB + r) * tq
        qpos = first + lax.broadcasted_iota(I32, (tq, 2 * tq), 0)
        kpos = first - tq + lax.broadcasted_iota(I32, (tq, 2 * tq), 1)
        qchunk = qpos // CHUNK
        kchunk = (kpos + tq) // CHUNK - tq // CHUNK
        valid = (kpos >= 0) & (kchunk <= qchunk) & (kchunk >= qchunk - SWA_WIN_CHUNKS)
        dist = jnp.abs(qpos - kpos).astype(F32)
        for h in range(SWA_HEADS):
            g = h // rep
            qh = q[:, h * HEAD_DIM:(h + 1) * HEAD_DIM]
            kg = kv[:, g * HEAD_DIM:(g + 1) * HEAD_DIM]
            vg = kv[:, (SWA_KV_HEADS + g) * HEAD_DIM:(SWA_KV_HEADS + g + 1) * HEAD_DIM]
            s = _dot_nt(qh, kg) * HEAD_DIM ** -0.5 - SLOPES_A[h] * dist
            s = jnp.where(valid, s, NEG_INF)
            sink = sink_ref[h]
            m = jnp.maximum(jnp.max(s, axis=-1, keepdims=True), sink)
            p = jnp.exp(s - m)
            denom = jnp.sum(p, axis=-1, keepdims=True) + jnp.exp(sink - m)
            o = _dot(p.astype(BF16), vg) / denom
            o_ref[rows, h * HEAD_DIM:(h + 1) * HEAD_DIM] = o.astype(BF16)


def _swa(sinks, a_q, a_kv, bn, t):
    rows = SWA_SUB * SWA_TQ
    nt = t // rows
    cur = pl.BlockSpec((rows, 256), lambda b, i: (b * nt + i, 0))
    return pl.pallas_call(
        _swa_kernel,
        grid=(bn, nt),
        in_specs=[
            pl.BlockSpec(memory_space=pltpu.SMEM),
            cur, cur,
            pl.BlockSpec((SWA_TQ, 256), lambda b, i: (jnp.maximum((b * nt + i) * SWA_SUB - 1, 0), 0)),
        ],
        out_specs=cur,
        out_shape=jax.ShapeDtypeStruct((bn * t, 256), BF16),
        compiler_params=_cparams(("arbitrary", "arbitrary")),
        name="swa_attention",
    )(sinks, a_q, a_kv, a_kv)


def _sortable_key(x):
    bits = lax.bitcast_convert_type(x, I32)
    return bits ^ ((bits >> 31) & 0x7FFFFFFF)


def _dsa_kernel(q_ref, qi_ref, sm_ref, kv_ref, vt_ref, ki_ref, tril_ref, o_ref, key_ref, s_ref, p_ref, w1_ref, w2_ref):
    i = pl.program_id(1)
    tq, tk = DSA_TQ, DSA_TK
    nblk = (i * tq + tq + tk - 1) // tk
    qpos = i * tq + lax.broadcasted_iota(I32, (1, tq), 1)
    qchunk = qpos // CHUNK
    row_k = lax.broadcasted_iota(I32, (tk, tq), 0)

    qi = qi_ref[...]
    lane_i = lax.broadcasted_iota(I32, (1, IDX_HEADS * IDX_DIM), 1) // IDX_DIM
    qi_stack = jnp.concatenate([jnp.where(lane_i == h, qi, jnp.zeros_like(qi)) for h in range(IDX_HEADS)], axis=0)
    sm_t = sm_ref[...].T
    w_idx = [sm_t[SM_WI + h:SM_WI + h + 1, :] for h in range(IDX_HEADS)]
    idx_scale = (IDX_DIM * IDX_HEADS) ** -0.5
    q = q_ref[...]
    q_stack = jnp.concatenate([q[:, h * HEAD_DIM:(h + 1) * HEAD_DIM] for h in range(DSA_HEADS)], axis=0)

    half = tk // 2
    guard = jnp.int32(FIELD_GUARD)

    def pack_fields(f):
        return (f[0:half] << 16) | f[half:tk] | guard

    def score_block(j, mask_inadmissible):
        rows = pl.ds(pl.multiple_of(j * tk, tk), tk)
        s_ref[j] = _dot_nt(kv_ref[rows, 0:HEAD_DIM], q_stack)
        d = _dot_nt(ki_ref[rows, :], qi_stack)
        acc = w_idx[0] * jnp.maximum(d[:, 0:tq], 0.0)
        for h in range(1, IDX_HEADS):
            acc = acc + w_idx[h] * jnp.maximum(d[:, h * tq:(h + 1) * tq], 0.0)
        sc = acc * idx_scale
        if mask_inadmissible:
            sc = jnp.where((j * tk + row_k) // CHUNK <= qchunk, sc, NEG_INF)
        key = _sortable_key(sc)
        key_ref[j] = key
        w1_ref[j] = pack_fields(lax.shift_right_logical(key ^ INT_MIN, 32 - FIELD_BITS))

    def full_block(j, carry):
        score_block(j, False)
        return carry

    lax.fori_loop(0, nblk - 1, full_block, 0)
    score_block(nblk - 1, True)

    k_eff = jnp.minimum(DSA_TOPK, (qchunk + 1) * CHUNK)

    def over_blocks(body, init):
        c = lax.fori_loop(0, nblk // 2, lambda jj, c: body(2 * jj + 1, body(2 * jj, c)), init)
        return lax.cond(nblk % 2 == 1, lambda c: body(nblk - 1, c), lambda c: c, c)

    def count(pred):
        def blk(j, c):
            m = pred(key_ref[j], j * tk + row_k).astype(I32)
            return c + jnp.sum(m.reshape(tk // 8, 8, tq), axis=0)
        return jnp.sum(over_blocks(blk, jnp.zeros((8, tq), I32)), axis=0, keepdims=True)

    def count_fields(w_ref, cand):
        cand2 = (cand << 16) | cand
        def blk(j, c):
            hit = ((w_ref[j] - cand2) >> 15) & 0x00010001
            return c + jnp.sum(hit.reshape(half // 8, 8, tq), axis=0)
        c = over_blocks(blk, jnp.zeros((8, tq), I32))
        return jnp.sum((c & 0xFFFF) + (c >> 16), axis=0, keepdims=True)

    def field_search(w_ref, k_want):
        def step(bi, carry):
            prefix, above = carry
            cand = prefix | (jnp.int32(1) << (FIELD_BITS - 1 - bi))
            cnt = count_fields(w_ref, cand)
            ok = cnt >= k_want
            return jnp.where(ok, cand, prefix), jnp.where(ok, above, cnt)
        return lax.fori_loop(0, FIELD_BITS, step, (jnp.zeros((1, tq), I32), jnp.zeros((1, tq), I32)))

    top, above = field_search(w1_ref, k_eff)
    field_max = (1 << FIELD_BITS) - 1

    def pack_mid(j, carry):
        ukey = key_ref[j] ^ INT_MIN
        member = lax.shift_right_logical(ukey, 32 - FIELD_BITS) == top
        mid = lax.shift_right_logical(ukey, 32 - 2 * FIELD_BITS) & field_max
        w2_ref[j] = pack_fields(jnp.where(member, mid, 0))
        return carry

    lax.fori_loop(0, nblk, pack_mid, 0)
    mid, above_mid = field_search(w2_ref, k_eff - above)

    def bit_step(bi, carry):
        prefix, c_gt = carry
        cand_u = prefix | (jnp.int32(1) << (31 - 2 * FIELD_BITS - bi))
        cand_s = cand_u ^ INT_MIN
        cnt = count(lambda key, kpos: key >= cand_s)
        ok = cnt >= k_eff
        return jnp.where(ok, cand_u, prefix), jnp.where(ok, c_gt, cnt)

    prefix, c_gt = lax.fori_loop(0, 32 - 2 * FIELD_BITS, bit_step,
                                 ((top << (32 - FIELD_BITS)) | (mid << (32 - 2 * FIELD_BITS)), above + above_mid))
    thr = prefix ^ INT_MIN

    need = (k_eff - c_gt).astype(F32)
    tril = tril_ref[...]

    offs = (row_k - qpos).astype(F32)
    p_ref[1] = jnp.zeros(p_ref.shape[1:], BF16)

    def att_block(j, carry):
        m_run, l_run, acc_part, ties_seen = carry
        slot = j % 2
        acc = acc_part + _dot(vt_ref[jnp.maximum(j - 1, 0)], p_ref[1 - slot])
        key = key_ref[j]
        tie = key == thr
        tie_rank = _dot(tril, jnp.where(tie, 1.0, 0.0).astype(BF16)) + ties_seen
        sel = (key > thr) | (tie & (tie_rank <= need))
        dist = jnp.where(sel, jnp.abs(offs + (j * tk).astype(F32)), jnp.inf)
        ms, ls, alphas = [], [], []
        for h in range(DSA_HEADS):
            cols = slice(h * tq, (h + 1) * tq)
            sh = s_ref[j, :, cols] - SLOPES_B[h] * dist
            m_old = m_run[:, cols]
            m_new = jnp.maximum(m_old, jnp.max(sh, axis=0, keepdims=True))
            alpha = jnp.exp(m_old - m_new)
            p = jnp.exp(sh - m_new)
            p_ref[slot, :, cols] = p.astype(BF16)
            ms.append(m_new)
            ls.append(alpha * l_run[:, cols] + jnp.sum(p, axis=0, keepdims=True))
            alphas.append(alpha)
        return (jnp.concatenate(ms, axis=1), jnp.concatenate(ls, axis=1), jnp.concatenate(alphas, axis=1) * acc,
                tie_rank[tk - 1:tk, :])

    init = (jnp.full((1, DSA_HEADS * tq), NEG_INF, F32), jnp.zeros((1, DSA_HEADS * tq), F32),
            jnp.zeros((HEAD_DIM, DSA_HEADS * tq), F32), jnp.zeros((1, tq), F32))
    _, l_run, acc_part, _ = lax.fori_loop(0, nblk, att_block, init)
    acc = acc_part + _dot(vt_ref[nblk - 1], p_ref[(nblk - 1) % 2])
    out = acc / l_run
    o_ref[...] = jnp.concatenate([out[:, h * tq:(h + 1) * tq] for h in range(DSA_HEADS)], axis=0).astype(BF16)


def _dsa(b_q, b_qi, small, b_kv, b_ki, bn, t):
    nt = t // DSA_TQ
    nkb = t // DSA_TK
    v_t = b_kv[:, HEAD_DIM:].reshape(bn * nkb, DSA_TK, HEAD_DIM).transpose(0, 2, 1)
    o_t = pl.pallas_call(
        _dsa_kernel,
        grid=(bn, nt),
        in_specs=[
            pl.BlockSpec((DSA_TQ, 256), lambda b, i: (b * nt + i, 0)),
            pl.BlockSpec((DSA_TQ, 128), lambda b, i: (b * nt + i, 0)),
            pl.BlockSpec((DSA_TQ, 128), lambda b, i: (b * nt + i, 0)),
            pl.BlockSpec((t, 128), lambda b, i: (b, 0)),
            pl.BlockSpec((nkb, HEAD_DIM, DSA_TK), lambda b, i: (b, 0, 0)),
            pl.BlockSpec((t, 128), lambda b, i: (b, 0)),
            pl.BlockSpec((DSA_TK, DSA_TK), lambda b, i: (0, 0)),
        ],
        out_specs=pl.BlockSpec((None, DSA_HEADS * HEAD_DIM, DSA_TQ), lambda b, i: (b * nt + i, 0, 0)),
        out_shape=jax.ShapeDtypeStruct((bn * nt, DSA_HEADS * HEAD_DIM, DSA_TQ), BF16),
        scratch_shapes=[pltpu.VMEM((nkb, DSA_TK, DSA_TQ), I32),
                        pltpu.VMEM((nkb, DSA_TK, DSA_HEADS * DSA_TQ), F32),
                        pltpu.VMEM((2, DSA_TK, DSA_HEADS * DSA_TQ), BF16),
                        pltpu.VMEM((nkb, DSA_TK // 2, DSA_TQ), I32), pltpu.VMEM((nkb, DSA_TK // 2, DSA_TQ), I32)],
        compiler_params=_cparams(("arbitrary", "arbitrary")),
        name="dsa_attention",
    )(b_q, b_qi, small, b_kv, v_t, b_ki, jnp.tril(jnp.ones((DSA_TK, DSA_TK), BF16)))
    return o_t.transpose(0, 2, 1).reshape(bn * t, DSA_HEADS * HEAD_DIM)


def _mlstm_kernel(qk_ref, v_ref, og_ref, sm_ref, convw_ref, convb_ref, gbias_ref, normgt_ref,
                  o_ref, tail_ref, ct_ref, nvec_ref, mst_ref, *, bn):
    c = pl.program_id(0)
    L = CHUNK
    nh, dh, width = MLSTM_HEADS, HEAD_DIM, MLSTM_HEADS * HEAD_DIM

    @pl.when(c == 0)
    def _():
        tail_ref[...] = jnp.zeros_like(tail_ref)
        ct_ref[...] = jnp.zeros_like(ct_ref)
        nvec_ref[...] = jnp.zeros_like(nvec_ref)
        mst_ref[...] = jnp.zeros_like(mst_ref)

    srow = lax.broadcasted_iota(I32, (L, width), 0)
    lane = lax.broadcasted_iota(I32, (L, width), 1)
    jlane = lane % dh
    causal_t = srow <= jlane
    diag_t = srow == jlane
    head_of_lane = lax.broadcasted_iota(I32, (1, width), 1) // dh
    tril = (lax.broadcasted_iota(I32, (L, L), 1) <= lax.broadcasted_iota(I32, (L, L), 0)).astype(F32)
    erow = lax.broadcasted_iota(I32, (128, width), 0)
    ecol_head = lax.broadcasted_iota(I32, (128, width), 1) // dh
    expand_ig = (erow == SM_IG + ecol_head).astype(F32)
    expand_fg = (erow == SM_FG + ecol_head).astype(F32)
    exact = dict(preferred_element_type=F32, precision=lax.Precision.HIGHEST)

    def head_blocks(a):
        out = jnp.where(head_of_lane == 0, a[0:dh], 0.0)
        for h in range(1, nh):
            out = out + jnp.where(head_of_lane == h, a[h * dh:(h + 1) * dh], 0.0)
        return out

    convw = convw_ref[...]
    for b in range(bn):
        cur = qk_ref[b]
        ext = jnp.concatenate([tail_ref[b], cur], axis=0)
        tail_ref[b] = cur[L - 8:L, :]
        y = convb_ref[...] + convw[MLSTM_CONV - 1:MLSTM_CONV, :] * cur
        for k in range(MLSTM_CONV - 1):
            off = 8 - (MLSTM_CONV - 1) + k
            y = y + convw[k:k + 1, :] * ext[off:off + L, :]
        qk = y * _sigmoid(y)
        q_all = qk[:, 0:width]
        k_all = qk[:, width:2 * width] * dh ** -0.5
        q_stack = jnp.concatenate([jnp.where(head_of_lane == h, q_all, 0.0) for h in range(nh)], axis=0).astype(BF16)
        v_all = v_ref[b]
        v_t = v_all.astype(F32).T.astype(BF16)

        gates = sm_ref[b] + gbias_ref[...]
        lf = jnp.minimum(gates, 0.0) - jnp.log(1.0 + jnp.exp(-jnp.abs(gates)))
        bcum = jnp.dot(tril, lf, **exact)
        ig_x = jnp.dot(gates, expand_ig, **exact)
        b_x = jnp.dot(bcum, expand_fg, **exact)
        b_q = jnp.sum(jnp.where(diag_t, b_x, 0.0), axis=0, keepdims=True)
        b_last = b_x[L - 1:L, :]
        m_prev = mst_ref[b]
        ct = ct_ref[b]
        nvec = nvec_ref[b]

        dlog = jnp.where(causal_t, b_q - b_x + ig_x, NEG_INF)
        inter = b_q + m_prev
        mj = jnp.maximum(inter, jnp.max(dlog, axis=0, keepdims=True))
        dw = jnp.exp(dlog - mj)
        iw = jnp.exp(inter - mj)
        sc = _dot_nt(k_all.astype(BF16), q_stack) * dw
        qn = _dot_nt(jnp.broadcast_to(nvec, (8, width)).astype(BF16), q_stack)[0:1, :]
        q_c = _dot_nt(ct.astype(BF16), q_stack)
        num = iw * q_c + head_blocks(_dot(v_t, sc.astype(BF16)))
        den = iw * qn + jnp.sum(sc, axis=0, keepdims=True)
        hj = num / jnp.maximum(jnp.abs(den), jnp.exp(-mj))

        dec = b_last - b_x + ig_x
        m_new = jnp.maximum(b_last + m_prev, jnp.max(dec, axis=0, keepdims=True))
        wc = jnp.exp(b_last + m_prev - m_new)
        kw = k_all * jnp.exp(dec - m_new)
        ct_ref[b] = wc * ct + head_blocks(_dot(v_t, kw.astype(BF16)))
        nvec_ref[b] = wc * nvec + jnp.sum(kw, axis=0, keepdims=True)
        mst_ref[b] = m_new

        mu = jnp.mean(hj, axis=0, keepdims=True)
        dev = hj - mu
        var = jnp.mean(dev * dev, axis=0, keepdims=True)
        hn_t = (dev * lax.rsqrt(var + LN_EPS) * normgt_ref[...]).T
        hn = jnp.concatenate([hn_t[h * dh:(h + 1) * dh, :] for h in range(nh)], axis=1)
        o_ref[b] = (_sigmoid(og_ref[b]) * hn).astype(BF16)


def _mlstm(c_qk, c_v, c_o, small, conv_w, conv_b, gate_bias, norm_g, bn, t):
    nc = t // CHUNK
    width = MLSTM_HEADS * HEAD_DIM
    norm_g_t = jnp.repeat(norm_g.reshape(MLSTM_HEADS, HEAD_DIM).T, HEAD_DIM, axis=1)
    blk = lambda w: pl.BlockSpec((bn, CHUNK, w), lambda c: (0, c, 0))
    full = lambda a: pl.BlockSpec(a.shape, lambda c: (0,) * a.ndim)
    return pl.pallas_call(
        functools.partial(_mlstm_kernel, bn=bn),
        grid=(nc,),
        in_specs=[blk(512), blk(256), blk(256), blk(128), full(conv_w), full(conv_b), full(gate_bias), full(norm_g_t)],
        out_specs=blk(256),
        out_shape=jax.ShapeDtypeStruct((bn, t, 256), BF16),
        scratch_shapes=[pltpu.VMEM((bn, 8, 512), F32), pltpu.VMEM((bn, HEAD_DIM, width), F32),
                        pltpu.VMEM((bn, 1, width), F32), pltpu.VMEM((bn, 1, width), F32)],
        compiler_params=_cparams(("arbitrary",)),
        name="mlstm",
    )(c_qk.reshape(bn, t, 512), c_v.reshape(bn, t, 256), c_o.reshape(bn, t, 256), small.reshape(bn, t, 128),
      conv_w, conv_b, gate_bias, norm_g_t)


def _s5_kernel(u_ref, perm_ref, bbre_ref, bbim_ref, pwre_ref, pwim_ref, cre_ref, cim_ref, dskip_ref, gluw_ref,
               glub_ref, o_ref, sre_ref, sim_ref, stre_ref, stim_ref, yp_ref):
    @pl.when(pl.program_id(1) == 0)
    def _():
        stre_ref[...] = jnp.zeros_like(stre_ref)
        stim_ref[...] = jnp.zeros_like(stim_ref)

    seg_len = S5_TT // S5_SEGS
    u = u_ref[...]
    ub = _dot(perm_ref[...], u.astype(BF16)).astype(BF16)
    sre_ref[...] = _dot(ub, bbre_ref[...])
    sim_ref[...] = _dot(ub, bbim_ref[...])
    a_re = jnp.broadcast_to(pwre_ref[0:1, :], (S5_SEGS, S5_LANES))
    a_im = jnp.broadcast_to(pwim_ref[0:1, :], (S5_SEGS, S5_LANES))

    def local_step(i, carry):
        s_re, s_im = carry
        rows = pl.ds(pl.multiple_of(i * S5_SEGS, S5_SEGS), S5_SEGS)
        n_re = a_re * s_re - a_im * s_im + sre_ref[rows, :]
        n_im = a_re * s_im + a_im * s_re + sim_ref[rows, :]
        sre_ref[rows, :] = n_re
        sim_ref[rows, :] = n_im
        return n_re, n_im

    zeros = jnp.zeros((S5_SEGS, S5_LANES), F32)
    e_re, e_im = lax.fori_loop(0, seg_len, local_step, (zeros, zeros), unroll=4)

    al_re, al_im = pwre_ref[seg_len - 1:seg_len, :], pwim_ref[seg_len - 1:seg_len, :]
    c_re, c_im = stre_ref[0:1, :], stim_ref[0:1, :]
    cs_re, cs_im = [], []
    for k in range(S5_SEGS):
        cs_re.append(c_re)
        cs_im.append(c_im)
        c_re, c_im = (e_re[k:k + 1, :] + al_re * c_re - al_im * c_im,
                      e_im[k:k + 1, :] + al_re * c_im + al_im * c_re)
    stre_ref[...] = jnp.broadcast_to(c_re, stre_ref.shape)
    stim_ref[...] = jnp.broadcast_to(c_im, stim_ref.shape)
    cin_re = jnp.concatenate(cs_re, axis=0)
    cin_im = jnp.concatenate(cs_im, axis=0)

    def correct_step(i, carry):
        rows = pl.ds(pl.multiple_of(i * S5_SEGS, S5_SEGS), S5_SEGS)
        p_re, p_im = pwre_ref[pl.ds(i, 1), :], pwim_ref[pl.ds(i, 1), :]
        sre_ref[rows, :] = sre_ref[rows, :] + p_re * cin_re - p_im * cin_im
        sim_ref[rows, :] = sim_ref[rows, :] + p_re * cin_im + p_im * cin_re
        return carry

    lax.fori_loop(0, seg_len, correct_step, 0, unroll=4)
    ycs = _dot(sre_ref[...].astype(BF16), cre_ref[...]) - _dot(sim_ref[...].astype(BF16), cim_ref[...])
    ngrp = GROUP_WIDTH // 128
    for g in range(ngrp):
        yp_ref[g] = ycs[:, g * 128:(g + 1) * 128]
    blocks = []
    for k in range(S5_SEGS):
        for i0 in range(0, seg_len, 8):
            rows = pl.ds(i0 * S5_SEGS + k, 8, stride=S5_SEGS)
            blocks.append(jnp.concatenate([yp_ref[g, rows, :] for g in range(ngrp)], axis=1))
    y = jnp.concatenate(blocks, axis=0) + dskip_ref[...] * u
    y = 0.5 * y * (1.0 + jnp.tanh(math.sqrt(2.0 / math.pi) * (y + 0.044715 * (y * y * y))))
    z = _dot(y.astype(BF16), gluw_ref[...]) + glub_ref[...]
    o_ref[...] = (y * _sigmoid(z)).astype(BF16)


def _s5(d_u, bb_re, bb_im, pw_re, pw_im, c_re_t, c_im_t, d_skip, glu_w, glu_b, bn, t):
    nt = t // S5_TT
    full = lambda a: pl.BlockSpec(a.shape, lambda b, i: (0,) * a.ndim)
    r = jnp.arange(S5_TT)
    perm = (r[None, :] == ((r % S5_SEGS) * (S5_TT // S5_SEGS) + r // S5_SEGS)[:, None]).astype(BF16)
    args = (perm, bb_re, bb_im, pw_re, pw_im, c_re_t, c_im_t, d_skip, glu_w, glu_b)
    return pl.pallas_call(
        _s5_kernel,
        grid=(bn, nt),
        in_specs=[pl.BlockSpec((S5_TT, 256), lambda b, i: (b * nt + i, 0))] + [full(a) for a in args],
        out_specs=pl.BlockSpec((S5_TT, 256), lambda b, i: (b * nt + i, 0)),
        out_shape=jax.ShapeDtypeStruct((bn * t, 256), BF16),
        scratch_shapes=[pltpu.VMEM((S5_TT, S5_LANES), F32), pltpu.VMEM((S5_TT, S5_LANES), F32),
                        pltpu.VMEM((8, S5_LANES), F32), pltpu.VMEM((8, S5_LANES), F32),
                        pltpu.VMEM((GROUP_WIDTH // 128, S5_TT, 128), F32)],
        compiler_params=_cparams(("arbitrary", "arbitrary")),
        name="s5_glu",
    )(d_u, *args)


def _outproj_ffn_kernel(x_ref, modm_ref, oa_ref, ob_ref, oc_ref, od_ref, wout_ref, lngm_ref, lnbm_ref,
                        modf_ref, w13_ref, w2_ref, lngf_ref, lnbf_ref, o_ref, h_ref):
    y = _dot(oa_ref[...], wout_ref[0])
    y = y + _dot(ob_ref[...], wout_ref[1])
    y = y + _dot(oc_ref[...], wout_ref[2])
    y = y + _dot(od_ref[...], wout_ref[3])
    x_mid = _residual_layer_norm(x_ref[...], modm_ref[2:3, :] * y, lngm_ref[...], lnbm_ref[...])
    o_ref[...] = _ffn_tile(x_mid, modf_ref, w13_ref, w2_ref, lngf_ref, lnbf_ref, h_ref)


def _outproj_ffn(x2, mod_mix, o_a, o_b, o_c, o_d, w_out4, ln_g_mix, ln_b_mix, mod_ffn, w13c, w2c, ln_g_ffn, ln_b_ffn,
                 rows_per_batch):
    n = x2.shape[0]
    tm = ROW_TILE
    tiles_per_batch = rows_per_batch // tm
    row = pl.BlockSpec((tm, D_MODEL), lambda i: (i, 0))
    mod = pl.BlockSpec((None, 8, D_MODEL), lambda i: (i // tiles_per_batch, 0, 0))
    mix = pl.BlockSpec((tm, GROUP_WIDTH), lambda i: (i, 0))
    vec = pl.BlockSpec((1, D_MODEL), lambda i: (0, 0))
    resident = lambda a: pl.BlockSpec(a.shape, lambda i: (0,) * a.ndim, pipeline_mode=pl.Buffered(1))
    return pl.pallas_call(
        _outproj_ffn_kernel,
        grid=(n // tm,),
        in_specs=[row, mod, mix, mix, mix, mix, resident(w_out4), vec, vec, mod, resident(w13c), resident(w2c), vec, vec],
        out_specs=row,
        out_shape=jax.ShapeDtypeStruct((n, D_MODEL), F32),
        scratch_shapes=[pltpu.VMEM((tm, D_FF), BF16)],
        compiler_params=_cparams(("arbitrary",)),
        name="outproj_ffn",
    )(x2, mod_mix, o_a, o_b, o_c, o_d, w_out4, ln_g_mix.reshape(1, -1), ln_b_mix.reshape(1, -1),
      mod_ffn, w13c, w2c, ln_g_ffn.reshape(1, -1), ln_b_ffn.reshape(1, -1))


def _permute_w_in(w_in):
    off = {}
    o = 0
    for name, s in (("qa", 256), ("ka", 128), ("va", 128), ("cq", 128), ("kb", 64), ("vb", 64), ("ki", 32),
                    ("wi", 4), ("qkc", 512), ("vc", 256), ("ig", 4), ("fg", 4), ("oc", 256), ("ud", 256)):
        off[name] = (o, o + s)
        o += s
    col = lambda n: w_in[:, off[n][0]:off[n][1]]
    small = jnp.concatenate([col("wi"), col("ig"), col("fg"),
                             jnp.zeros((w_in.shape[0], 128 - 12), w_in.dtype)], axis=1)
    parts = [col("qa"), col("ka"), col("va"), col("cq"), col("kb"), col("vb")] + [col("ki")] * 4 + [
        col("qkc"), col("vc"), col("oc"), col("ud"), small]
    return jnp.concatenate(parts, axis=1).astype(BF16)


def _s5_params(lam_re, lam_im, log_step, b_re, b_im, c_re, c_im):
    dt = jnp.exp(log_step)[:, None]
    mag = jnp.exp(lam_re * dt)
    a_re, a_im = mag * jnp.cos(lam_im * dt), mag * jnp.sin(lam_im * dt)
    den = lam_re * lam_re + lam_im * lam_im
    kap_re = ((a_re - 1.0) * lam_re + a_im * lam_im) / den
    kap_im = (a_im * lam_re - (a_re - 1.0) * lam_im) / den
    bb_re = kap_re[..., None] * b_re - kap_im[..., None] * b_im
    bb_im = kap_re[..., None] * b_im + kap_im[..., None] * b_re
    eye = jnp.eye(S5_GROUPS, dtype=F32)

    def in_mat(bb):
        return jnp.einsum("gph,gk->ghkp", bb, eye).reshape(S5_GROUPS * S5_GROUP_CH, S5_LANES).astype(BF16)

    def out_mat(cc):
        return jnp.einsum("gop,gk->gpko", cc, eye).reshape(S5_LANES, S5_GROUPS * S5_GROUP_CH).astype(BF16)

    n = jnp.arange(1, S5_TT // S5_SEGS + 1, dtype=F32)[:, None, None]
    pw_mag = jnp.exp(n * (lam_re * dt))
    pw_re = (pw_mag * jnp.cos(n * (lam_im * dt))).at[0].set(a_re).reshape(-1, S5_LANES)
    pw_im = (pw_mag * jnp.sin(n * (lam_im * dt))).at[0].set(a_im).reshape(-1, S5_LANES)
    return in_mat(bb_re), in_mat(bb_im), pw_re, pw_im, out_mat(c_re), out_mat(c_im)


def kernel(x, c, ada_w, ada_b, ln_g, ln_b, ffn_w13, ffn_w2, w_in, w_out, sinks, w_uq, w_iq, conv_w, conv_b, ig_b,
           fg_b, mh_norm_g, lam_re, lam_im, log_step, b_re, b_im, c_re, c_im, d_skip, glu_w, glu_b):
    bn, t, d = x.shape
    assert d == D_MODEL and t % max(ROW_TILE, DSA_TK, S5_TT) == 0 and bn <= 8
    n = bn * t
    nl = ada_w.shape[0]
    c_pad = jnp.zeros((8, d), F32).at[:bn].set(c)
    mod_all = _ada_mod(c_pad, ada_w, ada_b)
    mod_all = mod_all[:, :bn].reshape(nl, bn, N_SUB, 3, d).transpose(0, 2, 1, 3, 4)
    mod_all = jnp.pad(mod_all, ((0, 0), (0, 0), (0, 0), (0, 5), (0, 0)))

    x2 = x.reshape(n, d)
    for l in range(nl):
        w13c = ffn_w13[l].astype(BF16)
        w2c = ffn_w2[l].astype(BF16)
        x2, (a_q, a_kv, b_q, b_qi, b_kv, b_ki, c_qk, c_v, c_o, d_u, small) = _ffn_inproj(
            x2, mod_all[l, 0], w13c[0], w2c[0], ln_g[l, 0], ln_b[l, 0], mod_all[l, 1],
            _permute_w_in(w_in[l]), w_uq[l].astype(BF16), w_iq[l].astype(BF16), t)
        o_a = _swa(sinks[l], a_q, a_kv, bn, t)
        o_b = _dsa(b_q, b_qi, small, b_kv, b_ki, bn, t)
        gate_bias = jnp.zeros((1, 128), F32).at[0, SM_IG:SM_IG + 4].set(ig_b[l]).at[0, SM_FG:SM_FG + 4].set(fg_b[l])
        o_c = _mlstm(c_qk, c_v, c_o, small, conv_w[l], conv_b[l].reshape(1, -1), gate_bias,
                     mh_norm_g[l].reshape(1, -1), bn, t).reshape(n, GROUP_WIDTH)
        s5p = _s5_params(lam_re[l], lam_im[l], log_step[l], b_re[l], b_im[l], c_re[l], c_im[l])
        o_d = _s5(d_u, *s5p, d_skip[l].reshape(1, -1), glu_w[l].astype(BF16), glu_b[l].reshape(1, -1), bn, t)
        x2 = _outproj_ffn(x2, mod_all[l, 1], o_a, o_b, o_c, o_d, w_out[l].astype(BF16).reshape(4, GROUP_WIDTH, d),
                          ln_g[l, 1], ln_b[l, 1], mod_all[l, 2], w13c[1], w2c[1], ln_g[l, 2], ln_b[l, 2], t)
    return x2.reshape(bn, t, d)
```

```python
import functools
import math

import jax
import jax.numpy as jnp
from jax import lax
from jax.experimental import pallas as pl
from jax.experimental.pallas import tpu as pltpu

F32 = jnp.float32
BF16 = jnp.bfloat16
I32 = jnp.int32

D_MODEL = 1024
DEPTH = 2
CHUNK = 64
HEAD_DIM = 64
GROUP_WIDTH = 256
SWA_HEADS = 4
SWA_KV_HEADS = 2
SWA_WIN_CHUNKS = 2
DSA_HEADS = 4
DSA_Q_RANK = 128
IDX_HEADS = 4
IDX_DIM = 32
DSA_TOPK = 256
MLSTM_HEADS = 4
MLSTM_CONV = 4
S5_GROUP_CH = 16
S5_GROUPS = 16
S5_STATE = 64
S5_LANES = S5_GROUPS * S5_STATE
D_FF = 2816
N_SUB = 3
ALPHA = (2 * DEPTH) ** 0.25
LN_EPS = 1e-5
NEG_INF = -1e30
INT_MIN = -(2 ** 31)
FIELD_BITS = 15
FIELD_GUARD = -(2 ** 31) + 2 ** 15

SLOPES_A = tuple(2.0 ** -(i + 1) for i in range(0, 8, 2))
SLOPES_B = tuple(2.0 ** -(i + 1) for i in range(1, 8, 2))

VMEM_LIMIT_BYTES = 56 * 1024 * 1024

FFN_TF = 256
FFN_NCHUNK = D_FF // FFN_TF
ROW_TILE = 512
SWA_TQ = 128
SWA_SUB = 4
DSA_TQ = 128
DSA_TK = 512
S5_TT = 512
S5_SEGS = 8

ZC_QA, ZC_KVA, ZC_CQ, ZC_KVB, ZC_KI = 0, 256, 512, 640, 768
ZC_QKC, ZC_VC, ZC_OC, ZC_UD, ZC_SMALL, Z_WIDTH = 896, 1408, 1664, 1920, 2176, 2304
SM_WI, SM_IG, SM_FG = 0, 4, 8


def _cparams(sem):
    return pltpu.CompilerParams(dimension_semantics=sem, vmem_limit_bytes=VMEM_LIMIT_BYTES)


def _dot(a, b):
    return jnp.dot(a, b, preferred_element_type=F32)


def _dot_nt(a, b):
    return lax.dot_general(a, b, (((1,), (1,)), ((), ())), preferred_element_type=F32)


def _sigmoid(x):
    return 1.0 / (1.0 + jnp.exp(-x))


def _residual_layer_norm(x, y, g, b):
    v = ALPHA * x + y
    mu = jnp.mean(v, axis=-1, keepdims=True)
    d = v - mu
    var = jnp.mean(d * d, axis=-1, keepdims=True)
    return d * lax.rsqrt(var + LN_EPS) * g + b


def _ada_kernel(c_ref, w_ref, b_ref, o_ref):
    c = c_ref[...]
    cs = c * _sigmoid(c)
    o_ref[...] = jnp.dot(cs, w_ref[...], preferred_element_type=F32,
                         precision=lax.Precision.HIGHEST) + b_ref[...]


def _ada_mod(c_pad, ada_w, ada_b):
    nl = ada_w.shape[0]
    ncol = ada_w.shape[2] // D_MODEL
    return pl.pallas_call(
        _ada_kernel,
        grid=(nl, ncol),
        in_specs=[
            pl.BlockSpec((8, D_MODEL), lambda l, j: (0, 0)),
            pl.BlockSpec((None, D_MODEL, D_MODEL), lambda l, j: (l, 0, j)),
            pl.BlockSpec((None, 1, D_MODEL), lambda l, j: (l, 0, j)),
        ],
        out_specs=pl.BlockSpec((None, 8, D_MODEL), lambda l, j: (l, 0, j)),
        out_shape=jax.ShapeDtypeStruct((nl, 8, ada_w.shape[2]), F32),
        compiler_params=_cparams(("arbitrary", "arbitrary")),
        name="ada_mod",
    )(c_pad, ada_w, ada_b.reshape(nl, 1, -1))


def _ffn_tile(x, mod_ref, w13_ref, w2_ref, lng_ref, lnb_ref, h_ref):
    shift, scale, gate = mod_ref[0:1, :], mod_ref[1:2, :], mod_ref[2:3, :]
    u = (x * (1.0 + scale) + shift).astype(BF16)
    for j in range(FFN_NCHUNK):
        a = _dot(u, w13_ref[:, j * FFN_TF:(j + 1) * FFN_TF])
        g = _dot(u, w13_ref[:, D_FF + j * FFN_TF:D_FF + (j + 1) * FFN_TF])
        h_ref[:, j * FFN_TF:(j + 1) * FFN_TF] = (a * _sigmoid(a) * g).astype(BF16)
    y = _dot(h_ref[...], w2_ref[...])
    return _residual_layer_norm(x, 0.5 * gate * y, lng_ref[...], lnb_ref[...])


def _inproj_tile(x, mod_ref, w_ref, wuq_ref, wiq_ref, outs):
    (aq_ref, akv_ref, bq_ref, bqi_ref, bkv_ref, bki_ref, cqk_ref, cv_ref, co_ref, du_ref, sm_ref) = outs
    shift, scale = mod_ref[0:1, :], mod_ref[1:2, :]
    u = (x * (1.0 + scale) + shift).astype(BF16)
    z = _dot(u, w_ref[...])
    aq_ref[...] = z[:, ZC_QA:ZC_KVA].astype(BF16)
    akv_ref[...] = z[:, ZC_KVA:ZC_CQ].astype(BF16)
    cq = z[:, ZC_CQ:ZC_KVB].astype(BF16)
    bq_ref[...] = (_dot(cq, wuq_ref[...]) * HEAD_DIM ** -0.5).astype(BF16)
    bqi_ref[...] = _dot(cq, wiq_ref[...]).astype(BF16)
    bkv_ref[...] = z[:, ZC_KVB:ZC_KI].astype(BF16)
    bki_ref[...] = z[:, ZC_KI:ZC_QKC].astype(BF16)
    cqk_ref[...] = z[:, ZC_QKC:ZC_VC]
    cv_ref[...] = z[:, ZC_VC:ZC_OC].astype(BF16)
    co_ref[...] = z[:, ZC_OC:ZC_UD]
    du_ref[...] = z[:, ZC_UD:ZC_SMALL]
    sm_ref[...] = z[:, ZC_SMALL:Z_WIDTH]


INPROJ_OUTPUTS = ((256, BF16), (256, BF16), (256, BF16), (128, BF16), (128, BF16), (128, BF16),
                  (512, F32), (256, BF16), (256, F32), (256, F32), (128, F32))


def _ffn_inproj_kernel(x_ref, modf_ref, w13_ref, w2_ref, lng_ref, lnb_ref, modm_ref, win_ref, wuq_ref, wiq_ref,
                       o_ref, *rest):
    outs, h_ref = rest[:-1], rest[-1]
    x_new = _ffn_tile(x_ref[...], modf_ref, w13_ref, w2_ref, lng_ref, lnb_ref, h_ref)
    o_ref[...] = x_new
    _inproj_tile(x_new, modm_ref, win_ref, wuq_ref, wiq_ref, outs)


def _ffn_inproj(x2, mod_ffn, w13c, w2c, ln_g, ln_b, mod_mix, w_perm, w_uq, w_iq, rows_per_batch):
    n = x2.shape[0]
    tm = ROW_TILE
    tiles_per_batch = rows_per_batch // tm
    row = pl.BlockSpec((tm, D_MODEL), lambda i: (i, 0))
    mod = pl.BlockSpec((None, 8, D_MODEL), lambda i: (i // tiles_per_batch, 0, 0))
    vec = pl.BlockSpec((1, D_MODEL), lambda i: (0, 0))
    resident = lambda a: pl.BlockSpec(a.shape, lambda i: (0,) * a.ndim, pipeline_mode=pl.Buffered(1))
    outs = pl.pallas_call(
        _ffn_inproj_kernel,
        grid=(n // tm,),
        in_specs=[row, mod, resident(w13c), resident(w2c), vec, vec, mod, resident(w_perm), resident(w_uq),
                  resident(w_iq)],
        out_specs=[row] + [pl.BlockSpec((tm, w), lambda i: (i, 0)) for w, _ in INPROJ_OUTPUTS],
        out_shape=[jax.ShapeDtypeStruct((n, D_MODEL), F32)]
        + [jax.ShapeDtypeStruct((n, w), dt) for w, dt in INPROJ_OUTPUTS],
        scratch_shapes=[pltpu.VMEM((tm, D_FF), BF16)],
        compiler_params=_cparams(("arbitrary",)),
        name="ffn_inproj",
    )(x2, mod_ffn, w13c, w2c, ln_g.reshape(1, -1), ln_b.reshape(1, -1), mod_mix, w_perm, w_uq, w_iq)
    return outs[0], outs[1:]


def _swa_kernel(sink_ref, q_ref, kvc_ref, kvp_ref, o_ref):
    i = pl.program_id(1)
    tq = SWA_TQ
    rep = SWA_HEADS // SWA_KV_HEADS
    for r in range(SWA_SUB):
        rows = slice(r * tq, (r + 1) * tq)
        q = q_ref[rows, :]
        prev = kvp_ref[...] if r == 0 else kvc_ref[(r - 1) * tq:r * tq, :]
        kv = jnp.concatenate([prev, kvc_ref[rows, :]], axis=0)
        first = (i * SWA_SUB + r) * tq
        qpos = first + lax.broadcasted_iota(I32, (tq, 2 * tq), 0)
        kpos = first - tq + lax.broadcasted_iota(I32, (tq, 2 * tq), 1)
        qchunk = qpos // CHUNK
        kchunk = (kpos + tq) // CHUNK - tq // CHUNK
        valid = (kpos >= 0) & (kchunk <= qchunk) & (kchunk >= qchunk - SWA_WIN_CHUNKS)
        dist = jnp.abs(qpos - kpos).astype(F32)
        for h in range(SWA_HEADS):
            g = h // rep
            qh = q[:, h * HEAD_DIM:(h + 1) * HEAD_DIM]
            kg = kv[:, g * HEAD_DIM:(g + 1) * HEAD_DIM]
            vg = kv[:, (SWA_KV_HEADS + g) * HEAD_DIM:(SWA_KV_HEADS + g + 1) * HEAD_DIM]
            s = _dot_nt(qh, kg) * HEAD_DIM ** -0.5 - SLOPES_A[h] * dist
            s = jnp.where(valid, s, NEG_INF)
            sink = sink_ref[h]
            m = jnp.maximum(jnp.max(s, axis=-1, keepdims=True), sink)
            p = jnp.exp(s - m)
            denom = jnp.sum(p, axis=-1, keepdims=True) + jnp.exp(sink - m)
            o = _dot(p.astype(BF16), vg) / denom
            o_ref[rows, h * HEAD_DIM:(h + 1) * HEAD_DIM] = o.astype(BF16)


def _swa(sinks, a_q, a_kv, bn, t):
    rows = SWA_SUB * SWA_TQ
    nt = t // rows
    cur = pl.BlockSpec((rows, 256), lambda b, i: (b * nt + i, 0))
    return pl.pallas_call(
        _swa_kernel,
        grid=(bn, nt),
        in_specs=[
            pl.BlockSpec(memory_space=pltpu.SMEM),
            cur, cur,
            pl.BlockSpec((SWA_TQ, 256), lambda b, i: (jnp.maximum((b * nt + i) * SWA_SUB - 1, 0), 0)),
        ],
        out_specs=cur,
        out_shape=jax.ShapeDtypeStruct((bn * t, 256), BF16),
        compiler_params=_cparams(("arbitrary", "arbitrary")),
        name="swa_attention",
    )(sinks, a_q, a_kv, a_kv)


def _sortable_key(x):
    bits = lax.bitcast_convert_type(x, I32)
    return bits ^ ((bits >> 31) & 0x7FFFFFFF)


def _dsa_kernel(q_ref, qi_ref, sm_ref, kv_ref, vt_ref, ki_ref, tril_ref, o_ref, key_ref, s_ref, p_ref, w1_ref, w2_ref):
    i = pl.program_id(1)
    tq, tk = DSA_TQ, DSA_TK
    nblk = (i * tq + tq + tk - 1) // tk
    qpos = i * tq + lax.broadcasted_iota(I32, (1, tq), 1)
    qchunk = qpos // CHUNK
    row_k = lax.broadcasted_iota(I32, (tk, tq), 0)

    qi = qi_ref[...]
    lane_i = lax.broadcasted_iota(I32, (1, IDX_HEADS * IDX_DIM), 1) // IDX_DIM
    qi_stack = jnp.concatenate([jnp.where(lane_i == h, qi, jnp.zeros_like(qi)) for h in range(IDX_HEADS)], axis=0)
    sm_t = sm_ref[...].T
    w_idx = [sm_t[SM_WI + h:SM_WI + h + 1, :] for h in range(IDX_HEADS)]
    idx_scale = (IDX_DIM * IDX_HEADS) ** -0.5
    q = q_ref[...]
    q_stack = jnp.concatenate([q[:, h * HEAD_DIM:(h + 1) * HEAD_DIM] for h in range(DSA_HEADS)], axis=0)

    half = tk // 2
    guard = jnp.int32(FIELD_GUARD)

    def pack_fields(f):
        return (f[0:half] << 16) | f[half:tk] | guard

    def score_block(j, mask_inadmissible):
        rows = pl.ds(pl.multiple_of(j * tk, tk), tk)
        s_ref[j] = _dot_nt(kv_ref[rows, 0:HEAD_DIM], q_stack)
        d = _dot_nt(ki_ref[rows, :], qi_stack)
        acc = w_idx[0] * jnp.maximum(d[:, 0:tq], 0.0)
        for h in range(1, IDX_HEADS):
            acc = acc + w_idx[h] * jnp.maximum(d[:, h * tq:(h + 1) * tq], 0.0)
        sc = acc * idx_scale
        if mask_inadmissible:
            sc = jnp.where((j * tk + row_k) // CHUNK <= qchunk, sc, NEG_INF)
        key = _sortable_key(sc)
        key_ref[j] = key
        w1_ref[j] = pack_fields(lax.shift_right_logical(key ^ INT_MIN, 32 - FIELD_BITS))

    def full_blocks(jj, carry):
        score_block(2 * jj, False)
        score_block(2 * jj + 1, False)
        return carry

    nfull = nblk - 1
    lax.fori_loop(0, nfull // 2, full_blocks, 0)

    @pl.when(nfull % 2 == 1)
    def _():
        score_block(nfull - 1, False)

    score_block(nblk - 1, True)

    k_eff = jnp.minimum(DSA_TOPK, (qchunk + 1) * CHUNK)

    def over_blocks(body, init):
        c = lax.fori_loop(0, nblk // 2, lambda jj, c: body(2 * jj + 1, body(2 * jj, c)), init)
        return lax.cond(nblk % 2 == 1, lambda c: body(nblk - 1, c), lambda c: c, c)

    def count(pred):
        def blk(j, c):
            m = pred(key_ref[j], j * tk + row_k).astype(I32)
            return c + jnp.sum(m.reshape(tk // 8, 8, tq), axis=0)
        return jnp.sum(over_blocks(blk, jnp.zeros((8, tq), I32)), axis=0, keepdims=True)

    def count_fields(w_ref, cand):
        cand2 = (cand << 16) | cand
        def blk(j, c):
            hit = ((w_ref[j] - cand2) >> 15) & 0x00010001
            return c + jnp.sum(hit.reshape(half // 8, 8, tq), axis=0)
        c = over_blocks(blk, jnp.zeros((8, tq), I32))
        return jnp.sum((c & 0xFFFF) + (c >> 16), axis=0, keepdims=True)

    def field_search(w_ref, k_want):
        def step(bi, carry):
            prefix, above = carry
            cand = prefix | (jnp.int32(1) << (FIELD_BITS - 1 - bi))
            cnt = count_fields(w_ref, cand)
            ok = cnt >= k_want
            return jnp.where(ok, cand, prefix), jnp.where(ok, above, cnt)
        return lax.fori_loop(0, FIELD_BITS, step, (jnp.zeros((1, tq), I32), jnp.zeros((1, tq), I32)))

    top, above = field_search(w1_ref, k_eff)
    field_max = (1 << FIELD_BITS) - 1

    def pack_mid(j, carry):
        ukey = key_ref[j] ^ INT_MIN
        member = lax.shift_right_logical(ukey, 32 - FIELD_BITS) == top
        mid = lax.shift_right_logical(ukey, 32 - 2 * FIELD_BITS) & field_max
        w2_ref[j] = pack_fields(jnp.where(member, mid, 0))
        return carry

    lax.fori_loop(0, nblk, pack_mid, 0)
    mid, above_mid = field_search(w2_ref, k_eff - above)

    def bit_step(bi, carry):
        prefix, c_gt = carry
        cand_u = prefix | (jnp.int32(1) << (31 - 2 * FIELD_BITS - bi))
        cand_s = cand_u ^ INT_MIN
        cnt = count(lambda key, kpos: key >= cand_s)
        ok = cnt >= k_eff
        return jnp.where(ok, cand_u, prefix), jnp.where(ok, c_gt, cnt)

    prefix, c_gt = lax.fori_loop(0, 32 - 2 * FIELD_BITS, bit_step,
                                 ((top << (32 - FIELD_BITS)) | (mid << (32 - 2 * FIELD_BITS)), above + above_mid))
    thr = prefix ^ INT_MIN

    need = (k_eff - c_gt).astype(F32)
    tril = tril_ref[...]

    offs = (row_k - qpos).astype(F32)
    p_ref[1] = jnp.zeros(p_ref.shape[1:], BF16)

    def att_block(j, carry):
        m_run, l_run, acc_part, ties_seen = carry
        slot = j % 2
        acc = acc_part + _dot(vt_ref[jnp.maximum(j - 1, 0)], p_ref[1 - slot])
        key = key_ref[j]
        tie = key == thr
        tie_rank = _dot(tril, jnp.where(tie, 1.0, 0.0).astype(BF16)) + ties_seen
        sel = (key > thr) | (tie & (tie_rank <= need))
        dist = jnp.where(sel, jnp.abs(offs + (j * tk).astype(F32)), jnp.inf)
        ms, ls, alphas = [], [], []
        for h in range(DSA_HEADS):
            cols = slice(h * tq, (h + 1) * tq)
            sh = s_ref[j, :, cols] - SLOPES_B[h] * dist
            m_old = m_run[:, cols]
            m_new = jnp.maximum(m_old, jnp.max(sh, axis=0, keepdims=True))
            alpha = jnp.exp(m_old - m_new)
            p = jnp.exp(sh - m_new)
            p_ref[slot, :, cols] = p.astype(BF16)
            ms.append(m_new)
            ls.append(alpha * l_run[:, cols] + jnp.sum(p, axis=0, keepdims=True))
            alphas.append(alpha)
        return (jnp.concatenate(ms, axis=1), jnp.concatenate(ls, axis=1), jnp.concatenate(alphas, axis=1) * acc,
                tie_rank[tk - 1:tk, :])

    init = (jnp.full((1, DSA_HEADS * tq), NEG_INF, F32), jnp.zeros((1, DSA_HEADS * tq), F32),
            jnp.zeros((HEAD_DIM, DSA_HEADS * tq), F32), jnp.zeros((1, tq), F32))
    _, l_run, acc_part, _ = lax.fori_loop(0, nblk, att_block, init)
    acc = acc_part + _dot(vt_ref[nblk - 1], p_ref[(nblk - 1) % 2])
    out = acc / l_run
    o_ref[...] = jnp.concatenate([out[:, h * tq:(h + 1) * tq] for h in range(DSA_HEADS)], axis=0).astype(BF16)


def _dsa(b_q, b_qi, small, b_kv, b_ki, bn, t):
    nt = t // DSA_TQ
    nkb = t // DSA_TK
    v_t = b_kv[:, HEAD_DIM:].reshape(bn * nkb, DSA_TK, HEAD_DIM).transpose(0, 2, 1)
    o_t = pl.pallas_call(
        _dsa_kernel,
        grid=(bn, nt),
        in_specs=[
            pl.BlockSpec((DSA_TQ, 256), lambda b, i: (b * nt + i, 0)),
            pl.BlockSpec((DSA_TQ, 128), lambda b, i: (b * nt + i, 0)),
            pl.BlockSpec((DSA_TQ, 128), lambda b, i: (b * nt + i, 0)),
            pl.BlockSpec((t, 128), lambda b, i: (b, 0)),
            pl.BlockSpec((nkb, HEAD_DIM, DSA_TK), lambda b, i: (b, 0, 0)),
            pl.BlockSpec((t, 128), lambda b, i: (b, 0)),
            pl.BlockSpec((DSA_TK, DSA_TK), lambda b, i: (0, 0)),
        ],
        out_specs=pl.BlockSpec((None, DSA_HEADS * HEAD_DIM, DSA_TQ), lambda b, i: (b * nt + i, 0, 0)),
        out_shape=jax.ShapeDtypeStruct((bn * nt, DSA_HEADS * HEAD_DIM, DSA_TQ), BF16),
        scratch_shapes=[pltpu.VMEM((nkb, DSA_TK, DSA_TQ), I32),
                        pltpu.VMEM((nkb, DSA_TK, DSA_HEADS * DSA_TQ), F32),
                        pltpu.VMEM((2, DSA_TK, DSA_HEADS * DSA_TQ), BF16),
                        pltpu.VMEM((nkb, DSA_TK // 2, DSA_TQ), I32), pltpu.VMEM((nkb, DSA_TK // 2, DSA_TQ), I32)],
        compiler_params=_cparams(("arbitrary", "arbitrary")),
        name="dsa_attention",
    )(b_q, b_qi, small, b_kv, v_t, b_ki, jnp.tril(jnp.ones((DSA_TK, DSA_TK), BF16)))
    return o_t.transpose(0, 2, 1).reshape(bn * t, DSA_HEADS * HEAD_DIM)


def _mlstm_kernel(qk_ref, v_ref, og_ref, sm_ref, convw_ref, convb_ref, gbias_ref, normgt_ref,
                  o_ref, tail_ref, ct_ref, nvec_ref, mst_ref, *, bn):
    c = pl.program_id(0)
    L = CHUNK
    nh, dh, width = MLSTM_HEADS, HEAD_DIM, MLSTM_HEADS * HEAD_DIM

    @pl.when(c == 0)
    def _():
        tail_ref[...] = jnp.zeros_like(tail_ref)
        ct_ref[...] = jnp.zeros_like(ct_ref)
        nvec_ref[...] = jnp.zeros_like(nvec_ref)
        mst_ref[...] = jnp.zeros_like(mst_ref)

    srow = lax.broadcasted_iota(I32, (L, width), 0)
    lane = lax.broadcasted_iota(I32, (L, width), 1)
    jlane = lane % dh
    causal_t = srow <= jlane
    diag_t = srow == jlane
    head_of_lane = lax.broadcasted_iota(I32, (1, width), 1) // dh
    tril = (lax.broadcasted_iota(I32, (L, L), 1) <= lax.broadcasted_iota(I32, (L, L), 0)).astype(F32)
    erow = lax.broadcasted_iota(I32, (128, width), 0)
    ecol_head = lax.broadcasted_iota(I32, (128, width), 1) // dh
    expand_ig = (erow == SM_IG + ecol_head).astype(F32)
    expand_fg = (erow == SM_FG + ecol_head).astype(F32)
    exact = dict(preferred_element_type=F32, precision=lax.Precision.HIGHEST)

    def head_blocks(a):
        out = jnp.where(head_of_lane == 0, a[0:dh], 0.0)
        for h in range(1, nh):
            out = out + jnp.where(head_of_lane == h, a[h * dh:(h + 1) * dh], 0.0)
        return out

    convw = convw_ref[...]
    for b in range(bn):
        cur = qk_ref[b]
        ext = jnp.concatenate([tail_ref[b], cur], axis=0)
        tail_ref[b] = cur[L - 8:L, :]
        y = convb_ref[...] + convw[MLSTM_CONV - 1:MLSTM_CONV, :] * cur
        for k in range(MLSTM_CONV - 1):
            off = 8 - (MLSTM_CONV - 1) + k
            y = y + convw[k:k + 1, :] * ext[off:off + L, :]
        qk = y * _sigmoid(y)
        q_all = qk[:, 0:width]
        k_all = qk[:, width:2 * width] * dh ** -0.5
        q_stack = jnp.concatenate([jnp.where(head_of_lane == h, q_all, 0.0) for h in range(nh)], axis=0).astype(BF16)
        v_all = v_ref[b]
        v_t = v_all.astype(F32).T.astype(BF16)

        gates = sm_ref[b] + gbias_ref[...]
        lf = jnp.minimum(gates, 0.0) - jnp.log(1.0 + jnp.exp(-jnp.abs(gates)))
        bcum = jnp.dot(tril, lf, **exact)
        ig_x = jnp.dot(gates, expand_ig, **exact)
        b_x = jnp.dot(bcum, expand_fg, **exact)
        b_q = jnp.sum(jnp.where(diag_t, b_x, 0.0), axis=0, keepdims=True)
        b_last = b_x[L - 1:L, :]
        m_prev = mst_ref[b]
        ct = ct_ref[b]
        nvec = nvec_ref[b]

        dlog = jnp.where(causal_t, b_q - b_x + ig_x, NEG_INF)
        inter = b_q + m_prev
        mj = jnp.maximum(inter, jnp.max(dlog, axis=0, keepdims=True))
        dw = jnp.exp(dlog - mj)
        iw = jnp.exp(inter - mj)
        sc = _dot_nt(k_all.astype(BF16), q_stack) * dw
        qn = _dot_nt(jnp.broadcast_to(nvec, (8, width)).astype(BF16), q_stack)[0:1, :]
        q_c = _dot_nt(ct.astype(BF16), q_stack)
        num = iw * q_c + head_blocks(_dot(v_t, sc.astype(BF16)))
        den = iw * qn + jnp.sum(sc, axis=0, keepdims=True)
        hj = num / jnp.maximum(jnp.abs(den), jnp.exp(-mj))

        dec = b_last - b_x + ig_x
        m_new = jnp.maximum(b_last + m_prev, jnp.max(dec, axis=0, keepdims=True))
        wc = jnp.exp(b_last + m_prev - m_new)
        kw = k_all * jnp.exp(dec - m_new)
        ct_ref[b] = wc * ct + head_blocks(_dot(v_t, kw.astype(BF16)))
        nvec_ref[b] = wc * nvec + jnp.sum(kw, axis=0, keepdims=True)
        mst_ref[b] = m_new

        mu = jnp.mean(hj, axis=0, keepdims=True)
        dev = hj - mu
        var = jnp.mean(dev * dev, axis=0, keepdims=True)
        hn_t = (dev * lax.rsqrt(var + LN_EPS) * normgt_ref[...]).T
        hn = jnp.concatenate([hn_t[h * dh:(h + 1) * dh, :] for h in range(nh)], axis=1)
        o_ref[b] = (_sigmoid(og_ref[b]) * hn).astype(BF16)


def _mlstm(c_qk, c_v, c_o, small, conv_w, conv_b, gate_bias, norm_g, bn, t):
    nc = t // CHUNK
    width = MLSTM_HEADS * HEAD_DIM
    norm_g_t = jnp.repeat(norm_g.reshape(MLSTM_HEADS, HEAD_DIM).T, HEAD_DIM, axis=1)
    blk = lambda w: pl.BlockSpec((bn, CHUNK, w), lambda c: (0, c, 0))
    full = lambda a: pl.BlockSpec(a.shape, lambda c: (0,) * a.ndim)
    return pl.pallas_call(
        functools.partial(_mlstm_kernel, bn=bn),
        grid=(nc,),
        in_specs=[blk(512), blk(256), blk(256), blk(128), full(conv_w), full(conv_b), full(gate_bias), full(norm_g_t)],
        out_specs=blk(256),
        out_shape=jax.ShapeDtypeStruct((bn, t, 256), BF16),
        scratch_shapes=[pltpu.VMEM((bn, 8, 512), F32), pltpu.VMEM((bn, HEAD_DIM, width), F32),
                        pltpu.VMEM((bn, 1, width), F32), pltpu.VMEM((bn, 1, width), F32)],
        compiler_params=_cparams(("arbitrary",)),
        name="mlstm",
    )(c_qk.reshape(bn, t, 512), c_v.reshape(bn, t, 256), c_o.reshape(bn, t, 256), small.reshape(bn, t, 128),
      conv_w, conv_b, gate_bias, norm_g_t)


def _s5_kernel(u_ref, perm_ref, bbre_ref, bbim_ref, pwre_ref, pwim_ref, cre_ref, cim_ref, dskip_ref, gluw_ref,
               glub_ref, o_ref, sre_ref, sim_ref, stre_ref, stim_ref, yp_ref):
    @pl.when(pl.program_id(1) == 0)
    def _():
        stre_ref[...] = jnp.zeros_like(stre_ref)
        stim_ref[...] = jnp.zeros_like(stim_ref)

    seg_len = S5_TT // S5_SEGS
    u = u_ref[...]
    ub = _dot(perm_ref[...], u.astype(BF16)).astype(BF16)
    sre_ref[...] = _dot(ub, bbre_ref[...])
    sim_ref[...] = _dot(ub, bbim_ref[...])
    a_re = jnp.broadcast_to(pwre_ref[0:1, :], (S5_SEGS, S5_LANES))
    a_im = jnp.broadcast_to(pwim_ref[0:1, :], (S5_SEGS, S5_LANES))

    def local_step(i, carry):
        s_re, s_im = carry
        rows = pl.ds(pl.multiple_of(i * S5_SEGS, S5_SEGS), S5_SEGS)
        n_re = a_re * s_re - a_im * s_im + sre_ref[rows, :]
        n_im = a_re * s_im + a_im * s_re + sim_ref[rows, :]
        sre_ref[rows, :] = n_re
        sim_ref[rows, :] = n_im
        return n_re, n_im

    zeros = jnp.zeros((S5_SEGS, S5_LANES), F32)
    e_re, e_im = lax.fori_loop(0, seg_len, local_step, (zeros, zeros), unroll=4)

    al_re, al_im = pwre_ref[seg_len - 1:seg_len, :], pwim_ref[seg_len - 1:seg_len, :]
    c_re, c_im = stre_ref[0:1, :], stim_ref[0:1, :]
    cs_re, cs_im = [], []
    for k in range(S5_SEGS):
        cs_re.append(c_re)
        cs_im.append(c_im)
        c_re, c_im = (e_re[k:k + 1, :] + al_re * c_re - al_im * c_im,
                      e_im[k:k + 1, :] + al_re * c_im + al_im * c_re)
    stre_ref[...] = jnp.broadcast_to(c_re, stre_ref.shape)
    stim_ref[...] = jnp.broadcast_to(c_im, stim_ref.shape)
    cin_re = jnp.concatenate(cs_re, axis=0)
    cin_im = jnp.concatenate(cs_im, axis=0)

    def correct_step(i, carry):
        rows = pl.ds(pl.multiple_of(i * S5_SEGS, S5_SEGS), S5_SEGS)
        p_re, p_im = pwre_ref[pl.ds(i, 1), :], pwim_ref[pl.ds(i, 1), :]
        sre_ref[rows, :] = sre_ref[rows, :] + p_re * cin_re - p_im * cin_im
        sim_ref[rows, :] = sim_ref[rows, :] + p_re * cin_im + p_im * cin_re
        return carry

    lax.fori_loop(0, seg_len, correct_step, 0, unroll=4)
    ycs = _dot(sre_ref[...].astype(BF16), cre_ref[...]) - _dot(sim_ref[...].astype(BF16), cim_ref[...])
    ngrp = GROUP_WIDTH // 128
    for g in range(ngrp):
        yp_ref[g] = ycs[:, g * 128:(g + 1) * 128]
    blocks = []
    for k in range(S5_SEGS):
        for i0 in range(0, seg_len, 8):
            rows = pl.ds(i0 * S5_SEGS + k, 8, stride=S5_SEGS)
            blocks.append(jnp.concatenate([yp_ref[g, rows, :] for g in range(ngrp)], axis=1))
    y = jnp.concatenate(blocks, axis=0) + dskip_ref[...] * u
    y = 0.5 * y * (1.0 + jnp.tanh(math.sqrt(2.0 / math.pi) * (y + 0.044715 * (y * y * y))))
    z = _dot(y.astype(BF16), gluw_ref[...]) + glub_ref[...]
    o_ref[...] = (y * _sigmoid(z)).astype(BF16)


def _s5(d_u, bb_re, bb_im, pw_re, pw_im, c_re_t, c_im_t, d_skip, glu_w, glu_b, bn, t):
    nt = t // S5_TT
    full = lambda a: pl.BlockSpec(a.shape, lambda b, i: (0,) * a.ndim)
    r = jnp.arange(S5_TT)
    perm = (r[None, :] == ((r % S5_SEGS) * (S5_TT // S5_SEGS) + r // S5_SEGS)[:, None]).astype(BF16)
    args = (perm, bb_re, bb_im, pw_re, pw_im, c_re_t, c_im_t, d_skip, glu_w, glu_b)
    return pl.pallas_call(
        _s5_kernel,
        grid=(bn, nt),
        in_specs=[pl.BlockSpec((S5_TT, 256), lambda b, i: (b * nt + i, 0))] + [full(a) for a in args],
        out_specs=pl.BlockSpec((S5_TT, 256), lambda b, i: (b * nt + i, 0)),
        out_shape=jax.ShapeDtypeStruct((bn * t, 256), BF16),
        scratch_shapes=[pltpu.VMEM((S5_TT, S5_LANES), F32), pltpu.VMEM((S5_TT, S5_LANES), F32),
                        pltpu.VMEM((8, S5_LANES), F32), pltpu.VMEM((8, S5_LANES), F32),
                        pltpu.VMEM((GROUP_WIDTH // 128, S5_TT, 128), F32)],
        compiler_params=_cparams(("arbitrary", "arbitrary")),
        name="s5_glu",
    )(d_u, *args)


def _outproj_ffn_kernel(x_ref, modm_ref, oa_ref, ob_ref, oc_ref, od_ref, wout_ref, lngm_ref, lnbm_ref,
                        modf_ref, w13_ref, w2_ref, lngf_ref, lnbf_ref, o_ref, h_ref):
    y = _dot(oa_ref[...], wout_ref[0])
    y = y + _dot(ob_ref[...], wout_ref[1])
    y = y + _dot(oc_ref[...], wout_ref[2])
    y = y + _dot(od_ref[...], wout_ref[3])
    x_mid = _residual_layer_norm(x_ref[...], modm_ref[2:3, :] * y, lngm_ref[...], lnbm_ref[...])
    o_ref[...] = _ffn_tile(x_mid, modf_ref, w13_ref, w2_ref, lngf_ref, lnbf_ref, h_ref)


def _outproj_ffn(x2, mod_mix, o_a, o_b, o_c, o_d, w_out4, ln_g_mix, ln_b_mix, mod_ffn, w13c, w2c, ln_g_ffn, ln_b_ffn,
                 rows_per_batch):
    n = x2.shape[0]
    tm = ROW_TILE
    tiles_per_batch = rows_per_batch // tm
    row = pl.BlockSpec((tm, D_MODEL), lambda i: (i, 0))
    mod = pl.BlockSpec((None, 8, D_MODEL), lambda i: (i // tiles_per_batch, 0, 0))
    mix = pl.BlockSpec((tm, GROUP_WIDTH), lambda i: (i, 0))
    vec = pl.BlockSpec((1, D_MODEL), lambda i: (0, 0))
    resident = lambda a: pl.BlockSpec(a.shape, lambda i: (0,) * a.ndim, pipeline_mode=pl.Buffered(1))
    return pl.pallas_call(
        _outproj_ffn_kernel,
        grid=(n // tm,),
        in_specs=[row, mod, mix, mix, mix, mix, resident(w_out4), vec, vec, mod, resident(w13c), resident(w2c), vec, vec],
        out_specs=row,
        out_shape=jax.ShapeDtypeStruct((n, D_MODEL), F32),
        scratch_shapes=[pltpu.VMEM((tm, D_FF), BF16)],
        compiler_params=_cparams(("arbitrary",)),
        name="outproj_ffn",
    )(x2, mod_mix, o_a, o_b, o_c, o_d, w_out4, ln_g_mix.reshape(1, -1), ln_b_mix.reshape(1, -1),
      mod_ffn, w13c, w2c, ln_g_ffn.reshape(1, -1), ln_b_ffn.reshape(1, -1))


def _permute_w_in(w_in):
    off = {}
    o = 0
    for name, s in (("qa", 256), ("ka", 128), ("va", 128), ("cq", 128), ("kb", 64), ("vb", 64), ("ki", 32),
                    ("wi", 4), ("qkc", 512), ("vc", 256), ("ig", 4), ("fg", 4), ("oc", 256), ("ud", 256)):
        off[name] = (o, o + s)
        o += s
    col = lambda n: w_in[:, off[n][0]:off[n][1]]
    small = jnp.concatenate([col("wi"), col("ig"), col("fg"),
                             jnp.zeros((w_in.shape[0], 128 - 12), w_in.dtype)], axis=1)
    parts = [col("qa"), col("ka"), col("va"), col("cq"), col("kb"), col("vb")] + [col("ki")] * 4 + [
        col("qkc"), col("vc"), col("oc"), col("ud"), small]
    return jnp.concatenate(parts, axis=1).astype(BF16)


def _s5_params(lam_re, lam_im, log_step, b_re, b_im, c_re, c_im):
    dt = jnp.exp(log_step)[:, None]
    mag = jnp.exp(lam_re * dt)
    a_re, a_im = mag * jnp.cos(lam_im * dt), mag * jnp.sin(lam_im * dt)
    den = lam_re * lam_re + lam_im * lam_im
    kap_re = ((a_re - 1.0) * lam_re + a_im * lam_im) / den
    kap_im = (a_im * lam_re - (a_re - 1.0) * lam_im) / den
    bb_re = kap_re[..., None] * b_re - kap_im[..., None] * b_im
    bb_im = kap_re[..., None] * b_im + kap_im[..., None] * b_re
    eye = jnp.eye(S5_GROUPS, dtype=F32)

    def in_mat(bb):
        return jnp.einsum("gph,gk->ghkp", bb, eye).reshape(S5_GROUPS * S5_GROUP_CH, S5_LANES).astype(BF16)

    def out_mat(cc):
        return jnp.einsum("gop,gk->gpko", cc, eye).reshape(S5_LANES, S5_GROUPS * S5_GROUP_CH).astype(BF16)

    n = jnp.arange(1, S5_TT // S5_SEGS + 1, dtype=F32)[:, None, None]
    pw_mag = jnp.exp(n * (lam_re * dt))
    pw_re = (pw_mag * jnp.cos(n * (lam_im * dt))).at[0].set(a_re).reshape(-1, S5_LANES)
    pw_im = (pw_mag * jnp.sin(n * (lam_im * dt))).at[0].set(a_im).reshape(-1, S5_LANES)
    return in_mat(bb_re), in_mat(bb_im), pw_re, pw_im, out_mat(c_re), out_mat(c_im)


def kernel(x, c, ada_w, ada_b, ln_g, ln_b, ffn_w13, ffn_w2, w_in, w_out, sinks, w_uq, w_iq, conv_w, conv_b, ig_b,
           fg_b, mh_norm_g, lam_re, lam_im, log_step, b_re, b_im, c_re, c_im, d_skip, glu_w, glu_b):
    bn, t, d = x.shape
    assert d == D_MODEL and t % max(ROW_TILE, DSA_TK, S5_TT) == 0 and bn <= 8
    n = bn * t
    nl = ada_w.shape[0]
    c_pad = jnp.zeros((8, d), F32).at[:bn].set(c)
    mod_all = _ada_mod(c_pad, ada_w, ada_b)
    mod_all = mod_all[:, :bn].reshape(nl, bn, N_SUB, 3, d).transpose(0, 2, 1, 3, 4)
    mod_all = jnp.pad(mod_all, ((0, 0), (0, 0), (0, 0), (0, 5), (0, 0)))

    x2 = x.reshape(n, d)
    for l in range(nl):
        w13c = ffn_w13[l].astype(BF16)
        w2c = ffn_w2[l].astype(BF16)
        x2, (a_q, a_kv, b_q, b_qi, b_kv, b_ki, c_qk, c_v, c_o, d_u, small) = _ffn_inproj(
            x2, mod_all[l, 0], w13c[0], w2c[0], ln_g[l, 0], ln_b[l, 0], mod_all[l, 1],
            _permute_w_in(w_in[l]), w_uq[l].astype(BF16), w_iq[l].astype(BF16), t)
        o_a = _swa(sinks[l], a_q, a_kv, bn, t)
        o_b = _dsa(b_q, b_qi, small, b_kv, b_ki, bn, t)
        gate_bias = jnp.zeros((1, 128), F32).at[0, SM_IG:SM_IG + 4].set(ig_b[l]).at[0, SM_FG:SM_FG + 4].set(fg_b[l])
        o_c = _mlstm(c_qk, c_v, c_o, small, conv_w[l], conv_b[l].reshape(1, -1), gate_bias,
                     mh_norm_g[l].reshape(1, -1), bn, t).reshape(n, GROUP_WIDTH)
        s5p = _s5_params(lam_re[l], lam_im[l], log_step[l], b_re[l], b_im[l], c_re[l], c_im[l])
        o_d = _s5(d_u, *s5p, d_skip[l].reshape(1, -1), glu_w[l].astype(BF16), glu_b[l].reshape(1, -1), bn, t)
        x2 = _outproj_ffn(x2, mod_all[l, 1], o_a, o_b, o_c, o_d, w_out[l].astype(BF16).reshape(4, GROUP_WIDTH, d),
                          ln_g[l, 1], ln_b[l, 1], mod_all[l, 2], w13c[1], w2c[1], ln_g[l, 2], ln_b[l, 2], t)
    return x2.reshape(bn, t, d)
```

```python
import functools
import math

import jax
import jax.numpy as jnp
from jax import lax
from jax.experimental import pallas as pl
from jax.experimental.pallas import tpu as pltpu

F32 = jnp.float32
BF16 = jnp.bfloat16
I32 = jnp.int32

D_MODEL = 1024
DEPTH = 2
CHUNK = 64
HEAD_DIM = 64
GROUP_WIDTH = 256
SWA_HEADS = 4
SWA_KV_HEADS = 2
SWA_WIN_CHUNKS = 2
DSA_HEADS = 4
DSA_Q_RANK = 128
IDX_HEADS = 4
IDX_DIM = 32
DSA_TOPK = 256
MLSTM_HEADS = 4
MLSTM_CONV = 4
S5_GROUP_CH = 16
S5_GROUPS = 16
S5_STATE = 64
S5_LANES = S5_GROUPS * S5_STATE
D_FF = 2816
N_SUB = 3
ALPHA = (2 * DEPTH) ** 0.25
LN_EPS = 1e-5
NEG_INF = -1e30
INT_MIN = -(2 ** 31)
FIELD_BITS = 15
FIELD_GUARD = -(2 ** 31) + 2 ** 15

SLOPES_A = tuple(2.0 ** -(i + 1) for i in range(0, 8, 2))
SLOPES_B = tuple(2.0 ** -(i + 1) for i in range(1, 8, 2))

VMEM_LIMIT_BYTES = 56 * 1024 * 1024

FFN_TF = 256
FFN_NCHUNK = D_FF // FFN_TF
ROW_TILE = 512
SWA_TQ = 128
SWA_SUB = 4
DSA_TQ = 128
DSA_TK = 512
S5_TT = 512
S5_SEGS = 8
S5_NB = 2

ZC_QA, ZC_KVA, ZC_CQ, ZC_KVB, ZC_KI = 0, 256, 512, 640, 768
ZC_QKC, ZC_VC, ZC_OC, ZC_UD, ZC_SMALL, Z_WIDTH = 896, 1408, 1664, 1920, 2176, 2304
SM_WI, SM_IG, SM_FG = 0, 4, 8


def _cparams(sem):
    return pltpu.CompilerParams(dimension_semantics=sem, vmem_limit_bytes=VMEM_LIMIT_BYTES)


def _dot(a, b):
    return jnp.dot(a, b, preferred_element_type=F32)


def _dot_nt(a, b):
    return lax.dot_general(a, b, (((1,), (1,)), ((), ())), preferred_element_type=F32)


def _sigmoid(x):
    return 1.0 / (1.0 + jnp.exp(-x))


def _residual_layer_norm(x, y, g, b):
    v = ALPHA * x + y
    mu = jnp.mean(v, axis=-1, keepdims=True)
    d = v - mu
    var = jnp.mean(d * d, axis=-1, keepdims=True)
    return d * lax.rsqrt(var + LN_EPS) * g + b


def _ada_kernel(c_ref, w_ref, b_ref, o_ref):
    c = c_ref[...]
    cs = c * _sigmoid(c)
    o_ref[...] = jnp.dot(cs, w_ref[...], preferred_element_type=F32,
                         precision=lax.Precision.HIGHEST) + b_ref[...]


def _ada_mod(c_pad, ada_w, ada_b):
    nl = ada_w.shape[0]
    ncol = ada_w.shape[2] // D_MODEL
    return pl.pallas_call(
        _ada_kernel,
        grid=(nl, ncol),
        in_specs=[
            pl.BlockSpec((8, D_MODEL), lambda l, j: (0, 0)),
            pl.BlockSpec((None, D_MODEL, D_MODEL), lambda l, j: (l, 0, j)),
            pl.BlockSpec((None, 1, D_MODEL), lambda l, j: (l, 0, j)),
        ],
        out_specs=pl.BlockSpec((None, 8, D_MODEL), lambda l, j: (l, 0, j)),
        out_shape=jax.ShapeDtypeStruct((nl, 8, ada_w.shape[2]), F32),
        compiler_params=_cparams(("arbitrary", "arbitrary")),
        name="ada_mod",
    )(c_pad, ada_w, ada_b.reshape(nl, 1, -1))


def _ffn_tile(x, mod_ref, w13_ref, w2_ref, lng_ref, lnb_ref, h_ref):
    shift, scale, gate = mod_ref[0:1, :], mod_ref[1:2, :], mod_ref[2:3, :]
    u = (x * (1.0 + scale) + shift).astype(BF16)
    for j in range(FFN_NCHUNK):
        a = _dot(u, w13_ref[:, j * FFN_TF:(j + 1) * FFN_TF])
        g = _dot(u, w13_ref[:, D_FF + j * FFN_TF:D_FF + (j + 1) * FFN_TF])
        h_ref[:, j * FFN_TF:(j + 1) * FFN_TF] = (a * _sigmoid(a) * g).astype(BF16)
    y = _dot(h_ref[...], w2_ref[...])
    return _residual_layer_norm(x, 0.5 * gate * y, lng_ref[...], lnb_ref[...])


def _inproj_tile(x, mod_ref, w_ref, wuq_ref, wiq_ref, outs):
    (aq_ref, akv_ref, bq_ref, bqi_ref, bkv_ref, bki_ref, cqk_ref, cv_ref, co_ref, du_ref, sm_ref) = outs
    shift, scale = mod_ref[0:1, :], mod_ref[1:2, :]
    u = (x * (1.0 + scale) + shift).astype(BF16)
    z = _dot(u, w_ref[...])
    aq_ref[...] = z[:, ZC_QA:ZC_KVA].astype(BF16)
    akv_ref[...] = z[:, ZC_KVA:ZC_CQ].astype(BF16)
    cq = z[:, ZC_CQ:ZC_KVB].astype(BF16)
    bq_ref[...] = (_dot(cq, wuq_ref[...]) * HEAD_DIM ** -0.5).astype(BF16)
    bqi_ref[...] = _dot(cq, wiq_ref[...]).astype(BF16)
    bkv_ref[...] = z[:, ZC_KVB:ZC_KI].astype(BF16)
    bki_ref[...] = z[:, ZC_KI:ZC_QKC].astype(BF16)
    cqk_ref[...] = z[:, ZC_QKC:ZC_VC]
    cv_ref[...] = z[:, ZC_VC:ZC_OC].astype(BF16)
    co_ref[...] = z[:, ZC_OC:ZC_UD]
    du_ref[...] = z[:, ZC_UD:ZC_SMALL]
    sm_ref[...] = z[:, ZC_SMALL:Z_WIDTH]


INPROJ_OUTPUTS = ((256, BF16), (256, BF16), (256, BF16), (128, BF16), (128, BF16), (128, BF16),
                  (512, F32), (256, BF16), (256, F32), (256, F32), (128, F32))


def _ffn_inproj_kernel(x_ref, modf_ref, w13_ref, w2_ref, lng_ref, lnb_ref, modm_ref, win_ref, wuq_ref, wiq_ref,
                       o_ref, *rest):
    outs, h_ref = rest[:-1], rest[-1]
    x_new = _ffn_tile(x_ref[...], modf_ref, w13_ref, w2_ref, lng_ref, lnb_ref, h_ref)
    o_ref[...] = x_new
    _inproj_tile(x_new, modm_ref, win_ref, wuq_ref, wiq_ref, outs)


def _ffn_inproj(x2, mod_ffn, w13c, w2c, ln_g, ln_b, mod_mix, w_perm, w_uq, w_iq, rows_per_batch):
    n = x2.shape[0]
    tm = ROW_TILE
    tiles_per_batch = rows_per_batch // tm
    row = pl.BlockSpec((tm, D_MODEL), lambda i: (i, 0))
    mod = pl.BlockSpec((None, 8, D_MODEL), lambda i: (i // tiles_per_batch, 0, 0))
    vec = pl.BlockSpec((1, D_MODEL), lambda i: (0, 0))
    resident = lambda a: pl.BlockSpec(a.shape, lambda i: (0,) * a.ndim, pipeline_mode=pl.Buffered(1))
    outs = pl.pallas_call(
        _ffn_inproj_kernel,
        grid=(n // tm,),
        in_specs=[row, mod, resident(w13c), resident(w2c), vec, vec, mod, resident(w_perm), resident(w_uq),
                  resident(w_iq)],
        out_specs=[row] + [pl.BlockSpec((tm, w), lambda i: (i, 0)) for w, _ in INPROJ_OUTPUTS],
        out_shape=[jax.ShapeDtypeStruct((n, D_MODEL), F32)]
        + [jax.ShapeDtypeStruct((n, w), dt) for w, dt in INPROJ_OUTPUTS],
        scratch_shapes=[pltpu.VMEM((tm, D_FF), BF16)],
        compiler_params=_cparams(("arbitrary",)),
        name="ffn_inproj",
    )(x2, mod_ffn, w13c, w2c, ln_g.reshape(1, -1), ln_b.reshape(1, -1), mod_mix, w_perm, w_uq, w_iq)
    return outs[0], outs[1:]


def _swa_kernel(sink_ref, q_ref, kvc_ref, kvp_ref, o_ref):
    i = pl.program_id(1)
    tq = SWA_TQ
    rep = SWA_HEADS // SWA_KV_HEADS
    for r in range(SWA_SUB):
        rows = slice(r * tq, (r + 1) * tq)
        q = q_ref[rows, :]
        prev = kvp_ref[...] if r == 0 else kvc_ref[(r - 1) * tq:r * tq, :]
        kv = jnp.concatenate([prev, kvc_ref[rows, :]], axis=0)
        first = (i * SWA_SUB + r) * tq
        qpos = first + lax.broadcasted_iota(I32, (tq, 2 * tq), 0)
        kpos = first - tq + lax.broadcasted_iota(I32, (tq, 2 * tq), 1)
        qchunk = qpos // CHUNK
        kchunk = (kpos + tq) // CHUNK - tq // CHUNK
        valid = (kpos >= 0) & (kchunk <= qchunk) & (kchunk >= qchunk - SWA_WIN_CHUNKS)
        dist = jnp.abs(qpos - kpos).astype(F32)
        for h in range(SWA_HEADS):
            g = h // rep
            qh = q[:, h * HEAD_DIM:(h + 1) * HEAD_DIM]
            kg = kv[:, g * HEAD_DIM:(g + 1) * HEAD_DIM]
            vg = kv[:, (SWA_KV_HEADS + g) * HEAD_DIM:(SWA_KV_HEADS + g + 1) * HEAD_DIM]
            s = _dot_nt(qh, kg) * HEAD_DIM ** -0.5 - SLOPES_A[h] * dist
            s = jnp.where(valid, s, NEG_INF)
            sink = sink_ref[h]
            m = jnp.maximum(jnp.max(s, axis=-1, keepdims=True), sink)
            p = jnp.exp(s - m)
            denom = jnp.sum(p, axis=-1, keepdims=True) + jnp.exp(sink - m)
            o = _dot(p.astype(BF16), vg) / denom
            o_ref[rows, h * HEAD_DIM:(h + 1) * HEAD_DIM] = o.astype(BF16)


def _swa(sinks, a_q, a_kv, bn, t):
    rows = SWA_SUB * SWA_TQ
    nt = t // rows
    cur = pl.BlockSpec((rows, 256), lambda b, i: (b * nt + i, 0))
    return pl.pallas_call(
        _swa_kernel,
        grid=(bn, nt),
        in_specs=[
            pl.BlockSpec(memory_space=pltpu.SMEM),
            cur, cur,
            pl.BlockSpec((SWA_TQ, 256), lambda b, i: (jnp.maximum((b * nt + i) * SWA_SUB - 1, 0), 0)),
        ],
        out_specs=cur,
        out_shape=jax.ShapeDtypeStruct((bn * t, 256), BF16),
        compiler_params=_cparams(("arbitrary", "arbitrary")),
        name="swa_attention",
    )(sinks, a_q, a_kv, a_kv)


def _sortable_key(x):
    bits = lax.bitcast_convert_type(x, I32)
    return bits ^ ((bits >> 31) & 0x7FFFFFFF)


def _dsa_kernel(q_ref, qi_ref, sm_ref, kv_ref, vt_ref, ki_ref, tril_ref, o_ref, key_ref, s_ref, p_ref, w1_ref, w2_ref):
    i = pl.program_id(1)
    tq, tk = DSA_TQ, DSA_TK
    nblk = (i * tq + tq + tk - 1) // tk
    qpos = i * tq + lax.broadcasted_iota(I32, (1, tq), 1)
    qchunk = qpos // CHUNK
    row_k = lax.broadcasted_iota(I32, (tk, tq), 0)

    qi = qi_ref[...]
    lane_i = lax.broadcasted_iota(I32, (1, IDX_HEADS * IDX_DIM), 1) // IDX_DIM
    qi_stack = jnp.concatenate([jnp.where(lane_i == h, qi, jnp.zeros_like(qi)) for h in range(IDX_HEADS)], axis=0)
    sm_t = sm_ref[...].T
    w_idx = [sm_t[SM_WI + h:SM_WI + h + 1, :] for h in range(IDX_HEADS)]
    idx_scale = (IDX_DIM * IDX_HEADS) ** -0.5
    q = q_ref[...]
    q_stack = jnp.concatenate([q[:, h * HEAD_DIM:(h + 1) * HEAD_DIM] for h in range(DSA_HEADS)], axis=0)

    half = tk // 2
    guard = jnp.int32(FIELD_GUARD)

    def pack_fields(f):
        return (f[0:half] << 16) | f[half:tk] | guard

    def score_block(j, mask_inadmissible):
        rows = pl.ds(pl.multiple_of(j * tk, tk), tk)
        s_ref[j] = _dot_nt(kv_ref[rows, 0:HEAD_DIM], q_stack)
        d = _dot_nt(ki_ref[rows, :], qi_stack)
        acc = w_idx[0] * jnp.maximum(d[:, 0:tq], 0.0)
        for h in range(1, IDX_HEADS):
            acc = acc + w_idx[h] * jnp.maximum(d[:, h * tq:(h + 1) * tq], 0.0)
        sc = acc * idx_scale
        if mask_inadmissible:
            sc = jnp.where((j * tk + row_k) // CHUNK <= qchunk, sc, NEG_INF)
        key = _sortable_key(sc)
        key_ref[j] = key
        w1_ref[j] = pack_fields(lax.shift_right_logical(key ^ INT_MIN, 32 - FIELD_BITS))

    def full_blocks(jj, carry):
        score_block(2 * jj, False)
        score_block(2 * jj + 1, False)
        return carry

    nfull = nblk - 1
    lax.fori_loop(0, nfull // 2, full_blocks, 0)

    @pl.when(nfull % 2 == 1)
    def _():
        score_block(nfull - 1, False)

    score_block(nblk - 1, True)

    k_eff = jnp.minimum(DSA_TOPK, (qchunk + 1) * CHUNK)

    def over_blocks(body, init):
        c = lax.fori_loop(0, nblk // 2, lambda jj, c: body(2 * jj + 1, body(2 * jj, c)), init)
        return lax.cond(nblk % 2 == 1, lambda c: body(nblk - 1, c), lambda c: c, c)

    def count(pred):
        def blk(j, c):
            m = pred(key_ref[j], j * tk + row_k).astype(I32)
            return c + jnp.sum(m.reshape(tk // 8, 8, tq), axis=0)
        return jnp.sum(over_blocks(blk, jnp.zeros((8, tq), I32)), axis=0, keepdims=True)

    def count_fields(w_ref, cand):
        cand2 = (cand << 16) | cand
        def blk(j, c):
            hit = ((w_ref[j] - cand2) >> 15) & 0x00010001
            return c + jnp.sum(hit.reshape(half // 8, 8, tq), axis=0)
        c = over_blocks(blk, jnp.zeros((8, tq), I32))
        return jnp.sum((c & 0xFFFF) + (c >> 16), axis=0, keepdims=True)

    def field_search(w_ref, k_want):
        def step(bi, carry):
            prefix, above = carry
            cand = prefix | (jnp.int32(1) << (FIELD_BITS - 1 - bi))
            cnt = count_fields(w_ref, cand)
            ok = cnt >= k_want
            return jnp.where(ok, cand, prefix), jnp.where(ok, above, cnt)
        return lax.fori_loop(0, FIELD_BITS, step, (jnp.zeros((1, tq), I32), jnp.zeros((1, tq), I32)))

    top, above = field_search(w1_ref, k_eff)
    field_max = (1 << FIELD_BITS) - 1

    def pack_mid(j, carry):
        ukey = key_ref[j] ^ INT_MIN
        member = lax.shift_right_logical(ukey, 32 - FIELD_BITS) == top
        mid = lax.shift_right_logical(ukey, 32 - 2 * FIELD_BITS) & field_max
        w2_ref[j] = pack_fields(jnp.where(member, mid, 0))
        return carry

    lax.fori_loop(0, nblk, pack_mid, 0)
    mid, above_mid = field_search(w2_ref, k_eff - above)

    def bit_step(bi, carry):
        prefix, c_gt = carry
        cand_u = prefix | (jnp.int32(1) << (31 - 2 * FIELD_BITS - bi))
        cand_s = cand_u ^ INT_MIN
        cnt = count(lambda key, kpos: key >= cand_s)
        ok = cnt >= k_eff
        return jnp.where(ok, cand_u, prefix), jnp.where(ok, c_gt, cnt)

    prefix, c_gt = lax.fori_loop(0, 32 - 2 * FIELD_BITS, bit_step,
                                 ((top << (32 - FIELD_BITS)) | (mid << (32 - 2 * FIELD_BITS)), above + above_mid))
    thr = prefix ^ INT_MIN

    need = (k_eff - c_gt).astype(F32)
    tril = tril_ref[...]

    offs = (row_k - qpos).astype(F32)
    p_ref[1] = jnp.zeros(p_ref.shape[1:], BF16)

    def att_block(j, carry):
        m_run, l_run, acc_part, ties_seen = carry
        slot = j % 2
        acc = acc_part + _dot(vt_ref[jnp.maximum(j - 1, 0)], p_ref[1 - slot])
        key = key_ref[j]
        tie = key == thr
        tie_rank = _dot(tril, jnp.where(tie, 1.0, 0.0).astype(BF16)) + ties_seen
        sel = (key > thr) | (tie & (tie_rank <= need))
        dist = jnp.where(sel, jnp.abs(offs + (j * tk).astype(F32)), jnp.inf)
        ms, ls, alphas = [], [], []
        for h in range(DSA_HEADS):
            cols = slice(h * tq, (h + 1) * tq)
            sh = s_ref[j, :, cols] - SLOPES_B[h] * dist
            m_old = m_run[:, cols]
            m_new = jnp.maximum(m_old, jnp.max(sh, axis=0, keepdims=True))
            alpha = jnp.exp(m_old - m_new)
            p = jnp.exp(sh - m_new)
            p_ref[slot, :, cols] = p.astype(BF16)
            ms.append(m_new)
            ls.append(alpha * l_run[:, cols] + jnp.sum(p, axis=0, keepdims=True))
            alphas.append(alpha)
        return (jnp.concatenate(ms, axis=1), jnp.concatenate(ls, axis=1), jnp.concatenate(alphas, axis=1) * acc,
                tie_rank[tk - 1:tk, :])

    init = (jnp.full((1, DSA_HEADS * tq), NEG_INF, F32), jnp.zeros((1, DSA_HEADS * tq), F32),
            jnp.zeros((HEAD_DIM, DSA_HEADS * tq), F32), jnp.zeros((1, tq), F32))
    _, l_run, acc_part, _ = lax.fori_loop(0, nblk, att_block, init)
    acc = acc_part + _dot(vt_ref[nblk - 1], p_ref[(nblk - 1) % 2])
    out = acc / l_run
    o_ref[...] = jnp.concatenate([out[:, h * tq:(h + 1) * tq] for h in range(DSA_HEADS)], axis=0).astype(BF16)


def _dsa(b_q, b_qi, small, b_kv, b_ki, bn, t):
    nt = t // DSA_TQ
    nkb = t // DSA_TK
    v_t = b_kv[:, HEAD_DIM:].reshape(bn * nkb, DSA_TK, HEAD_DIM).transpose(0, 2, 1)
    o_t = pl.pallas_call(
        _dsa_kernel,
        grid=(bn, nt),
        in_specs=[
            pl.BlockSpec((DSA_TQ, 256), lambda b, i: (b * nt + i, 0)),
            pl.BlockSpec((DSA_TQ, 128), lambda b, i: (b * nt + i, 0)),
            pl.BlockSpec((DSA_TQ, 128), lambda b, i: (b * nt + i, 0)),
            pl.BlockSpec((t, 128), lambda b, i: (b, 0)),
            pl.BlockSpec((nkb, HEAD_DIM, DSA_TK), lambda b, i: (b, 0, 0)),
            pl.BlockSpec((t, 128), lambda b, i: (b, 0)),
            pl.BlockSpec((DSA_TK, DSA_TK), lambda b, i: (0, 0)),
        ],
        out_specs=pl.BlockSpec((None, DSA_HEADS * HEAD_DIM, DSA_TQ), lambda b, i: (b * nt + i, 0, 0)),
        out_shape=jax.ShapeDtypeStruct((bn * nt, DSA_HEADS * HEAD_DIM, DSA_TQ), BF16),
        scratch_shapes=[pltpu.VMEM((nkb, DSA_TK, DSA_TQ), I32),
                        pltpu.VMEM((nkb, DSA_TK, DSA_HEADS * DSA_TQ), F32),
                        pltpu.VMEM((2, DSA_TK, DSA_HEADS * DSA_TQ), BF16),
                        pltpu.VMEM((nkb, DSA_TK // 2, DSA_TQ), I32), pltpu.VMEM((nkb, DSA_TK // 2, DSA_TQ), I32)],
        compiler_params=_cparams(("arbitrary", "arbitrary")),
        name="dsa_attention",
    )(b_q, b_qi, small, b_kv, v_t, b_ki, jnp.tril(jnp.ones((DSA_TK, DSA_TK), BF16)))
    return o_t.transpose(0, 2, 1).reshape(bn * t, DSA_HEADS * HEAD_DIM)


def _mlstm_kernel(qk_ref, v_ref, og_ref, sm_ref, convw_ref, convb_ref, gbias_ref, normgt_ref,
                  o_ref, tail_ref, ct_ref, nvec_ref, mst_ref, *, bn):
    c = pl.program_id(0)
    L = CHUNK
    nh, dh, width = MLSTM_HEADS, HEAD_DIM, MLSTM_HEADS * HEAD_DIM

    @pl.when(c == 0)
    def _():
        tail_ref[...] = jnp.zeros_like(tail_ref)
        ct_ref[...] = jnp.zeros_like(ct_ref)
        nvec_ref[...] = jnp.zeros_like(nvec_ref)
        mst_ref[...] = jnp.zeros_like(mst_ref)

    srow = lax.broadcasted_iota(I32, (L, width), 0)
    lane = lax.broadcasted_iota(I32, (L, width), 1)
    jlane = lane % dh
    causal_t = srow <= jlane
    diag_t = srow == jlane
    head_of_lane = lax.broadcasted_iota(I32, (1, width), 1) // dh
    tril = (lax.broadcasted_iota(I32, (L, L), 1) <= lax.broadcasted_iota(I32, (L, L), 0)).astype(F32)
    erow = lax.broadcasted_iota(I32, (128, width), 0)
    ecol_head = lax.broadcasted_iota(I32, (128, width), 1) // dh
    expand_ig = (erow == SM_IG + ecol_head).astype(F32)
    expand_fg = (erow == SM_FG + ecol_head).astype(F32)
    exact = dict(preferred_element_type=F32, precision=lax.Precision.HIGHEST)

    def head_blocks(a):
        out = jnp.where(head_of_lane == 0, a[0:dh], 0.0)
        for h in range(1, nh):
            out = out + jnp.where(head_of_lane == h, a[h * dh:(h + 1) * dh], 0.0)
        return out

    convw = convw_ref[...]
    for b in range(bn):
        cur = qk_ref[b]
        ext = jnp.concatenate([tail_ref[b], cur], axis=0)
        tail_ref[b] = cur[L - 8:L, :]
        y = convb_ref[...] + convw[MLSTM_CONV - 1:MLSTM_CONV, :] * cur
        for k in range(MLSTM_CONV - 1):
            off = 8 - (MLSTM_CONV - 1) + k
            y = y + convw[k:k + 1, :] * ext[off:off + L, :]
        qk = y * _sigmoid(y)
        q_all = qk[:, 0:width]
        k_all = qk[:, width:2 * width] * dh ** -0.5
        q_stack = jnp.concatenate([jnp.where(head_of_lane == h, q_all, 0.0) for h in range(nh)], axis=0).astype(BF16)
        v_all = v_ref[b]
        v_t = v_all.astype(F32).T.astype(BF16)

        gates = sm_ref[b] + gbias_ref[...]
        lf = jnp.minimum(gates, 0.0) - jnp.log(1.0 + jnp.exp(-jnp.abs(gates)))
        bcum = jnp.dot(tril, lf, **exact)
        ig_x = jnp.dot(gates, expand_ig, **exact)
        b_x = jnp.dot(bcum, expand_fg, **exact)
        b_q = jnp.sum(jnp.where(diag_t, b_x, 0.0), axis=0, keepdims=True)
        b_last = b_x[L - 1:L, :]
        m_prev = mst_ref[b]
        ct = ct_ref[b]
        nvec = nvec_ref[b]

        dlog = jnp.where(causal_t, b_q - b_x + ig_x, NEG_INF)
        inter = b_q + m_prev
        mj = jnp.maximum(inter, jnp.max(dlog, axis=0, keepdims=True))
        dw = jnp.exp(dlog - mj)
        iw = jnp.exp(inter - mj)
        sc = _dot_nt(k_all.astype(BF16), q_stack) * dw
        qn = _dot_nt(jnp.broadcast_to(nvec, (8, width)).astype(BF16), q_stack)[0:1, :]
        q_c = _dot_nt(ct.astype(BF16), q_stack)
        num = iw * q_c + head_blocks(_dot(v_t, sc.astype(BF16)))
        den = iw * qn + jnp.sum(sc, axis=0, keepdims=True)
        hj = num / jnp.maximum(jnp.abs(den), jnp.exp(-mj))

        dec = b_last - b_x + ig_x
        m_new = jnp.maximum(b_last + m_prev, jnp.max(dec, axis=0, keepdims=True))
        wc = jnp.exp(b_last + m_prev - m_new)
        kw = k_all * jnp.exp(dec - m_new)
        ct_ref[b] = wc * ct + head_blocks(_dot(v_t, kw.astype(BF16)))
        nvec_ref[b] = wc * nvec + jnp.sum(kw, axis=0, keepdims=True)
        mst_ref[b] = m_new

        mu = jnp.mean(hj, axis=0, keepdims=True)
        dev = hj - mu
        var = jnp.mean(dev * dev, axis=0, keepdims=True)
        hn_t = (dev * lax.rsqrt(var + LN_EPS) * normgt_ref[...]).T
        hn = jnp.concatenate([hn_t[h * dh:(h + 1) * dh, :] for h in range(nh)], axis=1)
        o_ref[b] = (_sigmoid(og_ref[b]) * hn).astype(BF16)


def _mlstm(c_qk, c_v, c_o, small, conv_w, conv_b, gate_bias, norm_g, bn, t):
    nc = t // CHUNK
    width = MLSTM_HEADS * HEAD_DIM
    norm_g_t = jnp.repeat(norm_g.reshape(MLSTM_HEADS, HEAD_DIM).T, HEAD_DIM, axis=1)
    blk = lambda w: pl.BlockSpec((bn, CHUNK, w), lambda c: (0, c, 0))
    full = lambda a: pl.BlockSpec(a.shape, lambda c: (0,) * a.ndim)
    return pl.pallas_call(
        functools.partial(_mlstm_kernel, bn=bn),
        grid=(nc,),
        in_specs=[blk(512), blk(256), blk(256), blk(128), full(conv_w), full(conv_b), full(gate_bias), full(norm_g_t)],
        out_specs=blk(256),
        out_shape=jax.ShapeDtypeStruct((bn, t, 256), BF16),
        scratch_shapes=[pltpu.VMEM((bn, 8, 512), F32), pltpu.VMEM((bn, HEAD_DIM, width), F32),
                        pltpu.VMEM((bn, 1, width), F32), pltpu.VMEM((bn, 1, width), F32)],
        compiler_params=_cparams(("arbitrary",)),
        name="mlstm",
    )(c_qk.reshape(bn, t, 512), c_v.reshape(bn, t, 256), c_o.reshape(bn, t, 256), small.reshape(bn, t, 128),
      conv_w, conv_b, gate_bias, norm_g_t)


def _s5_kernel(u_ref, perm_ref, bbre_ref, bbim_ref, pwre_ref, pwim_ref, cre_ref, cim_ref, dskip_ref, gluw_ref,
               glub_ref, o_ref, sre_ref, sim_ref, stre_ref, stim_ref, yp_ref):
    @pl.when(pl.program_id(1) == 0)
    def _():
        stre_ref[...] = jnp.zeros_like(stre_ref)
        stim_ref[...] = jnp.zeros_like(stim_ref)

    nb = S5_NB
    seg_len = S5_TT // S5_SEGS
    us = [u_ref[b] for b in range(nb)]
    for b in range(nb):
        ub = _dot(perm_ref[...], us[b].astype(BF16)).astype(BF16)
        sre_ref[b] = _dot(ub, bbre_ref[...])
        sim_ref[b] = _dot(ub, bbim_ref[...])
    a_re = jnp.broadcast_to(pwre_ref[0:1, :], (S5_SEGS, S5_LANES))
    a_im = jnp.broadcast_to(pwim_ref[0:1, :], (S5_SEGS, S5_LANES))

    def local_step(i, carry):
        rows = pl.ds(pl.multiple_of(i * S5_SEGS, S5_SEGS), S5_SEGS)
        out = []
        for b in range(nb):
            s_re, s_im = carry[b]
            n_re = a_re * s_re - a_im * s_im + sre_ref[b, rows, :]
            n_im = a_re * s_im + a_im * s_re + sim_ref[b, rows, :]
            sre_ref[b, rows, :] = n_re
            sim_ref[b, rows, :] = n_im
            out.append((n_re, n_im))
        return tuple(out)

    zeros = jnp.zeros((S5_SEGS, S5_LANES), F32)
    ends = lax.fori_loop(0, seg_len, local_step, tuple((zeros, zeros) for _ in range(nb)), unroll=4)

    al_re, al_im = pwre_ref[seg_len - 1:seg_len, :], pwim_ref[seg_len - 1:seg_len, :]
    cins = []
    for b in range(nb):
        e_re, e_im = ends[b]
        c_re, c_im = stre_ref[b, 0:1, :], stim_ref[b, 0:1, :]
        cs_re, cs_im = [], []
        for k in range(S5_SEGS):
            cs_re.append(c_re)
            cs_im.append(c_im)
            c_re, c_im = (e_re[k:k + 1, :] + al_re * c_re - al_im * c_im,
                          e_im[k:k + 1, :] + al_re * c_im + al_im * c_re)
        stre_ref[b] = jnp.broadcast_to(c_re, stre_ref.shape[1:])
        stim_ref[b] = jnp.broadcast_to(c_im, stim_ref.shape[1:])
        cins.append((jnp.concatenate(cs_re, axis=0), jnp.concatenate(cs_im, axis=0)))

    def correct_step(i, carry):
        rows = pl.ds(pl.multiple_of(i * S5_SEGS, S5_SEGS), S5_SEGS)
        p_re, p_im = pwre_ref[pl.ds(i, 1), :], pwim_ref[pl.ds(i, 1), :]
        for b in range(nb):
            cin_re, cin_im = cins[b]
            sre_ref[b, rows, :] = sre_ref[b, rows, :] + p_re * cin_re - p_im * cin_im
            sim_ref[b, rows, :] = sim_ref[b, rows, :] + p_re * cin_im + p_im * cin_re
        return carry

    lax.fori_loop(0, seg_len, correct_step, 0, unroll=4)
    ngrp = GROUP_WIDTH // 128
    for b in range(nb):
        ycs = _dot(sre_ref[b].astype(BF16), cre_ref[...]) - _dot(sim_ref[b].astype(BF16), cim_ref[...])
        for g in range(ngrp):
            yp_ref[b, g] = ycs[:, g * 128:(g + 1) * 128]
        blocks = []
        for k in range(S5_SEGS):
            for i0 in range(0, seg_len, 8):
                rows = pl.ds(i0 * S5_SEGS + k, 8, stride=S5_SEGS)
                blocks.append(jnp.concatenate([yp_ref[b, g, rows, :] for g in range(ngrp)], axis=1))
        y = jnp.concatenate(blocks, axis=0) + dskip_ref[...] * us[b]
        y = 0.5 * y * (1.0 + jnp.tanh(math.sqrt(2.0 / math.pi) * (y + 0.044715 * (y * y * y))))
        z = _dot(y.astype(BF16), gluw_ref[...]) + glub_ref[...]
        o_ref[b] = (y * _sigmoid(z)).astype(BF16)


def _s5(d_u, bb_re, bb_im, pw_re, pw_im, c_re_t, c_im_t, d_skip, glu_w, glu_b, bn, t):
    nt = t // S5_TT
    nb = S5_NB
    full = lambda a: pl.BlockSpec(a.shape, lambda b, i: (0,) * a.ndim)
    r = jnp.arange(S5_TT)
    perm = (r[None, :] == ((r % S5_SEGS) * (S5_TT // S5_SEGS) + r // S5_SEGS)[:, None]).astype(BF16)
    args = (perm, bb_re, bb_im, pw_re, pw_im, c_re_t, c_im_t, d_skip, glu_w, glu_b)
    tile = pl.BlockSpec((nb, S5_TT, GROUP_WIDTH), lambda b, i: (b, i, 0))
    return pl.pallas_call(
        _s5_kernel,
        grid=(bn // nb, nt),
        in_specs=[tile] + [full(a) for a in args],
        out_specs=tile,
        out_shape=jax.ShapeDtypeStruct((bn, t, GROUP_WIDTH), BF16),
        scratch_shapes=[pltpu.VMEM((nb, S5_TT, S5_LANES), F32), pltpu.VMEM((nb, S5_TT, S5_LANES), F32),
                        pltpu.VMEM((nb, 8, S5_LANES), F32), pltpu.VMEM((nb, 8, S5_LANES), F32),
                        pltpu.VMEM((nb, GROUP_WIDTH // 128, S5_TT, 128), F32)],
        compiler_params=_cparams(("arbitrary", "arbitrary")),
        name="s5_glu",
    )(d_u.reshape(bn, t, GROUP_WIDTH), *args).reshape(bn * t, GROUP_WIDTH)


def _outproj_ffn_kernel(x_ref, modm_ref, oa_ref, ob_ref, oc_ref, od_ref, wout_ref, lngm_ref, lnbm_ref,
                        modf_ref, w13_ref, w2_ref, lngf_ref, lnbf_ref, o_ref, h_ref):
    y = _dot(oa_ref[...], wout_ref[0])
    y = y + _dot(ob_ref[...], wout_ref[1])
    y = y + _dot(oc_ref[...], wout_ref[2])
    y = y + _dot(od_ref[...], wout_ref[3])
    x_mid = _residual_layer_norm(x_ref[...], modm_ref[2:3, :] * y, lngm_ref[...], lnbm_ref[...])
    o_ref[...] = _ffn_tile(x_mid, modf_ref, w13_ref, w2_ref, lngf_ref, lnbf_ref, h_ref)


def _outproj_ffn(x2, mod_mix, o_a, o_b, o_c, o_d, w_out4, ln_g_mix, ln_b_mix, mod_ffn, w13c, w2c, ln_g_ffn, ln_b_ffn,
                 rows_per_batch):
    n = x2.shape[0]
    tm = ROW_TILE
    tiles_per_batch = rows_per_batch // tm
    row = pl.BlockSpec((tm, D_MODEL), lambda i: (i, 0))
    mod = pl.BlockSpec((None, 8, D_MODEL), lambda i: (i // tiles_per_batch, 0, 0))
    mix = pl.BlockSpec((tm, GROUP_WIDTH), lambda i: (i, 0))
    vec = pl.BlockSpec((1, D_MODEL), lambda i: (0, 0))
    resident = lambda a: pl.BlockSpec(a.shape, lambda i: (0,) * a.ndim, pipeline_mode=pl.Buffered(1))
    return pl.pallas_call(
        _outproj_ffn_kernel,
        grid=(n // tm,),
        in_specs=[row, mod, mix, mix, mix, mix, resident(w_out4), vec, vec, mod, resident(w13c), resident(w2c), vec, vec],
        out_specs=row,
        out_shape=jax.ShapeDtypeStruct((n, D_MODEL), F32),
        scratch_shapes=[pltpu.VMEM((tm, D_FF), BF16)],
        compiler_params=_cparams(("arbitrary",)),
        name="outproj_ffn",
    )(x2, mod_mix, o_a, o_b, o_c, o_d, w_out4, ln_g_mix.reshape(1, -1), ln_b_mix.reshape(1, -1),
      mod_ffn, w13c, w2c, ln_g_ffn.reshape(1, -1), ln_b_ffn.reshape(1, -1))


def _permute_w_in(w_in):
    off = {}
    o = 0
    for name, s in (("qa", 256), ("ka", 128), ("va", 128), ("cq", 128), ("kb", 64), ("vb", 64), ("ki", 32),
                    ("wi", 4), ("qkc", 512), ("vc", 256), ("ig", 4), ("fg", 4), ("oc", 256), ("ud", 256)):
        off[name] = (o, o + s)
        o += s
    col = lambda n: w_in[:, off[n][0]:off[n][1]]
    small = jnp.concatenate([col("wi"), col("ig"), col("fg"),
                             jnp.zeros((w_in.shape[0], 128 - 12), w_in.dtype)], axis=1)
    parts = [col("qa"), col("ka"), col("va"), col("cq"), col("kb"), col("vb")] + [col("ki")] * 4 + [
        col("qkc"), col("vc"), col("oc"), col("ud"), small]
    return jnp.concatenate(parts, axis=1).astype(BF16)


def _s5_params(lam_re, lam_im, log_step, b_re, b_im, c_re, c_im):
    dt = jnp.exp(log_step)[:, None]
    mag = jnp.exp(lam_re * dt)
    a_re, a_im = mag * jnp.cos(lam_im * dt), mag * jnp.sin(lam_im * dt)
    den = lam_re * lam_re + lam_im * lam_im
    kap_re = ((a_re - 1.0) * lam_re + a_im * lam_im) / den
    kap_im = (a_im * lam_re - (a_re - 1.0) * lam_im) / den
    bb_re = kap_re[..., None] * b_re - kap_im[..., None] * b_im
    bb_im = kap_re[..., None] * b_im + kap_im[..., None] * b_re
    eye = jnp.eye(S5_GROUPS, dtype=F32)

    def in_mat(bb):
        return jnp.einsum("gph,gk->ghkp", bb, eye).reshape(S5_GROUPS * S5_GROUP_CH, S5_LANES).astype(BF16)

    def out_mat(cc):
        return jnp.einsum("gop,gk->gpko", cc, eye).reshape(S5_LANES, S5_GROUPS * S5_GROUP_CH).astype(BF16)

    n = jnp.arange(1, S5_TT // S5_SEGS + 1, dtype=F32)[:, None, None]
    pw_mag = jnp.exp(n * (lam_re * dt))
    pw_re = (pw_mag * jnp.cos(n * (lam_im * dt))).at[0].set(a_re).reshape(-1, S5_LANES)
    pw_im = (pw_mag * jnp.sin(n * (lam_im * dt))).at[0].set(a_im).reshape(-1, S5_LANES)
    return in_mat(bb_re), in_mat(bb_im), pw_re, pw_im, out_mat(c_re), out_mat(c_im)


def kernel(x, c, ada_w, ada_b, ln_g, ln_b, ffn_w13, ffn_w2, w_in, w_out, sinks, w_uq, w_iq, conv_w, conv_b, ig_b,
           fg_b, mh_norm_g, lam_re, lam_im, log_step, b_re, b_im, c_re, c_im, d_skip, glu_w, glu_b):
    bn, t, d = x.shape
    assert d == D_MODEL and t % max(ROW_TILE, DSA_TK, S5_TT) == 0 and bn <= 8 and bn % S5_NB == 0
    n = bn * t
    nl = ada_w.shape[0]
    c_pad = jnp.zeros((8, d), F32).at[:bn].set(c)
    mod_all = _ada_mod(c_pad, ada_w, ada_b)
    mod_all = mod_all[:, :bn].reshape(nl, bn, N_SUB, 3, d).transpose(0, 2, 1, 3, 4)
    mod_all = jnp.pad(mod_all, ((0, 0), (0, 0), (0, 0), (0, 5), (0, 0)))

    x2 = x.reshape(n, d)
    for l in range(nl):
        w13c = ffn_w13[l].astype(BF16)
        w2c = ffn_w2[l].astype(BF16)
        x2, (a_q, a_kv, b_q, b_qi, b_kv, b_ki, c_qk, c_v, c_o, d_u, small) = _ffn_inproj(
            x2, mod_all[l, 0], w13c[0], w2c[0], ln_g[l, 0], ln_b[l, 0], mod_all[l, 1],
            _permute_w_in(w_in[l]), w_uq[l].astype(BF16), w_iq[l].astype(BF16), t)
        o_a = _swa(sinks[l], a_q, a_kv, bn, t)
        o_b = _dsa(b_q, b_qi, small, b_kv, b_ki, bn, t)
        gate_bias = jnp.zeros((1, 128), F32).at[0, SM_IG:SM_IG + 4].set(ig_b[l]).at[0, SM_FG:SM_FG + 4].set(fg_b[l])
        o_c = _mlstm(c_qk, c_v, c_o, small, conv_w[l], conv_b[l].reshape(1, -1), gate_bias,
                     mh_norm_g[l].reshape(1, -1), bn, t).reshape(n, GROUP_WIDTH)
        s5p = _s5_params(lam_re[l], lam_im[l], log_step[l], b_re[l], b_im[l], c_re[l], c_im[l])
        o_d = _s5(d_u, *s5p, d_skip[l].reshape(1, -1), glu_w[l].astype(BF16), glu_b[l].reshape(1, -1), bn, t)
        x2 = _outproj_ffn(x2, mod_all[l, 1], o_a, o_b, o_c, o_d, w_out[l].astype(BF16).reshape(4, GROUP_WIDTH, d),
                          ln_g[l, 1], ln_b[l, 1], mod_all[l, 2], w13c[1], w2c[1], ln_g[l, 2], ln_b[l, 2], t)
    return x2.reshape(bn, t, d)
```

```python
import functools
import math

import jax
import jax.numpy as jnp
from jax import lax
from jax.experimental import pallas as pl
from jax.experimental.pallas import tpu as pltpu

F32 = jnp.float32
BF16 = jnp.bfloat16
I32 = jnp.int32

D_MODEL = 1024
DEPTH = 2
CHUNK = 64
HEAD_DIM = 64
GROUP_WIDTH = 256
SWA_HEADS = 4
SWA_KV_HEADS = 2
SWA_WIN_CHUNKS = 2
DSA_HEADS = 4
DSA_Q_RANK = 128
IDX_HEADS = 4
IDX_DIM = 32
DSA_TOPK = 256
MLSTM_HEADS = 4
MLSTM_CONV = 4
MLSTM_CPS = 4
S5_GROUP_CH = 16
S5_GROUPS = 16
S5_STATE = 64
S5_LANES = S5_GROUPS * S5_STATE
D_FF = 2816
N_SUB = 3
ALPHA = (2 * DEPTH) ** 0.25
LN_EPS = 1e-5
NEG_INF = -1e30
INT_MIN = -(2 ** 31)
FIELD_BITS = 15
FIELD_GUARD = -(2 ** 31) + 2 ** 15

SLOPES_A = tuple(2.0 ** -(i + 1) for i in range(0, 8, 2))
SLOPES_B = tuple(2.0 ** -(i + 1) for i in range(1, 8, 2))

VMEM_LIMIT_BYTES = 56 * 1024 * 1024

FFN_TF = 256
FFN_NCHUNK = D_FF // FFN_TF
ROW_TILE = 512
SWA_TQ = 128
SWA_SUB = 4
DSA_TQ = 128
DSA_TK = 512
S5_TT = 512
S5_SEGS = 8
S5_NB = 2

ZC_QA, ZC_KVA, ZC_CQ, ZC_KVB, ZC_KI = 0, 256, 512, 640, 768
ZC_QKC, ZC_VC, ZC_OC, ZC_UD, ZC_SMALL, Z_WIDTH = 896, 1408, 1664, 1920, 2176, 2304
SM_WI, SM_IG, SM_FG = 0, 4, 8


def _cparams(sem):
    return pltpu.CompilerParams(dimension_semantics=sem, vmem_limit_bytes=VMEM_LIMIT_BYTES)


def _dot(a, b):
    return jnp.dot(a, b, preferred_element_type=F32)


def _dot_nt(a, b):
    return lax.dot_general(a, b, (((1,), (1,)), ((), ())), preferred_element_type=F32)


def _sigmoid(x):
    return 1.0 / (1.0 + jnp.exp(-x))


def _residual_layer_norm(x, y, g, b):
    v = ALPHA * x + y
    mu = jnp.mean(v, axis=-1, keepdims=True)
    d = v - mu
    var = jnp.mean(d * d, axis=-1, keepdims=True)
    return d * lax.rsqrt(var + LN_EPS) * g + b


def _ada_kernel(c_ref, w_ref, b_ref, o_ref):
    c = c_ref[...]
    cs = c * _sigmoid(c)
    o_ref[...] = jnp.dot(cs, w_ref[...], preferred_element_type=F32,
                         precision=lax.Precision.HIGHEST) + b_ref[...]


def _ada_mod(c_pad, ada_w, ada_b):
    nl = ada_w.shape[0]
    ncol = ada_w.shape[2] // D_MODEL
    return pl.pallas_call(
        _ada_kernel,
        grid=(nl, ncol),
        in_specs=[
            pl.BlockSpec((8, D_MODEL), lambda l, j: (0, 0)),
            pl.BlockSpec((None, D_MODEL, D_MODEL), lambda l, j: (l, 0, j)),
            pl.BlockSpec((None, 1, D_MODEL), lambda l, j: (l, 0, j)),
        ],
        out_specs=pl.BlockSpec((None, 8, D_MODEL), lambda l, j: (l, 0, j)),
        out_shape=jax.ShapeDtypeStruct((nl, 8, ada_w.shape[2]), F32),
        compiler_params=_cparams(("arbitrary", "arbitrary")),
        name="ada_mod",
    )(c_pad, ada_w, ada_b.reshape(nl, 1, -1))


def _ffn_tile(x, mod_ref, w13_ref, w2_ref, lng_ref, lnb_ref, h_ref):
    shift, scale, gate = mod_ref[0:1, :], mod_ref[1:2, :], mod_ref[2:3, :]
    u = (x * (1.0 + scale) + shift).astype(BF16)
    for j in range(FFN_NCHUNK):
        a = _dot(u, w13_ref[:, j * FFN_TF:(j + 1) * FFN_TF])
        g = _dot(u, w13_ref[:, D_FF + j * FFN_TF:D_FF + (j + 1) * FFN_TF])
        h_ref[:, j * FFN_TF:(j + 1) * FFN_TF] = (a * _sigmoid(a) * g).astype(BF16)
    y = _dot(h_ref[...], w2_ref[...])
    return _residual_layer_norm(x, 0.5 * gate * y, lng_ref[...], lnb_ref[...])


def _inproj_tile(x, mod_ref, w_ref, wuq_ref, wiq_ref, outs):
    (aq_ref, akv_ref, bq_ref, bqi_ref, bkv_ref, bki_ref, cqk_ref, cv_ref, co_ref, du_ref, sm_ref) = outs
    shift, scale = mod_ref[0:1, :], mod_ref[1:2, :]
    u = (x * (1.0 + scale) + shift).astype(BF16)
    z = _dot(u, w_ref[...])
    aq_ref[...] = z[:, ZC_QA:ZC_KVA].astype(BF16)
    akv_ref[...] = z[:, ZC_KVA:ZC_CQ].astype(BF16)
    cq = z[:, ZC_CQ:ZC_KVB].astype(BF16)
    bq_ref[...] = (_dot(cq, wuq_ref[...]) * HEAD_DIM ** -0.5).astype(BF16)
    bqi_ref[...] = _dot(cq, wiq_ref[...]).astype(BF16)
    bkv_ref[...] = z[:, ZC_KVB:ZC_KI].astype(BF16)
    bki_ref[...] = z[:, ZC_KI:ZC_QKC].astype(BF16)
    cqk_ref[...] = z[:, ZC_QKC:ZC_VC]
    cv_ref[...] = z[:, ZC_VC:ZC_OC].astype(BF16)
    co_ref[...] = z[:, ZC_OC:ZC_UD]
    du_ref[...] = z[:, ZC_UD:ZC_SMALL]
    sm_ref[...] = z[:, ZC_SMALL:Z_WIDTH]


INPROJ_OUTPUTS = ((256, BF16), (256, BF16), (256, BF16), (128, BF16), (128, BF16), (128, BF16),
                  (512, F32), (256, BF16), (256, F32), (256, F32), (128, F32))


def _ffn_inproj_kernel(x_ref, modf_ref, w13_ref, w2_ref, lng_ref, lnb_ref, modm_ref, win_ref, wuq_ref, wiq_ref,
                       o_ref, *rest):
    outs, h_ref = rest[:-1], rest[-1]
    x_new = _ffn_tile(x_ref[...], modf_ref, w13_ref, w2_ref, lng_ref, lnb_ref, h_ref)
    o_ref[...] = x_new
    _inproj_tile(x_new, modm_ref, win_ref, wuq_ref, wiq_ref, outs)


def _ffn_inproj(x2, mod_ffn, w13c, w2c, ln_g, ln_b, mod_mix, w_perm, w_uq, w_iq, rows_per_batch):
    n = x2.shape[0]
    tm = ROW_TILE
    tiles_per_batch = rows_per_batch // tm
    row = pl.BlockSpec((tm, D_MODEL), lambda i: (i, 0))
    mod = pl.BlockSpec((None, 8, D_MODEL), lambda i: (i // tiles_per_batch, 0, 0))
    vec = pl.BlockSpec((1, D_MODEL), lambda i: (0, 0))
    resident = lambda a: pl.BlockSpec(a.shape, lambda i: (0,) * a.ndim, pipeline_mode=pl.Buffered(1))
    outs = pl.pallas_call(
        _ffn_inproj_kernel,
        grid=(n // tm,),
        in_specs=[row, mod, resident(w13c), resident(w2c), vec, vec, mod, resident(w_perm), resident(w_uq),
                  resident(w_iq)],
        out_specs=[row] + [pl.BlockSpec((tm, w), lambda i: (i, 0)) for w, _ in INPROJ_OUTPUTS],
        out_shape=[jax.ShapeDtypeStruct((n, D_MODEL), F32)]
        + [jax.ShapeDtypeStruct((n, w), dt) for w, dt in INPROJ_OUTPUTS],
        scratch_shapes=[pltpu.VMEM((tm, D_FF), BF16)],
        compiler_params=_cparams(("arbitrary",)),
        name="ffn_inproj",
    )(x2, mod_ffn, w13c, w2c, ln_g.reshape(1, -1), ln_b.reshape(1, -1), mod_mix, w_perm, w_uq, w_iq)
    return outs[0], outs[1:]


def _swa_kernel(sink_ref, q_ref, kvc_ref, kvp_ref, o_ref):
    i = pl.program_id(1)
    tq = SWA_TQ
    rep = SWA_HEADS // SWA_KV_HEADS
    for r in range(SWA_SUB):
        rows = slice(r * tq, (r + 1) * tq)
        q = q_ref[rows, :]
        prev = kvp_ref[...] if r == 0 else kvc_ref[(r - 1) * tq:r * tq, :]
        kv = jnp.concatenate([prev, kvc_ref[rows, :]], axis=0)
        first = (i * SWA_SUB + r) * tq
        qpos = first + lax.broadcasted_iota(I32, (tq, 2 * tq), 0)
        kpos = first - tq + lax.broadcasted_iota(I32, (tq, 2 * tq), 1)
        qchunk = qpos // CHUNK
        kchunk = (kpos + tq) // CHUNK - tq // CHUNK
        valid = (kpos >= 0) & (kchunk <= qchunk) & (kchunk >= qchunk - SWA_WIN_CHUNKS)
        dist = jnp.abs(qpos - kpos).astype(F32)
        for h in range(SWA_HEADS):
            g = h // rep
            qh = q[:, h * HEAD_DIM:(h + 1) * HEAD_DIM]
            kg = kv[:, g * HEAD_DIM:(g + 1) * HEAD_DIM]
            vg = kv[:, (SWA_KV_HEADS + g) * HEAD_DIM:(SWA_KV_HEADS + g + 1) * HEAD_DIM]
            s = _dot_nt(qh, kg) * HEAD_DIM ** -0.5 - SLOPES_A[h] * dist
            s = jnp.where(valid, s, NEG_INF)
            sink = sink_ref[h]
            m = jnp.maximum(jnp.max(s, axis=-1, keepdims=True), sink)
            p = jnp.exp(s - m)
            denom = jnp.sum(p, axis=-1, keepdims=True) + jnp.exp(sink - m)
            o = _dot(p.astype(BF16), vg) / denom
            o_ref[rows, h * HEAD_DIM:(h + 1) * HEAD_DIM] = o.astype(BF16)


def _swa(sinks, a_q, a_kv, bn, t):
    rows = SWA_SUB * SWA_TQ
    nt = t // rows
    cur = pl.BlockSpec((rows, 256), lambda b, i: (b * nt + i, 0))
    return pl.pallas_call(
        _swa_kernel,
        grid=(bn, nt),
        in_specs=[
            pl.BlockSpec(memory_space=pltpu.SMEM),
            cur, cur,
            pl.BlockSpec((SWA_TQ, 256), lambda b, i: (jnp.maximum((b * nt + i) * SWA_SUB - 1, 0), 0)),
        ],
        out_specs=cur,
        out_shape=jax.ShapeDtypeStruct((bn * t, 256), BF16),
        compiler_params=_cparams(("arbitrary", "arbitrary")),
        name="swa_attention",
    )(sinks, a_q, a_kv, a_kv)


def _sortable_key(x):
    bits = lax.bitcast_convert_type(x, I32)
    return bits ^ ((bits >> 31) & 0x7FFFFFFF)


def _dsa_kernel(q_ref, qi_ref, sm_ref, kv_ref, vt_ref, ki_ref, tril_ref, o_ref, key_ref, s_ref, p_ref, w1_ref, w2_ref):
    i = pl.program_id(1)
    tq, tk = DSA_TQ, DSA_TK
    nblk = (i * tq + tq + tk - 1) // tk
    qpos = i * tq + lax.broadcasted_iota(I32, (1, tq), 1)
    qchunk = qpos // CHUNK
    row_k = lax.broadcasted_iota(I32, (tk, tq), 0)

    qi = qi_ref[...]
    lane_i = lax.broadcasted_iota(I32, (1, IDX_HEADS * IDX_DIM), 1) // IDX_DIM
    qi_stack = jnp.concatenate([jnp.where(lane_i == h, qi, jnp.zeros_like(qi)) for h in range(IDX_HEADS)], axis=0)
    sm_t = sm_ref[...].T
    w_idx = [sm_t[SM_WI + h:SM_WI + h + 1, :] for h in range(IDX_HEADS)]
    idx_scale = (IDX_DIM * IDX_HEADS) ** -0.5
    q = q_ref[...]
    q_stack = jnp.concatenate([q[:, h * HEAD_DIM:(h + 1) * HEAD_DIM] for h in range(DSA_HEADS)], axis=0)

    half = tk // 2
    guard = jnp.int32(FIELD_GUARD)

    def pack_fields(f):
        return (f[0:half] << 16) | f[half:tk] | guard

    def score_block(j, mask_inadmissible):
        rows = pl.ds(pl.multiple_of(j * tk, tk), tk)
        s_ref[j] = _dot_nt(kv_ref[rows, 0:HEAD_DIM], q_stack)
        d = _dot_nt(ki_ref[rows, :], qi_stack)
        acc = w_idx[0] * jnp.maximum(d[:, 0:tq], 0.0)
        for h in range(1, IDX_HEADS):
            acc = acc + w_idx[h] * jnp.maximum(d[:, h * tq:(h + 1) * tq], 0.0)
        sc = acc * idx_scale
        if mask_inadmissible:
            sc = jnp.where((j * tk + row_k) // CHUNK <= qchunk, sc, NEG_INF)
        key = _sortable_key(sc)
        key_ref[j] = key
        w1_ref[j] = pack_fields(lax.shift_right_logical(key ^ INT_MIN, 32 - FIELD_BITS))

    def full_blocks(jj, carry):
        score_block(2 * jj, False)
        score_block(2 * jj + 1, False)
        return carry

    nfull = nblk - 1
    lax.fori_loop(0, nfull // 2, full_blocks, 0)

    @pl.when(nfull % 2 == 1)
    def _():
        score_block(nfull - 1, False)

    score_block(nblk - 1, True)

    k_eff = jnp.minimum(DSA_TOPK, (qchunk + 1) * CHUNK)

    def over_blocks(body, init):
        c = lax.fori_loop(0, nblk // 2, lambda jj, c: body(2 * jj + 1, body(2 * jj, c)), init)
        return lax.cond(nblk % 2 == 1, lambda c: body(nblk - 1, c), lambda c: c, c)

    def count(pred):
        def blk(j, c):
            m = pred(key_ref[j], j * tk + row_k).astype(I32)
            return c + jnp.sum(m.reshape(tk // 8, 8, tq), axis=0)
        return jnp.sum(over_blocks(blk, jnp.zeros((8, tq), I32)), axis=0, keepdims=True)

    def count_fields(w_ref, cand):
        cand2 = (cand << 16) | cand
        def blk(j, c):
            hit = ((w_ref[j] - cand2) >> 15) & 0x00010001
            return c + jnp.sum(hit.reshape(half // 8, 8, tq), axis=0)
        c = over_blocks(blk, jnp.zeros((8, tq), I32))
        return jnp.sum((c & 0xFFFF) + (c >> 16), axis=0, keepdims=True)

    def field_search(w_ref, k_want):
        def step(bi, carry):
            prefix, above = carry
            cand = prefix | (jnp.int32(1) << (FIELD_BITS - 1 - bi))
            cnt = count_fields(w_ref, cand)
            ok = cnt >= k_want
            return jnp.where(ok, cand, prefix), jnp.where(ok, above, cnt)
        return lax.fori_loop(0, FIELD_BITS, step, (jnp.zeros((1, tq), I32), jnp.zeros((1, tq), I32)))

    top, above = field_search(w1_ref, k_eff)
    field_max = (1 << FIELD_BITS) - 1

    def pack_mid(j, carry):
        ukey = key_ref[j] ^ INT_MIN
        member = lax.shift_right_logical(ukey, 32 - FIELD_BITS) == top
        mid = lax.shift_right_logical(ukey, 32 - 2 * FIELD_BITS) & field_max
        w2_ref[j] = pack_fields(jnp.where(member, mid, 0))
        return carry

    lax.fori_loop(0, nblk, pack_mid, 0)
    mid, above_mid = field_search(w2_ref, k_eff - above)

    def bit_step(bi, carry):
        prefix, c_gt = carry
        cand_u = prefix | (jnp.int32(1) << (31 - 2 * FIELD_BITS - bi))
        cand_s = cand_u ^ INT_MIN
        cnt = count(lambda key, kpos: key >= cand_s)
        ok = cnt >= k_eff
        return jnp.where(ok, cand_u, prefix), jnp.where(ok, c_gt, cnt)

    prefix, c_gt = lax.fori_loop(0, 32 - 2 * FIELD_BITS, bit_step,
                                 ((top << (32 - FIELD_BITS)) | (mid << (32 - 2 * FIELD_BITS)), above + above_mid))
    thr = prefix ^ INT_MIN

    need = (k_eff - c_gt).astype(F32)
    tril = tril_ref[...]

    offs = (row_k - qpos).astype(F32)
    p_ref[1] = jnp.zeros(p_ref.shape[1:], BF16)

    def att_block(j, carry):
        m_run, l_run, acc_part, ties_seen = carry
        slot = j % 2
        acc = acc_part + _dot(vt_ref[jnp.maximum(j - 1, 0)], p_ref[1 - slot])
        key = key_ref[j]
        tie = key == thr
        tie_rank = _dot(tril, jnp.where(tie, 1.0, 0.0).astype(BF16)) + ties_seen
        sel = (key > thr) | (tie & (tie_rank <= need))
        dist = jnp.where(sel, jnp.abs(offs + (j * tk).astype(F32)), jnp.inf)
        ms, ls, alphas = [], [], []
        for h in range(DSA_HEADS):
            cols = slice(h * tq, (h + 1) * tq)
            sh = s_ref[j, :, cols] - SLOPES_B[h] * dist
            m_old = m_run[:, cols]
            m_new = jnp.maximum(m_old, jnp.max(sh, axis=0, keepdims=True))
            alpha = jnp.exp(m_old - m_new)
            p = jnp.exp(sh - m_new)
            p_ref[slot, :, cols] = p.astype(BF16)
            ms.append(m_new)
            ls.append(alpha * l_run[:, cols] + jnp.sum(p, axis=0, keepdims=True))
            alphas.append(alpha)
        return (jnp.concatenate(ms, axis=1), jnp.concatenate(ls, axis=1), jnp.concatenate(alphas, axis=1) * acc,
                tie_rank[tk - 1:tk, :])

    init = (jnp.full((1, DSA_HEADS * tq), NEG_INF, F32), jnp.zeros((1, DSA_HEADS * tq), F32),
            jnp.zeros((HEAD_DIM, DSA_HEADS * tq), F32), jnp.zeros((1, tq), F32))
    _, l_run, acc_part, _ = lax.fori_loop(0, nblk, att_block, init)
    acc = acc_part + _dot(vt_ref[nblk - 1], p_ref[(nblk - 1) % 2])
    out = acc / l_run
    o_ref[...] = jnp.concatenate([out[:, h * tq:(h + 1) * tq] for h in range(DSA_HEADS)], axis=0).astype(BF16)


def _dsa(b_q, b_qi, small, b_kv, b_ki, bn, t):
    nt = t // DSA_TQ
    nkb = t // DSA_TK
    v_t = b_kv[:, HEAD_DIM:].reshape(bn * nkb, DSA_TK, HEAD_DIM).transpose(0, 2, 1)
    o_t = pl.pallas_call(
        _dsa_kernel,
        grid=(bn, nt),
        in_specs=[
            pl.BlockSpec((DSA_TQ, 256), lambda b, i: (b * nt + i, 0)),
            pl.BlockSpec((DSA_TQ, 128), lambda b, i: (b * nt + i, 0)),
            pl.BlockSpec((DSA_TQ, 128), lambda b, i: (b * nt + i, 0)),
            pl.BlockSpec((t, 128), lambda b, i: (b, 0)),
            pl.BlockSpec((nkb, HEAD_DIM, DSA_TK), lambda b, i: (b, 0, 0)),
            pl.BlockSpec((t, 128), lambda b, i: (b, 0)),
            pl.BlockSpec((DSA_TK, DSA_TK), lambda b, i: (0, 0)),
        ],
        out_specs=pl.BlockSpec((None, DSA_HEADS * HEAD_DIM, DSA_TQ), lambda b, i: (b * nt + i, 0, 0)),
        out_shape=jax.ShapeDtypeStruct((bn * nt, DSA_HEADS * HEAD_DIM, DSA_TQ), BF16),
        scratch_shapes=[pltpu.VMEM((nkb, DSA_TK, DSA_TQ), I32),
                        pltpu.VMEM((nkb, DSA_TK, DSA_HEADS * DSA_TQ), F32),
                        pltpu.VMEM((2, DSA_TK, DSA_HEADS * DSA_TQ), BF16),
                        pltpu.VMEM((nkb, DSA_TK // 2, DSA_TQ), I32), pltpu.VMEM((nkb, DSA_TK // 2, DSA_TQ), I32)],
        compiler_params=_cparams(("arbitrary", "arbitrary")),
        name="dsa_attention",
    )(b_q, b_qi, small, b_kv, v_t, b_ki, jnp.tril(jnp.ones((DSA_TK, DSA_TK), BF16)))
    return o_t.transpose(0, 2, 1).reshape(bn * t, DSA_HEADS * HEAD_DIM)


def _mlstm_kernel(qk_ref, v_ref, og_ref, sm_ref, convw_ref, convb_ref, gbias_ref, normgt_ref,
                  o_ref, tail_ref, ct_ref, nvec_ref, mst_ref, *, bn):
    c = pl.program_id(0)
    L = CHUNK
    nh, dh, width = MLSTM_HEADS, HEAD_DIM, MLSTM_HEADS * HEAD_DIM

    @pl.when(c == 0)
    def _():
        tail_ref[...] = jnp.zeros_like(tail_ref)
        ct_ref[...] = jnp.zeros_like(ct_ref)
        nvec_ref[...] = jnp.zeros_like(nvec_ref)
        mst_ref[...] = jnp.zeros_like(mst_ref)

    srow = lax.broadcasted_iota(I32, (L, width), 0)
    lane = lax.broadcasted_iota(I32, (L, width), 1)
    jlane = lane % dh
    causal_t = srow <= jlane
    diag_t = srow == jlane
    head_of_lane = lax.broadcasted_iota(I32, (1, width), 1) // dh
    tril = (lax.broadcasted_iota(I32, (L, L), 1) <= lax.broadcasted_iota(I32, (L, L), 0)).astype(F32)
    erow = lax.broadcasted_iota(I32, (128, width), 0)
    ecol_head = lax.broadcasted_iota(I32, (128, width), 1) // dh
    expand_ig = (erow == SM_IG + ecol_head).astype(F32)
    expand_fg = (erow == SM_FG + ecol_head).astype(F32)
    exact = dict(preferred_element_type=F32, precision=lax.Precision.HIGHEST)

    def head_blocks(a):
        out = jnp.where(head_of_lane == 0, a[0:dh], 0.0)
        for h in range(1, nh):
            out = out + jnp.where(head_of_lane == h, a[h * dh:(h + 1) * dh], 0.0)
        return out

    convw = convw_ref[...]
    for cc, b in [(cc, b) for cc in range(MLSTM_CPS) for b in range(bn)]:
        rows = slice(cc * L, (cc + 1) * L)
        cur = qk_ref[b, rows, :]
        ext = jnp.concatenate([tail_ref[b], cur], axis=0)
        tail_ref[b] = cur[L - 8:L, :]
        y = convb_ref[...] + convw[MLSTM_CONV - 1:MLSTM_CONV, :] * cur
        for k in range(MLSTM_CONV - 1):
            off = 8 - (MLSTM_CONV - 1) + k
            y = y + convw[k:k + 1, :] * ext[off:off + L, :]
        qk = y * _sigmoid(y)
        q_all = qk[:, 0:width]
        k_all = qk[:, width:2 * width] * dh ** -0.5
        q_stack = jnp.concatenate([jnp.where(head_of_lane == h, q_all, 0.0) for h in range(nh)], axis=0).astype(BF16)
        v_all = v_ref[b, rows, :]
        v_t = v_all.astype(F32).T.astype(BF16)

        gates = sm_ref[b, rows, :] + gbias_ref[...]
        lf = jnp.minimum(gates, 0.0) - jnp.log(1.0 + jnp.exp(-jnp.abs(gates)))
        bcum = jnp.dot(tril, lf, **exact)
        ig_x = jnp.dot(gates, expand_ig, **exact)
        b_x = jnp.dot(bcum, expand_fg, **exact)
        b_q = jnp.sum(jnp.where(diag_t, b_x, 0.0), axis=0, keepdims=True)
        b_last = b_x[L - 1:L, :]
        m_prev = mst_ref[b]
        ct = ct_ref[b]
        nvec = nvec_ref[b]

        dlog = jnp.where(causal_t, b_q - b_x + ig_x, NEG_INF)
        inter = b_q + m_prev
        mj = jnp.maximum(inter, jnp.max(dlog, axis=0, keepdims=True))
        dw = jnp.exp(dlog - mj)
        iw = jnp.exp(inter - mj)
        sc = _dot_nt(k_all.astype(BF16), q_stack) * dw
        qn = _dot_nt(jnp.broadcast_to(nvec, (8, width)).astype(BF16), q_stack)[0:1, :]
        q_c = _dot_nt(ct.astype(BF16), q_stack)
        num = iw * q_c + head_blocks(_dot(v_t, sc.astype(BF16)))
        den = iw * qn + jnp.sum(sc, axis=0, keepdims=True)
        hj = num / jnp.maximum(jnp.abs(den), jnp.exp(-mj))

        dec = b_last - b_x + ig_x
        m_new = jnp.maximum(b_last + m_prev, jnp.max(dec, axis=0, keepdims=True))
        wc = jnp.exp(b_last + m_prev - m_new)
        kw = k_all * jnp.exp(dec - m_new)
        ct_ref[b] = wc * ct + head_blocks(_dot(v_t, kw.astype(BF16)))
        nvec_ref[b] = wc * nvec + jnp.sum(kw, axis=0, keepdims=True)
        mst_ref[b] = m_new

        mu = jnp.mean(hj, axis=0, keepdims=True)
        dev = hj - mu
        var = jnp.mean(dev * dev, axis=0, keepdims=True)
        hn_t = (dev * lax.rsqrt(var + LN_EPS) * normgt_ref[...]).T
        hn = jnp.concatenate([hn_t[h * dh:(h + 1) * dh, :] for h in range(nh)], axis=1)
        o_ref[b, rows, :] = (_sigmoid(og_ref[b, rows, :]) * hn).astype(BF16)


def _mlstm(c_qk, c_v, c_o, small, conv_w, conv_b, gate_bias, norm_g, bn, t):
    nc = t // (CHUNK * MLSTM_CPS)
    width = MLSTM_HEADS * HEAD_DIM
    norm_g_t = jnp.repeat(norm_g.reshape(MLSTM_HEADS, HEAD_DIM).T, HEAD_DIM, axis=1)
    blk = lambda w: pl.BlockSpec((bn, CHUNK * MLSTM_CPS, w), lambda c: (0, c, 0))
    full = lambda a: pl.BlockSpec(a.shape, lambda c: (0,) * a.ndim)
    return pl.pallas_call(
        functools.partial(_mlstm_kernel, bn=bn),
        grid=(nc,),
        in_specs=[blk(512), blk(256), blk(256), blk(128), full(conv_w), full(conv_b), full(gate_bias), full(norm_g_t)],
        out_specs=blk(256),
        out_shape=jax.ShapeDtypeStruct((bn, t, 256), BF16),
        scratch_shapes=[pltpu.VMEM((bn, 8, 512), F32), pltpu.VMEM((bn, HEAD_DIM, width), F32),
                        pltpu.VMEM((bn, 1, width), F32), pltpu.VMEM((bn, 1, width), F32)],
        compiler_params=_cparams(("arbitrary",)),
        name="mlstm",
    )(c_qk.reshape(bn, t, 512), c_v.reshape(bn, t, 256), c_o.reshape(bn, t, 256), small.reshape(bn, t, 128),
      conv_w, conv_b, gate_bias, norm_g_t)


def _s5_kernel(u_ref, perm_ref, bbre_ref, bbim_ref, pwre_ref, pwim_ref, cre_ref, cim_ref, dskip_ref, gluw_ref,
               glub_ref, o_ref, sre_ref, sim_ref, stre_ref, stim_ref, yp_ref):
    @pl.when(pl.program_id(1) == 0)
    def _():
        stre_ref[...] = jnp.zeros_like(stre_ref)
        stim_ref[...] = jnp.zeros_like(stim_ref)

    nb = S5_NB
    seg_len = S5_TT // S5_SEGS
    us = [u_ref[b] for b in range(nb)]
    for b in range(nb):
        ub = _dot(perm_ref[...], us[b].astype(BF16)).astype(BF16)
        sre_ref[b] = _dot(ub, bbre_ref[...])
        sim_ref[b] = _dot(ub, bbim_ref[...])
    a_re = jnp.broadcast_to(pwre_ref[0:1, :], (S5_SEGS, S5_LANES))
    a_im = jnp.broadcast_to(pwim_ref[0:1, :], (S5_SEGS, S5_LANES))

    def local_step(i, carry):
        rows = pl.ds(pl.multiple_of(i * S5_SEGS, S5_SEGS), S5_SEGS)
        out = []
        for b in range(nb):
            s_re, s_im = carry[b]
            n_re = a_re * s_re - a_im * s_im + sre_ref[b, rows, :]
            n_im = a_re * s_im + a_im * s_re + sim_ref[b, rows, :]
            sre_ref[b, rows, :] = n_re
            sim_ref[b, rows, :] = n_im
            out.append((n_re, n_im))
        return tuple(out)

    zeros = jnp.zeros((S5_SEGS, S5_LANES), F32)
    ends = lax.fori_loop(0, seg_len, local_step, tuple((zeros, zeros) for _ in range(nb)), unroll=4)

    al_re, al_im = pwre_ref[seg_len - 1:seg_len, :], pwim_ref[seg_len - 1:seg_len, :]
    cins = []
    for b in range(nb):
        e_re, e_im = ends[b]
        c_re, c_im = stre_ref[b, 0:1, :], stim_ref[b, 0:1, :]
        cs_re, cs_im = [], []
        for k in range(S5_SEGS):
            cs_re.append(c_re)
            cs_im.append(c_im)
            c_re, c_im = (e_re[k:k + 1, :] + al_re * c_re - al_im * c_im,
                          e_im[k:k + 1, :] + al_re * c_im + al_im * c_re)
        stre_ref[b] = jnp.broadcast_to(c_re, stre_ref.shape[1:])
        stim_ref[b] = jnp.broadcast_to(c_im, stim_ref.shape[1:])
        cins.append((jnp.concatenate(cs_re, axis=0), jnp.concatenate(cs_im, axis=0)))

    def correct_step(i, carry):
        rows = pl.ds(pl.multiple_of(i * S5_SEGS, S5_SEGS), S5_SEGS)
        p_re, p_im = pwre_ref[pl.ds(i, 1), :], pwim_ref[pl.ds(i, 1), :]
        for b in range(nb):
            cin_re, cin_im = cins[b]
            sre_ref[b, rows, :] = sre_ref[b, rows, :] + p_re * cin_re - p_im * cin_im
            sim_ref[b, rows, :] = sim_ref[b, rows, :] + p_re * cin_im + p_im * cin_re
        return carry

    lax.fori_loop(0, seg_len, correct_step, 0, unroll=4)
    ngrp = GROUP_WIDTH // 128
    for b in range(nb):
        ycs = _dot(sre_ref[b].astype(BF16), cre_ref[...]) - _dot(sim_ref[b].astype(BF16), cim_ref[...])
        for g in range(ngrp):
            yp_ref[b, g] = ycs[:, g * 128:(g + 1) * 128]
        blocks = []
        for k in range(S5_SEGS):
            for i0 in range(0, seg_len, 8):
                rows = pl.ds(i0 * S5_SEGS + k, 8, stride=S5_SEGS)
                blocks.append(jnp.concatenate([yp_ref[b, g, rows, :] for g in range(ngrp)], axis=1))
        y = jnp.concatenate(blocks, axis=0) + dskip_ref[...] * us[b]
        y = 0.5 * y * (1.0 + jnp.tanh(math.sqrt(2.0 / math.pi) * (y + 0.044715 * (y * y * y))))
        z = _dot(y.astype(BF16), gluw_ref[...]) + glub_ref[...]
        o_ref[b] = (y * _sigmoid(z)).astype(BF16)


def _s5(d_u, bb_re, bb_im, pw_re, pw_im, c_re_t, c_im_t, d_skip, glu_w, glu_b, bn, t):
    nt = t // S5_TT
    nb = S5_NB
    full = lambda a: pl.BlockSpec(a.shape, lambda b, i: (0,) * a.ndim)
    r = jnp.arange(S5_TT)
    perm = (r[None, :] == ((r % S5_SEGS) * (S5_TT // S5_SEGS) + r // S5_SEGS)[:, None]).astype(BF16)
    args = (perm, bb_re, bb_im, pw_re, pw_im, c_re_t, c_im_t, d_skip, glu_w, glu_b)
    tile = pl.BlockSpec((nb, S5_TT, GROUP_WIDTH), lambda b, i: (b, i, 0))
    return pl.pallas_call(
        _s5_kernel,
        grid=(bn // nb, nt),
        in_specs=[tile] + [full(a) for a in args],
        out_specs=tile,
        out_shape=jax.ShapeDtypeStruct((bn, t, GROUP_WIDTH), BF16),
        scratch_shapes=[pltpu.VMEM((nb, S5_TT, S5_LANES), F32), pltpu.VMEM((nb, S5_TT, S5_LANES), F32),
                        pltpu.VMEM((nb, 8, S5_LANES), F32), pltpu.VMEM((nb, 8, S5_LANES), F32),
                        pltpu.VMEM((nb, GROUP_WIDTH // 128, S5_TT, 128), F32)],
        compiler_params=_cparams(("arbitrary", "arbitrary")),
        name="s5_glu",
    )(d_u.reshape(bn, t, GROUP_WIDTH), *args).reshape(bn * t, GROUP_WIDTH)


def _outproj_ffn_kernel(x_ref, modm_ref, oa_ref, ob_ref, oc_ref, od_ref, wout_ref, lngm_ref, lnbm_ref,
                        modf_ref, w13_ref, w2_ref, lngf_ref, lnbf_ref, o_ref, h_ref):
    y = _dot(oa_ref[...], wout_ref[0])
    y = y + _dot(ob_ref[...], wout_ref[1])
    y = y + _dot(oc_ref[...], wout_ref[2])
    y = y + _dot(od_ref[...], wout_ref[3])
    x_mid = _residual_layer_norm(x_ref[...], modm_ref[2:3, :] * y, lngm_ref[...], lnbm_ref[...])
    o_ref[...] = _ffn_tile(x_mid, modf_ref, w13_ref, w2_ref, lngf_ref, lnbf_ref, h_ref)


def _outproj_ffn(x2, mod_mix, o_a, o_b, o_c, o_d, w_out4, ln_g_mix, ln_b_mix, mod_ffn, w13c, w2c, ln_g_ffn, ln_b_ffn,
                 rows_per_batch):
    n = x2.shape[0]
    tm = ROW_TILE
    tiles_per_batch = rows_per_batch // tm
    row = pl.BlockSpec((tm, D_MODEL), lambda i: (i, 0))
    mod = pl.BlockSpec((None, 8, D_MODEL), lambda i: (i // tiles_per_batch, 0, 0))
    mix = pl.BlockSpec((tm, GROUP_WIDTH), lambda i: (i, 0))
    vec = pl.BlockSpec((1, D_MODEL), lambda i: (0, 0))
    resident = lambda a: pl.BlockSpec(a.shape, lambda i: (0,) * a.ndim, pipeline_mode=pl.Buffered(1))
    return pl.pallas_call(
        _outproj_ffn_kernel,
        grid=(n // tm,),
        in_specs=[row, mod, mix, mix, mix, mix, resident(w_out4), vec, vec, mod, resident(w13c), resident(w2c), vec, vec],
        out_specs=row,
        out_shape=jax.ShapeDtypeStruct((n, D_MODEL), F32),
        scratch_shapes=[pltpu.VMEM((tm, D_FF), BF16)],
        compiler_params=_cparams(("arbitrary",)),
        name="outproj_ffn",
    )(x2, mod_mix, o_a, o_b, o_c, o_d, w_out4, ln_g_mix.reshape(1, -1), ln_b_mix.reshape(1, -1),
      mod_ffn, w13c, w2c, ln_g_ffn.reshape(1, -1), ln_b_ffn.reshape(1, -1))


def _permute_w_in(w_in):
    off = {}
    o = 0
    for name, s in (("qa", 256), ("ka", 128), ("va", 128), ("cq", 128), ("kb", 64), ("vb", 64), ("ki", 32),
                    ("wi", 4), ("qkc", 512), ("vc", 256), ("ig", 4), ("fg", 4), ("oc", 256), ("ud", 256)):
        off[name] = (o, o + s)
        o += s
    col = lambda n: w_in[:, off[n][0]:off[n][1]]
    small = jnp.concatenate([col("wi"), col("ig"), col("fg"),
                             jnp.zeros((w_in.shape[0], 128 - 12), w_in.dtype)], axis=1)
    parts = [col("qa"), col("ka"), col("va"), col("cq"), col("kb"), col("vb")] + [col("ki")] * 4 + [
        col("qkc"), col("vc"), col("oc"), col("ud"), small]
    return jnp.concatenate(parts, axis=1).astype(BF16)


def _s5_params(lam_re, lam_im, log_step, b_re, b_im, c_re, c_im):
    dt = jnp.exp(log_step)[:, None]
    mag = jnp.exp(lam_re * dt)
    a_re, a_im = mag * jnp.cos(lam_im * dt), mag * jnp.sin(lam_im * dt)
    den = lam_re * lam_re + lam_im * lam_im
    kap_re = ((a_re - 1.0) * lam_re + a_im * lam_im) / den
    kap_im = (a_im * lam_re - (a_re - 1.0) * lam_im) / den
    bb_re = kap_re[..., None] * b_re - kap_im[..., None] * b_im
    bb_im = kap_re[..., None] * b_im + kap_im[..., None] * b_re
    eye = jnp.eye(S5_GROUPS, dtype=F32)

    def in_mat(bb):
        return jnp.einsum("gph,gk->ghkp", bb, eye).reshape(S5_GROUPS * S5_GROUP_CH, S5_LANES).astype(BF16)

    def out_mat(cc):
        return jnp.einsum("gop,gk->gpko", cc, eye).reshape(S5_LANES, S5_GROUPS * S5_GROUP_CH).astype(BF16)

    n = jnp.arange(1, S5_TT // S5_SEGS + 1, dtype=F32)[:, None, None]
    pw_mag = jnp.exp(n * (lam_re * dt))
    pw_re = (pw_mag * jnp.cos(n * (lam_im * dt))).at[0].set(a_re).reshape(-1, S5_LANES)
    pw_im = (pw_mag * jnp.sin(n * (lam_im * dt))).at[0].set(a_im).reshape(-1, S5_LANES)
    return in_mat(bb_re), in_mat(bb_im), pw_re, pw_im, out_mat(c_re), out_mat(c_im)


def kernel(x, c, ada_w, ada_b, ln_g, ln_b, ffn_w13, ffn_w2, w_in, w_out, sinks, w_uq, w_iq, conv_w, conv_b, ig_b,
           fg_b, mh_norm_g, lam_re, lam_im, log_step, b_re, b_im, c_re, c_im, d_skip, glu_w, glu_b):
    bn, t, d = x.shape
    assert d == D_MODEL and t % max(ROW_TILE, DSA_TK, S5_TT) == 0 and bn <= 8 and bn % S5_NB == 0
    n = bn * t
    nl = ada_w.shape[0]
    c_pad = jnp.zeros((8, d), F32).at[:bn].set(c)
    mod_all = _ada_mod(c_pad, ada_w, ada_b)
    mod_all = mod_all[:, :bn].reshape(nl, bn, N_SUB, 3, d).transpose(0, 2, 1, 3, 4)
    mod_all = jnp.pad(mod_all, ((0, 0), (0, 0), (0, 0), (0, 5), (0, 0)))

    x2 = x.reshape(n, d)
    for l in range(nl):
        w13c = ffn_w13[l].astype(BF16)
        w2c = ffn_w2[l].astype(BF16)
        x2, (a_q, a_kv, b_q, b_qi, b_kv, b_ki, c_qk, c_v, c_o, d_u, small) = _ffn_inproj(
            x2, mod_all[l, 0], w13c[0], w2c[0], ln_g[l, 0], ln_b[l, 0], mod_all[l, 1],
            _permute_w_in(w_in[l]), w_uq[l].astype(BF16), w_iq[l].astype(BF16), t)
        o_a = _swa(sinks[l], a_q, a_kv, bn, t)
        o_b = _dsa(b_q, b_qi, small, b_kv, b_ki, bn, t)
        gate_bias = jnp.zeros((1, 128), F32).at[0, SM_IG:SM_IG + 4].set(ig_b[l]).at[0, SM_FG:SM_FG + 4].set(fg_b[l])
        o_c = _mlstm(c_qk, c_v, c_o, small, conv_w[l], conv_b[l].reshape(1, -1), gate_bias,
                     mh_norm_g[l].reshape(1, -1), bn, t).reshape(n, GROUP_WIDTH)
        s5p = _s5_params(lam_re[l], lam_im[l], log_step[l], b_re[l], b_im[l], c_re[l], c_im[l])
        o_d = _s5(d_u, *s5p, d_skip[l].reshape(1, -1), glu_w[l].astype(BF16), glu_b[l].reshape(1, -1), bn, t)
        x2 = _outproj_ffn(x2, mod_all[l, 1], o_a, o_b, o_c, o_d, w_out[l].astype(BF16).reshape(4, GROUP_WIDTH, d),
                          ln_g[l, 1], ln_b[l, 1], mod_all[l, 2], w13c[1], w2c[1], ln_g[l, 2], ln_b[l, 2], t)
    return x2.reshape(bn, t, d)
```

```python
import functools
import math

import jax
import jax.numpy as jnp
from jax import lax
from jax.experimental import pallas as pl
from jax.experimental.pallas import tpu as pltpu

F32 = jnp.float32
BF16 = jnp.bfloat16
I32 = jnp.int32

D_MODEL = 1024
DEPTH = 2
CHUNK = 64
HEAD_DIM = 64
GROUP_WIDTH = 256
SWA_HEADS = 4
SWA_KV_HEADS = 2
SWA_WIN_CHUNKS = 2
DSA_HEADS = 4
DSA_Q_RANK = 128
IDX_HEADS = 4
IDX_DIM = 32
DSA_TOPK = 256
MLSTM_HEADS = 4
MLSTM_CONV = 4
MLSTM_CPS = 4
S5_GROUP_CH = 16
S5_GROUPS = 16
S5_STATE = 64
S5_LANES = S5_GROUPS * S5_STATE
D_FF = 2816
N_SUB = 3
ALPHA = (2 * DEPTH) ** 0.25
LN_EPS = 1e-5
NEG_INF = -1e30
INT_MIN = -(2 ** 31)
FIELD_BITS = 15
FIELD_GUARD = -(2 ** 31) + 2 ** 15

SLOPES_A = tuple(2.0 ** -(i + 1) for i in range(0, 8, 2))
SLOPES_B = tuple(2.0 ** -(i + 1) for i in range(1, 8, 2))

VMEM_LIMIT_BYTES = 56 * 1024 * 1024

FFN_TF = 256
FFN_NCHUNK = D_FF // FFN_TF
ROW_TILE = 512
SWA_TQ = 128
SWA_SUB = 4
DSA_TQ = 128
DSA_TK = 512
S5_TT = 512
S5_SEGS = 8
S5_NB = 2

ZC_QA, ZC_KVA, ZC_CQ, ZC_KVB, ZC_KI = 0, 256, 512, 640, 768
ZC_QKC, ZC_VC, ZC_OC, ZC_UD, ZC_SMALL, Z_WIDTH = 896, 1408, 1664, 1920, 2176, 2304
SM_WI, SM_IG, SM_FG = 0, 4, 8


def _cparams(sem):
    return pltpu.CompilerParams(dimension_semantics=sem, vmem_limit_bytes=VMEM_LIMIT_BYTES)


def _dot(a, b):
    return jnp.dot(a, b, preferred_element_type=F32)


def _dot_nt(a, b):
    return lax.dot_general(a, b, (((1,), (1,)), ((), ())), preferred_element_type=F32)


def _sigmoid(x):
    return 1.0 / (1.0 + jnp.exp(-x))


def _residual_layer_norm(x, y, g, b):
    v = ALPHA * x + y
    mu = jnp.mean(v, axis=-1, keepdims=True)
    d = v - mu
    var = jnp.mean(d * d, axis=-1, keepdims=True)
    return d * lax.rsqrt(var + LN_EPS) * g + b


def _ada_kernel(c_ref, w_ref, b_ref, o_ref):
    c = c_ref[...]
    cs = c * _sigmoid(c)
    o_ref[...] = jnp.dot(cs, w_ref[...], preferred_element_type=F32,
                         precision=lax.Precision.HIGHEST) + b_ref[...]


def _ada_mod(c_pad, ada_w, ada_b):
    nl = ada_w.shape[0]
    ncol = ada_w.shape[2] // D_MODEL
    return pl.pallas_call(
        _ada_kernel,
        grid=(nl, ncol),
        in_specs=[
            pl.BlockSpec((8, D_MODEL), lambda l, j: (0, 0)),
            pl.BlockSpec((None, D_MODEL, D_MODEL), lambda l, j: (l, 0, j)),
            pl.BlockSpec((None, 1, D_MODEL), lambda l, j: (l, 0, j)),
        ],
        out_specs=pl.BlockSpec((None, 8, D_MODEL), lambda l, j: (l, 0, j)),
        out_shape=jax.ShapeDtypeStruct((nl, 8, ada_w.shape[2]), F32),
        compiler_params=_cparams(("arbitrary", "arbitrary")),
        name="ada_mod",
    )(c_pad, ada_w, ada_b.reshape(nl, 1, -1))


def _ffn_tile(x, mod_ref, w13_ref, w2_ref, lng_ref, lnb_ref, h_ref):
    shift, scale, gate = mod_ref[0:1, :], mod_ref[1:2, :], mod_ref[2:3, :]
    u = (x * (1.0 + scale) + shift).astype(BF16)
    for j in range(FFN_NCHUNK):
        a = _dot(u, w13_ref[:, j * FFN_TF:(j + 1) * FFN_TF])
        g = _dot(u, w13_ref[:, D_FF + j * FFN_TF:D_FF + (j + 1) * FFN_TF])
        h_ref[:, j * FFN_TF:(j + 1) * FFN_TF] = (a * _sigmoid(a) * g).astype(BF16)
    y = _dot(h_ref[...], w2_ref[...])
    return _residual_layer_norm(x, 0.5 * gate * y, lng_ref[...], lnb_ref[...])


def _inproj_tile(x, mod_ref, w_ref, wuq_ref, wiq_ref, outs):
    (aq_ref, akv_ref, bq_ref, bqi_ref, bkv_ref, bki_ref, cqk_ref, cv_ref, co_ref, du_ref, sm_ref) = outs
    shift, scale = mod_ref[0:1, :], mod_ref[1:2, :]
    u = (x * (1.0 + scale) + shift).astype(BF16)
    z = _dot(u, w_ref[...])
    aq_ref[...] = z[:, ZC_QA:ZC_KVA].astype(BF16)
    akv_ref[...] = z[:, ZC_KVA:ZC_CQ].astype(BF16)
    cq = z[:, ZC_CQ:ZC_KVB].astype(BF16)
    bq_ref[...] = (_dot(cq, wuq_ref[...]) * HEAD_DIM ** -0.5).astype(BF16)
    bqi_ref[...] = _dot(cq, wiq_ref[...]).astype(BF16)
    bkv_ref[...] = z[:, ZC_KVB:ZC_KI].astype(BF16)
    bki_ref[...] = z[:, ZC_KI:ZC_QKC].astype(BF16)
    cqk_ref[...] = z[:, ZC_QKC:ZC_VC]
    cv_ref[...] = z[:, ZC_VC:ZC_OC].astype(BF16)
    co_ref[...] = z[:, ZC_OC:ZC_UD]
    du_ref[...] = z[:, ZC_UD:ZC_SMALL]
    sm_ref[...] = z[:, ZC_SMALL:Z_WIDTH]


INPROJ_OUTPUTS = ((256, BF16), (256, BF16), (256, BF16), (128, BF16), (128, BF16), (128, BF16),
                  (512, F32), (256, BF16), (256, F32), (256, F32), (128, F32))


def _ffn_inproj_kernel(x_ref, modf_ref, w13_ref, w2_ref, lng_ref, lnb_ref, modm_ref, win_ref, wuq_ref, wiq_ref,
                       o_ref, *rest):
    outs, h_ref = rest[:-1], rest[-1]
    x_new = _ffn_tile(x_ref[...], modf_ref, w13_ref, w2_ref, lng_ref, lnb_ref, h_ref)
    o_ref[...] = x_new
    _inproj_tile(x_new, modm_ref, win_ref, wuq_ref, wiq_ref, outs)


def _ffn_inproj(x2, mod_ffn, w13c, w2c, ln_g, ln_b, mod_mix, w_perm, w_uq, w_iq, rows_per_batch):
    n = x2.shape[0]
    tm = ROW_TILE
    tiles_per_batch = rows_per_batch // tm
    row = pl.BlockSpec((tm, D_MODEL), lambda i: (i, 0))
    mod = pl.BlockSpec((None, 8, D_MODEL), lambda i: (i // tiles_per_batch, 0, 0))
    vec = pl.BlockSpec((1, D_MODEL), lambda i: (0, 0))
    resident = lambda a: pl.BlockSpec(a.shape, lambda i: (0,) * a.ndim, pipeline_mode=pl.Buffered(1))
    outs = pl.pallas_call(
        _ffn_inproj_kernel,
        grid=(n // tm,),
        in_specs=[row, mod, resident(w13c), resident(w2c), vec, vec, mod, resident(w_perm), resident(w_uq),
                  resident(w_iq)],
        out_specs=[row] + [pl.BlockSpec((tm, w), lambda i: (i, 0)) for w, _ in INPROJ_OUTPUTS],
        out_shape=[jax.ShapeDtypeStruct((n, D_MODEL), F32)]
        + [jax.ShapeDtypeStruct((n, w), dt) for w, dt in INPROJ_OUTPUTS],
        scratch_shapes=[pltpu.VMEM((tm, D_FF), BF16)],
        compiler_params=_cparams(("arbitrary",)),
        name="ffn_inproj",
    )(x2, mod_ffn, w13c, w2c, ln_g.reshape(1, -1), ln_b.reshape(1, -1), mod_mix, w_perm, w_uq, w_iq)
    return outs[0], outs[1:]


def _swa_kernel(sink_ref, q_ref, kvc_ref, kvp_ref, o_ref):
    i = pl.program_id(1)
    tq = SWA_TQ
    rep = SWA_HEADS // SWA_KV_HEADS
    for r in range(SWA_SUB):
        rows = slice(r * tq, (r + 1) * tq)
        q = q_ref[rows, :]
        prev = kvp_ref[...] if r == 0 else kvc_ref[(r - 1) * tq:r * tq, :]
        kv = jnp.concatenate([prev, kvc_ref[rows, :]], axis=0)
        first = (i * SWA_SUB + r) * tq
        qpos = first + lax.broadcasted_iota(I32, (tq, 2 * tq), 0)
        kpos = first - tq + lax.broadcasted_iota(I32, (tq, 2 * tq), 1)
        qchunk = qpos // CHUNK
        kchunk = (kpos + tq) // CHUNK - tq // CHUNK
        valid = (kpos >= 0) & (kchunk <= qchunk) & (kchunk >= qchunk - SWA_WIN_CHUNKS)
        dist = jnp.abs(qpos - kpos).astype(F32)
        for h in range(SWA_HEADS):
            g = h // rep
            qh = q[:, h * HEAD_DIM:(h + 1) * HEAD_DIM]
            kg = kv[:, g * HEAD_DIM:(g + 1) * HEAD_DIM]
            vg = kv[:, (SWA_KV_HEADS + g) * HEAD_DIM:(SWA_KV_HEADS + g + 1) * HEAD_DIM]
            s = _dot_nt(qh, kg) * HEAD_DIM ** -0.5 - SLOPES_A[h] * dist
            s = jnp.where(valid, s, NEG_INF)
            sink = sink_ref[h]
            m = jnp.maximum(jnp.max(s, axis=-1, keepdims=True), sink)
            p = jnp.exp(s - m)
            denom = jnp.sum(p, axis=-1, keepdims=True) + jnp.exp(sink - m)
            o = _dot(p.astype(BF16), vg) / denom
            o_ref[rows, h * HEAD_DIM:(h + 1) * HEAD_DIM] = o.astype(BF16)


def _swa(sinks, a_q, a_kv, bn, t):
    rows = SWA_SUB * SWA_TQ
    nt = t // rows
    cur = pl.BlockSpec((rows, 256), lambda b, i: (b * nt + i, 0))
    return pl.pallas_call(
        _swa_kernel,
        grid=(bn, nt),
        in_specs=[
            pl.BlockSpec(memory_space=pltpu.SMEM),
            cur, cur,
            pl.BlockSpec((SWA_TQ, 256), lambda b, i: (jnp.maximum((b * nt + i) * SWA_SUB - 1, 0), 0)),
        ],
        out_specs=cur,
        out_shape=jax.ShapeDtypeStruct((bn * t, 256), BF16),
        compiler_params=_cparams(("arbitrary", "arbitrary")),
        name="swa_attention",
    )(sinks, a_q, a_kv, a_kv)


def _sortable_key(x):
    bits = lax.bitcast_convert_type(x, I32)
    return bits ^ ((bits >> 31) & 0x7FFFFFFF)


def _dsa_kernel(q_ref, qi_ref, sm_ref, kv_ref, vt_ref, ki_ref, tril_ref, o_ref, key_ref, s_ref, p_ref, w1_ref, w2_ref):
    i = pl.program_id(1)
    tq, tk = DSA_TQ, DSA_TK
    nblk = (i * tq + tq + tk - 1) // tk
    qpos = i * tq + lax.broadcasted_iota(I32, (1, tq), 1)
    qchunk = qpos // CHUNK
    row_k = lax.broadcasted_iota(I32, (tk, tq), 0)

    qi = qi_ref[...]
    lane_i = lax.broadcasted_iota(I32, (1, IDX_HEADS * IDX_DIM), 1) // IDX_DIM
    qi_stack = jnp.concatenate([jnp.where(lane_i == h, qi, jnp.zeros_like(qi)) for h in range(IDX_HEADS)], axis=0)
    sm_t = sm_ref[...].T
    w_idx = [sm_t[SM_WI + h:SM_WI + h + 1, :] for h in range(IDX_HEADS)]
    idx_scale = (IDX_DIM * IDX_HEADS) ** -0.5
    q = q_ref[...]
    q_stack = jnp.concatenate([q[:, h * HEAD_DIM:(h + 1) * HEAD_DIM] for h in range(DSA_HEADS)], axis=0)

    half = tk // 2
    guard = jnp.int32(FIELD_GUARD)

    def pack_fields(f):
        return (f[0:half] << 16) | f[half:tk] | guard

    def score_block(j, mask_inadmissible):
        rows = pl.ds(pl.multiple_of(j * tk, tk), tk)
        s_ref[j] = _dot_nt(kv_ref[rows, 0:HEAD_DIM], q_stack)
        d = _dot_nt(ki_ref[rows, :], qi_stack)
        acc = w_idx[0] * jnp.maximum(d[:, 0:tq], 0.0)
        for h in range(1, IDX_HEADS):
            acc = acc + w_idx[h] * jnp.maximum(d[:, h * tq:(h + 1) * tq], 0.0)
        sc = acc * idx_scale
        if mask_inadmissible:
            sc = jnp.where((j * tk + row_k) // CHUNK <= qchunk, sc, NEG_INF)
        key = _sortable_key(sc)
        key_ref[j] = key
        w1_ref[j] = pack_fields(lax.shift_right_logical(key ^ INT_MIN, 32 - FIELD_BITS))

    def full_blocks(jj, carry):
        for r in range(4):
            score_block(4 * jj + r, False)
        return carry

    nfull = nblk - 1
    lax.fori_loop(0, nfull // 4, full_blocks, 0)

    @pl.when(nfull % 4 >= 2)
    def _():
        score_block((nfull // 4) * 4, False)
        score_block((nfull // 4) * 4 + 1, False)

    @pl.when(nfull % 2 == 1)
    def _():
        score_block(nfull - 1, False)

    score_block(nblk - 1, True)

    k_eff = jnp.minimum(DSA_TOPK, (qchunk + 1) * CHUNK)

    def over_blocks(body, init):
        c = lax.fori_loop(0, nblk // 2, lambda jj, c: body(2 * jj + 1, body(2 * jj, c)), init)
        return lax.cond(nblk % 2 == 1, lambda c: body(nblk - 1, c), lambda c: c, c)

    def count(pred):
        def blk(j, c):
            m = pred(key_ref[j], j * tk + row_k).astype(I32)
            return c + jnp.sum(m.reshape(tk // 8, 8, tq), axis=0)
        return jnp.sum(over_blocks(blk, jnp.zeros((8, tq), I32)), axis=0, keepdims=True)

    def count_fields(w_ref, cand):
        cand2 = (cand << 16) | cand
        def blk(j, c):
            hit = ((w_ref[j] - cand2) >> 15) & 0x00010001
            return c + jnp.sum(hit.reshape(half // 8, 8, tq), axis=0)
        c = over_blocks(blk, jnp.zeros((8, tq), I32))
        return jnp.sum((c & 0xFFFF) + (c >> 16), axis=0, keepdims=True)

    def field_search(w_ref, k_want):
        def step(bi, carry):
            prefix, above = carry
            cand = prefix | (jnp.int32(1) << (FIELD_BITS - 1 - bi))
            cnt = count_fields(w_ref, cand)
            ok = cnt >= k_want
            return jnp.where(ok, cand, prefix), jnp.where(ok, above, cnt)
        return lax.fori_loop(0, FIELD_BITS, step, (jnp.zeros((1, tq), I32), jnp.zeros((1, tq), I32)))

    top, above = field_search(w1_ref, k_eff)
    field_max = (1 << FIELD_BITS) - 1

    def pack_mid(j, carry):
        ukey = key_ref[j] ^ INT_MIN
        member = lax.shift_right_logical(ukey, 32 - FIELD_BITS) == top
        mid = lax.shift_right_logical(ukey, 32 - 2 * FIELD_BITS) & field_max
        w2_ref[j] = pack_fields(jnp.where(member, mid, 0))
        return carry

    lax.fori_loop(0, nblk, pack_mid, 0)
    mid, above_mid = field_search(w2_ref, k_eff - above)

    def bit_step(bi, carry):
        prefix, c_gt = carry
        cand_u = prefix | (jnp.int32(1) << (31 - 2 * FIELD_BITS - bi))
        cand_s = cand_u ^ INT_MIN
        cnt = count(lambda key, kpos: key >= cand_s)
        ok = cnt >= k_eff
        return jnp.where(ok, cand_u, prefix), jnp.where(ok, c_gt, cnt)

    prefix, c_gt = lax.fori_loop(0, 32 - 2 * FIELD_BITS, bit_step,
                                 ((top << (32 - FIELD_BITS)) | (mid << (32 - 2 * FIELD_BITS)), above + above_mid))
    thr = prefix ^ INT_MIN

    need = (k_eff - c_gt).astype(F32)
    tril = tril_ref[...]

    offs = (row_k - qpos).astype(F32)
    p_ref[1] = jnp.zeros(p_ref.shape[1:], BF16)

    def att_block(j, carry):
        m_run, l_run, acc_part, ties_seen = carry
        slot = j % 2
        acc = acc_part + _dot(vt_ref[jnp.maximum(j - 1, 0)], p_ref[1 - slot])
        key = key_ref[j]
        tie = key == thr
        tie_rank = _dot(tril, jnp.where(tie, 1.0, 0.0).astype(BF16)) + ties_seen
        sel = (key > thr) | (tie & (tie_rank <= need))
        dist = jnp.where(sel, jnp.abs(offs + (j * tk).astype(F32)), jnp.inf)
        ms, ls, alphas = [], [], []
        for h in range(DSA_HEADS):
            cols = slice(h * tq, (h + 1) * tq)
            sh = s_ref[j, :, cols] - SLOPES_B[h] * dist
            m_old = m_run[:, cols]
            m_new = jnp.maximum(m_old, jnp.max(sh, axis=0, keepdims=True))
            alpha = jnp.exp(m_old - m_new)
            p = jnp.exp(sh - m_new)
            p_ref[slot, :, cols] = p.astype(BF16)
            ms.append(m_new)
            ls.append(alpha * l_run[:, cols] + jnp.sum(p, axis=0, keepdims=True))
            alphas.append(alpha)
        return (jnp.concatenate(ms, axis=1), jnp.concatenate(ls, axis=1), jnp.concatenate(alphas, axis=1) * acc,
                tie_rank[tk - 1:tk, :])

    init = (jnp.full((1, DSA_HEADS * tq), NEG_INF, F32), jnp.zeros((1, DSA_HEADS * tq), F32),
            jnp.zeros((HEAD_DIM, DSA_HEADS * tq), F32), jnp.zeros((1, tq), F32))
    _, l_run, acc_part, _ = lax.fori_loop(0, nblk, att_block, init)
    acc = acc_part + _dot(vt_ref[nblk - 1], p_ref[(nblk - 1) % 2])
    out = acc / l_run
    o_ref[...] = jnp.concatenate([out[:, h * tq:(h + 1) * tq] for h in range(DSA_HEADS)], axis=0).astype(BF16)


def _dsa(b_q, b_qi, small, b_kv, b_ki, bn, t):
    nt = t // DSA_TQ
    nkb = t // DSA_TK
    v_t = b_kv[:, HEAD_DIM:].reshape(bn * nkb, DSA_TK, HEAD_DIM).transpose(0, 2, 1)
    o_t = pl.pallas_call(
        _dsa_kernel,
        grid=(bn, nt),
        in_specs=[
            pl.BlockSpec((DSA_TQ, 256), lambda b, i: (b * nt + i, 0)),
            pl.BlockSpec((DSA_TQ, 128), lambda b, i: (b * nt + i, 0)),
            pl.BlockSpec((DSA_TQ, 128), lambda b, i: (b * nt + i, 0)),
            pl.BlockSpec((t, 128), lambda b, i: (b, 0)),
            pl.BlockSpec((nkb, HEAD_DIM, DSA_TK), lambda b, i: (b, 0, 0)),
            pl.BlockSpec((t, 128), lambda b, i: (b, 0)),
            pl.BlockSpec((DSA_TK, DSA_TK), lambda b, i: (0, 0)),
        ],
        out_specs=pl.BlockSpec((None, DSA_HEADS * HEAD_DIM, DSA_TQ), lambda b, i: (b * nt + i, 0, 0)),
        out_shape=jax.ShapeDtypeStruct((bn * nt, DSA_HEADS * HEAD_DIM, DSA_TQ), BF16),
        scratch_shapes=[pltpu.VMEM((nkb, DSA_TK, DSA_TQ), I32),
                        pltpu.VMEM((nkb, DSA_TK, DSA_HEADS * DSA_TQ), F32),
                        pltpu.VMEM((2, DSA_TK, DSA_HEADS * DSA_TQ), BF16),
                        pltpu.VMEM((nkb, DSA_TK // 2, DSA_TQ), I32), pltpu.VMEM((nkb, DSA_TK // 2, DSA_TQ), I32)],
        compiler_params=_cparams(("arbitrary", "arbitrary")),
        name="dsa_attention",
    )(b_q, b_qi, small, b_kv, v_t, b_ki, jnp.tril(jnp.ones((DSA_TK, DSA_TK), BF16)))
    return o_t.transpose(0, 2, 1).reshape(bn * t, DSA_HEADS * HEAD_DIM)


def _mlstm_kernel(qk_ref, v_ref, og_ref, sm_ref, convw_ref, convb_ref, gbias_ref, normgt_ref,
                  o_ref, tail_ref, ct_ref, nvec_ref, mst_ref, *, bn):
    c = pl.program_id(0)
    L = CHUNK
    nh, dh, width = MLSTM_HEADS, HEAD_DIM, MLSTM_HEADS * HEAD_DIM

    @pl.when(c == 0)
    def _():
        tail_ref[...] = jnp.zeros_like(tail_ref)
        ct_ref[...] = jnp.zeros_like(ct_ref)
        nvec_ref[...] = jnp.zeros_like(nvec_ref)
        mst_ref[...] = jnp.zeros_like(mst_ref)

    srow = lax.broadcasted_iota(I32, (L, width), 0)
    lane = lax.broadcasted_iota(I32, (L, width), 1)
    jlane = lane % dh
    causal_t = srow <= jlane
    diag_t = srow == jlane
    head_of_lane = lax.broadcasted_iota(I32, (1, width), 1) // dh
    tril = (lax.broadcasted_iota(I32, (L, L), 1) <= lax.broadcasted_iota(I32, (L, L), 0)).astype(F32)
    erow = lax.broadcasted_iota(I32, (128, width), 0)
    ecol_head = lax.broadcasted_iota(I32, (128, width), 1) // dh
    expand_ig = (erow == SM_IG + ecol_head).astype(F32)
    expand_fg = (erow == SM_FG + ecol_head).astype(F32)
    exact = dict(preferred_element_type=F32, precision=lax.Precision.HIGHEST)

    def head_blocks(a):
        out = jnp.where(head_of_lane == 0, a[0:dh], 0.0)
        for h in range(1, nh):
            out = out + jnp.where(head_of_lane == h, a[h * dh:(h + 1) * dh], 0.0)
        return out

    convw = convw_ref[...]
    for cc, b in [(cc, b) for cc in range(MLSTM_CPS) for b in range(bn)]:
        rows = slice(cc * L, (cc + 1) * L)
        cur = qk_ref[b, rows, :]
        ext = jnp.concatenate([tail_ref[b], cur], axis=0)
        tail_ref[b] = cur[L - 8:L, :]
        y = convb_ref[...] + convw[MLSTM_CONV - 1:MLSTM_CONV, :] * cur
        for k in range(MLSTM_CONV - 1):
            off = 8 - (MLSTM_CONV - 1) + k
            y = y + convw[k:k + 1, :] * ext[off:off + L, :]
        qk = y * _sigmoid(y)
        q_all = qk[:, 0:width]
        k_all = qk[:, width:2 * width] * dh ** -0.5
        q_stack = jnp.concatenate([jnp.where(head_of_lane == h, q_all, 0.0) for h in range(nh)], axis=0).astype(BF16)
        v_all = v_ref[b, rows, :]
        v_t = v_all.astype(F32).T.astype(BF16)

        gates = sm_ref[b, rows, :] + gbias_ref[...]
        lf = jnp.minimum(gates, 0.0) - jnp.log(1.0 + jnp.exp(-jnp.abs(gates)))
        bcum = jnp.dot(tril, lf, **exact)
        ig_x = jnp.dot(gates, expand_ig, **exact)
        b_x = jnp.dot(bcum, expand_fg, **exact)
        b_q = jnp.sum(jnp.where(diag_t, b_x, 0.0), axis=0, keepdims=True)
        b_last = b_x[L - 1:L, :]
        m_prev = mst_ref[b]
        ct = ct_ref[b]
        nvec = nvec_ref[b]

        dlog = jnp.where(causal_t, b_q - b_x + ig_x, NEG_INF)
        inter = b_q + m_prev
        mj = jnp.maximum(inter, jnp.max(dlog, axis=0, keepdims=True))
        dw = jnp.exp(dlog - mj)
        iw = jnp.exp(inter - mj)
        sc = _dot_nt(k_all.astype(BF16), q_stack) * dw
        qn = _dot_nt(jnp.broadcast_to(nvec, (8, width)).astype(BF16), q_stack)[0:1, :]
        q_c = _dot_nt(ct.astype(BF16), q_stack)
        num = iw * q_c + head_blocks(_dot(v_t, sc.astype(BF16)))
        den = iw * qn + jnp.sum(sc, axis=0, keepdims=True)
        hj = num / jnp.maximum(jnp.abs(den), jnp.exp(-mj))

        dec = b_last - b_x + ig_x
        m_new = jnp.maximum(b_last + m_prev, jnp.max(dec, axis=0, keepdims=True))
        wc = jnp.exp(b_last + m_prev - m_new)
        kw = k_all * jnp.exp(dec - m_new)
        ct_ref[b] = wc * ct + head_blocks(_dot(v_t, kw.astype(BF16)))
        nvec_ref[b] = wc * nvec + jnp.sum(kw, axis=0, keepdims=True)
        mst_ref[b] = m_new

        mu = jnp.mean(hj, axis=0, keepdims=True)
        dev = hj - mu
        var = jnp.mean(dev * dev, axis=0, keepdims=True)
        hn_t = (dev * lax.rsqrt(var + LN_EPS) * normgt_ref[...]).T
        hn = jnp.concatenate([hn_t[h * dh:(h + 1) * dh, :] for h in range(nh)], axis=1)
        o_ref[b, rows, :] = (_sigmoid(og_ref[b, rows, :]) * hn).astype(BF16)


def _mlstm(c_qk, c_v, c_o, small, conv_w, conv_b, gate_bias, norm_g, bn, t):
    nc = t // (CHUNK * MLSTM_CPS)
    width = MLSTM_HEADS * HEAD_DIM
    norm_g_t = jnp.repeat(norm_g.reshape(MLSTM_HEADS, HEAD_DIM).T, HEAD_DIM, axis=1)
    blk = lambda w: pl.BlockSpec((bn, CHUNK * MLSTM_CPS, w), lambda c: (0, c, 0))
    full = lambda a: pl.BlockSpec(a.shape, lambda c: (0,) * a.ndim)
    return pl.pallas_call(
        functools.partial(_mlstm_kernel, bn=bn),
        grid=(nc,),
        in_specs=[blk(512), blk(256), blk(256), blk(128), full(conv_w), full(conv_b), full(gate_bias), full(norm_g_t)],
        out_specs=blk(256),
        out_shape=jax.ShapeDtypeStruct((bn, t, 256), BF16),
        scratch_shapes=[pltpu.VMEM((bn, 8, 512), F32), pltpu.VMEM((bn, HEAD_DIM, width), F32),
                        pltpu.VMEM((bn, 1, width), F32), pltpu.VMEM((bn, 1, width), F32)],
        compiler_params=_cparams(("arbitrary",)),
        name="mlstm",
    )(c_qk.reshape(bn, t, 512), c_v.reshape(bn, t, 256), c_o.reshape(bn, t, 256), small.reshape(bn, t, 128),
      conv_w, conv_b, gate_bias, norm_g_t)


def _s5_kernel(u_ref, perm_ref, bbre_ref, bbim_ref, pwre_ref, pwim_ref, cre_ref, cim_ref, dskip_ref, gluw_ref,
               glub_ref, o_ref, sre_ref, sim_ref, stre_ref, stim_ref, yp_ref):
    @pl.when(pl.program_id(1) == 0)
    def _():
        stre_ref[...] = jnp.zeros_like(stre_ref)
        stim_ref[...] = jnp.zeros_like(stim_ref)

    nb = S5_NB
    seg_len = S5_TT // S5_SEGS
    us = [u_ref[b] for b in range(nb)]
    for b in range(nb):
        ub = _dot(perm_ref[...], us[b].astype(BF16)).astype(BF16)
        sre_ref[b] = _dot(ub, bbre_ref[...])
        sim_ref[b] = _dot(ub, bbim_ref[...])
    a_re = jnp.broadcast_to(pwre_ref[0:1, :], (S5_SEGS, S5_LANES))
    a_im = jnp.broadcast_to(pwim_ref[0:1, :], (S5_SEGS, S5_LANES))

    def local_step(i, carry):
        rows = pl.ds(pl.multiple_of(i * S5_SEGS, S5_SEGS), S5_SEGS)
        out = []
        for b in range(nb):
            s_re, s_im = carry[b]
            n_re = a_re * s_re - a_im * s_im + sre_ref[b, rows, :]
            n_im = a_re * s_im + a_im * s_re + sim_ref[b, rows, :]
            sre_ref[b, rows, :] = n_re
            sim_ref[b, rows, :] = n_im
            out.append((n_re, n_im))
        return tuple(out)

    zeros = jnp.zeros((S5_SEGS, S5_LANES), F32)
    ends = lax.fori_loop(0, seg_len, local_step, tuple((zeros, zeros) for _ in range(nb)), unroll=4)

    al_re, al_im = pwre_ref[seg_len - 1:seg_len, :], pwim_ref[seg_len - 1:seg_len, :]
    cins = []
    for b in range(nb):
        e_re, e_im = ends[b]
        c_re, c_im = stre_ref[b, 0:1, :], stim_ref[b, 0:1, :]
        cs_re, cs_im = [], []
        for k in range(S5_SEGS):
            cs_re.append(c_re)
            cs_im.append(c_im)
            c_re, c_im = (e_re[k:k + 1, :] + al_re * c_re - al_im * c_im,
                          e_im[k:k + 1, :] + al_re * c_im + al_im * c_re)
        stre_ref[b] = jnp.broadcast_to(c_re, stre_ref.shape[1:])
        stim_ref[b] = jnp.broadcast_to(c_im, stim_ref.shape[1:])
        cins.append((jnp.concatenate(cs_re, axis=0), jnp.concatenate(cs_im, axis=0)))

    def correct_step(i, carry):
        rows = pl.ds(pl.multiple_of(i * S5_SEGS, S5_SEGS), S5_SEGS)
        p_re, p_im = pwre_ref[pl.ds(i, 1), :], pwim_ref[pl.ds(i, 1), :]
        for b in range(nb):
            cin_re, cin_im = cins[b]
            sre_ref[b, rows, :] = sre_ref[b, rows, :] + p_re * cin_re - p_im * cin_im
            sim_ref[b, rows, :] = sim_ref[b, rows, :] + p_re * cin_im + p_im * cin_re
        return carry

    lax.fori_loop(0, seg_len, correct_step, 0, unroll=4)
    ngrp = GROUP_WIDTH // 128
    for b in range(nb):
        ycs = _dot(sre_ref[b].astype(BF16), cre_ref[...]) - _dot(sim_ref[b].astype(BF16), cim_ref[...])
        for g in range(ngrp):
            yp_ref[b, g] = ycs[:, g * 128:(g + 1) * 128]
        blocks = []
        for k in range(S5_SEGS):
            for i0 in range(0, seg_len, 8):
                rows = pl.ds(i0 * S5_SEGS + k, 8, stride=S5_SEGS)
                blocks.append(jnp.concatenate([yp_ref[b, g, rows, :] for g in range(ngrp)], axis=1))
        y = jnp.concatenate(blocks, axis=0) + dskip_ref[...] * us[b]
        y = 0.5 * y * (1.0 + jnp.tanh(math.sqrt(2.0 / math.pi) * (y + 0.044715 * (y * y * y))))
        z = _dot(y.astype(BF16), gluw_ref[...]) + glub_ref[...]
        o_ref[b] = (y * _sigmoid(z)).astype(BF16)


def _s5(d_u, bb_re, bb_im, pw_re, pw_im, c_re_t, c_im_t, d_skip, glu_w, glu_b, bn, t):
    nt = t // S5_TT
    nb = S5_NB
    full = lambda a: pl.BlockSpec(a.shape, lambda b, i: (0,) * a.ndim)
    r = jnp.arange(S5_TT)
    perm = (r[None, :] == ((r % S5_SEGS) * (S5_TT // S5_SEGS) + r // S5_SEGS)[:, None]).astype(BF16)
    args = (perm, bb_re, bb_im, pw_re, pw_im, c_re_t, c_im_t, d_skip, glu_w, glu_b)
    tile = pl.BlockSpec((nb, S5_TT, GROUP_WIDTH), lambda b, i: (b, i, 0))
    return pl.pallas_call(
        _s5_kernel,
        grid=(bn // nb, nt),
        in_specs=[tile] + [full(a) for a in args],
        out_specs=tile,
        out_shape=jax.ShapeDtypeStruct((bn, t, GROUP_WIDTH), BF16),
        scratch_shapes=[pltpu.VMEM((nb, S5_TT, S5_LANES), F32), pltpu.VMEM((nb, S5_TT, S5_LANES), F32),
                        pltpu.VMEM((nb, 8, S5_LANES), F32), pltpu.VMEM((nb, 8, S5_LANES), F32),
                        pltpu.VMEM((nb, GROUP_WIDTH // 128, S5_TT, 128), F32)],
        compiler_params=_cparams(("arbitrary", "arbitrary")),
        name="s5_glu",
    )(d_u.reshape(bn, t, GROUP_WIDTH), *args).reshape(bn * t, GROUP_WIDTH)


def _outproj_ffn_kernel(x_ref, modm_ref, oa_ref, ob_ref, oc_ref, od_ref, wout_ref, lngm_ref, lnbm_ref,
                        modf_ref, w13_ref, w2_ref, lngf_ref, lnbf_ref, o_ref, h_ref):
    y = _dot(oa_ref[...], wout_ref[0])
    y = y + _dot(ob_ref[...], wout_ref[1])
    y = y + _dot(oc_ref[...], wout_ref[2])
    y = y + _dot(od_ref[...], wout_ref[3])
    x_mid = _residual_layer_norm(x_ref[...], modm_ref[2:3, :] * y, lngm_ref[...], lnbm_ref[...])
    o_ref[...] = _ffn_tile(x_mid, modf_ref, w13_ref, w2_ref, lngf_ref, lnbf_ref, h_ref)


def _outproj_ffn(x2, mod_mix, o_a, o_b, o_c, o_d, w_out4, ln_g_mix, ln_b_mix, mod_ffn, w13c, w2c, ln_g_ffn, ln_b_ffn,
                 rows_per_batch):
    n = x2.shape[0]
    tm = ROW_TILE
    tiles_per_batch = rows_per_batch // tm
    row = pl.BlockSpec((tm, D_MODEL), lambda i: (i, 0))
    mod = pl.BlockSpec((None, 8, D_MODEL), lambda i: (i // tiles_per_batch, 0, 0))
    mix = pl.BlockSpec((tm, GROUP_WIDTH), lambda i: (i, 0))
    vec = pl.BlockSpec((1, D_MODEL), lambda i: (0, 0))
    resident = lambda a: pl.BlockSpec(a.shape, lambda i: (0,) * a.ndim, pipeline_mode=pl.Buffered(1))
    return pl.pallas_call(
        _outproj_ffn_kernel,
        grid=(n // tm,),
        in_specs=[row, mod, mix, mix, mix, mix, resident(w_out4), vec, vec, mod, resident(w13c), resident(w2c), vec, vec],
        out_specs=row,
        out_shape=jax.ShapeDtypeStruct((n, D_MODEL), F32),
        scratch_shapes=[pltpu.VMEM((tm, D_FF), BF16)],
        compiler_params=_cparams(("arbitrary",)),
        name="outproj_ffn",
    )(x2, mod_mix, o_a, o_b, o_c, o_d, w_out4, ln_g_mix.reshape(1, -1), ln_b_mix.reshape(1, -1),
      mod_ffn, w13c, w2c, ln_g_ffn.reshape(1, -1), ln_b_ffn.reshape(1, -1))


def _permute_w_in(w_in):
    off = {}
    o = 0
    for name, s in (("qa", 256), ("ka", 128), ("va", 128), ("cq", 128), ("kb", 64), ("vb", 64), ("ki", 32),
                    ("wi", 4), ("qkc", 512), ("vc", 256), ("ig", 4), ("fg", 4), ("oc", 256), ("ud", 256)):
        off[name] = (o, o + s)
        o += s
    col = lambda n: w_in[:, off[n][0]:off[n][1]]
    small = jnp.concatenate([col("wi"), col("ig"), col("fg"),
                             jnp.zeros((w_in.shape[0], 128 - 12), w_in.dtype)], axis=1)
    parts = [col("qa"), col("ka"), col("va"), col("cq"), col("kb"), col("vb")] + [col("ki")] * 4 + [
        col("qkc"), col("vc"), col("oc"), col("ud"), small]
    return jnp.concatenate(parts, axis=1).astype(BF16)


def _s5_params(lam_re, lam_im, log_step, b_re, b_im, c_re, c_im):
    dt = jnp.exp(log_step)[:, None]
    mag = jnp.exp(lam_re * dt)
    a_re, a_im = mag * jnp.cos(lam_im * dt), mag * jnp.sin(lam_im * dt)
    den = lam_re * lam_re + lam_im * lam_im
    kap_re = ((a_re - 1.0) * lam_re + a_im * lam_im) / den
    kap_im = (a_im * lam_re - (a_re - 1.0) * lam_im) / den
    bb_re = kap_re[..., None] * b_re - kap_im[..., None] * b_im
    bb_im = kap_re[..., None] * b_im + kap_im[..., None] * b_re
    eye = jnp.eye(S5_GROUPS, dtype=F32)

    def in_mat(bb):
        return jnp.einsum("gph,gk->ghkp", bb, eye).reshape(S5_GROUPS * S5_GROUP_CH, S5_LANES).astype(BF16)

    def out_mat(cc):
        return jnp.einsum("gop,gk->gpko", cc, eye).reshape(S5_LANES, S5_GROUPS * S5_GROUP_CH).astype(BF16)

    n = jnp.arange(1, S5_TT // S5_SEGS + 1, dtype=F32)[:, None, None]
    pw_mag = jnp.exp(n * (lam_re * dt))
    pw_re = (pw_mag * jnp.cos(n * (lam_im * dt))).at[0].set(a_re).reshape(-1, S5_LANES)
    pw_im = (pw_mag * jnp.sin(n * (lam_im * dt))).at[0].set(a_im).reshape(-1, S5_LANES)
    return in_mat(bb_re), in_mat(bb_im), pw_re, pw_im, out_mat(c_re), out_mat(c_im)


def kernel(x, c, ada_w, ada_b, ln_g, ln_b, ffn_w13, ffn_w2, w_in, w_out, sinks, w_uq, w_iq, conv_w, conv_b, ig_b,
           fg_b, mh_norm_g, lam_re, lam_im, log_step, b_re, b_im, c_re, c_im, d_skip, glu_w, glu_b):
    bn, t, d = x.shape
    assert d == D_MODEL and t % max(ROW_TILE, DSA_TK, S5_TT) == 0 and bn <= 8 and bn % S5_NB == 0
    n = bn * t
    nl = ada_w.shape[0]
    c_pad = jnp.zeros((8, d), F32).at[:bn].set(c)
    mod_all = _ada_mod(c_pad, ada_w, ada_b)
    mod_all = mod_all[:, :bn].reshape(nl, bn, N_SUB, 3, d).transpose(0, 2, 1, 3, 4)
    mod_all = jnp.pad(mod_all, ((0, 0), (0, 0), (0, 0), (0, 5), (0, 0)))

    x2 = x.reshape(n, d)
    for l in range(nl):
        w13c = ffn_w13[l].astype(BF16)
        w2c = ffn_w2[l].astype(BF16)
        x2, (a_q, a_kv, b_q, b_qi, b_kv, b_ki, c_qk, c_v, c_o, d_u, small) = _ffn_inproj(
            x2, mod_all[l, 0], w13c[0], w2c[0], ln_g[l, 0], ln_b[l, 0], mod_all[l, 1],
            _permute_w_in(w_in[l]), w_uq[l].astype(BF16), w_iq[l].astype(BF16), t)
        o_a = _swa(sinks[l], a_q, a_kv, bn, t)
        o_b = _dsa(b_q, b_qi, small, b_kv, b_ki, bn, t)
        gate_bias = jnp.zeros((1, 128), F32).at[0, SM_IG:SM_IG + 4].set(ig_b[l]).at[0, SM_FG:SM_FG + 4].set(fg_b[l])
        o_c = _mlstm(c_qk, c_v, c_o, small, conv_w[l], conv_b[l].reshape(1, -1), gate_bias,
                     mh_norm_g[l].reshape(1, -1), bn, t).reshape(n, GROUP_WIDTH)
        s5p = _s5_params(lam_re[l], lam_im[l], log_step[l], b_re[l], b_im[l], c_re[l], c_im[l])
        o_d = _s5(d_u, *s5p, d_skip[l].reshape(1, -1), glu_w[l].astype(BF16), glu_b[l].reshape(1, -1), bn, t)
        x2 = _outproj_ffn(x2, mod_all[l, 1], o_a, o_b, o_c, o_d, w_out[l].astype(BF16).reshape(4, GROUP_WIDTH, d),
                          ln_g[l, 1], ln_b[l, 1], mod_all[l, 2], w13c[1], w2c[1], ln_g[l, 2], ln_b[l, 2], t)
    return x2.reshape(bn, t, d)
```

```python
import functools
import math

import jax
import jax.numpy as jnp
from jax import lax
from jax.experimental import pallas as pl
from jax.experimental.pallas import tpu as pltpu

F32 = jnp.float32
BF16 = jnp.bfloat16
I32 = jnp.int32

D_MODEL = 1024
DEPTH = 2
CHUNK = 64
HEAD_DIM = 64
GROUP_WIDTH = 256
SWA_HEADS = 4
SWA_KV_HEADS = 2
SWA_WIN_CHUNKS = 2
DSA_HEADS = 4
DSA_Q_RANK = 128
IDX_HEADS = 4
IDX_DIM = 32
DSA_TOPK = 256
MLSTM_HEADS = 4
MLSTM_CONV = 4
MLSTM_CPS = 8
S5_GROUP_CH = 16
S5_GROUPS = 16
S5_STATE = 64
S5_LANES = S5_GROUPS * S5_STATE
D_FF = 2816
N_SUB = 3
ALPHA = (2 * DEPTH) ** 0.25
LN_EPS = 1e-5
NEG_INF = -1e30
INT_MIN = -(2 ** 31)
FIELD_BITS = 15
FIELD_GUARD = -(2 ** 31) + 2 ** 15

SLOPES_A = tuple(2.0 ** -(i + 1) for i in range(0, 8, 2))
SLOPES_B = tuple(2.0 ** -(i + 1) for i in range(1, 8, 2))

VMEM_LIMIT_BYTES = 56 * 1024 * 1024

FFN_TF = 256
FFN_NCHUNK = D_FF // FFN_TF
ROW_TILE = 512
SWA_TQ = 128
SWA_SUB = 4
DSA_TQ = 128
DSA_TK = 512
S5_TT = 512
S5_SEGS = 8
S5_NB = 2

ZC_QA, ZC_KVA, ZC_CQ, ZC_KVB, ZC_KI = 0, 256, 512, 640, 768
ZC_QKC, ZC_VC, ZC_OC, ZC_UD, ZC_SMALL, Z_WIDTH = 896, 1408, 1664, 1920, 2176, 2304
SM_WI, SM_IG, SM_FG = 0, 4, 8


def _cparams(sem):
    return pltpu.CompilerParams(dimension_semantics=sem, vmem_limit_bytes=VMEM_LIMIT_BYTES)


def _dot(a, b):
    return jnp.dot(a, b, preferred_element_type=F32)


def _dot_nt(a, b):
    return lax.dot_general(a, b, (((1,), (1,)), ((), ())), preferred_element_type=F32)


def _sigmoid(x):
    return 1.0 / (1.0 + jnp.exp(-x))


def _residual_layer_norm(x, y, g, b):
    v = ALPHA * x + y
    mu = jnp.mean(v, axis=-1, keepdims=True)
    d = v - mu
    var = jnp.mean(d * d, axis=-1, keepdims=True)
    return d * lax.rsqrt(var + LN_EPS) * g + b


def _ada_kernel(c_ref, w_ref, b_ref, o_ref):
    c = c_ref[...]
    cs = c * _sigmoid(c)
    o_ref[...] = jnp.dot(cs, w_ref[...], preferred_element_type=F32,
                         precision=lax.Precision.HIGHEST) + b_ref[...]


def _ada_mod(c_pad, ada_w, ada_b):
    nl = ada_w.shape[0]
    ncol = ada_w.shape[2] // D_MODEL
    return pl.pallas_call(
        _ada_kernel,
        grid=(nl, ncol),
        in_specs=[
            pl.BlockSpec((8, D_MODEL), lambda l, j: (0, 0)),
            pl.BlockSpec((None, D_MODEL, D_MODEL), lambda l, j: (l, 0, j)),
            pl.BlockSpec((None, 1, D_MODEL), lambda l, j: (l, 0, j)),
        ],
        out_specs=pl.BlockSpec((None, 8, D_MODEL), lambda l, j: (l, 0, j)),
        out_shape=jax.ShapeDtypeStruct((nl, 8, ada_w.shape[2]), F32),
        compiler_params=_cparams(("arbitrary", "arbitrary")),
        name="ada_mod",
    )(c_pad, ada_w, ada_b.reshape(nl, 1, -1))


def _ffn_tile(x, mod_ref, w13_ref, w2_ref, lng_ref, lnb_ref, h_ref):
    shift, scale, gate = mod_ref[0:1, :], mod_ref[1:2, :], mod_ref[2:3, :]
    u = (x * (1.0 + scale) + shift).astype(BF16)
    for j in range(FFN_NCHUNK):
        a = _dot(u, w13_ref[:, j * FFN_TF:(j + 1) * FFN_TF])
        g = _dot(u, w13_ref[:, D_FF + j * FFN_TF:D_FF + (j + 1) * FFN_TF])
        h_ref[:, j * FFN_TF:(j + 1) * FFN_TF] = (a * _sigmoid(a) * g).astype(BF16)
    y = _dot(h_ref[...], w2_ref[...])
    return _residual_layer_norm(x, 0.5 * gate * y, lng_ref[...], lnb_ref[...])


def _inproj_tile(x, mod_ref, w_ref, wuq_ref, wiq_ref, outs):
    (aq_ref, akv_ref, bq_ref, bqi_ref, bkv_ref, bki_ref, cqk_ref, cv_ref, co_ref, du_ref, sm_ref) = outs
    shift, scale = mod_ref[0:1, :], mod_ref[1:2, :]
    u = (x * (1.0 + scale) + shift).astype(BF16)
    z = _dot(u, w_ref[...])
    aq_ref[...] = z[:, ZC_QA:ZC_KVA].astype(BF16)
    akv_ref[...] = z[:, ZC_KVA:ZC_CQ].astype(BF16)
    cq = z[:, ZC_CQ:ZC_KVB].astype(BF16)
    bq_ref[...] = (_dot(cq, wuq_ref[...]) * HEAD_DIM ** -0.5).astype(BF16)
    bqi_ref[...] = _dot(cq, wiq_ref[...]).astype(BF16)
    bkv_ref[...] = z[:, ZC_KVB:ZC_KI].astype(BF16)
    bki_ref[...] = z[:, ZC_KI:ZC_QKC].astype(BF16)
    cqk_ref[...] = z[:, ZC_QKC:ZC_VC]
    cv_ref[...] = z[:, ZC_VC:ZC_OC].astype(BF16)
    co_ref[...] = z[:, ZC_OC:ZC_UD]
    du_ref[...] = z[:, ZC_UD:ZC_SMALL]
    sm_ref[...] = z[:, ZC_SMALL:Z_WIDTH]


INPROJ_OUTPUTS = ((256, BF16), (256, BF16), (256, BF16), (128, BF16), (128, BF16), (128, BF16),
                  (512, F32), (256, BF16), (256, F32), (256, F32), (128, F32))


def _ffn_inproj_kernel(x_ref, modf_ref, w13_ref, w2_ref, lng_ref, lnb_ref, modm_ref, win_ref, wuq_ref, wiq_ref,
                       o_ref, *rest):
    outs, h_ref = rest[:-1], rest[-1]
    x_new = _ffn_tile(x_ref[...], modf_ref, w13_ref, w2_ref, lng_ref, lnb_ref, h_ref)
    o_ref[...] = x_new
    _inproj_tile(x_new, modm_ref, win_ref, wuq_ref, wiq_ref, outs)


def _ffn_inproj(x2, mod_ffn, w13c, w2c, ln_g, ln_b, mod_mix, w_perm, w_uq, w_iq, rows_per_batch):
    n = x2.shape[0]
    tm = ROW_TILE
    tiles_per_batch = rows_per_batch // tm
    row = pl.BlockSpec((tm, D_MODEL), lambda i: (i, 0))
    mod = pl.BlockSpec((None, 8, D_MODEL), lambda i: (i // tiles_per_batch, 0, 0))
    vec = pl.BlockSpec((1, D_MODEL), lambda i: (0, 0))
    resident = lambda a: pl.BlockSpec(a.shape, lambda i: (0,) * a.ndim, pipeline_mode=pl.Buffered(1))
    outs = pl.pallas_call(
        _ffn_inproj_kernel,
        grid=(n // tm,),
        in_specs=[row, mod, resident(w13c), resident(w2c), vec, vec, mod, resident(w_perm), resident(w_uq),
                  resident(w_iq)],
        out_specs=[row] + [pl.BlockSpec((tm, w), lambda i: (i, 0)) for w, _ in INPROJ_OUTPUTS],
        out_shape=[jax.ShapeDtypeStruct((n, D_MODEL), F32)]
        + [jax.ShapeDtypeStruct((n, w), dt) for w, dt in INPROJ_OUTPUTS],
        scratch_shapes=[pltpu.VMEM((tm, D_FF), BF16)],
        compiler_params=_cparams(("arbitrary",)),
        name="ffn_inproj",
    )(x2, mod_ffn, w13c, w2c, ln_g.reshape(1, -1), ln_b.reshape(1, -1), mod_mix, w_perm, w_uq, w_iq)
    return outs[0], outs[1:]


def _swa_kernel(sink_ref, q_ref, kvc_ref, kvp_ref, o_ref):
    i = pl.program_id(1)
    tq = SWA_TQ
    rep = SWA_HEADS // SWA_KV_HEADS
    for r in range(SWA_SUB):
        rows = slice(r * tq, (r + 1) * tq)
        q = q_ref[rows, :]
        prev = kvp_ref[...] if r == 0 else kvc_ref[(r - 1) * tq:r * tq, :]
        kv = jnp.concatenate([prev, kvc_ref[rows, :]], axis=0)
        first = (i * SWA_SUB + r) * tq
        qpos = first + lax.broadcasted_iota(I32, (tq, 2 * tq), 0)
        kpos = first - tq + lax.broadcasted_iota(I32, (tq, 2 * tq), 1)
        qchunk = qpos // CHUNK
        kchunk = (kpos + tq) // CHUNK - tq // CHUNK
        valid = (kpos >= 0) & (kchunk <= qchunk) & (kchunk >= qchunk - SWA_WIN_CHUNKS)
        dist = jnp.abs(qpos - kpos).astype(F32)
        for h in range(SWA_HEADS):
            g = h // rep
            qh = q[:, h * HEAD_DIM:(h + 1) * HEAD_DIM]
            kg = kv[:, g * HEAD_DIM:(g + 1) * HEAD_DIM]
            vg = kv[:, (SWA_KV_HEADS + g) * HEAD_DIM:(SWA_KV_HEADS + g + 1) * HEAD_DIM]
            s = _dot_nt(qh, kg) * HEAD_DIM ** -0.5 - SLOPES_A[h] * dist
            s = jnp.where(valid, s, NEG_INF)
            sink = sink_ref[h]
            m = jnp.maximum(jnp.max(s, axis=-1, keepdims=True), sink)
            p = jnp.exp(s - m)
            denom = jnp.sum(p, axis=-1, keepdims=True) + jnp.exp(sink - m)
            o = _dot(p.astype(BF16), vg) / denom
            o_ref[rows, h * HEAD_DIM:(h + 1) * HEAD_DIM] = o.astype(BF16)


def _swa(sinks, a_q, a_kv, bn, t):
    rows = SWA_SUB * SWA_TQ
    nt = t // rows
    cur = pl.BlockSpec((rows, 256), lambda b, i: (b * nt + i, 0))
    return pl.pallas_call(
        _swa_kernel,
        grid=(bn, nt),
        in_specs=[
            pl.BlockSpec(memory_space=pltpu.SMEM),
            cur, cur,
            pl.BlockSpec((SWA_TQ, 256), lambda b, i: (jnp.maximum((b * nt + i) * SWA_SUB - 1, 0), 0)),
        ],
        out_specs=cur,
        out_shape=jax.ShapeDtypeStruct((bn * t, 256), BF16),
        compiler_params=_cparams(("arbitrary", "arbitrary")),
        name="swa_attention",
    )(sinks, a_q, a_kv, a_kv)


def _sortable_key(x):
    bits = lax.bitcast_convert_type(x, I32)
    return bits ^ ((bits >> 31) & 0x7FFFFFFF)


def _dsa_kernel(q_ref, qi_ref, sm_ref, kv_ref, vt_ref, ki_ref, tril_ref, o_ref, key_ref, s_ref, p_ref, w1_ref, w2_ref):
    i = pl.program_id(1)
    tq, tk = DSA_TQ, DSA_TK
    nblk = (i * tq + tq + tk - 1) // tk
    qpos = i * tq + lax.broadcasted_iota(I32, (1, tq), 1)
    qchunk = qpos // CHUNK
    row_k = lax.broadcasted_iota(I32, (tk, tq), 0)

    qi = qi_ref[...]
    lane_i = lax.broadcasted_iota(I32, (1, IDX_HEADS * IDX_DIM), 1) // IDX_DIM
    qi_stack = jnp.concatenate([jnp.where(lane_i == h, qi, jnp.zeros_like(qi)) for h in range(IDX_HEADS)], axis=0)
    sm_t = sm_ref[...].T
    w_idx = [sm_t[SM_WI + h:SM_WI + h + 1, :] for h in range(IDX_HEADS)]
    idx_scale = (IDX_DIM * IDX_HEADS) ** -0.5
    q = q_ref[...]
    q_stack = jnp.concatenate([q[:, h * HEAD_DIM:(h + 1) * HEAD_DIM] for h in range(DSA_HEADS)], axis=0)

    half = tk // 2
    guard = jnp.int32(FIELD_GUARD)

    def pack_fields(f):
        return (f[0:half] << 16) | f[half:tk] | guard

    def score_block(j, mask_inadmissible):
        rows = pl.ds(pl.multiple_of(j * tk, tk), tk)
        s_ref[j] = _dot_nt(kv_ref[rows, 0:HEAD_DIM], q_stack)
        d = _dot_nt(ki_ref[rows, :], qi_stack)
        acc = w_idx[0] * jnp.maximum(d[:, 0:tq], 0.0)
        for h in range(1, IDX_HEADS):
            acc = acc + w_idx[h] * jnp.maximum(d[:, h * tq:(h + 1) * tq], 0.0)
        sc = acc * idx_scale
        if mask_inadmissible:
            sc = jnp.where((j * tk + row_k) // CHUNK <= qchunk, sc, NEG_INF)
        key = _sortable_key(sc)
        key_ref[j] = key
        w1_ref[j] = pack_fields(lax.shift_right_logical(key ^ INT_MIN, 32 - FIELD_BITS))

    def full_blocks(jj, carry):
        for r in range(4):
            score_block(4 * jj + r, False)
        return carry

    nfull = nblk - 1
    lax.fori_loop(0, nfull // 4, full_blocks, 0)

    @pl.when(nfull % 4 >= 2)
    def _():
        score_block((nfull // 4) * 4, False)
        score_block((nfull // 4) * 4 + 1, False)

    @pl.when(nfull % 2 == 1)
    def _():
        score_block(nfull - 1, False)
        score_block(nblk - 1, True)

    @pl.when(nfull % 2 == 0)
    def _():
        score_block(nblk - 1, True)

    k_eff = jnp.minimum(DSA_TOPK, (qchunk + 1) * CHUNK)

    def over_blocks(body, init):
        c = lax.fori_loop(0, nblk // 2, lambda jj, c: body(2 * jj + 1, body(2 * jj, c)), init)
        return lax.cond(nblk % 2 == 1, lambda c: body(nblk - 1, c), lambda c: c, c)

    def count(pred):
        def blk(j, c):
            m = pred(key_ref[j], j * tk + row_k).astype(I32)
            return c + jnp.sum(m.reshape(tk // 8, 8, tq), axis=0)
        return jnp.sum(over_blocks(blk, jnp.zeros((8, tq), I32)), axis=0, keepdims=True)

    def count_fields(w_ref, cand):
        cand2 = (cand << 16) | cand
        def blk(j, c):
            hit = ((w_ref[j] - cand2) >> 15) & 0x00010001
            return c + jnp.sum(hit.reshape(half // 8, 8, tq), axis=0)
        c = over_blocks(blk, jnp.zeros((8, tq), I32))
        return jnp.sum((c & 0xFFFF) + (c >> 16), axis=0, keepdims=True)

    def field_search(w_ref, k_want):
        def step(bi, carry):
            prefix, above = carry
            cand = prefix | (jnp.int32(1) << (FIELD_BITS - 1 - bi))
            cnt = count_fields(w_ref, cand)
            ok = cnt >= k_want
            return jnp.where(ok, cand, prefix), jnp.where(ok, above, cnt)
        return lax.fori_loop(0, FIELD_BITS, step, (jnp.zeros((1, tq), I32), jnp.zeros((1, tq), I32)))

    top, above = field_search(w1_ref, k_eff)
    field_max = (1 << FIELD_BITS) - 1

    def pack_mid(j, carry):
        ukey = key_ref[j] ^ INT_MIN
        member = lax.shift_right_logical(ukey, 32 - FIELD_BITS) == top
        mid = lax.shift_right_logical(ukey, 32 - 2 * FIELD_BITS) & field_max
        w2_ref[j] = pack_fields(jnp.where(member, mid, 0))
        return carry

    lax.fori_loop(0, nblk, pack_mid, 0)
    mid, above_mid = field_search(w2_ref, k_eff - above)

    def bit_step(bi, carry):
        prefix, c_gt = carry
        cand_u = prefix | (jnp.int32(1) << (31 - 2 * FIELD_BITS - bi))
        cand_s = cand_u ^ INT_MIN
        cnt = count(lambda key, kpos: key >= cand_s)
        ok = cnt >= k_eff
        return jnp.where(ok, cand_u, prefix), jnp.where(ok, c_gt, cnt)

    prefix, c_gt = lax.fori_loop(0, 32 - 2 * FIELD_BITS, bit_step,
                                 ((top << (32 - FIELD_BITS)) | (mid << (32 - 2 * FIELD_BITS)), above + above_mid))
    thr = prefix ^ INT_MIN

    need = (k_eff - c_gt).astype(F32)
    tril = tril_ref[...]

    offs = (row_k - qpos).astype(F32)
    p_ref[1] = jnp.zeros(p_ref.shape[1:], BF16)

    def att_block(j, carry):
        m_run, l_run, acc_part, ties_seen = carry
        slot = j % 2
        acc = acc_part + _dot(vt_ref[jnp.maximum(j - 1, 0)], p_ref[1 - slot])
        key = key_ref[j]
        tie = key == thr
        tie_rank = _dot(tril, jnp.where(tie, 1.0, 0.0).astype(BF16)) + ties_seen
        sel = (key > thr) | (tie & (tie_rank <= need))
        dist = jnp.where(sel, jnp.abs(offs + (j * tk).astype(F32)), jnp.inf)
        ms, ls, alphas = [], [], []
        for h in range(DSA_HEADS):
            cols = slice(h * tq, (h + 1) * tq)
            sh = s_ref[j, :, cols] - SLOPES_B[h] * dist
            m_old = m_run[:, cols]
            m_new = jnp.maximum(m_old, jnp.max(sh, axis=0, keepdims=True))
            alpha = jnp.exp(m_old - m_new)
            p = jnp.exp(sh - m_new)
            p_ref[slot, :, cols] = p.astype(BF16)
            ms.append(m_new)
            ls.append(alpha * l_run[:, cols] + jnp.sum(p, axis=0, keepdims=True))
            alphas.append(alpha)
        return (jnp.concatenate(ms, axis=1), jnp.concatenate(ls, axis=1), jnp.concatenate(alphas, axis=1) * acc,
                tie_rank[tk - 1:tk, :])

    init = (jnp.full((1, DSA_HEADS * tq), NEG_INF, F32), jnp.zeros((1, DSA_HEADS * tq), F32),
            jnp.zeros((HEAD_DIM, DSA_HEADS * tq), F32), jnp.zeros((1, tq), F32))
    _, l_run, acc_part, _ = lax.fori_loop(0, nblk, att_block, init)
    acc = acc_part + _dot(vt_ref[nblk - 1], p_ref[(nblk - 1) % 2])
    out = acc / l_run
    o_ref[...] = jnp.concatenate([out[:, h * tq:(h + 1) * tq] for h in range(DSA_HEADS)], axis=0).astype(BF16)


def _dsa(b_q, b_qi, small, b_kv, b_ki, bn, t):
    nt = t // DSA_TQ
    nkb = t // DSA_TK
    v_t = b_kv[:, HEAD_DIM:].reshape(bn * nkb, DSA_TK, HEAD_DIM).transpose(0, 2, 1)
    o_t = pl.pallas_call(
        _dsa_kernel,
        grid=(bn, nt),
        in_specs=[
            pl.BlockSpec((DSA_TQ, 256), lambda b, i: (b * nt + i, 0)),
            pl.BlockSpec((DSA_TQ, 128), lambda b, i: (b * nt + i, 0)),
            pl.BlockSpec((DSA_TQ, 128), lambda b, i: (b * nt + i, 0)),
            pl.BlockSpec((t, 128), lambda b, i: (b, 0)),
            pl.BlockSpec((nkb, HEAD_DIM, DSA_TK), lambda b, i: (b, 0, 0)),
            pl.BlockSpec((t, 128), lambda b, i: (b, 0)),
            pl.BlockSpec((DSA_TK, DSA_TK), lambda b, i: (0, 0)),
        ],
        out_specs=pl.BlockSpec((None, DSA_HEADS * HEAD_DIM, DSA_TQ), lambda b, i: (b * nt + i, 0, 0)),
        out_shape=jax.ShapeDtypeStruct((bn * nt, DSA_HEADS * HEAD_DIM, DSA_TQ), BF16),
        scratch_shapes=[pltpu.VMEM((nkb, DSA_TK, DSA_TQ), I32),
                        pltpu.VMEM((nkb, DSA_TK, DSA_HEADS * DSA_TQ), F32),
                        pltpu.VMEM((2, DSA_TK, DSA_HEADS * DSA_TQ), BF16),
                        pltpu.VMEM((nkb, DSA_TK // 2, DSA_TQ), I32), pltpu.VMEM((nkb, DSA_TK // 2, DSA_TQ), I32)],
        compiler_params=_cparams(("arbitrary", "arbitrary")),
        name="dsa_attention",
    )(b_q, b_qi, small, b_kv, v_t, b_ki, jnp.tril(jnp.ones((DSA_TK, DSA_TK), BF16)))
    return o_t.transpose(0, 2, 1).reshape(bn * t, DSA_HEADS * HEAD_DIM)


def _mlstm_kernel(qk_ref, v_ref, og_ref, sm_ref, convw_ref, convb_ref, gbias_ref, normgt_ref,
                  o_ref, tail_ref, ct_ref, nvec_ref, mst_ref, *, bn):
    c = pl.program_id(0)
    L = CHUNK
    nh, dh, width = MLSTM_HEADS, HEAD_DIM, MLSTM_HEADS * HEAD_DIM

    @pl.when(c == 0)
    def _():
        tail_ref[...] = jnp.zeros_like(tail_ref)
        ct_ref[...] = jnp.zeros_like(ct_ref)
        nvec_ref[...] = jnp.zeros_like(nvec_ref)
        mst_ref[...] = jnp.zeros_like(mst_ref)

    srow = lax.broadcasted_iota(I32, (L, width), 0)
    lane = lax.broadcasted_iota(I32, (L, width), 1)
    jlane = lane % dh
    causal_t = srow <= jlane
    diag_t = srow == jlane
    head_of_lane = lax.broadcasted_iota(I32, (1, width), 1) // dh
    tril = (lax.broadcasted_iota(I32, (L, L), 1) <= lax.broadcasted_iota(I32, (L, L), 0)).astype(F32)
    erow = lax.broadcasted_iota(I32, (128, width), 0)
    ecol_head = lax.broadcasted_iota(I32, (128, width), 1) // dh
    expand_ig = (erow == SM_IG + ecol_head).astype(F32)
    expand_fg = (erow == SM_FG + ecol_head).astype(F32)
    exact = dict(preferred_element_type=F32, precision=lax.Precision.HIGHEST)

    def head_blocks(a):
        out = jnp.where(head_of_lane == 0, a[0:dh], 0.0)
        for h in range(1, nh):
            out = out + jnp.where(head_of_lane == h, a[h * dh:(h + 1) * dh], 0.0)
        return out

    convw = convw_ref[...]
    for cc, b in [(cc, b) for cc in range(MLSTM_CPS) for b in range(bn)]:
        rows = slice(cc * L, (cc + 1) * L)
        cur = qk_ref[b, rows, :]
        ext = jnp.concatenate([tail_ref[b], cur], axis=0)
        tail_ref[b] = cur[L - 8:L, :]
        y = convb_ref[...] + convw[MLSTM_CONV - 1:MLSTM_CONV, :] * cur
        for k in range(MLSTM_CONV - 1):
            off = 8 - (MLSTM_CONV - 1) + k
            y = y + convw[k:k + 1, :] * ext[off:off + L, :]
        qk = y * _sigmoid(y)
        q_all = qk[:, 0:width]
        k_all = qk[:, width:2 * width] * dh ** -0.5
        q_stack = jnp.concatenate([jnp.where(head_of_lane == h, q_all, 0.0) for h in range(nh)], axis=0).astype(BF16)
        v_all = v_ref[b, rows, :]
        v_t = v_all.astype(F32).T.astype(BF16)

        gates = sm_ref[b, rows, :] + gbias_ref[...]
        lf = jnp.minimum(gates, 0.0) - jnp.log(1.0 + jnp.exp(-jnp.abs(gates)))
        bcum = jnp.dot(tril, lf, **exact)
        ig_x = jnp.dot(gates, expand_ig, **exact)
        b_x = jnp.dot(bcum, expand_fg, **exact)
        b_q = jnp.sum(jnp.where(diag_t, b_x, 0.0), axis=0, keepdims=True)
        b_last = b_x[L - 1:L, :]
        m_prev = mst_ref[b]
        ct = ct_ref[b]
        nvec = nvec_ref[b]

        dlog = jnp.where(causal_t, b_q - b_x + ig_x, NEG_INF)
        inter = b_q + m_prev
        mj = jnp.maximum(inter, jnp.max(dlog, axis=0, keepdims=True))
        dw = jnp.exp(dlog - mj)
        iw = jnp.exp(inter - mj)
        sc = _dot_nt(k_all.astype(BF16), q_stack) * dw
        qn = _dot_nt(jnp.broadcast_to(nvec, (8, width)).astype(BF16), q_stack)[0:1, :]
        q_c = _dot_nt(ct.astype(BF16), q_stack)
        num = iw * q_c + head_blocks(_dot(v_t, sc.astype(BF16)))
        den = iw * qn + jnp.sum(sc, axis=0, keepdims=True)
        hj = num / jnp.maximum(jnp.abs(den), jnp.exp(-mj))

        dec = b_last - b_x + ig_x
        m_new = jnp.maximum(b_last + m_prev, jnp.max(dec, axis=0, keepdims=True))
        wc = jnp.exp(b_last + m_prev - m_new)
        kw = k_all * jnp.exp(dec - m_new)
        ct_ref[b] = wc * ct + head_blocks(_dot(v_t, kw.astype(BF16)))
        nvec_ref[b] = wc * nvec + jnp.sum(kw, axis=0, keepdims=True)
        mst_ref[b] = m_new

        mu = jnp.mean(hj, axis=0, keepdims=True)
        dev = hj - mu
        var = jnp.mean(dev * dev, axis=0, keepdims=True)
        hn_t = (dev * lax.rsqrt(var + LN_EPS) * normgt_ref[...]).T
        hn = jnp.concatenate([hn_t[h * dh:(h + 1) * dh, :] for h in range(nh)], axis=1)
        o_ref[b, rows, :] = (_sigmoid(og_ref[b, rows, :]) * hn).astype(BF16)


def _mlstm(c_qk, c_v, c_o, small, conv_w, conv_b, gate_bias, norm_g, bn, t):
    nc = t // (CHUNK * MLSTM_CPS)
    width = MLSTM_HEADS * HEAD_DIM
    norm_g_t = jnp.repeat(norm_g.reshape(MLSTM_HEADS, HEAD_DIM).T, HEAD_DIM, axis=1)
    blk = lambda w: pl.BlockSpec((bn, CHUNK * MLSTM_CPS, w), lambda c: (0, c, 0))
    full = lambda a: pl.BlockSpec(a.shape, lambda c: (0,) * a.ndim)
    return pl.pallas_call(
        functools.partial(_mlstm_kernel, bn=bn),
        grid=(nc,),
        in_specs=[blk(512), blk(256), blk(256), blk(128), full(conv_w), full(conv_b), full(gate_bias), full(norm_g_t)],
        out_specs=blk(256),
        out_shape=jax.ShapeDtypeStruct((bn, t, 256), BF16),
        scratch_shapes=[pltpu.VMEM((bn, 8, 512), F32), pltpu.VMEM((bn, HEAD_DIM, width), F32),
                        pltpu.VMEM((bn, 1, width), F32), pltpu.VMEM((bn, 1, width), F32)],
        compiler_params=_cparams(("arbitrary",)),
        name="mlstm",
    )(c_qk.reshape(bn, t, 512), c_v.reshape(bn, t, 256), c_o.reshape(bn, t, 256), small.reshape(bn, t, 128),
      conv_w, conv_b, gate_bias, norm_g_t)


def _s5_kernel(u_ref, perm_ref, bbre_ref, bbim_ref, pwre_ref, pwim_ref, cre_ref, cim_ref, dskip_ref, gluw_ref,
               glub_ref, o_ref, sre_ref, sim_ref, stre_ref, stim_ref, yp_ref):
    @pl.when(pl.program_id(1) == 0)
    def _():
        stre_ref[...] = jnp.zeros_like(stre_ref)
        stim_ref[...] = jnp.zeros_like(stim_ref)

    nb = S5_NB
    seg_len = S5_TT // S5_SEGS
    us = [u_ref[b] for b in range(nb)]
    for b in range(nb):
        ub = _dot(perm_ref[...], us[b].astype(BF16)).astype(BF16)
        sre_ref[b] = _dot(ub, bbre_ref[...])
        sim_ref[b] = _dot(ub, bbim_ref[...])
    a_re = jnp.broadcast_to(pwre_ref[0:1, :], (S5_SEGS, S5_LANES))
    a_im = jnp.broadcast_to(pwim_ref[0:1, :], (S5_SEGS, S5_LANES))

    def local_step(i, carry):
        rows = pl.ds(pl.multiple_of(i * S5_SEGS, S5_SEGS), S5_SEGS)
        out = []
        for b in range(nb):
            s_re, s_im = carry[b]
            n_re = a_re * s_re - a_im * s_im + sre_ref[b, rows, :]
            n_im = a_re * s_im + a_im * s_re + sim_ref[b, rows, :]
            sre_ref[b, rows, :] = n_re
            sim_ref[b, rows, :] = n_im
            out.append((n_re, n_im))
        return tuple(out)

    zeros = jnp.zeros((S5_SEGS, S5_LANES), F32)
    ends = lax.fori_loop(0, seg_len, local_step, tuple((zeros, zeros) for _ in range(nb)), unroll=4)

    al_re, al_im = pwre_ref[seg_len - 1:seg_len, :], pwim_ref[seg_len - 1:seg_len, :]
    cins = []
    for b in range(nb):
        e_re, e_im = ends[b]
        c_re, c_im = stre_ref[b, 0:1, :], stim_ref[b, 0:1, :]
        cs_re, cs_im = [], []
        for k in range(S5_SEGS):
            cs_re.append(c_re)
            cs_im.append(c_im)
            c_re, c_im = (e_re[k:k + 1, :] + al_re * c_re - al_im * c_im,
                          e_im[k:k + 1, :] + al_re * c_im + al_im * c_re)
        stre_ref[b] = jnp.broadcast_to(c_re, stre_ref.shape[1:])
        stim_ref[b] = jnp.broadcast_to(c_im, stim_ref.shape[1:])
        cins.append((jnp.concatenate(cs_re, axis=0), jnp.concatenate(cs_im, axis=0)))

    def correct_step(i, carry):
        rows = pl.ds(pl.multiple_of(i * S5_SEGS, S5_SEGS), S5_SEGS)
        p_re, p_im = pwre_ref[pl.ds(i, 1), :], pwim_ref[pl.ds(i, 1), :]
        for b in range(nb):
            cin_re, cin_im = cins[b]
            sre_ref[b, rows, :] = sre_ref[b, rows, :] + p_re * cin_re - p_im * cin_im
            sim_ref[b, rows, :] = sim_ref[b, rows, :] + p_re * cin_im + p_im * cin_re
        return carry

    lax.fori_loop(0, seg_len, correct_step, 0, unroll=4)
    ngrp = GROUP_WIDTH // 128
    for b in range(nb):
        ycs = _dot(sre_ref[b].astype(BF16), cre_ref[...]) - _dot(sim_ref[b].astype(BF16), cim_ref[...])
        for g in range(ngrp):
            yp_ref[b, g] = ycs[:, g * 128:(g + 1) * 128]
        blocks = []
        for k in range(S5_SEGS):
            for i0 in range(0, seg_len, 8):
                rows = pl.ds(i0 * S5_SEGS + k, 8, stride=S5_SEGS)
                blocks.append(jnp.concatenate([yp_ref[b, g, rows, :] for g in range(ngrp)], axis=1))
        y = jnp.concatenate(blocks, axis=0) + dskip_ref[...] * us[b]
        y = 0.5 * y * (1.0 + jnp.tanh(math.sqrt(2.0 / math.pi) * (y + 0.044715 * (y * y * y))))
        z = _dot(y.astype(BF16), gluw_ref[...]) + glub_ref[...]
        o_ref[b] = (y * _sigmoid(z)).astype(BF16)


def _s5(d_u, bb_re, bb_im, pw_re, pw_im, c_re_t, c_im_t, d_skip, glu_w, glu_b, bn, t):
    nt = t // S5_TT
    nb = S5_NB
    full = lambda a: pl.BlockSpec(a.shape, lambda b, i: (0,) * a.ndim)
    r = jnp.arange(S5_TT)
    perm = (r[None, :] == ((r % S5_SEGS) * (S5_TT // S5_SEGS) + r // S5_SEGS)[:, None]).astype(BF16)
    args = (perm, bb_re, bb_im, pw_re, pw_im, c_re_t, c_im_t, d_skip, glu_w, glu_b)
    tile = pl.BlockSpec((nb, S5_TT, GROUP_WIDTH), lambda b, i: (b, i, 0))
    return pl.pallas_call(
        _s5_kernel,
        grid=(bn // nb, nt),
        in_specs=[tile] + [full(a) for a in args],
        out_specs=tile,
        out_shape=jax.ShapeDtypeStruct((bn, t, GROUP_WIDTH), BF16),
        scratch_shapes=[pltpu.VMEM((nb, S5_TT, S5_LANES), F32), pltpu.VMEM((nb, S5_TT, S5_LANES), F32),
                        pltpu.VMEM((nb, 8, S5_LANES), F32), pltpu.VMEM((nb, 8, S5_LANES), F32),
                        pltpu.VMEM((nb, GROUP_WIDTH // 128, S5_TT, 128), F32)],
        compiler_params=_cparams(("arbitrary", "arbitrary")),
        name="s5_glu",
    )(d_u.reshape(bn, t, GROUP_WIDTH), *args).reshape(bn * t, GROUP_WIDTH)


def _outproj_ffn_kernel(x_ref, modm_ref, oa_ref, ob_ref, oc_ref, od_ref, wout_ref, lngm_ref, lnbm_ref,
                        modf_ref, w13_ref, w2_ref, lngf_ref, lnbf_ref, o_ref, h_ref):
    y = _dot(oa_ref[...], wout_ref[0])
    y = y + _dot(ob_ref[...], wout_ref[1])
    y = y + _dot(oc_ref[...], wout_ref[2])
    y = y + _dot(od_ref[...], wout_ref[3])
    x_mid = _residual_layer_norm(x_ref[...], modm_ref[2:3, :] * y, lngm_ref[...], lnbm_ref[...])
    o_ref[...] = _ffn_tile(x_mid, modf_ref, w13_ref, w2_ref, lngf_ref, lnbf_ref, h_ref)


def _outproj_ffn(x2, mod_mix, o_a, o_b, o_c, o_d, w_out4, ln_g_mix, ln_b_mix, mod_ffn, w13c, w2c, ln_g_ffn, ln_b_ffn,
                 rows_per_batch):
    n = x2.shape[0]
    tm = ROW_TILE
    tiles_per_batch = rows_per_batch // tm
    row = pl.BlockSpec((tm, D_MODEL), lambda i: (i, 0))
    mod = pl.BlockSpec((None, 8, D_MODEL), lambda i: (i // tiles_per_batch, 0, 0))
    mix = pl.BlockSpec((tm, GROUP_WIDTH), lambda i: (i, 0))
    vec = pl.BlockSpec((1, D_MODEL), lambda i: (0, 0))
    resident = lambda a: pl.BlockSpec(a.shape, lambda i: (0,) * a.ndim, pipeline_mode=pl.Buffered(1))
    return pl.pallas_call(
        _outproj_ffn_kernel,
        grid=(n // tm,),
        in_specs=[row, mod, mix, mix, mix, mix, resident(w_out4), vec, vec, mod, resident(w13c), resident(w2c), vec, vec],
        out_specs=row,
        out_shape=jax.ShapeDtypeStruct((n, D_MODEL), F32),
        scratch_shapes=[pltpu.VMEM((tm, D_FF), BF16)],
        compiler_params=_cparams(("arbitrary",)),
        name="outproj_ffn",
    )(x2, mod_mix, o_a, o_b, o_c, o_d, w_out4, ln_g_mix.reshape(1, -1), ln_b_mix.reshape(1, -1),
      mod_ffn, w13c, w2c, ln_g_ffn.reshape(1, -1), ln_b_ffn.reshape(1, -1))


def _permute_w_in(w_in):
    off = {}
    o = 0
    for name, s in (("qa", 256), ("ka", 128), ("va", 128), ("cq", 128), ("kb", 64), ("vb", 64), ("ki", 32),
                    ("wi", 4), ("qkc", 512), ("vc", 256), ("ig", 4), ("fg", 4), ("oc", 256), ("ud", 256)):
        off[name] = (o, o + s)
        o += s
    col = lambda n: w_in[:, off[n][0]:off[n][1]]
    small = jnp.concatenate([col("wi"), col("ig"), col("fg"),
                             jnp.zeros((w_in.shape[0], 128 - 12), w_in.dtype)], axis=1)
    parts = [col("qa"), col("ka"), col("va"), col("cq"), col("kb"), col("vb")] + [col("ki")] * 4 + [
        col("qkc"), col("vc"), col("oc"), col("ud"), small]
    return jnp.concatenate(parts, axis=1).astype(BF16)


def _s5_params(lam_re, lam_im, log_step, b_re, b_im, c_re, c_im):
    dt = jnp.exp(log_step)[:, None]
    mag = jnp.exp(lam_re * dt)
    a_re, a_im = mag * jnp.cos(lam_im * dt), mag * jnp.sin(lam_im * dt)
    den = lam_re * lam_re + lam_im * lam_im
    kap_re = ((a_re - 1.0) * lam_re + a_im * lam_im) / den
    kap_im = (a_im * lam_re - (a_re - 1.0) * lam_im) / den
    bb_re = kap_re[..., None] * b_re - kap_im[..., None] * b_im
    bb_im = kap_re[..., None] * b_im + kap_im[..., None] * b_re
    eye = jnp.eye(S5_GROUPS, dtype=F32)

    def in_mat(bb):
        return jnp.einsum("gph,gk->ghkp", bb, eye).reshape(S5_GROUPS * S5_GROUP_CH, S5_LANES).astype(BF16)

    def out_mat(cc):
        return jnp.einsum("gop,gk->gpko", cc, eye).reshape(S5_LANES, S5_GROUPS * S5_GROUP_CH).astype(BF16)

    n = jnp.arange(1, S5_TT // S5_SEGS + 1, dtype=F32)[:, None, None]
    pw_mag = jnp.exp(n * (lam_re * dt))
    pw_re = (pw_mag * jnp.cos(n * (lam_im * dt))).at[0].set(a_re).reshape(-1, S5_LANES)
    pw_im = (pw_mag * jnp.sin(n * (lam_im * dt))).at[0].set(a_im).reshape(-1, S5_LANES)
    return in_mat(bb_re), in_mat(bb_im), pw_re, pw_im, out_mat(c_re), out_mat(c_im)


def kernel(x, c, ada_w, ada_b, ln_g, ln_b, ffn_w13, ffn_w2, w_in, w_out, sinks, w_uq, w_iq, conv_w, conv_b, ig_b,
           fg_b, mh_norm_g, lam_re, lam_im, log_step, b_re, b_im, c_re, c_im, d_skip, glu_w, glu_b):
    bn, t, d = x.shape
    assert d == D_MODEL and t % max(ROW_TILE, DSA_TK, S5_TT) == 0 and bn <= 8 and bn % S5_NB == 0
    n = bn * t
    nl = ada_w.shape[0]
    c_pad = jnp.zeros((8, d), F32).at[:bn].set(c)
    mod_all = _ada_mod(c_pad, ada_w, ada_b)
    mod_all = mod_all[:, :bn].reshape(nl, bn, N_SUB, 3, d).transpose(0, 2, 1, 3, 4)
    mod_all = jnp.pad(mod_all, ((0, 0), (0, 0), (0, 0), (0, 5), (0, 0)))

    x2 = x.reshape(n, d)
    for l in range(nl):
        w13c = ffn_w13[l].astype(BF16)
        w2c = ffn_w2[l].astype(BF16)
        x2, (a_q, a_kv, b_q, b_qi, b_kv, b_ki, c_qk, c_v, c_o, d_u, small) = _ffn_inproj(
            x2, mod_all[l, 0], w13c[0], w2c[0], ln_g[l, 0], ln_b[l, 0], mod_all[l, 1],
            _permute_w_in(w_in[l]), w_uq[l].astype(BF16), w_iq[l].astype(BF16), t)
        o_a = _swa(sinks[l], a_q, a_kv, bn, t)
        o_b = _dsa(b_q, b_qi, small, b_kv, b_ki, bn, t)
        gate_bias = jnp.zeros((1, 128), F32).at[0, SM_IG:SM_IG + 4].set(ig_b[l]).at[0, SM_FG:SM_FG + 4].set(fg_b[l])
        o_c = _mlstm(c_qk, c_v, c_o, small, conv_w[l], conv_b[l].reshape(1, -1), gate_bias,
                     mh_norm_g[l].reshape(1, -1), bn, t).reshape(n, GROUP_WIDTH)
        s5p = _s5_params(lam_re[l], lam_im[l], log_step[l], b_re[l], b_im[l], c_re[l], c_im[l])
        o_d = _s5(d_u, *s5p, d_skip[l].reshape(1, -1), glu_w[l].astype(BF16), glu_b[l].reshape(1, -1), bn, t)
        x2 = _outproj_ffn(x2, mod_all[l, 1], o_a, o_b, o_c, o_d, w_out[l].astype(BF16).reshape(4, GROUP_WIDTH, d),
                          ln_g[l, 1], ln_b[l, 1], mod_all[l, 2], w13c[1], w2c[1], ln_g[l, 2], ln_b[l, 2], t)
    return x2.reshape(bn, t, d)
```

```python
import functools
import math

import jax
import jax.numpy as jnp
from jax import lax
from jax.experimental import pallas as pl
from jax.experimental.pallas import tpu as pltpu

F32 = jnp.float32
BF16 = jnp.bfloat16
I32 = jnp.int32

D_MODEL = 1024
DEPTH = 2
CHUNK = 64
HEAD_DIM = 64
GROUP_WIDTH = 256
SWA_HEADS = 4
SWA_KV_HEADS = 2
SWA_WIN_CHUNKS = 2
DSA_HEADS = 4
DSA_Q_RANK = 128
IDX_HEADS = 4
IDX_DIM = 32
DSA_TOPK = 256
MLSTM_HEADS = 4
MLSTM_CONV = 4
MLSTM_CPS = 8
S5_GROUP_CH = 16
S5_GROUPS = 16
S5_STATE = 64
S5_LANES = S5_GROUPS * S5_STATE
D_FF = 2816
N_SUB = 3
ALPHA = (2 * DEPTH) ** 0.25
LN_EPS = 1e-5
NEG_INF = -1e30
INT_MIN = -(2 ** 31)
FIELD_BITS = 15
FIELD_GUARD = -(2 ** 31) + 2 ** 15

SLOPES_A = tuple(2.0 ** -(i + 1) for i in range(0, 8, 2))
SLOPES_B = tuple(2.0 ** -(i + 1) for i in range(1, 8, 2))

VMEM_LIMIT_BYTES = 56 * 1024 * 1024

FFN_TF = 256
FFN_NCHUNK = D_FF // FFN_TF
ROW_TILE = 512
SWA_TQ = 128
SWA_SUB = 4
DSA_TQ = 128
DSA_TK = 512
S5_TT = 512
S5_SEGS = 8
S5_NB = 2

ZC_QA, ZC_KVA, ZC_CQ, ZC_KVB, ZC_KI = 0, 256, 512, 640, 768
ZC_QKC, ZC_VC, ZC_OC, ZC_UD, ZC_SMALL, Z_WIDTH = 896, 1408, 1664, 1920, 2176, 2304
SM_WI, SM_IG, SM_FG = 0, 4, 8


def _cparams(sem):
    return pltpu.CompilerParams(dimension_semantics=sem, vmem_limit_bytes=VMEM_LIMIT_BYTES)


def _dot(a, b):
    return jnp.dot(a, b, preferred_element_type=F32)


def _dot_nt(a, b):
    return lax.dot_general(a, b, (((1,), (1,)), ((), ())), preferred_element_type=F32)


def _sigmoid(x):
    return 1.0 / (1.0 + jnp.exp(-x))


def _residual_layer_norm(x, y, g, b):
    v = ALPHA * x + y
    mu = jnp.mean(v, axis=-1, keepdims=True)
    d = v - mu
    var = jnp.mean(d * d, axis=-1, keepdims=True)
    return d * lax.rsqrt(var + LN_EPS) * g + b


def _ada_kernel(c_ref, w_ref, b_ref, o_ref):
    c = c_ref[...]
    cs = c * _sigmoid(c)
    o_ref[...] = jnp.dot(cs, w_ref[...], preferred_element_type=F32,
                         precision=lax.Precision.HIGHEST) + b_ref[...]


def _ada_mod(c_pad, ada_w, ada_b):
    nl = ada_w.shape[0]
    ncol = ada_w.shape[2] // D_MODEL
    return pl.pallas_call(
        _ada_kernel,
        grid=(nl, ncol),
        in_specs=[
            pl.BlockSpec((8, D_MODEL), lambda l, j: (0, 0)),
            pl.BlockSpec((None, D_MODEL, D_MODEL), lambda l, j: (l, 0, j)),
            pl.BlockSpec((None, 1, D_MODEL), lambda l, j: (l, 0, j)),
        ],
        out_specs=pl.BlockSpec((None, 8, D_MODEL), lambda l, j: (l, 0, j)),
        out_shape=jax.ShapeDtypeStruct((nl, 8, ada_w.shape[2]), F32),
        compiler_params=_cparams(("arbitrary", "arbitrary")),
        name="ada_mod",
    )(c_pad, ada_w, ada_b.reshape(nl, 1, -1))


def _ffn_tile(x, mod_ref, w13_ref, w2_ref, lng_ref, lnb_ref, h_ref):
    shift, scale, gate = mod_ref[0:1, :], mod_ref[1:2, :], mod_ref[2:3, :]
    u = (x * (1.0 + scale) + shift).astype(BF16)
    for j in range(FFN_NCHUNK):
        a = _dot(u, w13_ref[:, j * FFN_TF:(j + 1) * FFN_TF])
        g = _dot(u, w13_ref[:, D_FF + j * FFN_TF:D_FF + (j + 1) * FFN_TF])
        h_ref[:, j * FFN_TF:(j + 1) * FFN_TF] = (a * _sigmoid(a) * g).astype(BF16)
    y = _dot(h_ref[...], w2_ref[...])
    return _residual_layer_norm(x, 0.5 * gate * y, lng_ref[...], lnb_ref[...])


def _inproj_tile(x, mod_ref, w_ref, wuq_ref, wiq_ref, outs):
    (aq_ref, akv_ref, bq_ref, bqi_ref, bkv_ref, bki_ref, cqk_ref, cv_ref, co_ref, du_ref, sm_ref) = outs
    shift, scale = mod_ref[0:1, :], mod_ref[1:2, :]
    u = (x * (1.0 + scale) + shift).astype(BF16)
    z = _dot(u, w_ref[...])
    aq_ref[...] = z[:, ZC_QA:ZC_KVA].astype(BF16)
    akv_ref[...] = z[:, ZC_KVA:ZC_CQ].astype(BF16)
    cq = z[:, ZC_CQ:ZC_KVB].astype(BF16)
    bq_ref[...] = (_dot(cq, wuq_ref[...]) * HEAD_DIM ** -0.5).astype(BF16)
    bqi_ref[...] = _dot(cq, wiq_ref[...]).astype(BF16)
    bkv_ref[...] = z[:, ZC_KVB:ZC_KI].astype(BF16)
    bki_ref[...] = z[:, ZC_KI:ZC_QKC].astype(BF16)
    cqk_ref[...] = z[:, ZC_QKC:ZC_VC]
    cv_ref[...] = z[:, ZC_VC:ZC_OC].astype(BF16)
    co_ref[...] = z[:, ZC_OC:ZC_UD]
    du_ref[...] = z[:, ZC_UD:ZC_SMALL]
    sm_ref[...] = z[:, ZC_SMALL:Z_WIDTH]


INPROJ_OUTPUTS = ((256, BF16), (256, BF16), (256, BF16), (128, BF16), (128, BF16), (128, BF16),
                  (512, F32), (256, BF16), (256, F32), (256, F32), (128, F32))


def _ffn_inproj_kernel(x_ref, modf_ref, w13_ref, w2_ref, lng_ref, lnb_ref, modm_ref, win_ref, wuq_ref, wiq_ref,
                       o_ref, *rest):
    outs, h_ref = rest[:-1], rest[-1]
    x_new = _ffn_tile(x_ref[...], modf_ref, w13_ref, w2_ref, lng_ref, lnb_ref, h_ref)
    o_ref[...] = x_new
    _inproj_tile(x_new, modm_ref, win_ref, wuq_ref, wiq_ref, outs)


def _ffn_inproj(x2, mod_ffn, w13c, w2c, ln_g, ln_b, mod_mix, w_perm, w_uq, w_iq, rows_per_batch):
    n = x2.shape[0]
    tm = ROW_TILE
    tiles_per_batch = rows_per_batch // tm
    row = pl.BlockSpec((tm, D_MODEL), lambda i: (i, 0))
    mod = pl.BlockSpec((None, 8, D_MODEL), lambda i: (i // tiles_per_batch, 0, 0))
    vec = pl.BlockSpec((1, D_MODEL), lambda i: (0, 0))
    resident = lambda a: pl.BlockSpec(a.shape, lambda i: (0,) * a.ndim, pipeline_mode=pl.Buffered(1))
    outs = pl.pallas_call(
        _ffn_inproj_kernel,
        grid=(n // tm,),
        in_specs=[row, mod, resident(w13c), resident(w2c), vec, vec, mod, resident(w_perm), resident(w_uq),
                  resident(w_iq)],
        out_specs=[row] + [pl.BlockSpec((tm, w), lambda i: (i, 0)) for w, _ in INPROJ_OUTPUTS],
        out_shape=[jax.ShapeDtypeStruct((n, D_MODEL), F32)]
        + [jax.ShapeDtypeStruct((n, w), dt) for w, dt in INPROJ_OUTPUTS],
        scratch_shapes=[pltpu.VMEM((tm, D_FF), BF16)],
        compiler_params=_cparams(("arbitrary",)),
        name="ffn_inproj",
    )(x2, mod_ffn, w13c, w2c, ln_g.reshape(1, -1), ln_b.reshape(1, -1), mod_mix, w_perm, w_uq, w_iq)
    return outs[0], outs[1:]


def _swa_kernel(sink_ref, q_ref, kvc_ref, kvp_ref, o_ref):
    i = pl.program_id(1)
    tq = SWA_TQ
    rep = SWA_HEADS // SWA_KV_HEADS
    for r in range(SWA_SUB):
        rows = slice(r * tq, (r + 1) * tq)
        q = q_ref[rows, :]
        prev = kvp_ref[...] if r == 0 else kvc_ref[(r - 1) * tq:r * tq, :]
        kv = jnp.concatenate([prev, kvc_ref[rows, :]], axis=0)
        first = (i * SWA_SUB + r) * tq
        qpos = first + lax.broadcasted_iota(I32, (tq, 2 * tq), 0)
        kpos = first - tq + lax.broadcasted_iota(I32, (tq, 2 * tq), 1)
        qchunk = qpos // CHUNK
        kchunk = (kpos + tq) // CHUNK - tq // CHUNK
        valid = (kpos >= 0) & (kchunk <= qchunk) & (kchunk >= qchunk - SWA_WIN_CHUNKS)
        dist = jnp.abs(qpos - kpos).astype(F32)
        for h in range(SWA_HEADS):
            g = h // rep
            qh = q[:, h * HEAD_DIM:(h + 1) * HEAD_DIM]
            kg = kv[:, g * HEAD_DIM:(g + 1) * HEAD_DIM]
            vg = kv[:, (SWA_KV_HEADS + g) * HEAD_DIM:(SWA_KV_HEADS + g + 1) * HEAD_DIM]
            s = _dot_nt(qh, kg) * HEAD_DIM ** -0.5 - SLOPES_A[h] * dist
            s = jnp.where(valid, s, NEG_INF)
            sink = sink_ref[h]
            m = jnp.maximum(jnp.max(s, axis=-1, keepdims=True), sink)
            p = jnp.exp(s - m)
            denom = jnp.sum(p, axis=-1, keepdims=True) + jnp.exp(sink - m)
            o = _dot(p.astype(BF16), vg) / denom
            o_ref[rows, h * HEAD_DIM:(h + 1) * HEAD_DIM] = o.astype(BF16)


def _swa(sinks, a_q, a_kv, bn, t):
    rows = SWA_SUB * SWA_TQ
    nt = t // rows
    cur = pl.BlockSpec((rows, 256), lambda b, i: (b * nt + i, 0))
    return pl.pallas_call(
        _swa_kernel,
        grid=(bn, nt),
        in_specs=[
            pl.BlockSpec(memory_space=pltpu.SMEM),
            cur, cur,
            pl.BlockSpec((SWA_TQ, 256), lambda b, i: (jnp.maximum((b * nt + i) * SWA_SUB - 1, 0), 0)),
        ],
        out_specs=cur,
        out_shape=jax.ShapeDtypeStruct((bn * t, 256), BF16),
        compiler_params=_cparams(("arbitrary", "arbitrary")),
        name="swa_attention",
    )(sinks, a_q, a_kv, a_kv)


def _sortable_key(x):
    bits = lax.bitcast_convert_type(x, I32)
    return bits ^ ((bits >> 31) & 0x7FFFFFFF)


def _dsa_kernel(q_ref, qi_ref, sm_ref, kv_ref, vt_ref, ki_ref, tril_ref, o_ref, key_ref, s_ref, p_ref, w1_ref, w2_ref):
    i = pl.program_id(1)
    tq, tk = DSA_TQ, DSA_TK
    nblk = (i * tq + tq + tk - 1) // tk
    qpos = i * tq + lax.broadcasted_iota(I32, (1, tq), 1)
    qchunk = qpos // CHUNK
    row_k = lax.broadcasted_iota(I32, (tk, tq), 0)

    qi = qi_ref[...]
    lane_i = lax.broadcasted_iota(I32, (1, IDX_HEADS * IDX_DIM), 1) // IDX_DIM
    qi_stack = jnp.concatenate([jnp.where(lane_i == h, qi, jnp.zeros_like(qi)) for h in range(IDX_HEADS)], axis=0)
    sm_t = sm_ref[...].T
    w_idx = [sm_t[SM_WI + h:SM_WI + h + 1, :] for h in range(IDX_HEADS)]
    idx_scale = (IDX_DIM * IDX_HEADS) ** -0.5
    q = q_ref[...]
    q_stack = jnp.concatenate([q[:, h * HEAD_DIM:(h + 1) * HEAD_DIM] for h in range(DSA_HEADS)], axis=0)

    half = tk // 2
    guard = jnp.int32(FIELD_GUARD)

    def pack_fields(f):
        return (f[0:half] << 16) | f[half:tk] | guard

    def score_block(j, mask_inadmissible):
        rows = pl.ds(pl.multiple_of(j * tk, tk), tk)
        s_ref[j] = _dot_nt(kv_ref[rows, 0:HEAD_DIM], q_stack)
        d = _dot_nt(ki_ref[rows, :], qi_stack)
        acc = w_idx[0] * jnp.maximum(d[:, 0:tq], 0.0)
        for h in range(1, IDX_HEADS):
            acc = acc + w_idx[h] * jnp.maximum(d[:, h * tq:(h + 1) * tq], 0.0)
        sc = acc * idx_scale
        if mask_inadmissible:
            sc = jnp.where((j * tk + row_k) // CHUNK <= qchunk, sc, NEG_INF)
        key = _sortable_key(sc)
        key_ref[j] = key
        w1_ref[j] = pack_fields(lax.shift_right_logical(key ^ INT_MIN, 32 - FIELD_BITS))

    def full_blocks(jj, carry):
        for r in range(4):
            score_block(4 * jj + r, False)
        return carry

    nfull = nblk - 1
    lax.fori_loop(0, nfull // 4, full_blocks, 0)

    @pl.when(nfull % 4 >= 2)
    def _():
        score_block((nfull // 4) * 4, False)
        score_block((nfull // 4) * 4 + 1, False)

    @pl.when(nfull % 2 == 1)
    def _():
        score_block(nfull - 1, False)
        score_block(nblk - 1, True)

    @pl.when(nfull % 2 == 0)
    def _():
        score_block(nblk - 1, True)

    k_eff = jnp.minimum(DSA_TOPK, (qchunk + 1) * CHUNK)

    def search(odd_blocks):
        def over_blocks(body, init):
            c = lax.fori_loop(0, nblk // 2, lambda jj, c: body(2 * jj + 1, body(2 * jj, c)), init)
            return body(nblk - 1, c) if odd_blocks else c

        def count(pred):
            def blk(j, c):
                m = pred(key_ref[j], j * tk + row_k).astype(I32)
                return c + jnp.sum(m.reshape(tk // 8, 8, tq), axis=0)
            return jnp.sum(over_blocks(blk, jnp.zeros((8, tq), I32)), axis=0, keepdims=True)

        def count_fields(w_ref, cand):
            cand2 = (cand << 16) | cand
            def blk(j, c):
                hit = ((w_ref[j] - cand2) >> 15) & 0x00010001
                return c + jnp.sum(hit.reshape(half // 8, 8, tq), axis=0)
            c = over_blocks(blk, jnp.zeros((8, tq), I32))
            return jnp.sum((c & 0xFFFF) + (c >> 16), axis=0, keepdims=True)

        def field_search(w_ref, k_want):
            def step(bi, carry):
                prefix, above = carry
                cand = prefix | (jnp.int32(1) << (FIELD_BITS - 1 - bi))
                cnt = count_fields(w_ref, cand)
                ok = cnt >= k_want
                return jnp.where(ok, cand, prefix), jnp.where(ok, above, cnt)
            return lax.fori_loop(0, FIELD_BITS, step, (jnp.zeros((1, tq), I32), jnp.zeros((1, tq), I32)))

        top, above = field_search(w1_ref, k_eff)
        field_max = (1 << FIELD_BITS) - 1

        def pack_mid(j, carry):
            ukey = key_ref[j] ^ INT_MIN
            member = lax.shift_right_logical(ukey, 32 - FIELD_BITS) == top
            mid = lax.shift_right_logical(ukey, 32 - 2 * FIELD_BITS) & field_max
            w2_ref[j] = pack_fields(jnp.where(member, mid, 0))
            return carry

        lax.fori_loop(0, nblk, pack_mid, 0)
        mid, above_mid = field_search(w2_ref, k_eff - above)

        def bit_step(bi, carry):
            prefix, c_gt = carry
            cand_u = prefix | (jnp.int32(1) << (31 - 2 * FIELD_BITS - bi))
            cand_s = cand_u ^ INT_MIN
            cnt = count(lambda key, kpos: key >= cand_s)
            ok = cnt >= k_eff
            return jnp.where(ok, cand_u, prefix), jnp.where(ok, c_gt, cnt)

        prefix, c_gt = lax.fori_loop(0, 32 - 2 * FIELD_BITS, bit_step,
                                     ((top << (32 - FIELD_BITS)) | (mid << (32 - 2 * FIELD_BITS)), above + above_mid))
        return prefix, c_gt

    prefix, c_gt = lax.cond(nblk % 2 == 1, lambda: search(True), lambda: search(False))
    thr = prefix ^ INT_MIN

    need = (k_eff - c_gt).astype(F32)
    tril = tril_ref[...]

    offs = (row_k - qpos).astype(F32)
    p_ref[1] = jnp.zeros(p_ref.shape[1:], BF16)

    def att_block(j, carry):
        m_run, l_run, acc_part, ties_seen = carry
        slot = j % 2
        acc = acc_part + _dot(vt_ref[jnp.maximum(j - 1, 0)], p_ref[1 - slot])
        key = key_ref[j]
        tie = key == thr
        tie_rank = _dot(tril, jnp.where(tie, 1.0, 0.0).astype(BF16)) + ties_seen
        sel = (key > thr) | (tie & (tie_rank <= need))
        dist = jnp.where(sel, jnp.abs(offs + (j * tk).astype(F32)), jnp.inf)
        ms, ls, alphas = [], [], []
        for h in range(DSA_HEADS):
            cols = slice(h * tq, (h + 1) * tq)
            sh = s_ref[j, :, cols] - SLOPES_B[h] * dist
            m_old = m_run[:, cols]
            m_new = jnp.maximum(m_old, jnp.max(sh, axis=0, keepdims=True))
            alpha = jnp.exp(m_old - m_new)
            p = jnp.exp(sh - m_new)
            p_ref[slot, :, cols] = p.astype(BF16)
            ms.append(m_new)
            ls.append(alpha * l_run[:, cols] + jnp.sum(p, axis=0, keepdims=True))
            alphas.append(alpha)
        return (jnp.concatenate(ms, axis=1), jnp.concatenate(ls, axis=1), jnp.concatenate(alphas, axis=1) * acc,
                tie_rank[tk - 1:tk, :])

    init = (jnp.full((1, DSA_HEADS * tq), NEG_INF, F32), jnp.zeros((1, DSA_HEADS * tq), F32),
            jnp.zeros((HEAD_DIM, DSA_HEADS * tq), F32), jnp.zeros((1, tq), F32))
    _, l_run, acc_part, _ = lax.fori_loop(0, nblk, att_block, init)
    acc = acc_part + _dot(vt_ref[nblk - 1], p_ref[(nblk - 1) % 2])
    out = acc / l_run
    o_ref[...] = jnp.concatenate([out[:, h * tq:(h + 1) * tq] for h in range(DSA_HEADS)], axis=0).astype(BF16)


def _dsa(b_q, b_qi, small, b_kv, b_ki, bn, t):
    nt = t // DSA_TQ
    nkb = t // DSA_TK
    v_t = b_kv[:, HEAD_DIM:].reshape(bn * nkb, DSA_TK, HEAD_DIM).transpose(0, 2, 1)
    o_t = pl.pallas_call(
        _dsa_kernel,
        grid=(bn, nt),
        in_specs=[
            pl.BlockSpec((DSA_TQ, 256), lambda b, i: (b * nt + i, 0)),
            pl.BlockSpec((DSA_TQ, 128), lambda b, i: (b * nt + i, 0)),
            pl.BlockSpec((DSA_TQ, 128), lambda b, i: (b * nt + i, 0)),
            pl.BlockSpec((t, 128), lambda b, i: (b, 0)),
            pl.BlockSpec((nkb, HEAD_DIM, DSA_TK), lambda b, i: (b, 0, 0)),
            pl.BlockSpec((t, 128), lambda b, i: (b, 0)),
            pl.BlockSpec((DSA_TK, DSA_TK), lambda b, i: (0, 0)),
        ],
        out_specs=pl.BlockSpec((None, DSA_HEADS * HEAD_DIM, DSA_TQ), lambda b, i: (b * nt + i, 0, 0)),
        out_shape=jax.ShapeDtypeStruct((bn * nt, DSA_HEADS * HEAD_DIM, DSA_TQ), BF16),
        scratch_shapes=[pltpu.VMEM((nkb, DSA_TK, DSA_TQ), I32),
                        pltpu.VMEM((nkb, DSA_TK, DSA_HEADS * DSA_TQ), F32),
                        pltpu.VMEM((2, DSA_TK, DSA_HEADS * DSA_TQ), BF16),
                        pltpu.VMEM((nkb, DSA_TK // 2, DSA_TQ), I32), pltpu.VMEM((nkb, DSA_TK // 2, DSA_TQ), I32)],
        compiler_params=_cparams(("arbitrary", "arbitrary")),
        name="dsa_attention",
    )(b_q, b_qi, small, b_kv, v_t, b_ki, jnp.tril(jnp.ones((DSA_TK, DSA_TK), BF16)))
    return o_t.transpose(0, 2, 1).reshape(bn * t, DSA_HEADS * HEAD_DIM)


def _mlstm_kernel(qk_ref, v_ref, og_ref, sm_ref, convw_ref, convb_ref, gbias_ref, normgt_ref,
                  o_ref, tail_ref, ct_ref, nvec_ref, mst_ref, *, bn):
    c = pl.program_id(0)
    L = CHUNK
    nh, dh, width = MLSTM_HEADS, HEAD_DIM, MLSTM_HEADS * HEAD_DIM

    @pl.when(c == 0)
    def _():
        tail_ref[...] = jnp.zeros_like(tail_ref)
        ct_ref[...] = jnp.zeros_like(ct_ref)
        nvec_ref[...] = jnp.zeros_like(nvec_ref)
        mst_ref[...] = jnp.zeros_like(mst_ref)

    srow = lax.broadcasted_iota(I32, (L, width), 0)
    lane = lax.broadcasted_iota(I32, (L, width), 1)
    jlane = lane % dh
    causal_t = srow <= jlane
    diag_t = srow == jlane
    head_of_lane = lax.broadcasted_iota(I32, (1, width), 1) // dh
    tril = (lax.broadcasted_iota(I32, (L, L), 1) <= lax.broadcasted_iota(I32, (L, L), 0)).astype(F32)
    erow = lax.broadcasted_iota(I32, (128, width), 0)
    ecol_head = lax.broadcasted_iota(I32, (128, width), 1) // dh
    expand_ig = (erow == SM_IG + ecol_head).astype(F32)
    expand_fg = (erow == SM_FG + ecol_head).astype(F32)
    exact = dict(preferred_element_type=F32, precision=lax.Precision.HIGHEST)

    def head_blocks(a):
        out = jnp.where(head_of_lane == 0, a[0:dh], 0.0)
        for h in range(1, nh):
            out = out + jnp.where(head_of_lane == h, a[h * dh:(h + 1) * dh], 0.0)
        return out

    convw = convw_ref[...]
    for cc, b in [(cc, b) for cc in range(MLSTM_CPS) for b in range(bn)]:
        rows = slice(cc * L, (cc + 1) * L)
        cur = qk_ref[b, rows, :]
        ext = jnp.concatenate([tail_ref[b], cur], axis=0)
        tail_ref[b] = cur[L - 8:L, :]
        y = convb_ref[...] + convw[MLSTM_CONV - 1:MLSTM_CONV, :] * cur
        for k in range(MLSTM_CONV - 1):
            off = 8 - (MLSTM_CONV - 1) + k
            y = y + convw[k:k + 1, :] * ext[off:off + L, :]
        qk = y * _sigmoid(y)
        q_all = qk[:, 0:width]
        k_all = qk[:, width:2 * width] * dh ** -0.5
        q_stack = jnp.concatenate([jnp.where(head_of_lane == h, q_all, 0.0) for h in range(nh)], axis=0).astype(BF16)
        v_all = v_ref[b, rows, :]
        v_t = v_all.astype(F32).T.astype(BF16)

        gates = sm_ref[b, rows, :] + gbias_ref[...]
        lf = jnp.minimum(gates, 0.0) - jnp.log(1.0 + jnp.exp(-jnp.abs(gates)))
        bcum = jnp.dot(tril, lf, **exact)
        ig_x = jnp.dot(gates, expand_ig, **exact)
        b_x = jnp.dot(bcum, expand_fg, **exact)
        b_q = jnp.sum(jnp.where(diag_t, b_x, 0.0), axis=0, keepdims=True)
        b_last = b_x[L - 1:L, :]
        m_prev = mst_ref[b]
        ct = ct_ref[b]
        nvec = nvec_ref[b]

        dlog = jnp.where(causal_t, b_q - b_x + ig_x, NEG_INF)
        inter = b_q + m_prev
        mj = jnp.maximum(inter, jnp.max(dlog, axis=0, keepdims=True))
        dw = jnp.exp(dlog - mj)
        iw = jnp.exp(inter - mj)
        sc = _dot_nt(k_all.astype(BF16), q_stack) * dw
        qn = _dot_nt(jnp.broadcast_to(nvec, (8, width)).astype(BF16), q_stack)[0:1, :]
        q_c = _dot_nt(ct.astype(BF16), q_stack)
        num = iw * q_c + head_blocks(_dot(v_t, sc.astype(BF16)))
        den = iw * qn + jnp.sum(sc, axis=0, keepdims=True)
        hj = num / jnp.maximum(jnp.abs(den), jnp.exp(-mj))

        dec = b_last - b_x + ig_x
        m_new = jnp.maximum(b_last + m_prev, jnp.max(dec, axis=0, keepdims=True))
        wc = jnp.exp(b_last + m_prev - m_new)
        kw = k_all * jnp.exp(dec - m_new)
        ct_ref[b] = wc * ct + head_blocks(_dot(v_t, kw.astype(BF16)))
        nvec_ref[b] = wc * nvec + jnp.sum(kw, axis=0, keepdims=True)
        mst_ref[b] = m_new

        mu = jnp.mean(hj, axis=0, keepdims=True)
        dev = hj - mu
        var = jnp.mean(dev * dev, axis=0, keepdims=True)
        hn_t = (dev * lax.rsqrt(var + LN_EPS) * normgt_ref[...]).T
        hn = jnp.concatenate([hn_t[h * dh:(h + 1) * dh, :] for h in range(nh)], axis=1)
        o_ref[b, rows, :] = (_sigmoid(og_ref[b, rows, :]) * hn).astype(BF16)


def _mlstm(c_qk, c_v, c_o, small, conv_w, conv_b, gate_bias, norm_g, bn, t):
    nc = t // (CHUNK * MLSTM_CPS)
    width = MLSTM_HEADS * HEAD_DIM
    norm_g_t = jnp.repeat(norm_g.reshape(MLSTM_HEADS, HEAD_DIM).T, HEAD_DIM, axis=1)
    blk = lambda w: pl.BlockSpec((bn, CHUNK * MLSTM_CPS, w), lambda c: (0, c, 0))
    full = lambda a: pl.BlockSpec(a.shape, lambda c: (0,) * a.ndim)
    return pl.pallas_call(
        functools.partial(_mlstm_kernel, bn=bn),
        grid=(nc,),
        in_specs=[blk(512), blk(256), blk(256), blk(128), full(conv_w), full(conv_b), full(gate_bias), full(norm_g_t)],
        out_specs=blk(256),
        out_shape=jax.ShapeDtypeStruct((bn, t, 256), BF16),
        scratch_shapes=[pltpu.VMEM((bn, 8, 512), F32), pltpu.VMEM((bn, HEAD_DIM, width), F32),
                        pltpu.VMEM((bn, 1, width), F32), pltpu.VMEM((bn, 1, width), F32)],
        compiler_params=_cparams(("arbitrary",)),
        name="mlstm",
    )(c_qk.reshape(bn, t, 512), c_v.reshape(bn, t, 256), c_o.reshape(bn, t, 256), small.reshape(bn, t, 128),
      conv_w, conv_b, gate_bias, norm_g_t)


def _s5_kernel(u_ref, perm_ref, bbre_ref, bbim_ref, pwre_ref, pwim_ref, cre_ref, cim_ref, dskip_ref, gluw_ref,
               glub_ref, o_ref, sre_ref, sim_ref, stre_ref, stim_ref, yp_ref):
    @pl.when(pl.program_id(1) == 0)
    def _():
        stre_ref[...] = jnp.zeros_like(stre_ref)
        stim_ref[...] = jnp.zeros_like(stim_ref)

    nb = S5_NB
    seg_len = S5_TT // S5_SEGS
    us = [u_ref[b] for b in range(nb)]
    for b in range(nb):
        ub = _dot(perm_ref[...], us[b].astype(BF16)).astype(BF16)
        sre_ref[b] = _dot(ub, bbre_ref[...])
        sim_ref[b] = _dot(ub, bbim_ref[...])
    a_re = jnp.broadcast_to(pwre_ref[0:1, :], (S5_SEGS, S5_LANES))
    a_im = jnp.broadcast_to(pwim_ref[0:1, :], (S5_SEGS, S5_LANES))

    def local_step(i, carry):
        rows = pl.ds(pl.multiple_of(i * S5_SEGS, S5_SEGS), S5_SEGS)
        out = []
        for b in range(nb):
            s_re, s_im = carry[b]
            n_re = a_re * s_re - a_im * s_im + sre_ref[b, rows, :]
            n_im = a_re * s_im + a_im * s_re + sim_ref[b, rows, :]
            sre_ref[b, rows, :] = n_re
            sim_ref[b, rows, :] = n_im
            out.append((n_re, n_im))
        return tuple(out)

    zeros = jnp.zeros((S5_SEGS, S5_LANES), F32)
    ends = lax.fori_loop(0, seg_len, local_step, tuple((zeros, zeros) for _ in range(nb)), unroll=4)

    al_re, al_im = pwre_ref[seg_len - 1:seg_len, :], pwim_ref[seg_len - 1:seg_len, :]
    cins = []
    for b in range(nb):
        e_re, e_im = ends[b]
        c_re, c_im = stre_ref[b, 0:1, :], stim_ref[b, 0:1, :]
        cs_re, cs_im = [], []
        for k in range(S5_SEGS):
            cs_re.append(c_re)
            cs_im.append(c_im)
            c_re, c_im = (e_re[k:k + 1, :] + al_re * c_re - al_im * c_im,
                          e_im[k:k + 1, :] + al_re * c_im + al_im * c_re)
        stre_ref[b] = jnp.broadcast_to(c_re, stre_ref.shape[1:])
        stim_ref[b] = jnp.broadcast_to(c_im, stim_ref.shape[1:])
        cins.append((jnp.concatenate(cs_re, axis=0), jnp.concatenate(cs_im, axis=0)))

    def correct_step(i, carry):
        rows = pl.ds(pl.multiple_of(i * S5_SEGS, S5_SEGS), S5_SEGS)
        p_re, p_im = pwre_ref[pl.ds(i, 1), :], pwim_ref[pl.ds(i, 1), :]
        for b in range(nb):
            cin_re, cin_im = cins[b]
            sre_ref[b, rows, :] = sre_ref[b, rows, :] + p_re * cin_re - p_im * cin_im
            sim_ref[b, rows, :] = sim_ref[b, rows, :] + p_re * cin_im + p_im * cin_re
        return carry

    lax.fori_loop(0, seg_len, correct_step, 0, unroll=4)
    ngrp = GROUP_WIDTH // 128
    for b in range(nb):
        ycs = _dot(sre_ref[b].astype(BF16), cre_ref[...]) - _dot(sim_ref[b].astype(BF16), cim_ref[...])
        for g in range(ngrp):
            yp_ref[b, g] = ycs[:, g * 128:(g + 1) * 128]
        blocks = []
        for k in range(S5_SEGS):
            for i0 in range(0, seg_len, 8):
                rows = pl.ds(i0 * S5_SEGS + k, 8, stride=S5_SEGS)
                blocks.append(jnp.concatenate([yp_ref[b, g, rows, :] for g in range(ngrp)], axis=1))
        y = jnp.concatenate(blocks, axis=0) + dskip_ref[...] * us[b]
        y = 0.5 * y * (1.0 + jnp.tanh(math.sqrt(2.0 / math.pi) * (y + 0.044715 * (y * y * y))))
        z = _dot(y.astype(BF16), gluw_ref[...]) + glub_ref[...]
        o_ref[b] = (y * _sigmoid(z)).astype(BF16)


def _s5(d_u, bb_re, bb_im, pw_re, pw_im, c_re_t, c_im_t, d_skip, glu_w, glu_b, bn, t):
    nt = t // S5_TT
    nb = S5_NB
    full = lambda a: pl.BlockSpec(a.shape, lambda b, i: (0,) * a.ndim)
    r = jnp.arange(S5_TT)
    perm = (r[None, :] == ((r % S5_SEGS) * (S5_TT // S5_SEGS) + r // S5_SEGS)[:, None]).astype(BF16)
    args = (perm, bb_re, bb_im, pw_re, pw_im, c_re_t, c_im_t, d_skip, glu_w, glu_b)
    tile = pl.BlockSpec((nb, S5_TT, GROUP_WIDTH), lambda b, i: (b, i, 0))
    return pl.pallas_call(
        _s5_kernel,
        grid=(bn // nb, nt),
        in_specs=[tile] + [full(a) for a in args],
        out_specs=tile,
        out_shape=jax.ShapeDtypeStruct((bn, t, GROUP_WIDTH), BF16),
        scratch_shapes=[pltpu.VMEM((nb, S5_TT, S5_LANES), F32), pltpu.VMEM((nb, S5_TT, S5_LANES), F32),
                        pltpu.VMEM((nb, 8, S5_LANES), F32), pltpu.VMEM((nb, 8, S5_LANES), F32),
                        pltpu.VMEM((nb, GROUP_WIDTH // 128, S5_TT, 128), F32)],
        compiler_params=_cparams(("arbitrary", "arbitrary")),
        name="s5_glu",
    )(d_u.reshape(bn, t, GROUP_WIDTH), *args).reshape(bn * t, GROUP_WIDTH)


def _outproj_ffn_kernel(x_ref, modm_ref, oa_ref, ob_ref, oc_ref, od_ref, wout_ref, lngm_ref, lnbm_ref,
                        modf_ref, w13_ref, w2_ref, lngf_ref, lnbf_ref, o_ref, h_ref):
    y = _dot(oa_ref[...], wout_ref[0])
    y = y + _dot(ob_ref[...], wout_ref[1])
    y = y + _dot(oc_ref[...], wout_ref[2])
    y = y + _dot(od_ref[...], wout_ref[3])
    x_mid = _residual_layer_norm(x_ref[...], modm_ref[2:3, :] * y, lngm_ref[...], lnbm_ref[...])
    o_ref[...] = _ffn_tile(x_mid, modf_ref, w13_ref, w2_ref, lngf_ref, lnbf_ref, h_ref)


def _outproj_ffn(x2, mod_mix, o_a, o_b, o_c, o_d, w_out4, ln_g_mix, ln_b_mix, mod_ffn, w13c, w2c, ln_g_ffn, ln_b_ffn,
                 rows_per_batch):
    n = x2.shape[0]
    tm = ROW_TILE
    tiles_per_batch = rows_per_batch // tm
    row = pl.BlockSpec((tm, D_MODEL), lambda i: (i, 0))
    mod = pl.BlockSpec((None, 8, D_MODEL), lambda i: (i // tiles_per_batch, 0, 0))
    mix = pl.BlockSpec((tm, GROUP_WIDTH), lambda i: (i, 0))
    vec = pl.BlockSpec((1, D_MODEL), lambda i: (0, 0))
    resident = lambda a: pl.BlockSpec(a.shape, lambda i: (0,) * a.ndim, pipeline_mode=pl.Buffered(1))
    return pl.pallas_call(
        _outproj_ffn_kernel,
        grid=(n // tm,),
        in_specs=[row, mod, mix, mix, mix, mix, resident(w_out4), vec, vec, mod, resident(w13c), resident(w2c), vec, vec],
        out_specs=row,
        out_shape=jax.ShapeDtypeStruct((n, D_MODEL), F32),
        scratch_shapes=[pltpu.VMEM((tm, D_FF), BF16)],
        compiler_params=_cparams(("arbitrary",)),
        name="outproj_ffn",
    )(x2, mod_mix, o_a, o_b, o_c, o_d, w_out4, ln_g_mix.reshape(1, -1), ln_b_mix.reshape(1, -1),
      mod_ffn, w13c, w2c, ln_g_ffn.reshape(1, -1), ln_b_ffn.reshape(1, -1))


def _permute_w_in(w_in):
    off = {}
    o = 0
    for name, s in (("qa", 256), ("ka", 128), ("va", 128), ("cq", 128), ("kb", 64), ("vb", 64), ("ki", 32),
                    ("wi", 4), ("qkc", 512), ("vc", 256), ("ig", 4), ("fg", 4), ("oc", 256), ("ud", 256)):
        off[name] = (o, o + s)
        o += s
    col = lambda n: w_in[:, off[n][0]:off[n][1]]
    small = jnp.concatenate([col("wi"), col("ig"), col("fg"),
                             jnp.zeros((w_in.shape[0], 128 - 12), w_in.dtype)], axis=1)
    parts = [col("qa"), col("ka"), col("va"), col("cq"), col("kb"), col("vb")] + [col("ki")] * 4 + [
        col("qkc"), col("vc"), col("oc"), col("ud"), small]
    return jnp.concatenate(parts, axis=1).astype(BF16)


def _s5_params(lam_re, lam_im, log_step, b_re, b_im, c_re, c_im):
    dt = jnp.exp(log_step)[:, None]
    mag = jnp.exp(lam_re * dt)
    a_re, a_im = mag * jnp.cos(lam_im * dt), mag * jnp.sin(lam_im * dt)
    den = lam_re * lam_re + lam_im * lam_im
    kap_re = ((a_re - 1.0) * lam_re + a_im * lam_im) / den
    kap_im = (a_im * lam_re - (a_re - 1.0) * lam_im) / den
    bb_re = kap_re[..., None] * b_re - kap_im[..., None] * b_im
    bb_im = kap_re[..., None] * b_im + kap_im[..., None] * b_re
    eye = jnp.eye(S5_GROUPS, dtype=F32)

    def in_mat(bb):
        return jnp.einsum("gph,gk->ghkp", bb, eye).reshape(S5_GROUPS * S5_GROUP_CH, S5_LANES).astype(BF16)

    def out_mat(cc):
        return jnp.einsum("gop,gk->gpko", cc, eye).reshape(S5_LANES, S5_GROUPS * S5_GROUP_CH).astype(BF16)

    n = jnp.arange(1, S5_TT // S5_SEGS + 1, dtype=F32)[:, None, None]
    pw_mag = jnp.exp(n * (lam_re * dt))
    pw_re = (pw_mag * jnp.cos(n * (lam_im * dt))).at[0].set(a_re).reshape(-1, S5_LANES)
    pw_im = (pw_mag * jnp.sin(n * (lam_im * dt))).at[0].set(a_im).reshape(-1, S5_LANES)
    return in_mat(bb_re), in_mat(bb_im), pw_re, pw_im, out_mat(c_re), out_mat(c_im)


def kernel(x, c, ada_w, ada_b, ln_g, ln_b, ffn_w13, ffn_w2, w_in, w_out, sinks, w_uq, w_iq, conv_w, conv_b, ig_b,
           fg_b, mh_norm_g, lam_re, lam_im, log_step, b_re, b_im, c_re, c_im, d_skip, glu_w, glu_b):
    bn, t, d = x.shape
    assert d == D_MODEL and t % max(ROW_TILE, DSA_TK, S5_TT) == 0 and bn <= 8 and bn % S5_NB == 0
    n = bn * t
    nl = ada_w.shape[0]
    c_pad = jnp.zeros((8, d), F32).at[:bn].set(c)
    mod_all = _ada_mod(c_pad, ada_w, ada_b)
    mod_all = mod_all[:, :bn].reshape(nl, bn, N_SUB, 3, d).transpose(0, 2, 1, 3, 4)
    mod_all = jnp.pad(mod_all, ((0, 0), (0, 0), (0, 0), (0, 5), (0, 0)))

    x2 = x.reshape(n, d)
    for l in range(nl):
        w13c = ffn_w13[l].astype(BF16)
        w2c = ffn_w2[l].astype(BF16)
        x2, (a_q, a_kv, b_q, b_qi, b_kv, b_ki, c_qk, c_v, c_o, d_u, small) = _ffn_inproj(
            x2, mod_all[l, 0], w13c[0], w2c[0], ln_g[l, 0], ln_b[l, 0], mod_all[l, 1],
            _permute_w_in(w_in[l]), w_uq[l].astype(BF16), w_iq[l].astype(BF16), t)
        o_a = _swa(sinks[l], a_q, a_kv, bn, t)
        o_b = _dsa(b_q, b_qi, small, b_kv, b_ki, bn, t)
        gate_bias = jnp.zeros((1, 128), F32).at[0, SM_IG:SM_IG + 4].set(ig_b[l]).at[0, SM_FG:SM_FG + 4].set(fg_b[l])
        o_c = _mlstm(c_qk, c_v, c_o, small, conv_w[l], conv_b[l].reshape(1, -1), gate_bias,
                     mh_norm_g[l].reshape(1, -1), bn, t).reshape(n, GROUP_WIDTH)
        s5p = _s5_params(lam_re[l], lam_im[l], log_step[l], b_re[l], b_im[l], c_re[l], c_im[l])
        o_d = _s5(d_u, *s5p, d_skip[l].reshape(1, -1), glu_w[l].astype(BF16), glu_b[l].reshape(1, -1), bn, t)
        x2 = _outproj_ffn(x2, mod_all[l, 1], o_a, o_b, o_c, o_d, w_out[l].astype(BF16).reshape(4, GROUP_WIDTH, d),
                          ln_g[l, 1], ln_b[l, 1], mod_all[l, 2], w13c[1], w2c[1], ln_g[l, 2], ln_b[l, 2], t)
    return x2.reshape(bn, t, d)
```

```python
import functools
import math

import jax
import jax.numpy as jnp
from jax import lax
from jax.experimental import pallas as pl
from jax.experimental.pallas import tpu as pltpu

F32 = jnp.float32
BF16 = jnp.bfloat16
I32 = jnp.int32

D_MODEL = 1024
DEPTH = 2
CHUNK = 64
HEAD_DIM = 64
GROUP_WIDTH = 256
SWA_HEADS = 4
SWA_KV_HEADS = 2
SWA_WIN_CHUNKS = 2
DSA_HEADS = 4
DSA_Q_RANK = 128
IDX_HEADS = 4
IDX_DIM = 32
DSA_TOPK = 256
MLSTM_HEADS = 4
MLSTM_CONV = 4
MLSTM_CPS = 8
S5_GROUP_CH = 16
S5_GROUPS = 16
S5_STATE = 64
S5_LANES = S5_GROUPS * S5_STATE
D_FF = 2816
N_SUB = 3
ALPHA = (2 * DEPTH) ** 0.25
LN_EPS = 1e-5
NEG_INF = -1e30
INT_MIN = -(2 ** 31)
FIELD_BITS = 15
FIELD_GUARD = -(2 ** 31) + 2 ** 15

SLOPES_A = tuple(2.0 ** -(i + 1) for i in range(0, 8, 2))
SLOPES_B = tuple(2.0 ** -(i + 1) for i in range(1, 8, 2))

VMEM_LIMIT_BYTES = 56 * 1024 * 1024

FFN_TF = 256
FFN_NCHUNK = D_FF // FFN_TF
ROW_TILE = 512
SWA_TQ = 128
SWA_SUB = 4
DSA_TQ = 128
DSA_TK = 512
S5_TT = 512
S5_SEGS = 8
S5_NB = 2

ZC_QA, ZC_KVA, ZC_CQ, ZC_KVB, ZC_KI = 0, 256, 512, 640, 768
ZC_QKC, ZC_VC, ZC_OC, ZC_UD, ZC_SMALL, Z_WIDTH = 896, 1408, 1664, 1920, 2176, 2304
SM_WI, SM_IG, SM_FG = 0, 4, 8


def _cparams(sem, fuse_inputs=None):
    return pltpu.CompilerParams(dimension_semantics=sem, vmem_limit_bytes=VMEM_LIMIT_BYTES,
                                allow_input_fusion=fuse_inputs)


def _dot(a, b):
    return jnp.dot(a, b, preferred_element_type=F32)


def _dot_nt(a, b):
    return lax.dot_general(a, b, (((1,), (1,)), ((), ())), preferred_element_type=F32)


def _sigmoid(x):
    return 1.0 / (1.0 + jnp.exp(-x))


def _residual_layer_norm(x, y, g, b):
    v = ALPHA * x + y
    mu = jnp.mean(v, axis=-1, keepdims=True)
    d = v - mu
    var = jnp.mean(d * d, axis=-1, keepdims=True)
    return d * lax.rsqrt(var + LN_EPS) * g + b


def _ada_kernel(c_ref, w_ref, b_ref, o_ref):
    c = c_ref[...]
    cs = c * _sigmoid(c)
    o_ref[...] = jnp.dot(cs, w_ref[...], preferred_element_type=F32,
                         precision=lax.Precision.HIGHEST) + b_ref[...]


def _ada_mod(c_pad, ada_w, ada_b):
    nl = ada_w.shape[0]
    ncol = ada_w.shape[2] // D_MODEL
    return pl.pallas_call(
        _ada_kernel,
        grid=(nl, ncol),
        in_specs=[
            pl.BlockSpec((8, D_MODEL), lambda l, j: (0, 0)),
            pl.BlockSpec((None, D_MODEL, D_MODEL), lambda l, j: (l, 0, j)),
            pl.BlockSpec((None, 1, D_MODEL), lambda l, j: (l, 0, j)),
        ],
        out_specs=pl.BlockSpec((None, 8, D_MODEL), lambda l, j: (l, 0, j)),
        out_shape=jax.ShapeDtypeStruct((nl, 8, ada_w.shape[2]), F32),
        compiler_params=_cparams(("arbitrary", "arbitrary")),
        name="ada_mod",
    )(c_pad, ada_w, ada_b.reshape(nl, 1, -1))


def _ffn_tile(x, mod_ref, w13_ref, w2_ref, lng_ref, lnb_ref, h_ref):
    shift, scale, gate = mod_ref[0:1, :], mod_ref[1:2, :], mod_ref[2:3, :]
    u = (x * (1.0 + scale) + shift).astype(BF16)
    for j in range(FFN_NCHUNK):
        a = _dot(u, w13_ref[:, j * FFN_TF:(j + 1) * FFN_TF])
        g = _dot(u, w13_ref[:, D_FF + j * FFN_TF:D_FF + (j + 1) * FFN_TF])
        h_ref[:, j * FFN_TF:(j + 1) * FFN_TF] = (a * _sigmoid(a) * g).astype(BF16)
    y = _dot(h_ref[...], w2_ref[...])
    return _residual_layer_norm(x, 0.5 * gate * y, lng_ref[...], lnb_ref[...])


def _inproj_tile(x, mod_ref, w_ref, wuq_ref, wiq_ref, outs):
    (aq_ref, akv_ref, bq_ref, bqi_ref, bkv_ref, bki_ref, cqk_ref, cv_ref, co_ref, du_ref, sm_ref) = outs
    shift, scale = mod_ref[0:1, :], mod_ref[1:2, :]
    u = (x * (1.0 + scale) + shift).astype(BF16)
    z = _dot(u, w_ref[...])
    aq_ref[...] = z[:, ZC_QA:ZC_KVA].astype(BF16)
    akv_ref[...] = z[:, ZC_KVA:ZC_CQ].astype(BF16)
    cq = z[:, ZC_CQ:ZC_KVB].astype(BF16)
    bq_ref[...] = (_dot(cq, wuq_ref[...]) * HEAD_DIM ** -0.5).astype(BF16)
    bqi_ref[...] = _dot(cq, wiq_ref[...]).astype(BF16)
    bkv_ref[...] = z[:, ZC_KVB:ZC_KI].astype(BF16)
    bki_ref[...] = z[:, ZC_KI:ZC_QKC].astype(BF16)
    cqk_ref[...] = z[:, ZC_QKC:ZC_VC]
    cv_ref[...] = z[:, ZC_VC:ZC_OC].astype(BF16)
    co_ref[...] = z[:, ZC_OC:ZC_UD]
    du_ref[...] = z[:, ZC_UD:ZC_SMALL]
    sm_ref[...] = z[:, ZC_SMALL:Z_WIDTH]


INPROJ_OUTPUTS = ((256, BF16), (256, BF16), (256, BF16), (128, BF16), (128, BF16), (128, BF16),
                  (512, F32), (256, BF16), (256, F32), (256, F32), (128, F32))


def _ffn_inproj_kernel(x_ref, modf_ref, w13_ref, w2_ref, lng_ref, lnb_ref, modm_ref, win_ref, wuq_ref, wiq_ref,
                       o_ref, *rest):
    outs, h_ref = rest[:-1], rest[-1]
    x_new = _ffn_tile(x_ref[...], modf_ref, w13_ref, w2_ref, lng_ref, lnb_ref, h_ref)
    o_ref[...] = x_new
    _inproj_tile(x_new, modm_ref, win_ref, wuq_ref, wiq_ref, outs)


def _ffn_inproj(x2, mod_ffn, w13c, w2c, ln_g, ln_b, mod_mix, w_perm, w_uq, w_iq, rows_per_batch):
    n = x2.shape[0]
    tm = ROW_TILE
    tiles_per_batch = rows_per_batch // tm
    row = pl.BlockSpec((tm, D_MODEL), lambda i: (i, 0))
    mod = pl.BlockSpec((None, 8, D_MODEL), lambda i: (i // tiles_per_batch, 0, 0))
    vec = pl.BlockSpec((1, D_MODEL), lambda i: (0, 0))
    resident = lambda a: pl.BlockSpec(a.shape, lambda i: (0,) * a.ndim, pipeline_mode=pl.Buffered(1))
    outs = pl.pallas_call(
        _ffn_inproj_kernel,
        grid=(n // tm,),
        in_specs=[row, mod, resident(w13c), resident(w2c), vec, vec, mod, resident(w_perm), resident(w_uq),
                  resident(w_iq)],
        out_specs=[row] + [pl.BlockSpec((tm, w), lambda i: (i, 0)) for w, _ in INPROJ_OUTPUTS],
        out_shape=[jax.ShapeDtypeStruct((n, D_MODEL), F32)]
        + [jax.ShapeDtypeStruct((n, w), dt) for w, dt in INPROJ_OUTPUTS],
        scratch_shapes=[pltpu.VMEM((tm, D_FF), BF16)],
        compiler_params=_cparams(("arbitrary",), [False, False, True, True, False, False, False, False, True, True]),
        name="ffn_inproj",
    )(x2, mod_ffn, w13c, w2c, ln_g.reshape(1, -1), ln_b.reshape(1, -1), mod_mix, w_perm, w_uq, w_iq)
    return outs[0], outs[1:]


def _swa_kernel(sink_ref, q_ref, kvc_ref, kvp_ref, o_ref):
    i = pl.program_id(1)
    tq = SWA_TQ
    rep = SWA_HEADS // SWA_KV_HEADS
    for r in range(SWA_SUB):
        rows = slice(r * tq, (r + 1) * tq)
        q = q_ref[rows, :]
        prev = kvp_ref[...] if r == 0 else kvc_ref[(r - 1) * tq:r * tq, :]
        kv = jnp.concatenate([prev, kvc_ref[rows, :]], axis=0)
        first = (i * SWA_SUB + r) * tq
        qpos = first + lax.broadcasted_iota(I32, (tq, 2 * tq), 0)
        kpos = first - tq + lax.broadcasted_iota(I32, (tq, 2 * tq), 1)
        qchunk = qpos // CHUNK
        kchunk = (kpos + tq) // CHUNK - tq // CHUNK
        valid = (kpos >= 0) & (kchunk <= qchunk) & (kchunk >= qchunk - SWA_WIN_CHUNKS)
        dist = jnp.abs(qpos - kpos).astype(F32)
        for h in range(SWA_HEADS):
            g = h // rep
            qh = q[:, h * HEAD_DIM:(h + 1) * HEAD_DIM]
            kg = kv[:, g * HEAD_DIM:(g + 1) * HEAD_DIM]
            vg = kv[:, (SWA_KV_HEADS + g) * HEAD_DIM:(SWA_KV_HEADS + g + 1) * HEAD_DIM]
            s = _dot_nt(qh, kg) * HEAD_DIM ** -0.5 - SLOPES_A[h] * dist
            s = jnp.where(valid, s, NEG_INF)
            sink = sink_ref[h]
            m = jnp.maximum(jnp.max(s, axis=-1, keepdims=True), sink)
            p = jnp.exp(s - m)
            denom = jnp.sum(p, axis=-1, keepdims=True) + jnp.exp(sink - m)
            o = _dot(p.astype(BF16), vg) / denom
            o_ref[rows, h * HEAD_DIM:(h + 1) * HEAD_DIM] = o.astype(BF16)


def _swa(sinks, a_q, a_kv, bn, t):
    rows = SWA_SUB * SWA_TQ
    nt = t // rows
    cur = pl.BlockSpec((rows, 256), lambda b, i: (b * nt + i, 0))
    return pl.pallas_call(
        _swa_kernel,
        grid=(bn, nt),
        in_specs=[
            pl.BlockSpec(memory_space=pltpu.SMEM),
            cur, cur,
            pl.BlockSpec((SWA_TQ, 256), lambda b, i: (jnp.maximum((b * nt + i) * SWA_SUB - 1, 0), 0)),
        ],
        out_specs=cur,
        out_shape=jax.ShapeDtypeStruct((bn * t, 256), BF16),
        compiler_params=_cparams(("arbitrary", "arbitrary")),
        name="swa_attention",
    )(sinks, a_q, a_kv, a_kv)


def _sortable_key(x):
    bits = lax.bitcast_convert_type(x, I32)
    return bits ^ ((bits >> 31) & 0x7FFFFFFF)


def _dsa_kernel(q_ref, qi_ref, sm_ref, kv_ref, vt_ref, ki_ref, tril_ref, o_ref, key_ref, s_ref, p_ref, w1_ref, w2_ref):
    i = pl.program_id(1)
    tq, tk = DSA_TQ, DSA_TK
    nblk = (i * tq + tq + tk - 1) // tk
    qpos = i * tq + lax.broadcasted_iota(I32, (1, tq), 1)
    qchunk = qpos // CHUNK
    row_k = lax.broadcasted_iota(I32, (tk, tq), 0)

    qi = qi_ref[...]
    lane_i = lax.broadcasted_iota(I32, (1, IDX_HEADS * IDX_DIM), 1) // IDX_DIM
    qi_stack = jnp.concatenate([jnp.where(lane_i == h, qi, jnp.zeros_like(qi)) for h in range(IDX_HEADS)], axis=0)
    sm_t = sm_ref[...].T
    w_idx = [sm_t[SM_WI + h:SM_WI + h + 1, :] for h in range(IDX_HEADS)]
    idx_scale = (IDX_DIM * IDX_HEADS) ** -0.5
    q = q_ref[...]
    q_stack = jnp.concatenate([q[:, h * HEAD_DIM:(h + 1) * HEAD_DIM] for h in range(DSA_HEADS)], axis=0)

    half = tk // 2
    guard = jnp.int32(FIELD_GUARD)

    def pack_fields(f):
        return (f[0:half] << 16) | f[half:tk] | guard

    def score_block(j, mask_inadmissible):
        rows = pl.ds(pl.multiple_of(j * tk, tk), tk)
        s_ref[j] = _dot_nt(kv_ref[rows, 0:HEAD_DIM], q_stack)
        d = _dot_nt(ki_ref[rows, :], qi_stack)
        acc = w_idx[0] * jnp.maximum(d[:, 0:tq], 0.0)
        for h in range(1, IDX_HEADS):
            acc = acc + w_idx[h] * jnp.maximum(d[:, h * tq:(h + 1) * tq], 0.0)
        sc = acc * idx_scale
        if mask_inadmissible:
            sc = jnp.where((j * tk + row_k) // CHUNK <= qchunk, sc, NEG_INF)
        key = _sortable_key(sc)
        key_ref[j] = key
        w1_ref[j] = pack_fields(lax.shift_right_logical(key ^ INT_MIN, 32 - FIELD_BITS))

    def full_blocks(jj, carry):
        for r in range(4):
            score_block(4 * jj + r, False)
        return carry

    nfull = nblk - 1
    lax.fori_loop(0, nfull // 4, full_blocks, 0)

    @pl.when(nfull % 4 >= 2)
    def _():
        score_block((nfull // 4) * 4, False)
        score_block((nfull // 4) * 4 + 1, False)

    @pl.when(nfull % 2 == 1)
    def _():
        score_block(nfull - 1, False)
        score_block(nblk - 1, True)

    @pl.when(nfull % 2 == 0)
    def _():
        score_block(nblk - 1, True)

    k_eff = jnp.minimum(DSA_TOPK, (qchunk + 1) * CHUNK)

    def search(odd_blocks):
        def over_blocks(body, init):
            c = lax.fori_loop(0, nblk // 2, lambda jj, c: body(2 * jj + 1, body(2 * jj, c)), init)
            return body(nblk - 1, c) if odd_blocks else c

        def count(pred):
            def blk(j, c):
                m = pred(key_ref[j], j * tk + row_k).astype(I32)
                return c + jnp.sum(m.reshape(tk // 8, 8, tq), axis=0)
            return jnp.sum(over_blocks(blk, jnp.zeros((8, tq), I32)), axis=0, keepdims=True)

        def count_fields(w_ref, cand):
            cand2 = (cand << 16) | cand
            def blk(j, c):
                hit = ((w_ref[j] - cand2) >> 15) & 0x00010001
                return c + jnp.sum(hit.reshape(half // 8, 8, tq), axis=0)
            c = over_blocks(blk, jnp.zeros((8, tq), I32))
            return jnp.sum((c & 0xFFFF) + (c >> 16), axis=0, keepdims=True)

        def field_search(w_ref, k_want):
            def step(bi, carry):
                prefix, above = carry
                cand = prefix | (jnp.int32(1) << (FIELD_BITS - 1 - bi))
                cnt = count_fields(w_ref, cand)
                ok = cnt >= k_want
                return jnp.where(ok, cand, prefix), jnp.where(ok, above, cnt)
            return lax.fori_loop(0, FIELD_BITS, step, (jnp.zeros((1, tq), I32), jnp.zeros((1, tq), I32)))

        top, above = field_search(w1_ref, k_eff)
        field_max = (1 << FIELD_BITS) - 1

        def pack_mid(j, carry):
            ukey = key_ref[j] ^ INT_MIN
            member = lax.shift_right_logical(ukey, 32 - FIELD_BITS) == top
            mid = lax.shift_right_logical(ukey, 32 - 2 * FIELD_BITS) & field_max
            w2_ref[j] = pack_fields(jnp.where(member, mid, 0))
            return carry

        lax.fori_loop(0, nblk, pack_mid, 0)
        mid, above_mid = field_search(w2_ref, k_eff - above)

        def bit_step(bi, carry):
            prefix, c_gt = carry
            cand_u = prefix | (jnp.int32(1) << (31 - 2 * FIELD_BITS - bi))
            cand_s = cand_u ^ INT_MIN
            cnt = count(lambda key, kpos: key >= cand_s)
            ok = cnt >= k_eff
            return jnp.where(ok, cand_u, prefix), jnp.where(ok, c_gt, cnt)

        prefix, c_gt = lax.fori_loop(0, 32 - 2 * FIELD_BITS, bit_step,
                                     ((top << (32 - FIELD_BITS)) | (mid << (32 - 2 * FIELD_BITS)), above + above_mid))
        return prefix, c_gt

    prefix, c_gt = lax.cond(nblk % 2 == 1, lambda: search(True), lambda: search(False))
    thr = prefix ^ INT_MIN

    need = (k_eff - c_gt).astype(F32)
    tril = tril_ref[...]

    offs = (row_k - qpos).astype(F32)
    p_ref[1] = jnp.zeros(p_ref.shape[1:], BF16)

    def att_block(j, carry):
        m_run, l_run, acc_part, ties_seen = carry
        slot = j % 2
        acc = acc_part + _dot(vt_ref[jnp.maximum(j - 1, 0)], p_ref[1 - slot])
        key = key_ref[j]
        tie = key == thr
        tie_rank = _dot(tril, jnp.where(tie, 1.0, 0.0).astype(BF16)) + ties_seen
        sel = (key > thr) | (tie & (tie_rank <= need))
        dist = jnp.where(sel, jnp.abs(offs + (j * tk).astype(F32)), jnp.inf)
        ms, ls, alphas = [], [], []
        for h in range(DSA_HEADS):
            cols = slice(h * tq, (h + 1) * tq)
            sh = s_ref[j, :, cols] - SLOPES_B[h] * dist
            m_old = m_run[:, cols]
            m_new = jnp.maximum(m_old, jnp.max(sh, axis=0, keepdims=True))
            alpha = jnp.exp(m_old - m_new)
            p = jnp.exp(sh - m_new)
            p_ref[slot, :, cols] = p.astype(BF16)
            ms.append(m_new)
            ls.append(alpha * l_run[:, cols] + jnp.sum(p, axis=0, keepdims=True))
            alphas.append(alpha)
        return (jnp.concatenate(ms, axis=1), jnp.concatenate(ls, axis=1), jnp.concatenate(alphas, axis=1) * acc,
                tie_rank[tk - 1:tk, :])

    init = (jnp.full((1, DSA_HEADS * tq), NEG_INF, F32), jnp.zeros((1, DSA_HEADS * tq), F32),
            jnp.zeros((HEAD_DIM, DSA_HEADS * tq), F32), jnp.zeros((1, tq), F32))
    _, l_run, acc_part, _ = lax.fori_loop(0, nblk, att_block, init)
    acc = acc_part + _dot(vt_ref[nblk - 1], p_ref[(nblk - 1) % 2])
    out = acc / l_run
    o_ref[...] = jnp.concatenate([out[:, h * tq:(h + 1) * tq] for h in range(DSA_HEADS)], axis=0).astype(BF16)


def _dsa(b_q, b_qi, small, b_kv, b_ki, bn, t):
    nt = t // DSA_TQ
    nkb = t // DSA_TK
    v_t = b_kv[:, HEAD_DIM:].reshape(bn * nkb, DSA_TK, HEAD_DIM).transpose(0, 2, 1)
    o_t = pl.pallas_call(
        _dsa_kernel,
        grid=(bn, nt),
        in_specs=[
            pl.BlockSpec((DSA_TQ, 256), lambda b, i: (b * nt + i, 0)),
            pl.BlockSpec((DSA_TQ, 128), lambda b, i: (b * nt + i, 0)),
            pl.BlockSpec((DSA_TQ, 128), lambda b, i: (b * nt + i, 0)),
            pl.BlockSpec((t, 128), lambda b, i: (b, 0)),
            pl.BlockSpec((nkb, HEAD_DIM, DSA_TK), lambda b, i: (b, 0, 0)),
            pl.BlockSpec((t, 128), lambda b, i: (b, 0)),
            pl.BlockSpec((DSA_TK, DSA_TK), lambda b, i: (0, 0)),
        ],
        out_specs=pl.BlockSpec((None, DSA_HEADS * HEAD_DIM, DSA_TQ), lambda b, i: (b * nt + i, 0, 0)),
        out_shape=jax.ShapeDtypeStruct((bn * nt, DSA_HEADS * HEAD_DIM, DSA_TQ), BF16),
        scratch_shapes=[pltpu.VMEM((nkb, DSA_TK, DSA_TQ), I32),
                        pltpu.VMEM((nkb, DSA_TK, DSA_HEADS * DSA_TQ), F32),
                        pltpu.VMEM((2, DSA_TK, DSA_HEADS * DSA_TQ), BF16),
                        pltpu.VMEM((nkb, DSA_TK // 2, DSA_TQ), I32), pltpu.VMEM((nkb, DSA_TK // 2, DSA_TQ), I32)],
        compiler_params=_cparams(("arbitrary", "arbitrary")),
        name="dsa_attention",
    )(b_q, b_qi, small, b_kv, v_t, b_ki, jnp.tril(jnp.ones((DSA_TK, DSA_TK), BF16)))
    return o_t.transpose(0, 2, 1).reshape(bn * t, DSA_HEADS * HEAD_DIM)


def _mlstm_kernel(qk_ref, v_ref, og_ref, sm_ref, convw_ref, convb_ref, gbias_ref, normgt_ref,
                  o_ref, tail_ref, ct_ref, nvec_ref, mst_ref, *, bn):
    c = pl.program_id(0)
    L = CHUNK
    nh, dh, width = MLSTM_HEADS, HEAD_DIM, MLSTM_HEADS * HEAD_DIM

    @pl.when(c == 0)
    def _():
        tail_ref[...] = jnp.zeros_like(tail_ref)
        ct_ref[...] = jnp.zeros_like(ct_ref)
        nvec_ref[...] = jnp.zeros_like(nvec_ref)
        mst_ref[...] = jnp.zeros_like(mst_ref)

    srow = lax.broadcasted_iota(I32, (L, width), 0)
    lane = lax.broadcasted_iota(I32, (L, width), 1)
    jlane = lane % dh
    causal_t = srow <= jlane
    diag_t = srow == jlane
    head_of_lane = lax.broadcasted_iota(I32, (1, width), 1) // dh
    tril = (lax.broadcasted_iota(I32, (L, L), 1) <= lax.broadcasted_iota(I32, (L, L), 0)).astype(F32)
    erow = lax.broadcasted_iota(I32, (128, width), 0)
    ecol_head = lax.broadcasted_iota(I32, (128, width), 1) // dh
    expand_ig = (erow == SM_IG + ecol_head).astype(F32)
    expand_fg = (erow == SM_FG + ecol_head).astype(F32)
    exact = dict(preferred_element_type=F32, precision=lax.Precision.HIGHEST)

    def head_blocks(a):
        out = jnp.where(head_of_lane == 0, a[0:dh], 0.0)
        for h in range(1, nh):
            out = out + jnp.where(head_of_lane == h, a[h * dh:(h + 1) * dh], 0.0)
        return out

    convw = convw_ref[...]
    for cc, b in [(cc, b) for cc in range(MLSTM_CPS) for b in range(bn)]:
        rows = slice(cc * L, (cc + 1) * L)
        cur = qk_ref[b, rows, :]
        ext = jnp.concatenate([tail_ref[b], cur], axis=0)
        tail_ref[b] = cur[L - 8:L, :]
        y = convb_ref[...] + convw[MLSTM_CONV - 1:MLSTM_CONV, :] * cur
        for k in range(MLSTM_CONV - 1):
            off = 8 - (MLSTM_CONV - 1) + k
            y = y + convw[k:k + 1, :] * ext[off:off + L, :]
        qk = y * _sigmoid(y)
        q_all = qk[:, 0:width]
        k_all = qk[:, width:2 * width] * dh ** -0.5
        q_stack = jnp.concatenate([jnp.where(head_of_lane == h, q_all, 0.0) for h in range(nh)], axis=0).astype(BF16)
        v_all = v_ref[b, rows, :]
        v_t = v_all.astype(F32).T.astype(BF16)

        gates = sm_ref[b, rows, :] + gbias_ref[...]
        lf = jnp.minimum(gates, 0.0) - jnp.log(1.0 + jnp.exp(-jnp.abs(gates)))
        bcum = jnp.dot(tril, lf, **exact)
        ig_x = jnp.dot(gates, expand_ig, **exact)
        b_x = jnp.dot(bcum, expand_fg, **exact)
        b_q = jnp.sum(jnp.where(diag_t, b_x, 0.0), axis=0, keepdims=True)
        b_last = b_x[L - 1:L, :]
        m_prev = mst_ref[b]
        ct = ct_ref[b]
        nvec = nvec_ref[b]

        dlog = jnp.where(causal_t, b_q - b_x + ig_x, NEG_INF)
        inter = b_q + m_prev
        mj = jnp.maximum(inter, jnp.max(dlog, axis=0, keepdims=True))
        dw = jnp.exp(dlog - mj)
        iw = jnp.exp(inter - mj)
        sc = _dot_nt(k_all.astype(BF16), q_stack) * dw
        qn = _dot_nt(jnp.broadcast_to(nvec, (8, width)).astype(BF16), q_stack)[0:1, :]
        q_c = _dot_nt(ct.astype(BF16), q_stack)
        num = iw * q_c + head_blocks(_dot(v_t, sc.astype(BF16)))
        den = iw * qn + jnp.sum(sc, axis=0, keepdims=True)
        hj = num / jnp.maximum(jnp.abs(den), jnp.exp(-mj))

        dec = b_last - b_x + ig_x
        m_new = jnp.maximum(b_last + m_prev, jnp.max(dec, axis=0, keepdims=True))
        wc = jnp.exp(b_last + m_prev - m_new)
        kw = k_all * jnp.exp(dec - m_new)
        ct_ref[b] = wc * ct + head_blocks(_dot(v_t, kw.astype(BF16)))
        nvec_ref[b] = wc * nvec + jnp.sum(kw, axis=0, keepdims=True)
        mst_ref[b] = m_new

        mu = jnp.mean(hj, axis=0, keepdims=True)
        dev = hj - mu
        var = jnp.mean(dev * dev, axis=0, keepdims=True)
        hn_t = (dev * lax.rsqrt(var + LN_EPS) * normgt_ref[...]).T
        hn = jnp.concatenate([hn_t[h * dh:(h + 1) * dh, :] for h in range(nh)], axis=1)
        o_ref[b, rows, :] = (_sigmoid(og_ref[b, rows, :]) * hn).astype(BF16)


def _mlstm(c_qk, c_v, c_o, small, conv_w, conv_b, gate_bias, norm_g, bn, t):
    nc = t // (CHUNK * MLSTM_CPS)
    width = MLSTM_HEADS * HEAD_DIM
    norm_g_t = jnp.repeat(norm_g.reshape(MLSTM_HEADS, HEAD_DIM).T, HEAD_DIM, axis=1)
    blk = lambda w: pl.BlockSpec((bn, CHUNK * MLSTM_CPS, w), lambda c: (0, c, 0))
    full = lambda a: pl.BlockSpec(a.shape, lambda c: (0,) * a.ndim)
    return pl.pallas_call(
        functools.partial(_mlstm_kernel, bn=bn),
        grid=(nc,),
        in_specs=[blk(512), blk(256), blk(256), blk(128), full(conv_w), full(conv_b), full(gate_bias), full(norm_g_t)],
        out_specs=blk(256),
        out_shape=jax.ShapeDtypeStruct((bn, t, 256), BF16),
        scratch_shapes=[pltpu.VMEM((bn, 8, 512), F32), pltpu.VMEM((bn, HEAD_DIM, width), F32),
                        pltpu.VMEM((bn, 1, width), F32), pltpu.VMEM((bn, 1, width), F32)],
        compiler_params=_cparams(("arbitrary",)),
        name="mlstm",
    )(c_qk.reshape(bn, t, 512), c_v.reshape(bn, t, 256), c_o.reshape(bn, t, 256), small.reshape(bn, t, 128),
      conv_w, conv_b, gate_bias, norm_g_t)


def _s5_kernel(u_ref, perm_ref, bbre_ref, bbim_ref, pwre_ref, pwim_ref, cre_ref, cim_ref, dskip_ref, gluw_ref,
               glub_ref, o_ref, sre_ref, sim_ref, stre_ref, stim_ref, yp_ref):
    @pl.when(pl.program_id(1) == 0)
    def _():
        stre_ref[...] = jnp.zeros_like(stre_ref)
        stim_ref[...] = jnp.zeros_like(stim_ref)

    nb = S5_NB
    seg_len = S5_TT // S5_SEGS
    us = [u_ref[b] for b in range(nb)]
    for b in range(nb):
        ub = _dot(perm_ref[...], us[b].astype(BF16)).astype(BF16)
        sre_ref[b] = _dot(ub, bbre_ref[...])
        sim_ref[b] = _dot(ub, bbim_ref[...])
    a_re = jnp.broadcast_to(pwre_ref[0:1, :], (S5_SEGS, S5_LANES))
    a_im = jnp.broadcast_to(pwim_ref[0:1, :], (S5_SEGS, S5_LANES))

    def local_step(i, carry):
        rows = pl.ds(pl.multiple_of(i * S5_SEGS, S5_SEGS), S5_SEGS)
        out = []
        for b in range(nb):
            s_re, s_im = carry[b]
            n_re = a_re * s_re - a_im * s_im + sre_ref[b, rows, :]
            n_im = a_re * s_im + a_im * s_re + sim_ref[b, rows, :]
            sre_ref[b, rows, :] = n_re
            sim_ref[b, rows, :] = n_im
            out.append((n_re, n_im))
        return tuple(out)

    zeros = jnp.zeros((S5_SEGS, S5_LANES), F32)
    ends = lax.fori_loop(0, seg_len, local_step, tuple((zeros, zeros) for _ in range(nb)), unroll=4)

    al_re, al_im = pwre_ref[seg_len - 1:seg_len, :], pwim_ref[seg_len - 1:seg_len, :]
    cins = []
    for b in range(nb):
        e_re, e_im = ends[b]
        c_re, c_im = stre_ref[b, 0:1, :], stim_ref[b, 0:1, :]
        cs_re, cs_im = [], []
        for k in range(S5_SEGS):
            cs_re.append(c_re)
            cs_im.append(c_im)
            c_re, c_im = (e_re[k:k + 1, :] + al_re * c_re - al_im * c_im,
                          e_im[k:k + 1, :] + al_re * c_im + al_im * c_re)
        stre_ref[b] = jnp.broadcast_to(c_re, stre_ref.shape[1:])
        stim_ref[b] = jnp.broadcast_to(c_im, stim_ref.shape[1:])
        cins.append((jnp.concatenate(cs_re, axis=0), jnp.concatenate(cs_im, axis=0)))

    def correct_step(i, carry):
        rows = pl.ds(pl.multiple_of(i * S5_SEGS, S5_SEGS), S5_SEGS)
        p_re, p_im = pwre_ref[pl.ds(i, 1), :], pwim_ref[pl.ds(i, 1), :]
        for b in range(nb):
            cin_re, cin_im = cins[b]
            sre_ref[b, rows, :] = sre_ref[b, rows, :] + p_re * cin_re - p_im * cin_im
            sim_ref[b, rows, :] = sim_ref[b, rows, :] + p_re * cin_im + p_im * cin_re
        return carry

    lax.fori_loop(0, seg_len, correct_step, 0, unroll=4)
    ngrp = GROUP_WIDTH // 128
    for b in range(nb):
        ycs = _dot(sre_ref[b].astype(BF16), cre_ref[...]) - _dot(sim_ref[b].astype(BF16), cim_ref[...])
        for g in range(ngrp):
            yp_ref[b, g] = ycs[:, g * 128:(g + 1) * 128]
        blocks = []
        for k in range(S5_SEGS):
            for i0 in range(0, seg_len, 8):
                rows = pl.ds(i0 * S5_SEGS + k, 8, stride=S5_SEGS)
                blocks.append(jnp.concatenate([yp_ref[b, g, rows, :] for g in range(ngrp)], axis=1))
        y = jnp.concatenate(blocks, axis=0) + dskip_ref[...] * us[b]
        y = 0.5 * y * (1.0 + jnp.tanh(math.sqrt(2.0 / math.pi) * (y + 0.044715 * (y * y * y))))
        z = _dot(y.astype(BF16), gluw_ref[...]) + glub_ref[...]
        o_ref[b] = (y * _sigmoid(z)).astype(BF16)


def _s5(d_u, bb_re, bb_im, pw_re, pw_im, c_re_t, c_im_t, d_skip, glu_w, glu_b, bn, t):
    nt = t // S5_TT
    nb = S5_NB
    full = lambda a: pl.BlockSpec(a.shape, lambda b, i: (0,) * a.ndim)
    r = jnp.arange(S5_TT)
    perm = (r[None, :] == ((r % S5_SEGS) * (S5_TT // S5_SEGS) + r // S5_SEGS)[:, None]).astype(BF16)
    args = (perm, bb_re, bb_im, pw_re, pw_im, c_re_t, c_im_t, d_skip, glu_w, glu_b)
    tile = pl.BlockSpec((nb, S5_TT, GROUP_WIDTH), lambda b, i: (b, i, 0))
    return pl.pallas_call(
        _s5_kernel,
        grid=(bn // nb, nt),
        in_specs=[tile] + [full(a) for a in args],
        out_specs=tile,
        out_shape=jax.ShapeDtypeStruct((bn, t, GROUP_WIDTH), BF16),
        scratch_shapes=[pltpu.VMEM((nb, S5_TT, S5_LANES), F32), pltpu.VMEM((nb, S5_TT, S5_LANES), F32),
                        pltpu.VMEM((nb, 8, S5_LANES), F32), pltpu.VMEM((nb, 8, S5_LANES), F32),
                        pltpu.VMEM((nb, GROUP_WIDTH // 128, S5_TT, 128), F32)],
        compiler_params=_cparams(("arbitrary", "arbitrary")),
        name="s5_glu",
    )(d_u.reshape(bn, t, GROUP_WIDTH), *args).reshape(bn * t, GROUP_WIDTH)


def _outproj_ffn_kernel(x_ref, modm_ref, oa_ref, ob_ref, oc_ref, od_ref, wout_ref, lngm_ref, lnbm_ref,
                        modf_ref, w13_ref, w2_ref, lngf_ref, lnbf_ref, o_ref, h_ref):
    y = _dot(oa_ref[...], wout_ref[0])
    y = y + _dot(ob_ref[...], wout_ref[1])
    y = y + _dot(oc_ref[...], wout_ref[2])
    y = y + _dot(od_ref[...], wout_ref[3])
    x_mid = _residual_layer_norm(x_ref[...], modm_ref[2:3, :] * y, lngm_ref[...], lnbm_ref[...])
    o_ref[...] = _ffn_tile(x_mid, modf_ref, w13_ref, w2_ref, lngf_ref, lnbf_ref, h_ref)


def _outproj_ffn(x2, mod_mix, o_a, o_b, o_c, o_d, w_out4, ln_g_mix, ln_b_mix, mod_ffn, w13c, w2c, ln_g_ffn, ln_b_ffn,
                 rows_per_batch):
    n = x2.shape[0]
    tm = ROW_TILE
    tiles_per_batch = rows_per_batch // tm
    row = pl.BlockSpec((tm, D_MODEL), lambda i: (i, 0))
    mod = pl.BlockSpec((None, 8, D_MODEL), lambda i: (i // tiles_per_batch, 0, 0))
    mix = pl.BlockSpec((tm, GROUP_WIDTH), lambda i: (i, 0))
    vec = pl.BlockSpec((1, D_MODEL), lambda i: (0, 0))
    resident = lambda a: pl.BlockSpec(a.shape, lambda i: (0,) * a.ndim, pipeline_mode=pl.Buffered(1))
    return pl.pallas_call(
        _outproj_ffn_kernel,
        grid=(n // tm,),
        in_specs=[row, mod, mix, mix, mix, mix, resident(w_out4), vec, vec, mod, resident(w13c), resident(w2c), vec, vec],
        out_specs=row,
        out_shape=jax.ShapeDtypeStruct((n, D_MODEL), F32),
        scratch_shapes=[pltpu.VMEM((tm, D_FF), BF16)],
        compiler_params=_cparams(("arbitrary",), [False] * 6 + [True] + [False] * 3 + [True, True, False, False]),
        name="outproj_ffn",
    )(x2, mod_mix, o_a, o_b, o_c, o_d, w_out4, ln_g_mix.reshape(1, -1), ln_b_mix.reshape(1, -1),
      mod_ffn, w13c, w2c, ln_g_ffn.reshape(1, -1), ln_b_ffn.reshape(1, -1))


def _permute_w_in(w_in):
    off = {}
    o = 0
    for name, s in (("qa", 256), ("ka", 128), ("va", 128), ("cq", 128), ("kb", 64), ("vb", 64), ("ki", 32),
                    ("wi", 4), ("qkc", 512), ("vc", 256), ("ig", 4), ("fg", 4), ("oc", 256), ("ud", 256)):
        off[name] = (o, o + s)
        o += s
    col = lambda n: w_in[:, off[n][0]:off[n][1]]
    small = jnp.concatenate([col("wi"), col("ig"), col("fg"),
                             jnp.zeros((w_in.shape[0], 128 - 12), w_in.dtype)], axis=1)
    parts = [col("qa"), col("ka"), col("va"), col("cq"), col("kb"), col("vb")] + [col("ki")] * 4 + [
        col("qkc"), col("vc"), col("oc"), col("ud"), small]
    return jnp.concatenate(parts, axis=1).astype(BF16)


def _s5_params(lam_re, lam_im, log_step, b_re, b_im, c_re, c_im):
    dt = jnp.exp(log_step)[:, None]
    mag = jnp.exp(lam_re * dt)
    a_re, a_im = mag * jnp.cos(lam_im * dt), mag * jnp.sin(lam_im * dt)
    den = lam_re * lam_re + lam_im * lam_im
    kap_re = ((a_re - 1.0) * lam_re + a_im * lam_im) / den
    kap_im = (a_im * lam_re - (a_re - 1.0) * lam_im) / den
    bb_re = kap_re[..., None] * b_re - kap_im[..., None] * b_im
    bb_im = kap_re[..., None] * b_im + kap_im[..., None] * b_re
    eye = jnp.eye(S5_GROUPS, dtype=F32)

    def in_mat(bb):
        return jnp.einsum("gph,gk->ghkp", bb, eye).reshape(S5_GROUPS * S5_GROUP_CH, S5_LANES).astype(BF16)

    def out_mat(cc):
        return jnp.einsum("gop,gk->gpko", cc, eye).reshape(S5_LANES, S5_GROUPS * S5_GROUP_CH).astype(BF16)

    n = jnp.arange(1, S5_TT // S5_SEGS + 1, dtype=F32)[:, None, None]
    pw_mag = jnp.exp(n * (lam_re * dt))
    pw_re = (pw_mag * jnp.cos(n * (lam_im * dt))).at[0].set(a_re).reshape(-1, S5_LANES)
    pw_im = (pw_mag * jnp.sin(n * (lam_im * dt))).at[0].set(a_im).reshape(-1, S5_LANES)
    return in_mat(bb_re), in_mat(bb_im), pw_re, pw_im, out_mat(c_re), out_mat(c_im)


def kernel(x, c, ada_w, ada_b, ln_g, ln_b, ffn_w13, ffn_w2, w_in, w_out, sinks, w_uq, w_iq, conv_w, conv_b, ig_b,
           fg_b, mh_norm_g, lam_re, lam_im, log_step, b_re, b_im, c_re, c_im, d_skip, glu_w, glu_b):
    bn, t, d = x.shape
    assert d == D_MODEL and t % max(ROW_TILE, DSA_TK, S5_TT) == 0 and bn <= 8 and bn % S5_NB == 0
    n = bn * t
    nl = ada_w.shape[0]
    c_pad = jnp.zeros((8, d), F32).at[:bn].set(c)
    mod_all = _ada_mod(c_pad, ada_w, ada_b)
    mod_all = mod_all[:, :bn].reshape(nl, bn, N_SUB, 3, d).transpose(0, 2, 1, 3, 4)
    mod_all = jnp.pad(mod_all, ((0, 0), (0, 0), (0, 0), (0, 5), (0, 0)))

    x2 = x.reshape(n, d)
    for l in range(nl):
        w13c = ffn_w13[l].astype(BF16)
        w2c = ffn_w2[l].astype(BF16)
        x2, (a_q, a_kv, b_q, b_qi, b_kv, b_ki, c_qk, c_v, c_o, d_u, small) = _ffn_inproj(
            x2, mod_all[l, 0], w13c[0], w2c[0], ln_g[l, 0], ln_b[l, 0], mod_all[l, 1],
            _permute_w_in(w_in[l]), w_uq[l].astype(BF16), w_iq[l].astype(BF16), t)
        o_a = _swa(sinks[l], a_q, a_kv, bn, t)
        o_b = _dsa(b_q, b_qi, small, b_kv, b_ki, bn, t)
        gate_bias = jnp.zeros((1, 128), F32).at[0, SM_IG:SM_IG + 4].set(ig_b[l]).at[0, SM_FG:SM_FG + 4].set(fg_b[l])
        o_c = _mlstm(c_qk, c_v, c_o, small, conv_w[l], conv_b[l].reshape(1, -1), gate_bias,
                     mh_norm_g[l].reshape(1, -1), bn, t).reshape(n, GROUP_WIDTH)
        s5p = _s5_params(lam_re[l], lam_im[l], log_step[l], b_re[l], b_im[l], c_re[l], c_im[l])
        o_d = _s5(d_u, *s5p, d_skip[l].reshape(1, -1), glu_w[l].astype(BF16), glu_b[l].reshape(1, -1), bn, t)
        x2 = _outproj_ffn(x2, mod_all[l, 1], o_a, o_b, o_c, o_d, w_out[l].astype(BF16).reshape(4, GROUP_WIDTH, d),
                          ln_g[l, 1], ln_b[l, 1], mod_all[l, 2], w13c[1], w2c[1], ln_g[l, 2], ln_b[l, 2], t)
    return x2.reshape(bn, t, d)
```
